```python
import math
import jax
import jax.numpy as jnp
from jax import lax
import numpy as np

D_MODEL = 1024
BATCH = 2
SEQ = 8192
DEPTH = 2

GRID_W = 64
CTX_LEN = 256
N_MOD = 6
DA_HEADS = 4
DA_QK = 64
DA_V = 2 * DA_QK
DA_QK_W = DA_HEADS * 2 * DA_QK
DA_V_W = DA_HEADS * DA_V
MLA_HEADS = 4
MLA_NOPE = 128
MLA_ROPE = 64
MLA_V = 128
MLA_Q_RANK = 384
MLA_KV_RANK = 256
MLA_V_W = MLA_HEADS * MLA_V
POOL_GROUPS = 4
POOL_WINDOWS = (2, 4, 8, 16)
POOL_GROUP_W = 128
POOL_W = POOL_GROUPS * POOL_GROUP_W
N_BRANCH = 3
BRANCH_W = 512
D_FF = 2816
N_EXPERTS = 8
TOP_K = 2
D_FF_EXPERT = 3584
ROPE_BASE = 10000.0
Q_BLOCK = 128
EPS = 1e-6
IN_SPLITS = (DA_QK_W, DA_QK_W, DA_V_W, MLA_Q_RANK, MLA_KV_RANK, MLA_ROPE, POOL_W, N_BRANCH * D_MODEL)
IN_W = 5824

kernel_name = 'hybrid_diffattn_mla_pool_moe_dit'


def rmsnorm(x, g):
    xf = x.astype(jnp.float32)
    y = xf * lax.rsqrt(jnp.mean(xf * xf, axis=-1, keepdims=True) + EPS)
    return (y * g.astype(jnp.float32)).astype(x.dtype)


def modulate(h, shift, scale):
    return h * (1.0 + scale) + shift


def adaln_params(cond, w, b):
    return jnp.matmul(jax.nn.silu(cond), w) + b


def split_columns(z):
    offs = []
    acc = 0
    for s in IN_SPLITS[:-1]:
        acc += s
        offs.append(acc)
    return jnp.split(z, offs, axis=-1)


def axial_rope_tables(row_pos, col_pos, rot_dim):
    axis_dim = rot_dim // 2
    n_freq = axis_dim // 2
    inv = jnp.exp(-math.log(ROPE_BASE) * jnp.arange(n_freq, dtype=jnp.float32) * (2.0 / axis_dim))
    ar = row_pos.astype(jnp.float32)[:, None] * inv
    ac = col_pos.astype(jnp.float32)[:, None] * inv
    return (jnp.cos(ar), jnp.sin(ar), jnp.cos(ac), jnp.sin(ac))


def rope_half(x, cos, sin):
    cos = cos[None, :, None, :].astype(x.dtype)
    sin = sin[None, :, None, :].astype(x.dtype)
    x1, x2 = jnp.split(x, 2, axis=-1)
    return jnp.concatenate([x1 * cos - x2 * sin, x2 * cos + x1 * sin], axis=-1)


def rope_2d(x, tabs):
    cr, sr, cc, sc = tabs
    a = x.shape[-1] // 2
    return jnp.concatenate([rope_half(x[..., :a], cr, sr), rope_half(x[..., a:], cc, sc)], axis=-1)


def sweep_query_blocks(fn, qs):
    B, S = qs[0].shape[:2]
    nb = S // Q_BLOCK
    blocks = tuple(jnp.moveaxis(q.reshape((B, nb, Q_BLOCK) + q.shape[2:]), 1, 0) for q in qs)
    out = lax.map(lambda blk: fn(*blk), blocks)
    out = jnp.moveaxis(out, 0, 1)
    return out.reshape((B, S) + out.shape[3:])


def diff_attention(q1, q2, k1, k2, v, lam):
    scale = DA_QK ** -0.5
    s1 = jnp.einsum('bqhd,bkhd->bhqk', q1, k1).astype(jnp.float32) * scale
    s2 = jnp.einsum('bqhd,bkhd->bhqk', q2, k2).astype(jnp.float32) * scale
    a = jax.nn.softmax(s1, axis=-1) - lam * jax.nn.softmax(s2, axis=-1)
    return jnp.einsum('bhqk,bkhd->bqhd', a.astype(v.dtype), v)


def diff_head_norm(o, g, lam_init):
    B, L = o.shape[:2]
    return (rmsnorm(o, g) * (1.0 - lam_init)).reshape(B, L, DA_V_W)


def mla_queries(qd, gq, w_uq):
    B, L = qd.shape[:2]
    qf = jnp.matmul(rmsnorm(qd, gq), w_uq).reshape(B, L, MLA_HEADS, MLA_NOPE + MLA_ROPE)
    return qf[..., :MLA_NOPE], qf[..., MLA_NOPE:]


def mla_keys_values(kvd, gkv, w_ukv):
    B, L = kvd.shape[:2]
    kvf = jnp.matmul(rmsnorm(kvd, gkv), w_ukv).reshape(B, L, MLA_HEADS, MLA_NOPE + MLA_V)
    return kvf[..., :MLA_NOPE], kvf[..., MLA_NOPE:]


def mla_attention(q_nope, q_rope, k_nope, k_rope, v):
    scale = (MLA_NOPE + MLA_ROPE) ** -0.5
    s = (jnp.einsum('bqhd,bkhd->bhqk', q_nope, k_nope)
         + jnp.einsum('bqhr,bkr->bhqk', q_rope, k_rope)).astype(jnp.float32) * scale
    p = jax.nn.softmax(s, axis=-1)
    return jnp.einsum('bhqk,bkhd->bqhd', p.astype(v.dtype), v)


def multi_scale_pool(u, pool_w, pool_scale):
    B, L, _ = u.shape
    ug = u.reshape(B, L, POOL_GROUPS, POOL_GROUP_W).astype(jnp.float32)
    cs = jnp.concatenate([jnp.zeros((B, 1, POOL_GROUPS, POOL_GROUP_W), jnp.float32),
                          jnp.cumsum(ug, axis=1)], axis=1)
    t = jnp.arange(L)
    outs = []
    for g, w in enumerate(POOL_WINDOWS):
        lo = jnp.clip(t - w // 2, 0, L)
        hi = jnp.clip(t - w // 2 + w, 0, L)
        cnt = (hi - lo).astype(jnp.float32)[None, :, None]
        mean = (cs[:, hi, g] - cs[:, lo, g]) / cnt
        outs.append(mean - ug[:, :, g])
    d = jnp.stack(outs, axis=2).astype(u.dtype)
    y = jnp.einsum('blgc,gcd->blgd', d, pool_w).reshape(B, L, POOL_W)
    return y * pool_scale


def merge_branches(branches, gate_logits, w_branch, w_out):
    B, L = branches.shape[:2]
    D = w_out.shape[0]
    g = jax.nn.sigmoid(gate_logits.astype(jnp.float32)).astype(branches.dtype).reshape(B, L, N_BRANCH, D)
    proj = jnp.einsum('blnc,ncd->blnd', branches, w_branch)
    return jnp.matmul(jnp.einsum('blnd,blnd->bld', g, proj), w_out)


def hybrid_mixer(h, hc, w_in, lam_vec, lam_init, subln_g, gq, w_uq, gkv, w_ukv,
                 pool_w, pool_scale, w_branch, w_out, da_tabs, mla_tabs, need_ctx):
    B, S, _ = h.shape
    C = hc.shape[1]
    da_q, da_k, da_v, mla_qd, mla_kvd, mla_kr, pool_in, gate_in = split_columns(jnp.matmul(h, w_in))
    cda_q, cda_k, cda_v, cmla_qd, cmla_kvd, cmla_kr, cpool_in, cgate_in = split_columns(jnp.matmul(hc, w_in))
    lv = lam_vec.astype(jnp.float32)
    lam = jnp.exp(jnp.sum(lv[0] * lv[1])) - jnp.exp(jnp.sum(lv[2] * lv[3])) + lam_init

    q = da_q.reshape(B, S, DA_HEADS, 2, DA_QK)
    k = da_k.reshape(B, S, DA_HEADS, 2, DA_QK)
    v = da_v.reshape(B, S, DA_HEADS, DA_V)
    cq = cda_q.reshape(B, C, DA_HEADS, 2, DA_QK)
    ck = cda_k.reshape(B, C, DA_HEADS, 2, DA_QK)
    cv = cda_v.reshape(B, C, DA_HEADS, DA_V)
    k1 = jnp.concatenate([ck[..., 0, :], rope_2d(k[..., 0, :], da_tabs)], axis=1)
    k2 = jnp.concatenate([ck[..., 1, :], rope_2d(k[..., 1, :], da_tabs)], axis=1)
    v_all = jnp.concatenate([cv, v], axis=1)
    q1 = rope_2d(q[..., 0, :], da_tabs)
    q2 = rope_2d(q[..., 1, :], da_tabs)
    o_da = sweep_query_blocks(lambda a, b: diff_attention(a, b, k1, k2, v_all, lam), (q1, q2))
    o_da = diff_head_norm(o_da, subln_g, lam_init)

    k_nope, v_m = mla_keys_values(mla_kvd, gkv, w_ukv)
    ck_nope, cv_m = mla_keys_values(cmla_kvd, gkv, w_ukv)
    k_rope = rope_2d(mla_kr[:, :, None, :], mla_tabs)[:, :, 0, :]
    kn_all = jnp.concatenate([ck_nope, k_nope], axis=1)
    kr_all = jnp.concatenate([cmla_kr, k_rope], axis=1)
    vm_all = jnp.concatenate([cv_m, v_m], axis=1)
    q_nope, q_rope = mla_queries(mla_qd, gq, w_uq)
    q_rope = rope_2d(q_rope, mla_tabs)
    o_mla = sweep_query_blocks(lambda a, b: mla_attention(a, b, kn_all, kr_all, vm_all),
                               (q_nope, q_rope)).reshape(B, S, MLA_V_W)

    o_pool = multi_scale_pool(pool_in, pool_w, pool_scale)

    mix = merge_branches(jnp.stack([o_da, o_mla, o_pool], axis=2), gate_in, w_branch, w_out)
    if not need_ctx:
        return mix, None

    co_da = diff_head_norm(diff_attention(cq[..., 0, :], cq[..., 1, :], ck[..., 0, :], ck[..., 1, :], cv, lam),
                           subln_g, lam_init)
    cq_nope, cq_rope = mla_queries(cmla_qd, gq, w_uq)
    co_mla = mla_attention(cq_nope, cq_rope, ck_nope, cmla_kr, cv_m).reshape(B, C, MLA_V_W)
    co_pool = multi_scale_pool(cpool_in, pool_w, pool_scale)
    mix_c = merge_branches(jnp.stack([co_da, co_mla, co_pool], axis=2), cgate_in, w_branch, w_out)
    return mix, mix_c


def swiglu(t, wg, wu, wd):
    return jnp.matmul(jax.nn.silu(jnp.matmul(t, wg)) * jnp.matmul(t, wu), wd)


def moe_swiglu(h, router, wg, wu, wd):
    B, L, D = h.shape
    t = h.reshape(B * L, D)
    logits = jnp.matmul(t, router).astype(jnp.float32)
    top_val, top_idx = lax.top_k(logits, TOP_K)
    top_w = jax.nn.softmax(top_val, axis=-1)
    gates = jnp.einsum('nk,nke->ne', top_w,
                       jax.nn.one_hot(top_idx, N_EXPERTS, dtype=jnp.float32)).astype(h.dtype)
    y = jnp.zeros_like(t)
    for e in range(N_EXPERTS):
        y = y + gates[:, e:e + 1] * swiglu(t, wg[e], wu[e], wd[e])
    return y.reshape(B, L, D)


def setup_inputs(seed: int = 0) -> dict:
    key = jax.random.key(seed)
    ks = jax.random.split(key, 32)
    f32 = jnp.float32
    D = D_MODEL
    n_dense = (DEPTH + 1) // 2
    n_moe = DEPTH // 2

    def nrm(k, shape, scale):
        return jax.random.normal(k, shape, f32) * scale

    def gain(k, shape):
        return 1.0 + 0.1 * jax.random.normal(k, shape, f32)

    return {
        'x': nrm(ks[0], (BATCH, SEQ, D), 1.0),
        'c': nrm(ks[1], (BATCH, D), 1.0),
        'ctx': nrm(ks[2], (BATCH, CTX_LEN, D), 1.0),
        'c_ctx': nrm(ks[3], (D,), 1.0),
        'w_mod': nrm(ks[4], (DEPTH, D, N_MOD * D), 0.5 * D ** -0.5),
        'b_mod': nrm(ks[5], (DEPTH, N_MOD * D), 0.02),
        'g_mix': gain(ks[6], (DEPTH, D)),
        'w_in': nrm(ks[7], (DEPTH, D, IN_W), D ** -0.5),
        'da_lambda': nrm(ks[8], (DEPTH, 4, DA_QK), 0.1),
        'da_subln': gain(ks[9], (DEPTH, DA_V)),
        'mla_gq': gain(ks[10], (DEPTH, MLA_Q_RANK)),
        'w_uq': nrm(ks[11], (DEPTH, MLA_Q_RANK, MLA_HEADS * (MLA_NOPE + MLA_ROPE)), MLA_Q_RANK ** -0.5),
        'mla_gkv': gain(ks[12], (DEPTH, MLA_KV_RANK)),
        'w_ukv': nrm(ks[13], (DEPTH, MLA_KV_RANK, MLA_HEADS * (MLA_NOPE + MLA_V)), MLA_KV_RANK ** -0.5),
        'pool_w': nrm(ks[14], (DEPTH, POOL_GROUPS, POOL_GROUP_W, POOL_GROUP_W), POOL_GROUP_W ** -0.5),
        'pool_scale': gain(ks[15], (DEPTH, POOL_W)),
        'w_branch': nrm(ks[16], (DEPTH, N_BRANCH, BRANCH_W, D), BRANCH_W ** -0.5),
        'w_out': nrm(ks[17], (DEPTH, D, D), D ** -0.5),
        'g_ffn': gain(ks[18], (DEPTH, D)),
        'ffn_w_gate': nrm(ks[19], (n_dense, D, D_FF), D ** -0.5),
        'ffn_w_up': nrm(ks[20], (n_dense, D, D_FF), D ** -0.5),
        'ffn_w_down': nrm(ks[21], (n_dense, D_FF, D), D_FF ** -0.5),
        'moe_router': nrm(ks[22], (n_moe, D, N_EXPERTS), D ** -0.5),
        'moe_w_gate': nrm(ks[23], (n_moe, N_EXPERTS, D, D_FF_EXPERT), D ** -0.5),
        'moe_w_up': nrm(ks[24], (n_moe, N_EXPERTS, D, D_FF_EXPERT), D ** -0.5),
        'moe_w_down': nrm(ks[25], (n_moe, N_EXPERTS, D_FF_EXPERT, D), D_FF_EXPERT ** -0.5),
        'g_final': gain(ks[26], (D,)),
    }


def reference(x, c, ctx, c_ctx, w_mod, b_mod, g_mix, w_in, da_lambda, da_subln, mla_gq, w_uq,
              mla_gkv, w_ukv, pool_w, pool_scale, w_branch, w_out, g_ffn, ffn_w_gate, ffn_w_up,
              ffn_w_down, moe_router, moe_w_gate, moe_w_up, moe_w_down, g_final):
    B, S, D = x.shape
    rows = S // GRID_W
    row_pos = jnp.repeat(jnp.arange(rows, dtype=jnp.int32), GRID_W)
    col_pos = jnp.tile(jnp.arange(GRID_W, dtype=jnp.int32), rows)
    da_tabs = axial_rope_tables(row_pos, col_pos, DA_QK)
    mla_tabs = axial_rope_tables(row_pos, col_pos, MLA_ROPE)

    xc = ctx
    for l in range(DEPTH):
        need_ctx = l < DEPTH - 1
        mod = adaln_params(c, w_mod[l], b_mod[l])[:, None, :]
        mod_c = adaln_params(c_ctx, w_mod[l], b_mod[l])[None, None, :]
        sh1, sc1, gt1, sh2, sc2, gt2 = jnp.split(mod, N_MOD, axis=-1)
        csh1, csc1, cgt1, csh2, csc2, cgt2 = jnp.split(mod_c, N_MOD, axis=-1)
        lam_init = 0.8 - 0.6 * math.exp(-0.3 * l)

        h = modulate(rmsnorm(x, g_mix[l]), sh1, sc1)
        hc = modulate(rmsnorm(xc, g_mix[l]), csh1, csc1)
        mix, mix_c = hybrid_mixer(h, hc, w_in[l], da_lambda[l], lam_init, da_subln[l], mla_gq[l], w_uq[l],
                                  mla_gkv[l], w_ukv[l], pool_w[l], pool_scale[l], w_branch[l], w_out[l],
                                  da_tabs, mla_tabs, need_ctx)
        x = x + gt1 * mix
        h = modulate(rmsnorm(x, g_ffn[l]), sh2, sc2)
        j = l // 2
        if l % 2 == 0:
            x = x + gt2 * swiglu(h, ffn_w_gate[j], ffn_w_up[j], ffn_w_down[j])
        else:
            x = x + gt2 * moe_swiglu(h, moe_router[j], moe_w_gate[j], moe_w_up[j], moe_w_down[j])

        if need_ctx:
            xc = xc + cgt1 * mix_c
            hc = modulate(rmsnorm(xc, g_ffn[l]), csh2, csc2)
            if l % 2 == 0:
                xc = xc + cgt2 * swiglu(hc, ffn_w_gate[j], ffn_w_up[j], ffn_w_down[j])
            else:
                xc = xc + cgt2 * moe_swiglu(hc, moe_router[j], moe_w_gate[j], moe_w_up[j], moe_w_down[j])

    return rmsnorm(x, g_final)
```

```python
import functools
import math

import jax
import jax.numpy as jnp
from jax import lax
from jax.experimental import pallas as pl
from jax.experimental.pallas import tpu as pltpu

F32 = jnp.float32
BF16 = jnp.bfloat16

GRID_W = 64
DA_HEADS = 4
DA_QK = 64
DA_V = 128
MLA_HEADS = 4
MLA_NOPE = 128
MLA_ROPE = 64
MLA_V = 128
MLA_Q_RANK = 384
MLA_KV_RANK = 256
POOL_WINDOWS = (2, 4, 8, 16)
POOL_GROUP_W = 128
N_BRANCH = 3
BRANCH_W = 512
ROPE_BASE = 10000.0
EPS = 1e-6
N_MOD = 6
LANES = 128
MLA_QK_PAD = 256
DA_SCALE = DA_QK ** -0.5
MLA_SCALE = (MLA_NOPE + MLA_ROPE) ** -0.5
NEG_BIG = -1e30
VMEM_LIMIT = 56 * 1024 * 1024

C_DAQ, C_DAK, C_DAV = 0, 512, 1024
C_QD = 1536
C_KVD = C_QD + MLA_Q_RANK
C_KR = C_KVD + MLA_KV_RANK
C_POOL = C_KR + LANES
W2_COLS = C_POOL + 512


def _sigmoid(v):
    return 1.0 / (1.0 + jnp.exp(-v))


def _params(sem, vmem=VMEM_LIMIT):
    return pltpu.CompilerParams(dimension_semantics=sem, vmem_limit_bytes=vmem)


def _const_spec(shape):
    nd = len(shape)
    return pl.BlockSpec(shape, lambda *_: (0,) * nd)


def _mod_kernel(cond_ref, w_ref, b_ref, o_ref):
    c = cond_ref[...]
    s = c * _sigmoid(c)
    o_ref[...] = jnp.dot(s, w_ref[...], preferred_element_type=F32,
                         precision=lax.Precision.HIGHEST) + b_ref[...]


def _mod_rows(cond8, w_mod, b_mod):
    depth, d, n = w_mod.shape
    tn = 1536
    return pl.pallas_call(
        _mod_kernel,
        out_shape=jax.ShapeDtypeStruct((depth, 8, n), F32),
        grid=(depth, n // tn),
        in_specs=[pl.BlockSpec((8, d), lambda l, j: (0, 0)),
                  pl.BlockSpec((None, d, tn), lambda l, j: (l, 0, j)),
                  pl.BlockSpec((None, 1, tn), lambda l, j: (l, 0, j))],
        out_specs=pl.BlockSpec((None, 8, tn), lambda l, j: (l, 0, j)),
        compiler_params=_params(("arbitrary", "arbitrary")),
        name="adaln_rows",
    )(cond8, w_mod, b_mod.reshape(depth, 1, n))


def _norm_mod(x, g, sh, sc):
    r = lax.rsqrt(jnp.mean(x * x, axis=-1, keepdims=True) + EPS)
    return (x * r * g) * (1.0 + sc) + sh


def _rms(v, g):
    return v * lax.rsqrt(jnp.mean(v * v, axis=-1, keepdims=True) + EPS) * g


def _inproj_kernel(x_ref, g_ref, sh_ref, sc_ref, w_ref, gq_ref, wuq_ref, gkv_ref, wukv_ref,
                   cos_ref, sa_ref, sb_ref,
                   qda_ref, kda_ref, vda_ref, qm_ref, km_ref, vm_ref, pool_ref):
    h = _norm_mod(x_ref[...], g_ref[...], sh_ref[...], sc_ref[...])
    z = jnp.dot(h.astype(BF16), w_ref[...], preferred_element_type=F32)
    cos = cos_ref[...]
    sa = sa_ref[...]
    sb = sb_ref[...]

    def rope(blk):
        return blk * cos + pltpu.roll(blk, LANES - 16, 1) * sa + pltpu.roll(blk, 16, 1) * sb

    lo = lax.broadcasted_iota(jnp.int32, cos.shape, 1) < MLA_ROPE

    for hh in range(DA_HEADS):
        c0 = hh * LANES
        qda_ref[:, c0:c0 + LANES] = (rope(z[:, C_DAQ + c0:C_DAQ + c0 + LANES]) * DA_SCALE).astype(BF16)
        kda_ref[:, c0:c0 + LANES] = rope(z[:, C_DAK + c0:C_DAK + c0 + LANES]).astype(BF16)
    vda_ref[...] = z[:, C_DAV:C_DAV + 512].astype(BF16)

    qn = _rms(z[:, C_QD:C_QD + MLA_Q_RANK], gq_ref[...])
    qf = jnp.dot(qn.astype(BF16), wuq_ref[...], preferred_element_type=F32)
    kvn = _rms(z[:, C_KVD:C_KVD + MLA_KV_RANK], gkv_ref[...])
    kvf = jnp.dot(kvn.astype(BF16), wukv_ref[...], preferred_element_type=F32)
    kr = jnp.where(lo, rope(z[:, C_KR:C_KR + LANES]), 0.0).astype(BF16)
    for hh in range(MLA_HEADS):
        c0 = hh * MLA_QK_PAD
        qm_ref[:, c0:c0 + LANES] = (qf[:, c0:c0 + LANES] * MLA_SCALE).astype(BF16)
        qr = jnp.where(lo, rope(qf[:, c0 + LANES:c0 + 2 * LANES]), 0.0)
        qm_ref[:, c0 + LANES:c0 + 2 * LANES] = (qr * MLA_SCALE).astype(BF16)
        km_ref[:, c0:c0 + LANES] = kvf[:, hh * LANES:(hh + 1) * LANES].astype(BF16)
        km_ref[:, c0 + LANES:c0 + 2 * LANES] = kr
    vm_ref[...] = kvf[:, 512:1024].astype(BF16)
    pool_ref[...] = z[:, C_POOL:C_POOL + 512]


def _inproj(x, modrows, g_mix, w2, gq, wuq, gkv, wukv, tabs, tm):
    b, s, d = x.shape
    tm = min(tm, s)
    cos, sa, sb = tabs
    row = lambda j: pl.BlockSpec((None, 1, d), lambda bi, i: (bi, 0, j))
    tab = pl.BlockSpec((tm, LANES), lambda bi, i: (i, 0))
    out = lambda w, dt: jax.ShapeDtypeStruct((b, s, w), dt)
    ospec = lambda w: pl.BlockSpec((None, tm, w), lambda bi, i: (bi, i, 0))
    return pl.pallas_call(
        _inproj_kernel,
        out_shape=(out(512, BF16), out(512, BF16), out(512, BF16), out(1024, BF16), out(1024, BF16),
                   out(512, BF16), out(512, F32)),
        grid=(b, s // tm),
        in_specs=[pl.BlockSpec((None, tm, d), lambda bi, i: (bi, i, 0)),
                  _const_spec((1, d)), row(0), row(1),
                  _const_spec(w2.shape), _const_spec(gq.shape), _const_spec(wuq.shape),
                  _const_spec(gkv.shape), _const_spec(wukv.shape), tab, tab, tab],
        out_specs=(ospec(512), ospec(512), ospec(512), ospec(1024), ospec(1024), ospec(512), ospec(512)),
        compiler_params=_params(("arbitrary", "arbitrary")),
        name="in_proj",
    )(x, g_mix, modrows, modrows, w2, gq, wuq, gkv, wukv, cos, sa, sb)


def _attn_kernel(*refs, diff, has_lat, tk, lam_init):
    it = iter(refs)
    q_ref, kc_ref, vc_ref = next(it), next(it), next(it)
    kl_ref = vl_ref = lam_ref = g_ref = None
    if has_lat:
        kl_ref, vl_ref = next(it), next(it)
    if diff:
        lam_ref, g_ref = next(it), next(it)
    o_ref = next(it)

    q = q_ref[...]
    tq = q.shape[0]
    if diff:
        lane = lax.broadcasted_iota(jnp.int32, q.shape, 1)
        zero = jnp.zeros_like(q)
        qs = jnp.concatenate([jnp.where(lane < DA_QK, q, zero), jnp.where(lane >= DA_QK, q, zero)], axis=0)
    else:
        qs = q
    rows = qs.shape[0]
    dv = vc_ref.shape[-1]

    def step(k, v, carry):
        m, l, acc = carry
        s = lax.dot_general(qs, k, (((1,), (1,)), ((), ())), preferred_element_type=F32)
        m_new = jnp.maximum(m, jnp.max(s, axis=-1, keepdims=True))
        alpha = jnp.exp(m - m_new)
        p = jnp.exp(s - m_new)
        l = alpha * l + jnp.sum(p, axis=-1, keepdims=True)
        acc = alpha * acc + jnp.dot(p.astype(BF16), v, preferred_element_type=F32)
        return m_new, l, acc

    carry = (jnp.full((rows, 1), NEG_BIG, F32), jnp.zeros((rows, 1), F32), jnp.zeros((rows, dv), F32))
    carry = step(kc_ref[...], vc_ref[...], carry)
    if has_lat:
        def body(i, c):
            off = pl.multiple_of(i * tk, tk)
            return step(kl_ref[pl.ds(off, tk), :], vl_ref[pl.ds(off, tk), :], c)
        carry = lax.fori_loop(0, kl_ref.shape[0] // tk, body, carry)
    _, l, acc = carry
    o = acc / l
    if diff:
        lv = lam_ref[...]
        a = jnp.sum(lv[0:1, :] * lv[1:2, :], axis=-1, keepdims=True)
        b = jnp.sum(lv[2:3, :] * lv[3:4, :], axis=-1, keepdims=True)
        lam = jnp.exp(a) - jnp.exp(b) + lam_init
        o = o[:tq] - lam * o[tq:]
        o = _rms(o, g_ref[...]) * (1.0 - lam_init)
    o_ref[...] = o.astype(o_ref.dtype)


def _attention(q, kc, vc, kl, vl, heads, dq, dv, tq, tk, diff=False, lam=None, g=None, lam_init=0.0,
               name="attn"):
    b, s, _ = q.shape
    c = kc.shape[1]
    tq = min(tq, s)
    has_lat = kl is not None
    args = [q, kc, vc]
    specs = [pl.BlockSpec((None, tq, dq), lambda bi, hi, qi: (bi, qi, hi)),
             pl.BlockSpec((None, c, dq), lambda bi, hi, qi: (bi, 0, hi)),
             pl.BlockSpec((None, c, dv), lambda bi, hi, qi: (bi, 0, hi))]
    if has_lat:
        sl = kl.shape[1]
        tk = min(tk, sl)
        args += [kl, vl]
        specs += [pl.BlockSpec((None, sl, dq), lambda bi, hi, qi: (bi, 0, hi)),
                  pl.BlockSpec((None, sl, dv), lambda bi, hi, qi: (bi, 0, hi))]
    if diff:
        args += [lam, g]
        specs += [_const_spec(lam.shape), _const_spec(g.shape)]
    kern = functools.partial(_attn_kernel, diff=diff, has_lat=has_lat, tk=tk, lam_init=lam_init)
    return pl.pallas_call(
        kern,
        out_shape=jax.ShapeDtypeStruct((b, s, heads * dv), BF16),
        grid=(b, heads, s // tq),
        in_specs=specs,
        out_specs=pl.BlockSpec((None, tq, dv), lambda bi, hi, qi: (bi, qi, hi)),
        compiler_params=_params(("arbitrary", "arbitrary", "arbitrary")),
        name=name,
    )(*args)


def _pool_kernel(prev_ref, cur_ref, next_ref, w_ref, sc_ref, o_ref, *, seq_len):
    i = pl.program_id(1)
    n = pl.num_programs(1)
    cur = cur_ref[...]
    tm = cur.shape[0]
    prev = jnp.where(i > 0, prev_ref[...], 0.0)
    nxt = jnp.where(i < n - 1, next_ref[...], 0.0)
    ext = jnp.concatenate([prev, cur, nxt], axis=0)
    ne = tm + 16
    t = i * tm + lax.broadcasted_iota(jnp.int32, (tm, 1), 0)
    for g, w in enumerate(POOL_WINDOWS):
        e = ext[:, g * LANES:(g + 1) * LANES]
        acc = e + pltpu.roll(e, 1, 0)
        half = 1
        while 2 * half < w:
            acc = pltpu.roll(acc, half, 0) + pltpu.roll(acc, ne - half, 0)
            half *= 2
        win = acc[8:8 + tm]
        lo = jnp.clip(t - w // 2, 0, seq_len)
        hi = jnp.clip(t - w // 2 + w, 0, seq_len)
        cnt = (hi - lo).astype(F32)
        dlt = win / cnt - cur[:, g * LANES:(g + 1) * LANES]
        y = jnp.dot(dlt.astype(BF16), w_ref[g], preferred_element_type=F32)
        o_ref[:, g * LANES:(g + 1) * LANES] = (y * sc_ref[:, g * LANES:(g + 1) * LANES]).astype(BF16)


def _pool(u, pool_w, pool_scale, tm):
    b, s, w = u.shape
    tm = min(tm, s)
    nb8 = s // 8
    r8 = tm // 8
    return pl.pallas_call(
        functools.partial(_pool_kernel, seq_len=s),
        out_shape=jax.ShapeDtypeStruct((b, s, w), BF16),
        grid=(b, s // tm),
        in_specs=[pl.BlockSpec((None, 8, w), lambda bi, i: (bi, jnp.maximum(i * r8 - 1, 0), 0)),
                  pl.BlockSpec((None, tm, w), lambda bi, i: (bi, i, 0)),
                  pl.BlockSpec((None, 8, w), lambda bi, i: (bi, jnp.minimum((i + 1) * r8, nb8 - 1), 0)),
                  _const_spec(pool_w.shape), _const_spec(pool_scale.shape)],
        out_specs=pl.BlockSpec((None, tm, w), lambda bi, i: (bi, i, 0)),
        compiler_params=_params(("arbitrary", "arbitrary")),
        name="pool_mixer",
    )(u, u, u, pool_w, pool_scale)


def _merge_kernel(*refs, moe, n_exp):
    (x_ref, gmix_ref, sh1_ref, sc1_ref, wgate_ref, oda_ref, omla_ref, opool_ref, wbr_ref, wout_ref,
     gt1_ref, gffn_ref, sh2_ref, sc2_ref) = refs[:14]
    if moe:
        router_ref, xo_ref, h2_ref, gates_ref = refs[14:]
    else:
        xo_ref, h2_ref = refs[14:]
    x = x_ref[...]
    d = x.shape[-1]
    h = _norm_mod(x, gmix_ref[...], sh1_ref[...], sc1_ref[...]).astype(BF16)
    merged = None
    for n, o_ref in enumerate((oda_ref, omla_ref, opool_ref)):
        gate = _sigmoid(jnp.dot(h, wgate_ref[:, n * d:(n + 1) * d], preferred_element_type=F32))
        proj = jnp.dot(o_ref[...], wbr_ref[n], preferred_element_type=F32)
        merged = gate * proj if merged is None else merged + gate * proj
    mix = jnp.dot(merged.astype(BF16), wout_ref[...], preferred_element_type=F32)
    xn = x + gt1_ref[...] * mix
    xo_ref[...] = xn
    h2 = _norm_mod(xn, gffn_ref[...], sh2_ref[...], sc2_ref[...])
    h2_ref[...] = h2.astype(BF16)
    if moe:
        logits = jnp.dot(h2, router_ref[...], preferred_element_type=F32, precision=lax.Precision.HIGHEST)
        lane = lax.broadcasted_iota(jnp.int32, logits.shape, 1)
        logits = jnp.where(lane < n_exp, logits, NEG_BIG)
        v1 = jnp.max(logits, axis=-1, keepdims=True)
        i1 = jnp.min(jnp.where(logits == v1, lane, n_exp), axis=-1, keepdims=True)
        rest = jnp.where(lane == i1, NEG_BIG, logits)
        v2 = jnp.max(rest, axis=-1, keepdims=True)
        i2 = jnp.min(jnp.where(rest == v2, lane, n_exp), axis=-1, keepdims=True)
        w1 = 1.0 / (1.0 + jnp.exp(v2 - v1))
        gates_ref[...] = jnp.where(lane == i1, w1, 0.0) + jnp.where(lane == i2, 1.0 - w1, 0.0)


def _merge(x, modrows, g_mix, wgate, o_da, o_mla, o_pool, wbr, wout, g_ffn, router, n_exp, tm):
    b, s, d = x.shape
    tm = min(tm, s)
    moe = router is not None
    row = lambda j: pl.BlockSpec((None, 1, d), lambda bi, i: (bi, 0, j))
    tile = lambda w: pl.BlockSpec((None, tm, w), lambda bi, i: (bi, i, 0))
    args = [x, g_mix, modrows, modrows, wgate, o_da, o_mla, o_pool, wbr, wout, modrows, g_ffn, modrows, modrows]
    specs = [tile(d), _const_spec((1, d)), row(0), row(1), _const_spec(wgate.shape),
             tile(BRANCH_W), tile(BRANCH_W), tile(BRANCH_W), _const_spec(wbr.shape), _const_spec(wout.shape),
             row(2), _const_spec((1, d)), row(3), row(4)]
    out_shape = [jax.ShapeDtypeStruct((b, s, d), F32), jax.ShapeDtypeStruct((b, s, d), BF16)]
    out_specs = [tile(d), tile(d)]
    if moe:
        args.append(router)
        specs.append(_const_spec(router.shape))
        out_shape.append(jax.ShapeDtypeStruct((b, s, LANES), F32))
        out_specs.append(tile(LANES))
    return pl.pallas_call(
        functools.partial(_merge_kernel, moe=moe, n_exp=n_exp),
        out_shape=tuple(out_shape),
        grid=(b, s // tm),
        in_specs=specs,
        out_specs=tuple(out_specs),
        compiler_params=_params(("arbitrary", "arbitrary")),
        name="merge_out",
    )(*args)


def _ffn_kernel(h_ref, x_ref, gt_ref, wg_ref, wu_ref, wd_ref, gfin_ref, o_ref, *, final):
    h = h_ref[...]
    gte = jnp.dot(h, wg_ref[...], preferred_element_type=F32)
    up = jnp.dot(h, wu_ref[...], preferred_element_type=F32)
    act = (gte * _sigmoid(gte) * up).astype(BF16)
    y = jnp.dot(act, wd_ref[...], preferred_element_type=F32)
    xo = x_ref[...] + gt_ref[...] * y
    if final:
        xo = _rms(xo, gfin_ref[...])
    o_ref[...] = xo


def _ffn(h2, x, modrows, wg, wu, wd, g_final, final, tm):
    b, s, d = x.shape
    tm = min(tm, s)
    tile = pl.BlockSpec((None, tm, d), lambda bi, i: (bi, i, 0))
    return pl.pallas_call(
        functools.partial(_ffn_kernel, final=final),
        out_shape=jax.ShapeDtypeStruct((b, s, d), F32),
        grid=(b, s // tm),
        in_specs=[tile, tile, pl.BlockSpec((None, 1, d), lambda bi, i: (bi, 0, 5)),
                  _const_spec(wg.shape), _const_spec(wu.shape), _const_spec(wd.shape), _const_spec((1, d))],
        out_specs=tile,
        compiler_params=_params(("arbitrary", "arbitrary")),
        name="ffn_swiglu",
    )(h2, x, modrows, wg, wu, wd, g_final)


def _moe_kernel(h_ref, x_ref, gt_ref, gates_ref, wg_ref, wu_ref, wd_ref, gfin_ref, o_ref, acc_ref, *, final):
    e = pl.program_id(2)
    f = pl.program_id(3)
    last = jnp.logical_and(e == pl.num_programs(2) - 1, f == pl.num_programs(3) - 1)

    @pl.when(jnp.logical_and(e == 0, f == 0))
    def _():
        acc_ref[...] = jnp.zeros_like(acc_ref)

    h = h_ref[...]
    gates = gates_ref[...]
    lane = lax.broadcasted_iota(jnp.int32, gates.shape, 1)
    ge = jnp.sum(jnp.where(lane == e, gates, 0.0), axis=-1, keepdims=True)
    gte = jnp.dot(h, wg_ref[...], preferred_element_type=F32)
    up = jnp.dot(h, wu_ref[...], preferred_element_type=F32)
    act = (gte * _sigmoid(gte) * up).astype(BF16)
    acc_ref[...] += ge * jnp.dot(act, wd_ref[...], preferred_element_type=F32)

    @pl.when(last)
    def _():
        xo = x_ref[...] + gt_ref[...] * acc_ref[...]
        if final:
            xo = _rms(xo, gfin_ref[...])
        o_ref[...] = xo


def _moe_dense(h2, x, modrows, gates, wg, wu, wd, g_final, final, tm, tf):
    b, s, d = x.shape
    n_exp, _, dff = wg.shape
    tm = min(tm, s)
    tile = lambda w: pl.BlockSpec((None, tm, w), lambda bi, i, e, f: (bi, i, 0))
    return pl.pallas_call(
        functools.partial(_moe_kernel, final=final),
        out_shape=jax.ShapeDtypeStruct((b, s, d), F32),
        grid=(b, s // tm, n_exp, dff // tf),
        in_specs=[tile(d), tile(d), pl.BlockSpec((None, 1, d), lambda bi, i, e, f: (bi, 0, 5)),
                  tile(gates.shape[-1]),
                  pl.BlockSpec((None, d, tf), lambda bi, i, e, f: (e, 0, f)),
                  pl.BlockSpec((None, d, tf), lambda bi, i, e, f: (e, 0, f)),
                  pl.BlockSpec((None, tf, d), lambda bi, i, e, f: (e, f, 0)),
                  pl.BlockSpec((1, d), lambda bi, i, e, f: (0, 0))],
        out_specs=tile(d),
        scratch_shapes=[pltpu.VMEM((tm, d), F32)],
        compiler_params=_params(("arbitrary",) * 4),
        name="moe_swiglu",
    )(h2, x, modrows, gates, wg, wu, wd, g_final)


def _rope_tables(seq):
    axis_dim = DA_QK // 2
    n_freq = axis_dim // 2
    inv = jnp.exp(-math.log(ROPE_BASE) * jnp.arange(n_freq, dtype=F32) * (2.0 / axis_dim))
    t = jnp.arange(seq, dtype=jnp.int32)
    ar = (t // GRID_W).astype(F32)[:, None] * inv
    ac = (t % GRID_W).astype(F32)[:, None] * inv
    cos = jnp.concatenate([jnp.cos(ar), jnp.cos(ar), jnp.cos(ac), jnp.cos(ac)], axis=-1)
    sin = jnp.concatenate([jnp.sin(ar), jnp.sin(ar), jnp.sin(ac), jnp.sin(ac)], axis=-1)
    first = (jnp.arange(DA_QK) % axis_dim) < n_freq
    sa = jnp.where(first, -sin, 0.0)
    sb = jnp.where(first, 0.0, sin)
    rep = lambda a: jnp.tile(a, (1, LANES // DA_QK))
    return rep(cos), rep(sa), rep(sb)


def _identity_tables(seq):
    return jnp.ones((seq, LANES), F32), jnp.zeros((seq, LANES), F32), jnp.zeros((seq, LANES), F32)


def _pack_layer_weights(w_in, w_uq, w_ukv):
    d = w_in.shape[0]
    zpad = jnp.zeros((d, LANES - MLA_ROPE), w_in.dtype)
    o_kr = 3 * 512 + MLA_Q_RANK + MLA_KV_RANK
    o_pool = o_kr + MLA_ROPE
    w2 = jnp.concatenate([w_in[:, :o_kr + MLA_ROPE], zpad, w_in[:, o_pool:o_pool + 512]], axis=1).astype(BF16)
    wgate = w_in[:, o_pool + 512:].astype(BF16)
    hq = MLA_NOPE + MLA_ROPE
    wq = w_uq.reshape(MLA_Q_RANK, MLA_HEADS, hq)
    wq = jnp.concatenate([wq, jnp.zeros((MLA_Q_RANK, MLA_HEADS, MLA_QK_PAD - hq), w_uq.dtype)], axis=-1)
    wuq = wq.reshape(MLA_Q_RANK, MLA_HEADS * MLA_QK_PAD).astype(BF16)
    wkv = w_ukv.reshape(MLA_KV_RANK, MLA_HEADS, MLA_NOPE + MLA_V)
    wukv = jnp.concatenate([wkv[:, :, :MLA_NOPE].reshape(MLA_KV_RANK, -1),
                            wkv[:, :, MLA_NOPE:].reshape(MLA_KV_RANK, -1)], axis=1).astype(BF16)
    return w2, wgate, wuq, wukv


def kernel(x, c, ctx, c_ctx, w_mod, b_mod, g_mix, w_in, da_lambda, da_subln, mla_gq, w_uq, mla_gkv, w_ukv,
           pool_w, pool_scale, w_branch, w_out, g_ffn, ffn_w_gate, ffn_w_up, ffn_w_down, moe_router,
           moe_w_gate, moe_w_up, moe_w_down, g_final):
    bsz, seq, d = x.shape
    n_ctx = ctx.shape[1]
    depth = w_mod.shape[0]

    cond8 = jnp.zeros((8, d), F32).at[:bsz].set(c).at[bsz].set(c_ctx)
    mod = _mod_rows(cond8, w_mod, b_mod)
    lat_tabs = _rope_tables(seq)
    ctx_tabs = _identity_tables(n_ctx)
    g_fin = g_final.reshape(1, d)

    xc = ctx
    for l in range(depth):
        need_ctx = l < depth - 1
        lam_init = 0.8 - 0.6 * math.exp(-0.3 * l)
        mod_lat = mod[l, :bsz][:, None, :]
        mod_ctx = jnp.broadcast_to(mod[l, bsz][None, None, :], (bsz, 1, N_MOD * d))
        w2, wgate, wuq, wukv = _pack_layer_weights(w_in[l], w_uq[l], w_ukv[l])
        gmix = g_mix[l].reshape(1, d)
        gffn = g_ffn[l].reshape(1, d)
        gq = mla_gq[l].reshape(1, -1)
        gkv = mla_gkv[l].reshape(1, -1)
        subln = da_subln[l].reshape(1, -1)
        lam = da_lambda[l]
        pw = pool_w[l].astype(BF16)
        psc = pool_scale[l].reshape(1, -1)
        wbr = w_branch[l].astype(BF16)
        wout = w_out[l].astype(BF16)
        j = l // 2
        dense = l % 2 == 0
        final = l == depth - 1

        qda, kda, vda, qm, km, vm, pin = _inproj(x, mod_lat, gmix, w2, gq, wuq, gkv, wukv, lat_tabs, 256)
        cqda, ckda, cvda, cqm, ckm, cvm, cpin = _inproj(xc, mod_ctx, gmix, w2, gq, wuq, gkv, wukv, ctx_tabs, 256)

        o_da = _attention(qda, ckda, cvda, kda, vda, DA_HEADS, LANES, DA_V, 256, 512, diff=True,
                          lam=lam, g=subln, lam_init=lam_init, name="diff_attn")
        o_mla = _attention(qm, ckm, cvm, km, vm, MLA_HEADS, MLA_QK_PAD, MLA_V, 512, 512, name="mla_attn")
        o_pool = _pool(pin, pw, psc, 512)

        router = None
        n_exp = moe_router.shape[-1]
        if not dense:
            router = jnp.zeros((d, LANES), F32).at[:, :n_exp].set(moe_router[j])
        outs = _merge(x, mod_lat, gmix, wgate, o_da, o_mla, o_pool, wbr, wout, gffn, router, n_exp, 256)

        if dense:
            wg = ffn_w_gate[j].astype(BF16)
            wu = ffn_w_up[j].astype(BF16)
            wd = ffn_w_down[j].astype(BF16)
            x = _ffn(outs[1], outs[0], mod_lat, wg, wu, wd, g_fin, final, 256)
        else:
            wg = moe_w_gate[j].astype(BF16)
            wu = moe_w_up[j].astype(BF16)
            wd = moe_w_down[j].astype(BF16)
            x = _moe_dense(outs[1], outs[0], mod_lat, outs[2], wg, wu, wd, g_fin, final, 512, 1792)

        if need_ctx:
            co_da = _attention(cqda, ckda, cvda, None, None, DA_HEADS, LANES, DA_V, 256, 512, diff=True,
                               lam=lam, g=subln, lam_init=lam_init, name="diff_attn_ctx")
            co_mla = _attention(cqm, ckm, cvm, None, None, MLA_HEADS, MLA_QK_PAD, MLA_V, 256, 512,
                                name="mla_attn_ctx")
            co_pool = _pool(cpin, pw, psc, 512)
            couts = _merge(xc, mod_ctx, gmix, wgate, co_da, co_mla, co_pool, wbr, wout, gffn, router, n_exp, 256)
            if dense:
                xc = _ffn(couts[1], couts[0], mod_ctx, wg, wu, wd, g_fin, False, 256)
            else:
                xc = _moe_dense(couts[1], couts[0], mod_ctx, couts[2], wg, wu, wd, g_fin, False, 512, 1792)
    return x
```

```python
import functools
import math

import jax
import jax.numpy as jnp
from jax import lax
from jax.experimental import pallas as pl
from jax.experimental.pallas import tpu as pltpu

F32 = jnp.float32
BF16 = jnp.bfloat16

GRID_W = 64
DA_HEADS = 4
DA_QK = 64
DA_V = 128
MLA_HEADS = 4
MLA_NOPE = 128
MLA_ROPE = 64
MLA_V = 128
MLA_Q_RANK = 384
MLA_KV_RANK = 256
POOL_WINDOWS = (2, 4, 8, 16)
POOL_GROUP_W = 128
N_BRANCH = 3
BRANCH_W = 512
ROPE_BASE = 10000.0
EPS = 1e-6
N_MOD = 6
LANES = 128
MLA_QK_PAD = 256
LOG2E = math.log2(math.e)
DA_SCALE = DA_QK ** -0.5 * LOG2E
MLA_SCALE = (MLA_NOPE + MLA_ROPE) ** -0.5 * LOG2E
NEG_BIG = -1e30
ATTN_UNROLL = 16
VMEM_LIMIT = 56 * 1024 * 1024

C_DAQ, C_DAK, C_DAV = 0, 512, 1024
C_QD = 1536
C_KVD = C_QD + MLA_Q_RANK
C_KR = C_KVD + MLA_KV_RANK
C_POOL = C_KR + LANES
W2_COLS = C_POOL + 512


def _sigmoid(v):
    return 1.0 / (1.0 + jnp.exp(-v))


def _params(sem, vmem=VMEM_LIMIT):
    return pltpu.CompilerParams(dimension_semantics=sem, vmem_limit_bytes=vmem)


def _const_spec(shape):
    nd = len(shape)
    return pl.BlockSpec(shape, lambda *_: (0,) * nd)


def _mod_kernel(cond_ref, w_ref, b_ref, o_ref):
    c = cond_ref[...]
    s = c * _sigmoid(c)
    o_ref[...] = jnp.dot(s, w_ref[...], preferred_element_type=F32,
                         precision=lax.Precision.HIGHEST) + b_ref[...]


def _mod_rows(cond8, w_mod, b_mod):
    depth, d, n = w_mod.shape
    tn = 1536
    return pl.pallas_call(
        _mod_kernel,
        out_shape=jax.ShapeDtypeStruct((depth, 8, n), F32),
        grid=(depth, n // tn),
        in_specs=[pl.BlockSpec((8, d), lambda l, j: (0, 0)),
                  pl.BlockSpec((None, d, tn), lambda l, j: (l, 0, j)),
                  pl.BlockSpec((None, 1, tn), lambda l, j: (l, 0, j))],
        out_specs=pl.BlockSpec((None, 8, tn), lambda l, j: (l, 0, j)),
        compiler_params=_params(("arbitrary", "arbitrary")),
        name="adaln_rows",
    )(cond8, w_mod, b_mod.reshape(depth, 1, n))


def _norm_mod(x, g, sh, sc):
    r = lax.rsqrt(jnp.mean(x * x, axis=-1, keepdims=True) + EPS)
    return (x * r * g) * (1.0 + sc) + sh


def _rms(v, g):
    return v * lax.rsqrt(jnp.mean(v * v, axis=-1, keepdims=True) + EPS) * g


def _inproj_kernel(x_ref, g_ref, sh_ref, sc_ref, w_ref, gq_ref, wuq_ref, gkv_ref, wukv_ref,
                   cos_ref, sa_ref, sb_ref,
                   qda_ref, kda_ref, vda_ref, qm_ref, km_ref, vm_ref, pool_ref):
    h = _norm_mod(x_ref[...], g_ref[...], sh_ref[...], sc_ref[...])
    z = jnp.dot(h.astype(BF16), w_ref[...], preferred_element_type=F32)
    cos = cos_ref[...]
    sa = sa_ref[...]
    sb = sb_ref[...]

    def rope(blk):
        return blk * cos + pltpu.roll(blk, LANES - 16, 1) * sa + pltpu.roll(blk, 16, 1) * sb

    lo = lax.broadcasted_iota(jnp.int32, cos.shape, 1) < MLA_ROPE

    for hh in range(DA_HEADS):
        c0 = hh * LANES
        qda_ref[:, c0:c0 + LANES] = (rope(z[:, C_DAQ + c0:C_DAQ + c0 + LANES]) * DA_SCALE).astype(BF16)
        kda_ref[:, c0:c0 + LANES] = rope(z[:, C_DAK + c0:C_DAK + c0 + LANES]).astype(BF16)

    qn = _rms(z[:, C_QD:C_QD + MLA_Q_RANK], gq_ref[...])
    qf = jnp.dot(qn.astype(BF16), wuq_ref[...], preferred_element_type=F32)
    kvn = _rms(z[:, C_KVD:C_KVD + MLA_KV_RANK], gkv_ref[...])
    kvf = jnp.dot(kvn.astype(BF16), wukv_ref[...], preferred_element_type=F32)
    kr = jnp.where(lo, rope(z[:, C_KR:C_KR + LANES]), 0.0).astype(BF16)
    for hh in range(MLA_HEADS):
        c0 = hh * MLA_QK_PAD
        qm_ref[:, c0:c0 + LANES] = (qf[:, c0:c0 + LANES] * MLA_SCALE).astype(BF16)
        qr = jnp.where(lo, rope(qf[:, c0 + LANES:c0 + 2 * LANES]), 0.0)
        qm_ref[:, c0 + LANES:c0 + 2 * LANES] = (qr * MLA_SCALE).astype(BF16)
        km_ref[:, c0:c0 + LANES] = kvf[:, hh * LANES:(hh + 1) * LANES].astype(BF16)
        km_ref[:, c0 + LANES:c0 + 2 * LANES] = kr
    ones = jnp.ones(cos.shape, BF16)
    for hh in range(MLA_HEADS):
        c0 = 2 * hh * LANES
        vda_ref[:, c0:c0 + LANES] = z[:, C_DAV + hh * LANES:C_DAV + (hh + 1) * LANES].astype(BF16)
        vda_ref[:, c0 + LANES:c0 + 2 * LANES] = ones
        vm_ref[:, c0:c0 + LANES] = kvf[:, 512 + hh * LANES:512 + (hh + 1) * LANES].astype(BF16)
        vm_ref[:, c0 + LANES:c0 + 2 * LANES] = ones
    pool_ref[...] = z[:, C_POOL:C_POOL + 512]


def _inproj(x, modrows, g_mix, w2, gq, wuq, gkv, wukv, tabs, tm):
    b, s, d = x.shape
    tm = min(tm, s)
    cos, sa, sb = tabs
    row = lambda j: pl.BlockSpec((None, 1, d), lambda bi, i: (bi, 0, j))
    tab = pl.BlockSpec((tm, LANES), lambda bi, i: (i, 0))
    out = lambda w, dt: jax.ShapeDtypeStruct((b, s, w), dt)
    ospec = lambda w: pl.BlockSpec((None, tm, w), lambda bi, i: (bi, i, 0))
    return pl.pallas_call(
        _inproj_kernel,
        out_shape=(out(512, BF16), out(512, BF16), out(1024, BF16), out(1024, BF16), out(1024, BF16),
                   out(1024, BF16), out(512, F32)),
        grid=(b, s // tm),
        in_specs=[pl.BlockSpec((None, tm, d), lambda bi, i: (bi, i, 0)),
                  _const_spec((1, d)), row(0), row(1),
                  _const_spec(w2.shape), _const_spec(gq.shape), _const_spec(wuq.shape),
                  _const_spec(gkv.shape), _const_spec(wukv.shape), tab, tab, tab],
        out_specs=(ospec(512), ospec(512), ospec(1024), ospec(1024), ospec(1024), ospec(1024), ospec(512)),
        compiler_params=_params(("arbitrary", "arbitrary")),
        name="in_proj",
    )(x, g_mix, modrows, modrows, w2, gq, wuq, gkv, wukv, cos, sa, sb)


def _attn_kernel(*refs, diff, has_lat, tk, lam_init, unroll):
    it = iter(refs)
    q_ref, kc_ref, vc_ref = next(it), next(it), next(it)
    kl_ref = vl_ref = lam_ref = g_ref = None
    if has_lat:
        kl_ref, vl_ref = next(it), next(it)
    if diff:
        lam_ref, g_ref = next(it), next(it)
    o_ref = next(it)
    m_ref, acc_ref = next(it), next(it)
    s_refs = (next(it), next(it))
    p_refs = (next(it), next(it))
    al_refs = (next(it), next(it))

    q = q_ref[...]
    tq = q.shape[0]
    if diff:
        lane = lax.broadcasted_iota(jnp.int32, q.shape, 1)
        zero = jnp.zeros_like(q)
        qs = jnp.concatenate([jnp.where(lane < DA_QK, q, zero), jnp.where(lane >= DA_QK, q, zero)], axis=0)
    else:
        qs = q
    rows = qs.shape[0]
    dv = o_ref.shape[-1]

    def scores(k):
        return lax.dot_general(qs, k, (((1,), (1,)), ((), ())), preferred_element_type=F32)

    def qk(slot, k):
        s_refs[slot][:, :k.shape[0]] = scores(k)

    def sm(slot, w):
        s_ref, p_ref = s_refs[slot], p_refs[slot]
        blocks = [slice(j * LANES, (j + 1) * LANES) for j in range(w // LANES)]
        mx = s_ref[:, blocks[0]]
        for blk in blocks[1:]:
            mx = jnp.maximum(mx, s_ref[:, blk])
        m = m_ref[...]
        m_new = jnp.maximum(m, jnp.max(mx, axis=-1, keepdims=True))
        al_refs[slot][...] = jnp.exp2(m - m_new)
        m_ref[...] = m_new
        for blk in blocks:
            p_ref[:, blk] = jnp.exp2(s_ref[:, blk] - m_new).astype(BF16)

    def pv(slot, v):
        pblk = p_refs[slot][:, :v.shape[0]]
        new = jnp.dot(pblk, v, preferred_element_type=F32)
        al = al_refs[slot][...]
        for blk in (slice(0, dv), slice(dv, 2 * dv)):
            acc_ref[:, blk] = al * acc_ref[:, blk] + new[:, blk]

    n_ctx = kc_ref.shape[0]
    n_lat = kl_ref.shape[0] // tk if has_lat else 0
    width = lambda c: n_ctx if c == 0 else tk
    k_of = lambda c: kc_ref[...] if c == 0 else kl_ref[(c - 1) * tk:c * tk, :]
    v_of = lambda c: vc_ref[...] if c == 0 else vl_ref[(c - 1) * tk:c * tk, :]
    lat = lambda ref, i: ref[pl.ds(pl.multiple_of(i * tk, tk), tk), :]

    def static_step(t):
        if t <= n_lat:
            qk(t % 2, k_of(t))
        if 0 <= t - 2 <= n_lat:
            pv(t % 2, v_of(t - 2))
        if 0 <= t - 1 <= n_lat:
            sm((t - 1) % 2, width(t - 1))

    m_ref[...] = jnp.full(m_ref.shape, NEG_BIG, F32)
    acc_ref[...] = jnp.zeros(acc_ref.shape, F32)
    t0 = min(3 + (n_lat - 2) % unroll, n_lat + 1) if n_lat >= 2 else n_lat + 1
    for t in range(t0):
        static_step(t)
    if t0 <= n_lat:

        def body(j, carry):
            for u in range(unroll):
                t = t0 + unroll * j + u
                slot = (t0 + u) % 2
                qk(slot, lat(kl_ref, t - 1))
                pv(slot, lat(vl_ref, t - 3))
                sm(1 - slot, tk)
            return carry

        lax.fori_loop(0, (n_lat - t0 + 1) // unroll, body, 0)
    for t in range(n_lat + 1, n_lat + 3):
        static_step(t)
    acc = acc_ref[...]
    o = acc[:, :dv] / acc[:, dv:]
    if diff:
        lv = lam_ref[...]
        a = jnp.sum(lv[0:1, :] * lv[1:2, :], axis=-1, keepdims=True)
        b = jnp.sum(lv[2:3, :] * lv[3:4, :], axis=-1, keepdims=True)
        lam = jnp.exp(a) - jnp.exp(b) + lam_init
        o = o[:tq] - lam * o[tq:]
        o = _rms(o, g_ref[...]) * (1.0 - lam_init)
    o_ref[...] = o.astype(o_ref.dtype)


def _attention(q, kc, vc, kl, vl, heads, dq, dv, tq, tk, diff=False, lam=None, g=None, lam_init=0.0,
               name="attn"):
    b, s, _ = q.shape
    c = kc.shape[1]
    tq = min(tq, s)
    has_lat = kl is not None
    args = [q, kc, vc]
    specs = [pl.BlockSpec((None, tq, dq), lambda bi, hi, qi: (bi, qi, hi)),
             pl.BlockSpec((None, c, dq), lambda bi, hi, qi: (bi, 0, hi)),
             pl.BlockSpec((None, c, 2 * dv), lambda bi, hi, qi: (bi, 0, hi))]
    if has_lat:
        sl = kl.shape[1]
        tk = min(tk, sl)
        args += [kl, vl]
        specs += [pl.BlockSpec((None, sl, dq), lambda bi, hi, qi: (bi, 0, hi)),
                  pl.BlockSpec((None, sl, 2 * dv), lambda bi, hi, qi: (bi, 0, hi))]
    if diff:
        args += [lam, g]
        specs += [_const_spec(lam.shape), _const_spec(g.shape)]
    kern = functools.partial(_attn_kernel, diff=diff, has_lat=has_lat, tk=tk, lam_init=lam_init,
                             unroll=ATTN_UNROLL)
    rows = 2 * tq if diff else tq
    wmax = max(tk, c) if has_lat else c
    scratch = [pltpu.VMEM((rows, LANES), F32), pltpu.VMEM((rows, 2 * dv), F32),
               pltpu.VMEM((rows, wmax), F32), pltpu.VMEM((rows, wmax), F32),
               pltpu.VMEM((rows, wmax), BF16), pltpu.VMEM((rows, wmax), BF16),
               pltpu.VMEM((rows, LANES), F32), pltpu.VMEM((rows, LANES), F32)]
    return pl.pallas_call(
        kern,
        out_shape=jax.ShapeDtypeStruct((b, s, heads * dv), BF16),
        grid=(b, heads, s // tq),
        in_specs=specs,
        out_specs=pl.BlockSpec((None, tq, dv), lambda bi, hi, qi: (bi, qi, hi)),
        scratch_shapes=scratch,
        compiler_params=_params(("arbitrary", "arbitrary", "arbitrary")),
        name=name,
    )(*args)


def _pool_kernel(prev_ref, cur_ref, next_ref, w_ref, sc_ref, o_ref, *, seq_len):
    i = pl.program_id(1)
    n = pl.num_programs(1)
    cur = cur_ref[...]
    tm = cur.shape[0]
    prev = jnp.where(i > 0, prev_ref[...], 0.0)
    nxt = jnp.where(i < n - 1, next_ref[...], 0.0)
    ext = jnp.concatenate([prev, cur, nxt], axis=0)
    ne = tm + 16
    t = i * tm + lax.broadcasted_iota(jnp.int32, (tm, 1), 0)
    for g, w in enumerate(POOL_WINDOWS):
        e = ext[:, g * LANES:(g + 1) * LANES]
        acc = e + pltpu.roll(e, 1, 0)
        half = 1
        while 2 * half < w:
            acc = pltpu.roll(acc, half, 0) + pltpu.roll(acc, ne - half, 0)
            half *= 2
        win = acc[8:8 + tm]
        lo = jnp.clip(t - w // 2, 0, seq_len)
        hi = jnp.clip(t - w // 2 + w, 0, seq_len)
        cnt = (hi - lo).astype(F32)
        dlt = win / cnt - cur[:, g * LANES:(g + 1) * LANES]
        y = jnp.dot(dlt.astype(BF16), w_ref[g], preferred_element_type=F32)
        o_ref[:, g * LANES:(g + 1) * LANES] = (y * sc_ref[:, g * LANES:(g + 1) * LANES]).astype(BF16)


def _pool(u, pool_w, pool_scale, tm):
    b, s, w = u.shape
    tm = min(tm, s)
    nb8 = s // 8
    r8 = tm // 8
    return pl.pallas_call(
        functools.partial(_pool_kernel, seq_len=s),
        out_shape=jax.ShapeDtypeStruct((b, s, w), BF16),
        grid=(b, s // tm),
        in_specs=[pl.BlockSpec((None, 8, w), lambda bi, i: (bi, jnp.maximum(i * r8 - 1, 0), 0)),
                  pl.BlockSpec((None, tm, w), lambda bi, i: (bi, i, 0)),
                  pl.BlockSpec((None, 8, w), lambda bi, i: (bi, jnp.minimum((i + 1) * r8, nb8 - 1), 0)),
                  _const_spec(pool_w.shape), _const_spec(pool_scale.shape)],
        out_specs=pl.BlockSpec((None, tm, w), lambda bi, i: (bi, i, 0)),
        compiler_params=_params(("arbitrary", "arbitrary")),
        name="pool_mixer",
    )(u, u, u, pool_w, pool_scale)


def _merge_kernel(*refs, moe, n_exp):
    (x_ref, gmix_ref, sh1_ref, sc1_ref, wgate_ref, oda_ref, omla_ref, opool_ref, wbr_ref, wout_ref,
     gt1_ref, gffn_ref, sh2_ref, sc2_ref) = refs[:14]
    if moe:
        router_ref, xo_ref, h2_ref, gates_ref = refs[14:]
    else:
        xo_ref, h2_ref = refs[14:]
    x = x_ref[...]
    d = x.shape[-1]
    h = _norm_mod(x, gmix_ref[...], sh1_ref[...], sc1_ref[...]).astype(BF16)
    merged = None
    for n, o_ref in enumerate((oda_ref, omla_ref, opool_ref)):
        gate = _sigmoid(jnp.dot(h, wgate_ref[:, n * d:(n + 1) * d], preferred_element_type=F32))
        proj = jnp.dot(o_ref[...], wbr_ref[n], preferred_element_type=F32)
        merged = gate * proj if merged is None else merged + gate * proj
    mix = jnp.dot(merged.astype(BF16), wout_ref[...], preferred_element_type=F32)
    xn = x + gt1_ref[...] * mix
    xo_ref[...] = xn
    h2 = _norm_mod(xn, gffn_ref[...], sh2_ref[...], sc2_ref[...])
    h2_ref[...] = h2.astype(BF16)
    if moe:
        logits = jnp.dot(h2, router_ref[...], preferred_element_type=F32, precision=lax.Precision.HIGHEST)
        lane = lax.broadcasted_iota(jnp.int32, logits.shape, 1)
        logits = jnp.where(lane < n_exp, logits, NEG_BIG)
        v1 = jnp.max(logits, axis=-1, keepdims=True)
        i1 = jnp.min(jnp.where(logits == v1, lane, n_exp), axis=-1, keepdims=True)
        rest = jnp.where(lane == i1, NEG_BIG, logits)
        v2 = jnp.max(rest, axis=-1, keepdims=True)
        i2 = jnp.min(jnp.where(rest == v2, lane, n_exp), axis=-1, keepdims=True)
        w1 = 1.0 / (1.0 + jnp.exp(v2 - v1))
        gates_ref[...] = jnp.where(lane == i1, w1, 0.0) + jnp.where(lane == i2, 1.0 - w1, 0.0)


def _merge(x, modrows, g_mix, wgate, o_da, o_mla, o_pool, wbr, wout, g_ffn, router, n_exp, tm):
    b, s, d = x.shape
    tm = min(tm, s)
    moe = router is not None
    row = lambda j: pl.BlockSpec((None, 1, d), lambda bi, i: (bi, 0, j))
    tile = lambda w: pl.BlockSpec((None, tm, w), lambda bi, i: (bi, i, 0))
    args = [x, g_mix, modrows, modrows, wgate, o_da, o_mla, o_pool, wbr, wout, modrows, g_ffn, modrows, modrows]
    specs = [tile(d), _const_spec((1, d)), row(0), row(1), _const_spec(wgate.shape),
             tile(BRANCH_W), tile(BRANCH_W), tile(BRANCH_W), _const_spec(wbr.shape), _const_spec(wout.shape),
             row(2), _const_spec((1, d)), row(3), row(4)]
    out_shape = [jax.ShapeDtypeStruct((b, s, d), F32), jax.ShapeDtypeStruct((b, s, d), BF16)]
    out_specs = [tile(d), tile(d)]
    if moe:
        args.append(router)
        specs.append(_const_spec(router.shape))
        out_shape.append(jax.ShapeDtypeStruct((b, s, LANES), F32))
        out_specs.append(tile(LANES))
    return pl.pallas_call(
        functools.partial(_merge_kernel, moe=moe, n_exp=n_exp),
        out_shape=tuple(out_shape),
        grid=(b, s // tm),
        in_specs=specs,
        out_specs=tuple(out_specs),
        compiler_params=_params(("arbitrary", "arbitrary")),
        name="merge_out",
    )(*args)


def _ffn_kernel(h_ref, x_ref, gt_ref, wg_ref, wu_ref, wd_ref, gfin_ref, o_ref, *, final):
    h = h_ref[...]
    gte = jnp.dot(h, wg_ref[...], preferred_element_type=F32)
    up = jnp.dot(h, wu_ref[...], preferred_element_type=F32)
    act = (gte * _sigmoid(gte) * up).astype(BF16)
    y = jnp.dot(act, wd_ref[...], preferred_element_type=F32)
    xo = x_ref[...] + gt_ref[...] * y
    if final:
        xo = _rms(xo, gfin_ref[...])
    o_ref[...] = xo


def _ffn(h2, x, modrows, wg, wu, wd, g_final, final, tm):
    b, s, d = x.shape
    tm = min(tm, s)
    tile = pl.BlockSpec((None, tm, d), lambda bi, i: (bi, i, 0))
    return pl.pallas_call(
        functools.partial(_ffn_kernel, final=final),
        out_shape=jax.ShapeDtypeStruct((b, s, d), F32),
        grid=(b, s // tm),
        in_specs=[tile, tile, pl.BlockSpec((None, 1, d), lambda bi, i: (bi, 0, 5)),
                  _const_spec(wg.shape), _const_spec(wu.shape), _const_spec(wd.shape), _const_spec((1, d))],
        out_specs=tile,
        compiler_params=_params(("arbitrary", "arbitrary")),
        name="ffn_swiglu",
    )(h2, x, modrows, wg, wu, wd, g_final)


def _moe_kernel(h_ref, x_ref, gt_ref, gates_ref, wg_ref, wu_ref, wd_ref, gfin_ref, o_ref, acc_ref, *, final):
    e = pl.program_id(2)
    f = pl.program_id(3)
    last = jnp.logical_and(e == pl.num_programs(2) - 1, f == pl.num_programs(3) - 1)

    @pl.when(jnp.logical_and(e == 0, f == 0))
    def _():
        acc_ref[...] = jnp.zeros_like(acc_ref)

    h = h_ref[...]
    gates = gates_ref[...]
    lane = lax.broadcasted_iota(jnp.int32, gates.shape, 1)
    ge = jnp.sum(jnp.where(lane == e, gates, 0.0), axis=-1, keepdims=True)
    gte = jnp.dot(h, wg_ref[...], preferred_element_type=F32)
    up = jnp.dot(h, wu_ref[...], preferred_element_type=F32)
    act = (gte * _sigmoid(gte) * up).astype(BF16)
    acc_ref[...] += ge * jnp.dot(act, wd_ref[...], preferred_element_type=F32)

    @pl.when(last)
    def _():
        xo = x_ref[...] + gt_ref[...] * acc_ref[...]
        if final:
            xo = _rms(xo, gfin_ref[...])
        o_ref[...] = xo


def _moe_dense(h2, x, modrows, gates, wg, wu, wd, g_final, final, tm, tf):
    b, s, d = x.shape
    n_exp, _, dff = wg.shape
    tm = min(tm, s)
    tile = lambda w: pl.BlockSpec((None, tm, w), lambda bi, i, e, f: (bi, i, 0))
    return pl.pallas_call(
        functools.partial(_moe_kernel, final=final),
        out_shape=jax.ShapeDtypeStruct((b, s, d), F32),
        grid=(b, s // tm, n_exp, dff // tf),
        in_specs=[tile(d), tile(d), pl.BlockSpec((None, 1, d), lambda bi, i, e, f: (bi, 0, 5)),
                  tile(gates.shape[-1]),
                  pl.BlockSpec((None, d, tf), lambda bi, i, e, f: (e, 0, f)),
                  pl.BlockSpec((None, d, tf), lambda bi, i, e, f: (e, 0, f)),
                  pl.BlockSpec((None, tf, d), lambda bi, i, e, f: (e, f, 0)),
                  pl.BlockSpec((1, d), lambda bi, i, e, f: (0, 0))],
        out_specs=tile(d),
        scratch_shapes=[pltpu.VMEM((tm, d), F32)],
        compiler_params=_params(("arbitrary",) * 4),
        name="moe_swiglu",
    )(h2, x, modrows, gates, wg, wu, wd, g_final)


def _rope_tables(seq):
    axis_dim = DA_QK // 2
    n_freq = axis_dim // 2
    inv = jnp.exp(-math.log(ROPE_BASE) * jnp.arange(n_freq, dtype=F32) * (2.0 / axis_dim))
    t = jnp.arange(seq, dtype=jnp.int32)
    ar = (t // GRID_W).astype(F32)[:, None] * inv
    ac = (t % GRID_W).astype(F32)[:, None] * inv
    cos = jnp.concatenate([jnp.cos(ar), jnp.cos(ar), jnp.cos(ac), jnp.cos(ac)], axis=-1)
    sin = jnp.concatenate([jnp.sin(ar), jnp.sin(ar), jnp.sin(ac), jnp.sin(ac)], axis=-1)
    first = (jnp.arange(DA_QK) % axis_dim) < n_freq
    sa = jnp.where(first, -sin, 0.0)
    sb = jnp.where(first, 0.0, sin)
    rep = lambda a: jnp.tile(a, (1, LANES // DA_QK))
    return rep(cos), rep(sa), rep(sb)


def _identity_tables(seq):
    return jnp.ones((seq, LANES), F32), jnp.zeros((seq, LANES), F32), jnp.zeros((seq, LANES), F32)


def _pack_layer_weights(w_in, w_uq, w_ukv):
    d = w_in.shape[0]
    zpad = jnp.zeros((d, LANES - MLA_ROPE), w_in.dtype)
    o_kr = 3 * 512 + MLA_Q_RANK + MLA_KV_RANK
    o_pool = o_kr + MLA_ROPE
    w2 = jnp.concatenate([w_in[:, :o_kr + MLA_ROPE], zpad, w_in[:, o_pool:o_pool + 512]], axis=1).astype(BF16)
    wgate = w_in[:, o_pool + 512:].astype(BF16)
    hq = MLA_NOPE + MLA_ROPE
    wq = w_uq.reshape(MLA_Q_RANK, MLA_HEADS, hq)
    wq = jnp.concatenate([wq, jnp.zeros((MLA_Q_RANK, MLA_HEADS, MLA_QK_PAD - hq), w_uq.dtype)], axis=-1)
    wuq = wq.reshape(MLA_Q_RANK, MLA_HEADS * MLA_QK_PAD).astype(BF16)
    wkv = w_ukv.reshape(MLA_KV_RANK, MLA_HEADS, MLA_NOPE + MLA_V)
    wukv = jnp.concatenate([wkv[:, :, :MLA_NOPE].reshape(MLA_KV_RANK, -1),
                            wkv[:, :, MLA_NOPE:].reshape(MLA_KV_RANK, -1)], axis=1).astype(BF16)
    return w2, wgate, wuq, wukv


def kernel(x, c, ctx, c_ctx, w_mod, b_mod, g_mix, w_in, da_lambda, da_subln, mla_gq, w_uq, mla_gkv, w_ukv,
           pool_w, pool_scale, w_branch, w_out, g_ffn, ffn_w_gate, ffn_w_up, ffn_w_down, moe_router,
           moe_w_gate, moe_w_up, moe_w_down, g_final):
    bsz, seq, d = x.shape
    n_ctx = ctx.shape[1]
    depth = w_mod.shape[0]

    cond8 = jnp.zeros((8, d), F32).at[:bsz].set(c).at[bsz].set(c_ctx)
    mod = _mod_rows(cond8, w_mod, b_mod)
    lat_tabs = _rope_tables(seq)
    ctx_tabs = _identity_tables(n_ctx)
    g_fin = g_final.reshape(1, d)

    xc = ctx
    for l in range(depth):
        need_ctx = l < depth - 1
        lam_init = 0.8 - 0.6 * math.exp(-0.3 * l)
        mod_lat = mod[l, :bsz][:, None, :]
        mod_ctx = jnp.broadcast_to(mod[l, bsz][None, None, :], (bsz, 1, N_MOD * d))
        w2, wgate, wuq, wukv = _pack_layer_weights(w_in[l], w_uq[l], w_ukv[l])
        gmix = g_mix[l].reshape(1, d)
        gffn = g_ffn[l].reshape(1, d)
        gq = mla_gq[l].reshape(1, -1)
        gkv = mla_gkv[l].reshape(1, -1)
        subln = da_subln[l].reshape(1, -1)
        lam = da_lambda[l]
        pw = pool_w[l].astype(BF16)
        psc = pool_scale[l].reshape(1, -1)
        wbr = w_branch[l].astype(BF16)
        wout = w_out[l].astype(BF16)
        j = l // 2
        dense = l % 2 == 0
        final = l == depth - 1

        qda, kda, vda, qm, km, vm, pin = _inproj(x, mod_lat, gmix, w2, gq, wuq, gkv, wukv, lat_tabs, 256)
        cqda, ckda, cvda, cqm, ckm, cvm, cpin = _inproj(xc, mod_ctx, gmix, w2, gq, wuq, gkv, wukv, ctx_tabs, 256)

        o_da = _attention(qda, ckda, cvda, kda, vda, DA_HEADS, LANES, DA_V, 256, 512, diff=True,
                          lam=lam, g=subln, lam_init=lam_init, name="diff_attn")
        o_mla = _attention(qm, ckm, cvm, km, vm, MLA_HEADS, MLA_QK_PAD, MLA_V, 512, 512, name="mla_attn")
        o_pool = _pool(pin, pw, psc, 512)

        router = None
        n_exp = moe_router.shape[-1]
        if not dense:
            router = jnp.zeros((d, LANES), F32).at[:, :n_exp].set(moe_router[j])
        outs = _merge(x, mod_lat, gmix, wgate, o_da, o_mla, o_pool, wbr, wout, gffn, router, n_exp, 256)

        if dense:
            wg = ffn_w_gate[j].astype(BF16)
            wu = ffn_w_up[j].astype(BF16)
            wd = ffn_w_down[j].astype(BF16)
            x = _ffn(outs[1], outs[0], mod_lat, wg, wu, wd, g_fin, final, 256)
        else:
            wg = moe_w_gate[j].astype(BF16)
            wu = moe_w_up[j].astype(BF16)
            wd = moe_w_down[j].astype(BF16)
            x = _moe_dense(outs[1], outs[0], mod_lat, outs[2], wg, wu, wd, g_fin, final, 512, 1792)

        if need_ctx:
            co_da = _attention(cqda, ckda, cvda, None, None, DA_HEADS, LANES, DA_V, 256, 512, diff=True,
                               lam=lam, g=subln, lam_init=lam_init, name="diff_attn_ctx")
            co_mla = _attention(cqm, ckm, cvm, None, None, MLA_HEADS, MLA_QK_PAD, MLA_V, 256, 512,
                                name="mla_attn_ctx")
            co_pool = _pool(cpin, pw, psc, 512)
            couts = _merge(xc, mod_ctx, gmix, wgate, co_da, co_mla, co_pool, wbr, wout, gffn, router, n_exp, 256)
            if dense:
                xc = _ffn(couts[1], couts[0], mod_ctx, wg, wu, wd, g_fin, False, 256)
            else:
                xc = _moe_dense(couts[1], couts[0], mod_ctx, couts[2], wg, wu, wd, g_fin, False, 512, 1792)
    return x
```

```python
import functools
import math

import jax
import jax.numpy as jnp
from jax import lax
from jax.experimental import pallas as pl
from jax.experimental.pallas import tpu as pltpu

F32 = jnp.float32
BF16 = jnp.bfloat16

GRID_W = 64
DA_HEADS = 4
DA_QK = 64
DA_V = 128
MLA_HEADS = 4
MLA_NOPE = 128
MLA_ROPE = 64
MLA_V = 128
MLA_Q_RANK = 384
MLA_KV_RANK = 256
POOL_WINDOWS = (2, 4, 8, 16)
POOL_GROUP_W = 128
N_BRANCH = 3
BRANCH_W = 512
ROPE_BASE = 10000.0
EPS = 1e-6
N_MOD = 6
LANES = 128
MLA_QK_PAD = 256
LOG2E = math.log2(math.e)
DA_SCALE = DA_QK ** -0.5 * LOG2E
MLA_SCALE = (MLA_NOPE + MLA_ROPE) ** -0.5 * LOG2E
NEG_BIG = -1e30
ATTN_UNROLL = 16
VMEM_LIMIT = 56 * 1024 * 1024

C_DAQ, C_DAK, C_DAV = 0, 512, 1024
C_QD = 1536
C_KVD = C_QD + MLA_Q_RANK
C_KR = C_KVD + MLA_KV_RANK
C_POOL = C_KR + LANES
W2_COLS = C_POOL + 512


def _sigmoid(v):
    return 1.0 / (1.0 + jnp.exp(-v))


def _params(sem, vmem=VMEM_LIMIT):
    return pltpu.CompilerParams(dimension_semantics=sem, vmem_limit_bytes=vmem)


def _const_spec(shape):
    nd = len(shape)
    return pl.BlockSpec(shape, lambda *_: (0,) * nd)


def _mod_kernel(cond_ref, w_ref, b_ref, o_ref):
    c = cond_ref[...]
    s = c * _sigmoid(c)
    o_ref[...] = jnp.dot(s, w_ref[...], preferred_element_type=F32,
                         precision=lax.Precision.HIGHEST) + b_ref[...]


def _mod_rows(cond8, w_mod, b_mod):
    depth, d, n = w_mod.shape
    tn = 1536
    return pl.pallas_call(
        _mod_kernel,
        out_shape=jax.ShapeDtypeStruct((depth, 8, n), F32),
        grid=(depth, n // tn),
        in_specs=[pl.BlockSpec((8, d), lambda l, j: (0, 0)),
                  pl.BlockSpec((None, d, tn), lambda l, j: (l, 0, j)),
                  pl.BlockSpec((None, 1, tn), lambda l, j: (l, 0, j))],
        out_specs=pl.BlockSpec((None, 8, tn), lambda l, j: (l, 0, j)),
        compiler_params=_params(("arbitrary", "arbitrary")),
        name="adaln_rows",
    )(cond8, w_mod, b_mod.reshape(depth, 1, n))


def _norm_mod(x, g, sh, sc):
    r = lax.rsqrt(jnp.mean(x * x, axis=-1, keepdims=True) + EPS)
    return (x * r * g) * (1.0 + sc) + sh


def _rms(v, g):
    return v * lax.rsqrt(jnp.mean(v * v, axis=-1, keepdims=True) + EPS) * g


def _inproj_kernel(x_ref, g_ref, sh_ref, sc_ref, w_ref, gq_ref, wuq_ref, gkv_ref, wukv_ref,
                   cos_ref, sa_ref, sb_ref,
                   qda_ref, kda_ref, vda_ref, qm_ref, km_ref, vm_ref, pool_ref):
    h = _norm_mod(x_ref[...], g_ref[...], sh_ref[...], sc_ref[...])
    z = jnp.dot(h.astype(BF16), w_ref[...], preferred_element_type=F32)
    cos = cos_ref[...]
    sa = sa_ref[...]
    sb = sb_ref[...]

    def rope(blk):
        return blk * cos + pltpu.roll(blk, LANES - 16, 1) * sa + pltpu.roll(blk, 16, 1) * sb

    lo = lax.broadcasted_iota(jnp.int32, cos.shape, 1) < MLA_ROPE

    for hh in range(DA_HEADS):
        c0 = hh * LANES
        qda_ref[:, c0:c0 + LANES] = (rope(z[:, C_DAQ + c0:C_DAQ + c0 + LANES]) * DA_SCALE).astype(BF16)
        kda_ref[:, c0:c0 + LANES] = rope(z[:, C_DAK + c0:C_DAK + c0 + LANES]).astype(BF16)

    qn = _rms(z[:, C_QD:C_QD + MLA_Q_RANK], gq_ref[...])
    qf = jnp.dot(qn.astype(BF16), wuq_ref[...], preferred_element_type=F32)
    kvn = _rms(z[:, C_KVD:C_KVD + MLA_KV_RANK], gkv_ref[...])
    kvf = jnp.dot(kvn.astype(BF16), wukv_ref[...], preferred_element_type=F32)
    kr = jnp.where(lo, rope(z[:, C_KR:C_KR + LANES]), 0.0).astype(BF16)
    for hh in range(MLA_HEADS):
        c0 = hh * MLA_QK_PAD
        qm_ref[:, c0:c0 + LANES] = (qf[:, c0:c0 + LANES] * MLA_SCALE).astype(BF16)
        qr = jnp.where(lo, rope(qf[:, c0 + LANES:c0 + 2 * LANES]), 0.0)
        qm_ref[:, c0 + LANES:c0 + 2 * LANES] = (qr * MLA_SCALE).astype(BF16)
        km_ref[:, c0:c0 + LANES] = kvf[:, hh * LANES:(hh + 1) * LANES].astype(BF16)
        km_ref[:, c0 + LANES:c0 + 2 * LANES] = kr
    ones = jnp.ones(cos.shape, BF16)
    for hh in range(MLA_HEADS):
        c0 = 2 * hh * LANES
        vda_ref[:, c0:c0 + LANES] = z[:, C_DAV + hh * LANES:C_DAV + (hh + 1) * LANES].astype(BF16)
        vda_ref[:, c0 + LANES:c0 + 2 * LANES] = ones
        vm_ref[:, c0:c0 + LANES] = kvf[:, 512 + hh * LANES:512 + (hh + 1) * LANES].astype(BF16)
        vm_ref[:, c0 + LANES:c0 + 2 * LANES] = ones
    pool_ref[...] = z[:, C_POOL:C_POOL + 512]


def _inproj(x, modrows, g_mix, w2, gq, wuq, gkv, wukv, tabs, tm):
    b, s, d = x.shape
    tm = min(tm, s)
    cos, sa, sb = tabs
    row = lambda j: pl.BlockSpec((None, 1, d), lambda bi, i: (bi, 0, j))
    tab = pl.BlockSpec((tm, LANES), lambda bi, i: (i, 0))
    out = lambda w, dt: jax.ShapeDtypeStruct((b, s, w), dt)
    ospec = lambda w: pl.BlockSpec((None, tm, w), lambda bi, i: (bi, i, 0))
    return pl.pallas_call(
        _inproj_kernel,
        out_shape=(out(512, BF16), out(512, BF16), out(1024, BF16), out(1024, BF16), out(1024, BF16),
                   out(1024, BF16), out(512, F32)),
        grid=(b, s // tm),
        in_specs=[pl.BlockSpec((None, tm, d), lambda bi, i: (bi, i, 0)),
                  _const_spec((1, d)), row(0), row(1),
                  _const_spec(w2.shape), _const_spec(gq.shape), _const_spec(wuq.shape),
                  _const_spec(gkv.shape), _const_spec(wukv.shape), tab, tab, tab],
        out_specs=(ospec(512), ospec(512), ospec(1024), ospec(1024), ospec(1024), ospec(1024), ospec(512)),
        compiler_params=_params(("arbitrary", "arbitrary")),
        name="in_proj",
    )(x, g_mix, modrows, modrows, w2, gq, wuq, gkv, wukv, cos, sa, sb)


def _attn_kernel(*refs, diff, has_lat, tk, lam_init, unroll):
    it = iter(refs)
    q_ref, kc_ref, vc_ref = next(it), next(it), next(it)
    kl_ref = vl_ref = lam_ref = g_ref = None
    if has_lat:
        kl_ref, vl_ref = next(it), next(it)
    if diff:
        lam_ref, g_ref = next(it), next(it)
    o_ref = next(it)
    m_ref, acc_ref = next(it), next(it)
    s_refs = (next(it), next(it))
    p_refs = (next(it), next(it))
    al_refs = (next(it), next(it))

    q = q_ref[...]
    tq = q.shape[0]
    if diff:
        lane = lax.broadcasted_iota(jnp.int32, q.shape, 1)
        zero = jnp.zeros_like(q)
        qs = jnp.concatenate([jnp.where(lane < DA_QK, q, zero), jnp.where(lane >= DA_QK, q, zero)], axis=0)
    else:
        qs = q
    rows = qs.shape[0]
    dv = o_ref.shape[-1]

    def scores(k):
        return lax.dot_general(qs, k, (((1,), (1,)), ((), ())), preferred_element_type=F32)

    def qk(slot, k):
        s_refs[slot][:, :k.shape[0]] = scores(k)

    def sm(slot, w):
        s_ref, p_ref = s_refs[slot], p_refs[slot]
        blocks = [slice(j * LANES, (j + 1) * LANES) for j in range(w // LANES)]
        mx = s_ref[:, blocks[0]]
        for blk in blocks[1:]:
            mx = jnp.maximum(mx, s_ref[:, blk])
        m = m_ref[...]
        m_new = jnp.maximum(m, jnp.max(mx, axis=-1, keepdims=True))
        al_refs[slot][...] = jnp.exp2(m - m_new)
        m_ref[...] = m_new
        for blk in blocks:
            p_ref[:, blk] = jnp.exp2(s_ref[:, blk] - m_new).astype(BF16)

    def pv(slot, v):
        pblk = p_refs[slot][:, :v.shape[0]]
        new = jnp.dot(pblk, v, preferred_element_type=F32)
        al = al_refs[slot][...]
        for blk in (slice(0, dv), slice(dv, 2 * dv)):
            acc_ref[:, blk] = al * acc_ref[:, blk] + new[:, blk]

    n_ctx = kc_ref.shape[0]
    n_lat = kl_ref.shape[0] // tk if has_lat else 0
    width = lambda c: n_ctx if c == 0 else tk
    k_of = lambda c: kc_ref[...] if c == 0 else kl_ref[(c - 1) * tk:c * tk, :]
    v_of = lambda c: vc_ref[...] if c == 0 else vl_ref[(c - 1) * tk:c * tk, :]
    lat = lambda ref, i: ref[pl.ds(pl.multiple_of(i * tk, tk), tk), :]

    def static_step(t):
        if t <= n_lat:
            qk(t % 2, k_of(t))
        if 0 <= t - 2 <= n_lat:
            pv(t % 2, v_of(t - 2))
        if 0 <= t - 1 <= n_lat:
            sm((t - 1) % 2, width(t - 1))

    m_ref[...] = jnp.full(m_ref.shape, NEG_BIG, F32)
    acc_ref[...] = jnp.zeros(acc_ref.shape, F32)
    t0 = min(3 + (n_lat - 2) % unroll, n_lat + 1) if n_lat >= 2 else n_lat + 1
    for t in range(t0):
        static_step(t)
    if t0 <= n_lat:

        def body(j, carry):
            for u in range(unroll):
                t = t0 + unroll * j + u
                slot = (t0 + u) % 2
                qk(slot, lat(kl_ref, t - 1))
                pv(slot, lat(vl_ref, t - 3))
                sm(1 - slot, tk)
            return carry

        lax.fori_loop(0, (n_lat - t0 + 1) // unroll, body, 0)
    for t in range(n_lat + 1, n_lat + 3):
        static_step(t)
    acc = acc_ref[...]
    o = acc[:, :dv] / acc[:, dv:]
    if diff:
        lv = lam_ref[...]
        a = jnp.sum(lv[0:1, :] * lv[1:2, :], axis=-1, keepdims=True)
        b = jnp.sum(lv[2:3, :] * lv[3:4, :], axis=-1, keepdims=True)
        lam = jnp.exp(a) - jnp.exp(b) + lam_init
        o = o[:tq] - lam * o[tq:]
        o = _rms(o, g_ref[...]) * (1.0 - lam_init)
    o_ref[...] = o.astype(o_ref.dtype)


def _attention(q, kc, vc, kl, vl, heads, dq, dv, tq, tk, diff=False, lam=None, g=None, lam_init=0.0,
               name="attn"):
    b, s, _ = q.shape
    c = kc.shape[1]
    tq = min(tq, s)
    has_lat = kl is not None
    args = [q, kc, vc]
    specs = [pl.BlockSpec((None, tq, dq), lambda bi, hi, qi: (bi, qi, hi)),
             pl.BlockSpec((None, c, dq), lambda bi, hi, qi: (bi, 0, hi)),
             pl.BlockSpec((None, c, 2 * dv), lambda bi, hi, qi: (bi, 0, hi))]
    if has_lat:
        sl = kl.shape[1]
        tk = min(tk, sl)
        args += [kl, vl]
        specs += [pl.BlockSpec((None, sl, dq), lambda bi, hi, qi: (bi, 0, hi)),
                  pl.BlockSpec((None, sl, 2 * dv), lambda bi, hi, qi: (bi, 0, hi))]
    if diff:
        args += [lam, g]
        specs += [_const_spec(lam.shape), _const_spec(g.shape)]
    kern = functools.partial(_attn_kernel, diff=diff, has_lat=has_lat, tk=tk, lam_init=lam_init,
                             unroll=ATTN_UNROLL)
    rows = 2 * tq if diff else tq
    wmax = max(tk, c) if has_lat else c
    scratch = [pltpu.VMEM((rows, LANES), F32), pltpu.VMEM((rows, 2 * dv), F32),
               pltpu.VMEM((rows, wmax), F32), pltpu.VMEM((rows, wmax), F32),
               pltpu.VMEM((rows, wmax), BF16), pltpu.VMEM((rows, wmax), BF16),
               pltpu.VMEM((rows, LANES), F32), pltpu.VMEM((rows, LANES), F32)]
    return pl.pallas_call(
        kern,
        out_shape=jax.ShapeDtypeStruct((b, s, heads * dv), BF16),
        grid=(b, heads, s // tq),
        in_specs=specs,
        out_specs=pl.BlockSpec((None, tq, dv), lambda bi, hi, qi: (bi, qi, hi)),
        scratch_shapes=scratch,
        compiler_params=_params(("arbitrary", "arbitrary", "arbitrary")),
        name=name,
    )(*args)


def _pool_kernel(prev_ref, cur_ref, next_ref, w_ref, sc_ref, o_ref, *, seq_len):
    i = pl.program_id(1)
    n = pl.num_programs(1)
    cur = cur_ref[...]
    tm = cur.shape[0]
    prev = jnp.where(i > 0, prev_ref[...], 0.0)
    nxt = jnp.where(i < n - 1, next_ref[...], 0.0)
    ext = jnp.concatenate([prev, cur, nxt], axis=0)
    ne = tm + 16
    t = i * tm + lax.broadcasted_iota(jnp.int32, (tm, 1), 0)
    for g, w in enumerate(POOL_WINDOWS):
        e = ext[:, g * LANES:(g + 1) * LANES]
        acc = e + pltpu.roll(e, 1, 0)
        half = 1
        while 2 * half < w:
            acc = pltpu.roll(acc, half, 0) + pltpu.roll(acc, ne - half, 0)
            half *= 2
        win = acc[8:8 + tm]
        lo = jnp.clip(t - w // 2, 0, seq_len)
        hi = jnp.clip(t - w // 2 + w, 0, seq_len)
        cnt = (hi - lo).astype(F32)
        dlt = win / cnt - cur[:, g * LANES:(g + 1) * LANES]
        y = jnp.dot(dlt.astype(BF16), w_ref[g], preferred_element_type=F32)
        o_ref[:, g * LANES:(g + 1) * LANES] = (y * sc_ref[:, g * LANES:(g + 1) * LANES]).astype(BF16)


def _pool(u, pool_w, pool_scale, tm):
    b, s, w = u.shape
    tm = min(tm, s)
    nb8 = s // 8
    r8 = tm // 8
    return pl.pallas_call(
        functools.partial(_pool_kernel, seq_len=s),
        out_shape=jax.ShapeDtypeStruct((b, s, w), BF16),
        grid=(b, s // tm),
        in_specs=[pl.BlockSpec((None, 8, w), lambda bi, i: (bi, jnp.maximum(i * r8 - 1, 0), 0)),
                  pl.BlockSpec((None, tm, w), lambda bi, i: (bi, i, 0)),
                  pl.BlockSpec((None, 8, w), lambda bi, i: (bi, jnp.minimum((i + 1) * r8, nb8 - 1), 0)),
                  _const_spec(pool_w.shape), _const_spec(pool_scale.shape)],
        out_specs=pl.BlockSpec((None, tm, w), lambda bi, i: (bi, i, 0)),
        compiler_params=_params(("arbitrary", "arbitrary")),
        name="pool_mixer",
    )(u, u, u, pool_w, pool_scale)


def _merge_kernel(*refs, moe, n_exp):
    (x_ref, gmix_ref, sh1_ref, sc1_ref, wgate_ref, oda_ref, omla_ref, opool_ref, wbr_ref, wout_ref,
     gt1_ref, gffn_ref, sh2_ref, sc2_ref) = refs[:14]
    if moe:
        router_ref, xo_ref, h2_ref, gates_ref = refs[14:]
    else:
        xo_ref, h2_ref = refs[14:]
    x = x_ref[...]
    d = x.shape[-1]
    h = _norm_mod(x, gmix_ref[...], sh1_ref[...], sc1_ref[...]).astype(BF16)
    merged = None
    for n, o_ref in enumerate((oda_ref, omla_ref, opool_ref)):
        gate = _sigmoid(jnp.dot(h, wgate_ref[:, n * d:(n + 1) * d], preferred_element_type=F32))
        proj = jnp.dot(o_ref[...], wbr_ref[n], preferred_element_type=F32)
        merged = gate * proj if merged is None else merged + gate * proj
    mix = jnp.dot(merged.astype(BF16), wout_ref[...], preferred_element_type=F32)
    xn = x + gt1_ref[...] * mix
    xo_ref[...] = xn
    h2 = _norm_mod(xn, gffn_ref[...], sh2_ref[...], sc2_ref[...])
    h2_ref[...] = h2.astype(BF16)
    if moe:
        logits = jnp.dot(h2, router_ref[...], preferred_element_type=F32, precision=lax.Precision.HIGHEST)
        lane = lax.broadcasted_iota(jnp.int32, logits.shape, 1)
        logits = jnp.where(lane < n_exp, logits, NEG_BIG)
        v1 = jnp.max(logits, axis=-1, keepdims=True)
        i1 = jnp.min(jnp.where(logits == v1, lane, n_exp), axis=-1, keepdims=True)
        rest = jnp.where(lane == i1, NEG_BIG, logits)
        v2 = jnp.max(rest, axis=-1, keepdims=True)
        i2 = jnp.min(jnp.where(rest == v2, lane, n_exp), axis=-1, keepdims=True)
        w1 = 1.0 / (1.0 + jnp.exp(v2 - v1))
        gates_ref[...] = jnp.where(lane == i1, w1, 0.0) + jnp.where(lane == i2, 1.0 - w1, 0.0)


def _merge(x, modrows, g_mix, wgate, o_da, o_mla, o_pool, wbr, wout, g_ffn, router, n_exp, tm):
    b, s, d = x.shape
    tm = min(tm, s)
    moe = router is not None
    row = lambda j: pl.BlockSpec((None, 1, d), lambda bi, i: (bi, 0, j))
    tile = lambda w: pl.BlockSpec((None, tm, w), lambda bi, i: (bi, i, 0))
    args = [x, g_mix, modrows, modrows, wgate, o_da, o_mla, o_pool, wbr, wout, modrows, g_ffn, modrows, modrows]
    specs = [tile(d), _const_spec((1, d)), row(0), row(1), _const_spec(wgate.shape),
             tile(BRANCH_W), tile(BRANCH_W), tile(BRANCH_W), _const_spec(wbr.shape), _const_spec(wout.shape),
             row(2), _const_spec((1, d)), row(3), row(4)]
    out_shape = [jax.ShapeDtypeStruct((b, s, d), F32), jax.ShapeDtypeStruct((b, s, d), BF16)]
    out_specs = [tile(d), tile(d)]
    if moe:
        args.append(router)
        specs.append(_const_spec(router.shape))
        out_shape.append(jax.ShapeDtypeStruct((b, s, LANES), F32))
        out_specs.append(tile(LANES))
    return pl.pallas_call(
        functools.partial(_merge_kernel, moe=moe, n_exp=n_exp),
        out_shape=tuple(out_shape),
        grid=(b, s // tm),
        in_specs=specs,
        out_specs=tuple(out_specs),
        compiler_params=_params(("arbitrary", "arbitrary")),
        name="merge_out",
    )(*args)


def _ffn_kernel(h_ref, x_ref, gt_ref, wg_ref, wu_ref, wd_ref, gfin_ref, o_ref, *, final):
    h = h_ref[...]
    gte = jnp.dot(h, wg_ref[...], preferred_element_type=F32)
    up = jnp.dot(h, wu_ref[...], preferred_element_type=F32)
    act = (gte * _sigmoid(gte) * up).astype(BF16)
    y = jnp.dot(act, wd_ref[...], preferred_element_type=F32)
    xo = x_ref[...] + gt_ref[...] * y
    if final:
        xo = _rms(xo, gfin_ref[...])
    o_ref[...] = xo


def _ffn(h2, x, modrows, wg, wu, wd, g_final, final, tm):
    b, s, d = x.shape
    tm = min(tm, s)
    tile = pl.BlockSpec((None, tm, d), lambda bi, i: (bi, i, 0))
    return pl.pallas_call(
        functools.partial(_ffn_kernel, final=final),
        out_shape=jax.ShapeDtypeStruct((b, s, d), F32),
        grid=(b, s // tm),
        in_specs=[tile, tile, pl.BlockSpec((None, 1, d), lambda bi, i: (bi, 0, 5)),
                  _const_spec(wg.shape), _const_spec(wu.shape), _const_spec(wd.shape), _const_spec((1, d))],
        out_specs=tile,
        compiler_params=_params(("arbitrary", "arbitrary")),
        name="ffn_swiglu",
    )(h2, x, modrows, wg, wu, wd, g_final)


ROUTE_SUB = 256
MOE_ROWS = 256


def _route_kernel(g_ref, rank_ref, rankt_ref, cnt_ref, *, n_exp):
    t = g_ref.shape[0]
    r_io = lax.broadcasted_iota(jnp.int32, (ROUTE_SUB, ROUTE_SUB), 0)
    c_io = lax.broadcasted_iota(jnp.int32, (ROUTE_SUB, ROUTE_SUB), 1)
    tri = jnp.where(c_io <= r_io, 1.0, 0.0).astype(BF16)
    carry = jnp.zeros((1, LANES), F32)
    for j in range(t // ROUTE_SUB):
        rows = slice(j * ROUTE_SUB, (j + 1) * ROUTE_SUB)
        routed = g_ref[rows, :] != 0.0
        incl = jnp.dot(tri, jnp.where(routed, 1.0, 0.0).astype(BF16), preferred_element_type=F32)
        rank_ref[rows, :] = jnp.where(routed, incl - 1.0 + carry, -1.0)
        carry = carry + incl[ROUTE_SUB - 1:ROUTE_SUB, :]
    rankt_ref[...] = rank_ref[...].T[0:n_exp, :]
    cnt_ref[...] = jnp.broadcast_to(carry, cnt_ref.shape).astype(jnp.int32)


def _route(gates, n_exp, tm):
    n = gates.shape[0]
    return pl.pallas_call(
        functools.partial(_route_kernel, n_exp=n_exp),
        out_shape=(jax.ShapeDtypeStruct((n, LANES), F32),
                   jax.ShapeDtypeStruct((n // tm, n_exp, tm), F32),
                   jax.ShapeDtypeStruct((n // tm, 8, LANES), jnp.int32)),
        grid=(n // tm,),
        in_specs=[pl.BlockSpec((tm, LANES), lambda i: (i, 0))],
        out_specs=(pl.BlockSpec((tm, LANES), lambda i: (i, 0)),
                   pl.BlockSpec((None, n_exp, tm), lambda i: (i, 0, 0)),
                   pl.BlockSpec((None, 8, LANES), lambda i: (i, 0, 0))),
        compiler_params=_params(("arbitrary",)),
        name="moe_route",
    )(gates)


def _moe_kernel(cnt_ref, h_ref, rank_ref, rankt_ref, gates_ref, wg_ref, wu_ref, wd_ref, o_ref,
                xg_ref, yg_ref, acc_ref, *, n_exp):
    i, e, f = pl.program_id(0), pl.program_id(1), pl.program_id(2)
    nf = pl.num_programs(2)
    t = h_ref.shape[0]
    n_rows = cnt_ref[i * n_exp + e]
    n_blk = lax.shift_right_logical(n_rows + (MOE_ROWS - 1), int(math.log2(MOE_ROWS)))

    @pl.when(jnp.logical_and(e == 0, f == 0))
    def _():
        acc_ref[...] = jnp.zeros_like(acc_ref)

    lane = lax.broadcasted_iota(jnp.int32, (t, LANES), 1)
    rank_row = rankt_ref[pl.ds(e, 1), :]
    rank_col = jnp.sum(jnp.where(lane == e, rank_ref[...], 0.0), axis=-1, keepdims=True)
    gate_col = jnp.sum(jnp.where(lane == e, gates_ref[...], 0.0), axis=-1, keepdims=True)

    def block(b, carry):
        base = pl.multiple_of(b * MOE_ROWS, MOE_ROWS)
        rows = pl.ds(base, MOE_ROWS)

        @pl.when(f == 0)
        def _():
            slot = (base + lax.broadcasted_iota(jnp.int32, (MOE_ROWS, t), 0)).astype(F32)
            onehot = jnp.where(rank_row == slot, 1.0, 0.0).astype(BF16)
            xg_ref[rows, :] = jnp.dot(onehot, h_ref[...], preferred_element_type=F32).astype(BF16)

        xb = xg_ref[rows, :]
        gte = jnp.dot(xb, wg_ref[...], preferred_element_type=F32)
        up = jnp.dot(xb, wu_ref[...], preferred_element_type=F32)
        act = (gte * _sigmoid(gte) * up).astype(BF16)
        part = jnp.dot(act, wd_ref[...], preferred_element_type=F32)

        @pl.when(f == 0)
        def _():
            yg_ref[rows, :] = part

        @pl.when(f > 0)
        def _():
            yg_ref[rows, :] += part

        @pl.when(f == nf - 1)
        def _():
            slot = (base + lax.broadcasted_iota(jnp.int32, (t, MOE_ROWS), 1)).astype(F32)
            onehot = jnp.where(rank_col == slot, 1.0, 0.0).astype(BF16)
            acc_ref[...] += gate_col * jnp.dot(onehot, yg_ref[rows, :].astype(BF16),
                                               preferred_element_type=F32)
        return carry

    lax.fori_loop(0, n_blk, block, 0)

    @pl.when(jnp.logical_and(e == n_exp - 1, f == nf - 1))
    def _():
        o_ref[...] = acc_ref[...]


def _moe_routed(h2, gates, wg, wu, wd, tm, tf):
    n, d = h2.shape
    n_exp, _, dff = wg.shape
    tm = min(tm, n)
    rank, rank_t, cnt = _route(gates, n_exp, tm)
    counts = cnt[:, 0, :n_exp].reshape(-1)
    grid_spec = pltpu.PrefetchScalarGridSpec(
        num_scalar_prefetch=1,
        grid=(n // tm, n_exp, dff // tf),
        in_specs=[pl.BlockSpec((tm, d), lambda i, e, f, c: (i, 0)),
                  pl.BlockSpec((tm, LANES), lambda i, e, f, c: (i, 0)),
                  pl.BlockSpec((None, n_exp, tm), lambda i, e, f, c: (i, 0, 0)),
                  pl.BlockSpec((tm, LANES), lambda i, e, f, c: (i, 0)),
                  pl.BlockSpec((None, d, tf), lambda i, e, f, c: (e, 0, f)),
                  pl.BlockSpec((None, d, tf), lambda i, e, f, c: (e, 0, f)),
                  pl.BlockSpec((None, tf, d), lambda i, e, f, c: (e, f, 0))],
        out_specs=pl.BlockSpec((tm, d), lambda i, e, f, c: (i, 0)),
        scratch_shapes=[pltpu.VMEM((tm, d), BF16), pltpu.VMEM((tm, d), F32), pltpu.VMEM((tm, d), F32)])
    return pl.pallas_call(
        functools.partial(_moe_kernel, n_exp=n_exp),
        out_shape=jax.ShapeDtypeStruct((n, d), F32),
        grid_spec=grid_spec,
        compiler_params=_params(("arbitrary",) * 3),
        name="moe_swiglu",
    )(counts, h2, rank, rank_t, gates, wg, wu, wd)


def _residual_kernel(x_ref, y_ref, gt_ref, gfin_ref, o_ref, *, final):
    xo = x_ref[...] + gt_ref[...] * y_ref[...]
    if final:
        xo = _rms(xo, gfin_ref[...])
    o_ref[...] = xo


def _residual(x, y, modrows, g_final, final, tm):
    b, s, d = x.shape
    tm = min(tm, s)
    tile = pl.BlockSpec((None, tm, d), lambda bi, i: (bi, i, 0))
    return pl.pallas_call(
        functools.partial(_residual_kernel, final=final),
        out_shape=jax.ShapeDtypeStruct((b, s, d), F32),
        grid=(b, s // tm),
        in_specs=[tile, tile, pl.BlockSpec((None, 1, d), lambda bi, i: (bi, 0, 5)), _const_spec((1, d))],
        out_specs=tile,
        compiler_params=_params(("arbitrary", "arbitrary")),
        name="moe_residual",
    )(x, y, modrows, g_final)


def _moe(h2, x, modrows, gates, wg, wu, wd, g_final, final):
    b, s, d = x.shape
    y = _moe_routed(h2.reshape(b * s, d), gates.reshape(b * s, LANES), wg, wu, wd, 1024, 1792)
    return _residual(x, y.reshape(b, s, d), modrows, g_final, final, 512)


def _rope_tables(seq):
    axis_dim = DA_QK // 2
    n_freq = axis_dim // 2
    inv = jnp.exp(-math.log(ROPE_BASE) * jnp.arange(n_freq, dtype=F32) * (2.0 / axis_dim))
    t = jnp.arange(seq, dtype=jnp.int32)
    ar = (t // GRID_W).astype(F32)[:, None] * inv
    ac = (t % GRID_W).astype(F32)[:, None] * inv
    cos = jnp.concatenate([jnp.cos(ar), jnp.cos(ar), jnp.cos(ac), jnp.cos(ac)], axis=-1)
    sin = jnp.concatenate([jnp.sin(ar), jnp.sin(ar), jnp.sin(ac), jnp.sin(ac)], axis=-1)
    first = (jnp.arange(DA_QK) % axis_dim) < n_freq
    sa = jnp.where(first, -sin, 0.0)
    sb = jnp.where(first, 0.0, sin)
    rep = lambda a: jnp.tile(a, (1, LANES // DA_QK))
    return rep(cos), rep(sa), rep(sb)


def _identity_tables(seq):
    return jnp.ones((seq, LANES), F32), jnp.zeros((seq, LANES), F32), jnp.zeros((seq, LANES), F32)


def _pack_layer_weights(w_in, w_uq, w_ukv):
    d = w_in.shape[0]
    zpad = jnp.zeros((d, LANES - MLA_ROPE), w_in.dtype)
    o_kr = 3 * 512 + MLA_Q_RANK + MLA_KV_RANK
    o_pool = o_kr + MLA_ROPE
    w2 = jnp.concatenate([w_in[:, :o_kr + MLA_ROPE], zpad, w_in[:, o_pool:o_pool + 512]], axis=1).astype(BF16)
    wgate = w_in[:, o_pool + 512:].astype(BF16)
    hq = MLA_NOPE + MLA_ROPE
    wq = w_uq.reshape(MLA_Q_RANK, MLA_HEADS, hq)
    wq = jnp.concatenate([wq, jnp.zeros((MLA_Q_RANK, MLA_HEADS, MLA_QK_PAD - hq), w_uq.dtype)], axis=-1)
    wuq = wq.reshape(MLA_Q_RANK, MLA_HEADS * MLA_QK_PAD).astype(BF16)
    wkv = w_ukv.reshape(MLA_KV_RANK, MLA_HEADS, MLA_NOPE + MLA_V)
    wukv = jnp.concatenate([wkv[:, :, :MLA_NOPE].reshape(MLA_KV_RANK, -1),
                            wkv[:, :, MLA_NOPE:].reshape(MLA_KV_RANK, -1)], axis=1).astype(BF16)
    return w2, wgate, wuq, wukv


def kernel(x, c, ctx, c_ctx, w_mod, b_mod, g_mix, w_in, da_lambda, da_subln, mla_gq, w_uq, mla_gkv, w_ukv,
           pool_w, pool_scale, w_branch, w_out, g_ffn, ffn_w_gate, ffn_w_up, ffn_w_down, moe_router,
           moe_w_gate, moe_w_up, moe_w_down, g_final):
    bsz, seq, d = x.shape
    n_ctx = ctx.shape[1]
    depth = w_mod.shape[0]

    cond8 = jnp.zeros((8, d), F32).at[:bsz].set(c).at[bsz].set(c_ctx)
    mod = _mod_rows(cond8, w_mod, b_mod)
    lat_tabs = _rope_tables(seq)
    ctx_tabs = _identity_tables(n_ctx)
    g_fin = g_final.reshape(1, d)

    xc = ctx
    for l in range(depth):
        need_ctx = l < depth - 1
        lam_init = 0.8 - 0.6 * math.exp(-0.3 * l)
        mod_lat = mod[l, :bsz][:, None, :]
        mod_ctx = jnp.broadcast_to(mod[l, bsz][None, None, :], (bsz, 1, N_MOD * d))
        w2, wgate, wuq, wukv = _pack_layer_weights(w_in[l], w_uq[l], w_ukv[l])
        gmix = g_mix[l].reshape(1, d)
        gffn = g_ffn[l].reshape(1, d)
        gq = mla_gq[l].reshape(1, -1)
        gkv = mla_gkv[l].reshape(1, -1)
        subln = da_subln[l].reshape(1, -1)
        lam = da_lambda[l]
        pw = pool_w[l].astype(BF16)
        psc = pool_scale[l].reshape(1, -1)
        wbr = w_branch[l].astype(BF16)
        wout = w_out[l].astype(BF16)
        j = l // 2
        dense = l % 2 == 0
        final = l == depth - 1

        qda, kda, vda, qm, km, vm, pin = _inproj(x, mod_lat, gmix, w2, gq, wuq, gkv, wukv, lat_tabs, 256)
        cqda, ckda, cvda, cqm, ckm, cvm, cpin = _inproj(xc, mod_ctx, gmix, w2, gq, wuq, gkv, wukv, ctx_tabs, 256)

        o_da = _attention(qda, ckda, cvda, kda, vda, DA_HEADS, LANES, DA_V, 256, 512, diff=True,
                          lam=lam, g=subln, lam_init=lam_init, name="diff_attn")
        o_mla = _attention(qm, ckm, cvm, km, vm, MLA_HEADS, MLA_QK_PAD, MLA_V, 512, 512, name="mla_attn")
        o_pool = _pool(pin, pw, psc, 512)

        router = None
        n_exp = moe_router.shape[-1]
        if not dense:
            router = jnp.zeros((d, LANES), F32).at[:, :n_exp].set(moe_router[j])
        outs = _merge(x, mod_lat, gmix, wgate, o_da, o_mla, o_pool, wbr, wout, gffn, router, n_exp, 256)

        if dense:
            wg = ffn_w_gate[j].astype(BF16)
            wu = ffn_w_up[j].astype(BF16)
            wd = ffn_w_down[j].astype(BF16)
            x = _ffn(outs[1], outs[0], mod_lat, wg, wu, wd, g_fin, final, 256)
        else:
            wg = moe_w_gate[j].astype(BF16)
            wu = moe_w_up[j].astype(BF16)
            wd = moe_w_down[j].astype(BF16)
            x = _moe(outs[1], outs[0], mod_lat, outs[2], wg, wu, wd, g_fin, final)

        if need_ctx:
            co_da = _attention(cqda, ckda, cvda, None, None, DA_HEADS, LANES, DA_V, 256, 512, diff=True,
                               lam=lam, g=subln, lam_init=lam_init, name="diff_attn_ctx")
            co_mla = _attention(cqm, ckm, cvm, None, None, MLA_HEADS, MLA_QK_PAD, MLA_V, 256, 512,
                                name="mla_attn_ctx")
            co_pool = _pool(cpin, pw, psc, 512)
            couts = _merge(xc, mod_ctx, gmix, wgate, co_da, co_mla, co_pool, wbr, wout, gffn, router, n_exp, 256)
            if dense:
                xc = _ffn(couts[1], couts[0], mod_ctx, wg, wu, wd, g_fin, False, 256)
            else:
                xc = _moe(couts[1], couts[0], mod_ctx, couts[2], wg, wu, wd, g_fin, False)
    return x
```

```python
import functools
import math

import jax
import jax.numpy as jnp
from jax import lax
from jax.experimental import pallas as pl
from jax.experimental.pallas import tpu as pltpu

F32 = jnp.float32
BF16 = jnp.bfloat16

GRID_W = 64
DA_HEADS = 4
DA_QK = 64
DA_V = 128
MLA_HEADS = 4
MLA_NOPE = 128
MLA_ROPE = 64
MLA_V = 128
MLA_Q_RANK = 384
MLA_KV_RANK = 256
POOL_WINDOWS = (2, 4, 8, 16)
POOL_GROUP_W = 128
N_BRANCH = 3
BRANCH_W = 512
ROPE_BASE = 10000.0
EPS = 1e-6
N_MOD = 6
LANES = 128
MLA_QK_PAD = 256
LOG2E = math.log2(math.e)
DA_SCALE = DA_QK ** -0.5 * LOG2E
MLA_SCALE = (MLA_NOPE + MLA_ROPE) ** -0.5 * LOG2E
NEG_BIG = -1e30
ATTN_Q_TILES = 1
VMEM_LIMIT = 56 * 1024 * 1024

C_DAQ, C_DAK, C_DAV = 0, 512, 1024
C_QD = 1536
C_KVD = C_QD + MLA_Q_RANK
C_KR = C_KVD + MLA_KV_RANK
C_POOL = C_KR + LANES
W2_COLS = C_POOL + 512


def _sigmoid(v):
    return 1.0 / (1.0 + jnp.exp(-v))


def _params(sem, vmem=VMEM_LIMIT):
    return pltpu.CompilerParams(dimension_semantics=sem, vmem_limit_bytes=vmem)


def _const_spec(shape):
    nd = len(shape)
    return pl.BlockSpec(shape, lambda *_: (0,) * nd)


def _mod_kernel(cond_ref, w_ref, b_ref, o_ref):
    c = cond_ref[...]
    s = c * _sigmoid(c)
    o_ref[...] = jnp.dot(s, w_ref[...], preferred_element_type=F32,
                         precision=lax.Precision.HIGHEST) + b_ref[...]


def _mod_rows(cond8, w_mod, b_mod):
    depth, d, n = w_mod.shape
    tn = 1536
    return pl.pallas_call(
        _mod_kernel,
        out_shape=jax.ShapeDtypeStruct((depth, 8, n), F32),
        grid=(depth, n // tn),
        in_specs=[pl.BlockSpec((8, d), lambda l, j: (0, 0)),
                  pl.BlockSpec((None, d, tn), lambda l, j: (l, 0, j)),
                  pl.BlockSpec((None, 1, tn), lambda l, j: (l, 0, j))],
        out_specs=pl.BlockSpec((None, 8, tn), lambda l, j: (l, 0, j)),
        compiler_params=_params(("arbitrary", "arbitrary")),
        name="adaln_rows",
    )(cond8, w_mod, b_mod.reshape(depth, 1, n))


def _norm_mod(x, g, sh, sc):
    r = lax.rsqrt(jnp.mean(x * x, axis=-1, keepdims=True) + EPS)
    return (x * r * g) * (1.0 + sc) + sh


def _rms(v, g):
    return v * lax.rsqrt(jnp.mean(v * v, axis=-1, keepdims=True) + EPS) * g


def _inproj_kernel(x_ref, g_ref, sh_ref, sc_ref, w_ref, gq_ref, wuq_ref, gkv_ref, wukv_ref,
                   cos_ref, sa_ref, sb_ref,
                   qda_ref, kda_ref, vda_ref, qm_ref, km_ref, vm_ref, pool_ref):
    h = _norm_mod(x_ref[...], g_ref[...], sh_ref[...], sc_ref[...])
    z = jnp.dot(h.astype(BF16), w_ref[...], preferred_element_type=F32)
    cos = cos_ref[...]
    sa = sa_ref[...]
    sb = sb_ref[...]

    def rope(blk):
        return blk * cos + pltpu.roll(blk, LANES - 16, 1) * sa + pltpu.roll(blk, 16, 1) * sb

    lo = lax.broadcasted_iota(jnp.int32, cos.shape, 1) < MLA_ROPE

    for hh in range(DA_HEADS):
        c0 = hh * LANES
        qda_ref[:, c0:c0 + LANES] = (rope(z[:, C_DAQ + c0:C_DAQ + c0 + LANES]) * DA_SCALE).astype(BF16)
        kda_ref[:, c0:c0 + LANES] = rope(z[:, C_DAK + c0:C_DAK + c0 + LANES]).astype(BF16)

    qn = _rms(z[:, C_QD:C_QD + MLA_Q_RANK], gq_ref[...])
    qf = jnp.dot(qn.astype(BF16), wuq_ref[...], preferred_element_type=F32)
    kvn = _rms(z[:, C_KVD:C_KVD + MLA_KV_RANK], gkv_ref[...])
    kvf = jnp.dot(kvn.astype(BF16), wukv_ref[...], preferred_element_type=F32)
    kr = jnp.where(lo, rope(z[:, C_KR:C_KR + LANES]), 0.0).astype(BF16)
    for hh in range(MLA_HEADS):
        c0 = hh * MLA_QK_PAD
        qm_ref[:, c0:c0 + LANES] = (qf[:, c0:c0 + LANES] * MLA_SCALE).astype(BF16)
        qr = jnp.where(lo, rope(qf[:, c0 + LANES:c0 + 2 * LANES]), 0.0)
        qm_ref[:, c0 + LANES:c0 + 2 * LANES] = (qr * MLA_SCALE).astype(BF16)
        km_ref[:, c0:c0 + LANES] = kvf[:, hh * LANES:(hh + 1) * LANES].astype(BF16)
        km_ref[:, c0 + LANES:c0 + 2 * LANES] = kr
    ones = jnp.ones(cos.shape, BF16)
    for hh in range(MLA_HEADS):
        c0 = 2 * hh * LANES
        vda_ref[:, c0:c0 + LANES] = z[:, C_DAV + hh * LANES:C_DAV + (hh + 1) * LANES].astype(BF16)
        vda_ref[:, c0 + LANES:c0 + 2 * LANES] = ones
        vm_ref[:, c0:c0 + LANES] = kvf[:, 512 + hh * LANES:512 + (hh + 1) * LANES].astype(BF16)
        vm_ref[:, c0 + LANES:c0 + 2 * LANES] = ones
    pool_ref[...] = z[:, C_POOL:C_POOL + 512]


def _inproj(x, modrows, g_mix, w2, gq, wuq, gkv, wukv, tabs, tm):
    b, s, d = x.shape
    tm = min(tm, s)
    cos, sa, sb = tabs
    row = lambda j: pl.BlockSpec((None, 1, d), lambda bi, i: (bi, 0, j))
    tab = pl.BlockSpec((tm, LANES), lambda bi, i: (i, 0))
    out = lambda w, dt: jax.ShapeDtypeStruct((b, s, w), dt)
    ospec = lambda w: pl.BlockSpec((None, tm, w), lambda bi, i: (bi, i, 0))
    return pl.pallas_call(
        _inproj_kernel,
        out_shape=(out(512, BF16), out(512, BF16), out(1024, BF16), out(1024, BF16), out(1024, BF16),
                   out(1024, BF16), out(512, F32)),
        grid=(b, s // tm),
        in_specs=[pl.BlockSpec((None, tm, d), lambda bi, i: (bi, i, 0)),
                  _const_spec((1, d)), row(0), row(1),
                  _const_spec(w2.shape), _const_spec(gq.shape), _const_spec(wuq.shape),
                  _const_spec(gkv.shape), _const_spec(wukv.shape), tab, tab, tab],
        out_specs=(ospec(512), ospec(512), ospec(1024), ospec(1024), ospec(1024), ospec(1024), ospec(512)),
        compiler_params=_params(("arbitrary", "arbitrary")),
        name="in_proj",
    )(x, g_mix, modrows, modrows, w2, gq, wuq, gkv, wukv, cos, sa, sb)


def _attn_kernel(*refs, diff, has_lat, tk, tq, lam_init):
    it = iter(refs)
    q_ref, kc_ref, vc_ref = next(it), next(it), next(it)
    kl_ref = vl_ref = lam_ref = g_ref = None
    if has_lat:
        kl_ref, vl_ref = next(it), next(it)
    if diff:
        lam_ref, g_ref = next(it), next(it)
    o_ref = next(it)
    m_ref, acc_ref = next(it), next(it)
    s_refs = (next(it), next(it))
    p_refs = (next(it), next(it))
    al_refs = (next(it), next(it))
    n_qt = q_ref.shape[0] // tq
    dv = o_ref.shape[-1]

    def stacked_q(qt):
        q = q_ref[qt * tq:(qt + 1) * tq, :]
        if not diff:
            return q
        lane = lax.broadcasted_iota(jnp.int32, q.shape, 1)
        zero = jnp.zeros_like(q)
        return jnp.concatenate([jnp.where(lane < DA_QK, q, zero), jnp.where(lane >= DA_QK, q, zero)], axis=0)

    qs = [stacked_q(qt) for qt in range(n_qt)]
    n_ctx = kc_ref.shape[0]
    n_lat = kl_ref.shape[0] // tk if has_lat else 0
    width = lambda c: n_ctx if c == 0 else tk
    k_of = lambda c: kc_ref[...] if c == 0 else kl_ref[(c - 1) * tk:c * tk, :]
    v_of = lambda c: vc_ref[...] if c == 0 else vl_ref[(c - 1) * tk:c * tk, :]

    def qk(slot, qt, c):
        k = k_of(c)
        s_refs[slot][:, :k.shape[0]] = lax.dot_general(qs[qt], k, (((1,), (1,)), ((), ())),
                                                       preferred_element_type=F32)

    def sm(slot, qt, c):
        s_ref, p_ref = s_refs[slot], p_refs[slot]
        blocks = [slice(j * LANES, (j + 1) * LANES) for j in range(width(c) // LANES)]
        mx = s_ref[:, blocks[0]]
        for blk in blocks[1:]:
            mx = jnp.maximum(mx, s_ref[:, blk])
        m = m_ref[qt]
        m_new = jnp.maximum(m, jnp.max(mx, axis=-1, keepdims=True))
        al_refs[slot][...] = jnp.exp2(m - m_new)
        m_ref[qt] = m_new
        for blk in blocks:
            p_ref[:, blk] = jnp.exp2(s_ref[:, blk] - m_new).astype(BF16)

    def pv(slot, qt, c):
        v = v_of(c)
        new = jnp.dot(p_refs[slot][:, :v.shape[0]], v, preferred_element_type=F32)
        al = al_refs[slot][...]
        for blk in (slice(0, dv), slice(dv, 2 * dv)):
            acc_ref[qt, :, blk] = al * acc_ref[qt, :, blk] + new[:, blk]

    def finish(qt):
        acc = acc_ref[qt]
        o = acc[:, :dv] / acc[:, dv:]
        if diff:
            lv = lam_ref[...]
            a = jnp.sum(lv[0:1, :] * lv[1:2, :], axis=-1, keepdims=True)
            b = jnp.sum(lv[2:3, :] * lv[3:4, :], axis=-1, keepdims=True)
            lam = jnp.exp(a) - jnp.exp(b) + lam_init
            o = o[:tq] - lam * o[tq:]
            o = _rms(o, g_ref[...]) * (1.0 - lam_init)
        o_ref[qt * tq:(qt + 1) * tq, :] = o.astype(o_ref.dtype)

    m_ref[...] = jnp.full(m_ref.shape, NEG_BIG, F32)
    acc_ref[...] = jnp.zeros(acc_ref.shape, F32)
    items = [(qt, c) for qt in range(n_qt) for c in range(n_lat + 1)]
    for g in range(len(items) + 2):
        if g < len(items):
            qk(g % 2, *items[g])
        if 0 <= g - 2 < len(items):
            pv(g % 2, *items[g - 2])
        if 0 <= g - 1 < len(items):
            sm((g - 1) % 2, *items[g - 1])
        if 0 <= g - 2 < len(items) and items[g - 2][1] == n_lat:
            finish(items[g - 2][0])


def _attention(q, kc, vc, kl, vl, heads, dq, dv, tq, tk, diff=False, lam=None, g=None, lam_init=0.0,
               name="attn"):
    b, s, _ = q.shape
    c = kc.shape[1]
    tq = min(tq, s)
    n_qt = min(ATTN_Q_TILES, s // tq)
    has_lat = kl is not None
    args = [q, kc, vc]
    specs = [pl.BlockSpec((None, n_qt * tq, dq), lambda bi, hi, qi: (bi, qi, hi)),
             pl.BlockSpec((None, c, dq), lambda bi, hi, qi: (bi, 0, hi)),
             pl.BlockSpec((None, c, 2 * dv), lambda bi, hi, qi: (bi, 0, hi))]
    if has_lat:
        sl = kl.shape[1]
        tk = min(tk, sl)
        args += [kl, vl]
        specs += [pl.BlockSpec((None, sl, dq), lambda bi, hi, qi: (bi, 0, hi)),
                  pl.BlockSpec((None, sl, 2 * dv), lambda bi, hi, qi: (bi, 0, hi))]
    if diff:
        args += [lam, g]
        specs += [_const_spec(lam.shape), _const_spec(g.shape)]
    kern = functools.partial(_attn_kernel, diff=diff, has_lat=has_lat, tk=tk, tq=tq, lam_init=lam_init)
    rows = 2 * tq if diff else tq
    wmax = max(tk, c) if has_lat else c
    scratch = [pltpu.VMEM((n_qt, rows, LANES), F32), pltpu.VMEM((n_qt, rows, 2 * dv), F32),
               pltpu.VMEM((rows, wmax), F32), pltpu.VMEM((rows, wmax), F32),
               pltpu.VMEM((rows, wmax), BF16), pltpu.VMEM((rows, wmax), BF16),
               pltpu.VMEM((rows, LANES), F32), pltpu.VMEM((rows, LANES), F32)]
    return pl.pallas_call(
        kern,
        out_shape=jax.ShapeDtypeStruct((b, s, heads * dv), BF16),
        grid=(b, heads, s // (n_qt * tq)),
        in_specs=specs,
        out_specs=pl.BlockSpec((None, n_qt * tq, dv), lambda bi, hi, qi: (bi, qi, hi)),
        scratch_shapes=scratch,
        compiler_params=_params(("arbitrary", "arbitrary", "arbitrary")),
        name=name,
    )(*args)


def _pool_kernel(prev_ref, cur_ref, next_ref, w_ref, sc_ref, o_ref, *, seq_len):
    i = pl.program_id(1)
    n = pl.num_programs(1)
    cur = cur_ref[...]
    tm = cur.shape[0]
    prev = jnp.where(i > 0, prev_ref[...], 0.0)
    nxt = jnp.where(i < n - 1, next_ref[...], 0.0)
    ext = jnp.concatenate([prev, cur, nxt], axis=0)
    ne = tm + 16
    t = i * tm + lax.broadcasted_iota(jnp.int32, (tm, 1), 0)
    for g, w in enumerate(POOL_WINDOWS):
        e = ext[:, g * LANES:(g + 1) * LANES]
        acc = e + pltpu.roll(e, 1, 0)
        half = 1
        while 2 * half < w:
            acc = pltpu.roll(acc, half, 0) + pltpu.roll(acc, ne - half, 0)
            half *= 2
        win = acc[8:8 + tm]
        lo = jnp.clip(t - w // 2, 0, seq_len)
        hi = jnp.clip(t - w // 2 + w, 0, seq_len)
        cnt = (hi - lo).astype(F32)
        dlt = win / cnt - cur[:, g * LANES:(g + 1) * LANES]
        y = jnp.dot(dlt.astype(BF16), w_ref[g], preferred_element_type=F32)
        o_ref[:, g * LANES:(g + 1) * LANES] = (y * sc_ref[:, g * LANES:(g + 1) * LANES]).astype(BF16)


def _pool(u, pool_w, pool_scale, tm):
    b, s, w = u.shape
    tm = min(tm, s)
    nb8 = s // 8
    r8 = tm // 8
    return pl.pallas_call(
        functools.partial(_pool_kernel, seq_len=s),
        out_shape=jax.ShapeDtypeStruct((b, s, w), BF16),
        grid=(b, s // tm),
        in_specs=[pl.BlockSpec((None, 8, w), lambda bi, i: (bi, jnp.maximum(i * r8 - 1, 0), 0)),
                  pl.BlockSpec((None, tm, w), lambda bi, i: (bi, i, 0)),
                  pl.BlockSpec((None, 8, w), lambda bi, i: (bi, jnp.minimum((i + 1) * r8, nb8 - 1), 0)),
                  _const_spec(pool_w.shape), _const_spec(pool_scale.shape)],
        out_specs=pl.BlockSpec((None, tm, w), lambda bi, i: (bi, i, 0)),
        compiler_params=_params(("arbitrary", "arbitrary")),
        name="pool_mixer",
    )(u, u, u, pool_w, pool_scale)


def _merge_kernel(*refs, moe, n_exp):
    (x_ref, gmix_ref, sh1_ref, sc1_ref, wgate_ref, oda_ref, omla_ref, opool_ref, wbr_ref, wout_ref,
     gt1_ref, gffn_ref, sh2_ref, sc2_ref) = refs[:14]
    if moe:
        router_ref, xo_ref, h2_ref, gates_ref = refs[14:]
    else:
        xo_ref, h2_ref = refs[14:]
    x = x_ref[...]
    d = x.shape[-1]
    h = _norm_mod(x, gmix_ref[...], sh1_ref[...], sc1_ref[...]).astype(BF16)
    merged = None
    for n, o_ref in enumerate((oda_ref, omla_ref, opool_ref)):
        gate = _sigmoid(jnp.dot(h, wgate_ref[:, n * d:(n + 1) * d], preferred_element_type=F32))
        proj = jnp.dot(o_ref[...], wbr_ref[n], preferred_element_type=F32)
        merged = gate * proj if merged is None else merged + gate * proj
    mix = jnp.dot(merged.astype(BF16), wout_ref[...], preferred_element_type=F32)
    xn = x + gt1_ref[...] * mix
    xo_ref[...] = xn
    h2 = _norm_mod(xn, gffn_ref[...], sh2_ref[...], sc2_ref[...])
    h2_ref[...] = h2.astype(BF16)
    if moe:
        logit = [jnp.sum(h2 * router_ref[e:e + 1, :], axis=-1, keepdims=True) for e in range(n_exp)]

        def top1(vals):
            best, idx = vals[0], jnp.zeros(vals[0].shape, jnp.int32)
            for e in range(1, n_exp):
                better = vals[e] > best
                best = jnp.where(better, vals[e], best)
                idx = jnp.where(better, e, idx)
            return best, idx

        v1, i1 = top1(logit)
        v2, i2 = top1([jnp.where(i1 == e, NEG_BIG, logit[e]) for e in range(n_exp)])
        w1 = 1.0 / (1.0 + jnp.exp(v2 - v1))
        lane = lax.broadcasted_iota(jnp.int32, gates_ref.shape, 1)
        gates_ref[...] = jnp.where(lane == i1, w1, 0.0) + jnp.where(lane == i2, 1.0 - w1, 0.0)


def _merge(x, modrows, g_mix, wgate, o_da, o_mla, o_pool, wbr, wout, g_ffn, router, n_exp, tm):
    b, s, d = x.shape
    tm = min(tm, s)
    moe = router is not None
    row = lambda j: pl.BlockSpec((None, 1, d), lambda bi, i: (bi, 0, j))
    tile = lambda w: pl.BlockSpec((None, tm, w), lambda bi, i: (bi, i, 0))
    args = [x, g_mix, modrows, modrows, wgate, o_da, o_mla, o_pool, wbr, wout, modrows, g_ffn, modrows, modrows]
    specs = [tile(d), _const_spec((1, d)), row(0), row(1), _const_spec(wgate.shape),
             tile(BRANCH_W), tile(BRANCH_W), tile(BRANCH_W), _const_spec(wbr.shape), _const_spec(wout.shape),
             row(2), _const_spec((1, d)), row(3), row(4)]
    out_shape = [jax.ShapeDtypeStruct((b, s, d), F32), jax.ShapeDtypeStruct((b, s, d), BF16)]
    out_specs = [tile(d), tile(d)]
    if moe:
        args.append(router)
        specs.append(_const_spec(router.shape))
        out_shape.append(jax.ShapeDtypeStruct((b, s, LANES), F32))
        out_specs.append(tile(LANES))
    return pl.pallas_call(
        functools.partial(_merge_kernel, moe=moe, n_exp=n_exp),
        out_shape=tuple(out_shape),
        grid=(b, s // tm),
        in_specs=specs,
        out_specs=tuple(out_specs),
        compiler_params=_params(("arbitrary", "arbitrary")),
        name="merge_out",
    )(*args)


def _ffn_kernel(h_ref, x_ref, gt_ref, wg_ref, wu_ref, wd_ref, gfin_ref, o_ref, *, final):
    h = h_ref[...]
    gte = jnp.dot(h, wg_ref[...], preferred_element_type=F32)
    up = jnp.dot(h, wu_ref[...], preferred_element_type=F32)
    act = (gte * _sigmoid(gte) * up).astype(BF16)
    y = jnp.dot(act, wd_ref[...], preferred_element_type=F32)
    xo = x_ref[...] + gt_ref[...] * y
    if final:
        xo = _rms(xo, gfin_ref[...])
    o_ref[...] = xo


def _ffn(h2, x, modrows, wg, wu, wd, g_final, final, tm):
    b, s, d = x.shape
    tm = min(tm, s)
    tile = pl.BlockSpec((None, tm, d), lambda bi, i: (bi, i, 0))
    return pl.pallas_call(
        functools.partial(_ffn_kernel, final=final),
        out_shape=jax.ShapeDtypeStruct((b, s, d), F32),
        grid=(b, s // tm),
        in_specs=[tile, tile, pl.BlockSpec((None, 1, d), lambda bi, i: (bi, 0, 5)),
                  _const_spec(wg.shape), _const_spec(wu.shape), _const_spec(wd.shape), _const_spec((1, d))],
        out_specs=tile,
        compiler_params=_params(("arbitrary", "arbitrary")),
        name="ffn_swiglu",
    )(h2, x, modrows, wg, wu, wd, g_final)


MOE_VMEM_LIMIT = 60 * 1024 * 1024
ROUTE_SUB = 256
MOE_ROWS = 256


def _route_kernel(g_ref, rank_ref, rankt_ref, cnt_ref, *, n_exp):
    t = g_ref.shape[0]
    r_io = lax.broadcasted_iota(jnp.int32, (ROUTE_SUB, ROUTE_SUB), 0)
    c_io = lax.broadcasted_iota(jnp.int32, (ROUTE_SUB, ROUTE_SUB), 1)
    tri = jnp.where(c_io <= r_io, 1.0, 0.0).astype(BF16)
    carry = jnp.zeros((1, LANES), F32)
    for j in range(t // ROUTE_SUB):
        rows = slice(j * ROUTE_SUB, (j + 1) * ROUTE_SUB)
        routed = g_ref[rows, :] != 0.0
        incl = jnp.dot(tri, jnp.where(routed, 1.0, 0.0).astype(BF16), preferred_element_type=F32)
        rank_ref[rows, :] = jnp.where(routed, incl - 1.0 + carry, -1.0)
        carry = carry + incl[ROUTE_SUB - 1:ROUTE_SUB, :]
    rankt_ref[...] = rank_ref[...].T[0:n_exp, :]
    cnt_ref[...] = jnp.broadcast_to(carry, cnt_ref.shape).astype(jnp.int32)


def _route(gates, n_exp, tm):
    n = gates.shape[0]
    return pl.pallas_call(
        functools.partial(_route_kernel, n_exp=n_exp),
        out_shape=(jax.ShapeDtypeStruct((n, LANES), F32),
                   jax.ShapeDtypeStruct((n // tm, n_exp, tm), F32),
                   jax.ShapeDtypeStruct((n // tm, 8, LANES), jnp.int32)),
        grid=(n // tm,),
        in_specs=[pl.BlockSpec((tm, LANES), lambda i: (i, 0))],
        out_specs=(pl.BlockSpec((tm, LANES), lambda i: (i, 0)),
                   pl.BlockSpec((None, n_exp, tm), lambda i: (i, 0, 0)),
                   pl.BlockSpec((None, 8, LANES), lambda i: (i, 0, 0))),
        compiler_params=_params(("arbitrary",)),
        name="moe_route",
    )(gates)


def _moe_kernel(cnt_ref, h_ref, rank_ref, rankt_ref, gates_ref, wg_ref, wu_ref, wd_ref, o_ref,
                xg_ref, yg_ref, *, n_exp):
    i, e, f = pl.program_id(0), pl.program_id(1), pl.program_id(2)
    nf = pl.num_programs(2)
    t = h_ref.shape[0]
    n_rows = cnt_ref[i * n_exp + e]
    n_blk = lax.shift_right_logical(n_rows + (MOE_ROWS - 1), int(math.log2(MOE_ROWS)))

    @pl.when(jnp.logical_and(e == 0, f == 0))
    def _():
        o_ref[...] = jnp.zeros_like(o_ref)

    lane = lax.broadcasted_iota(jnp.int32, (t, LANES), 1)
    rank_row = rankt_ref[pl.ds(e, 1), :]
    rank_col = jnp.sum(jnp.where(lane == e, rank_ref[...], 0.0), axis=-1, keepdims=True)
    gate_col = jnp.sum(jnp.where(lane == e, gates_ref[...], 0.0), axis=-1, keepdims=True)

    def block(b, carry):
        base = pl.multiple_of(b * MOE_ROWS, MOE_ROWS)
        rows = pl.ds(base, MOE_ROWS)

        @pl.when(f == 0)
        def _():
            slot = (base + lax.broadcasted_iota(jnp.int32, (MOE_ROWS, t), 0)).astype(F32)
            onehot = jnp.where(rank_row == slot, 1.0, 0.0).astype(BF16)
            xg_ref[rows, :] = jnp.dot(onehot, h_ref[...], preferred_element_type=F32).astype(BF16)

        xb = xg_ref[rows, :]
        gte = jnp.dot(xb, wg_ref[...], preferred_element_type=F32)
        up = jnp.dot(xb, wu_ref[...], preferred_element_type=F32)
        act = (gte * _sigmoid(gte) * up).astype(BF16)
        part = jnp.dot(act, wd_ref[...], preferred_element_type=F32)

        @pl.when(f == 0)
        def _():
            yg_ref[rows, :] = part

        @pl.when(f > 0)
        def _():
            yg_ref[rows, :] += part

        @pl.when(f == nf - 1)
        def _():
            slot = (base + lax.broadcasted_iota(jnp.int32, (t, MOE_ROWS), 1)).astype(F32)
            onehot = jnp.where(rank_col == slot, 1.0, 0.0).astype(BF16)
            o_ref[...] += gate_col * jnp.dot(onehot, yg_ref[rows, :].astype(BF16),
                                             preferred_element_type=F32)
        return carry

    lax.fori_loop(0, n_blk, block, 0)


def _moe_routed(h2, gates, wg, wu, wd, tm, tf):
    n, d = h2.shape
    n_exp, _, dff = wg.shape
    tm = min(tm, n)
    rank, rank_t, cnt = _route(gates, n_exp, tm)
    counts = cnt[:, 0, :n_exp].reshape(-1)
    grid_spec = pltpu.PrefetchScalarGridSpec(
        num_scalar_prefetch=1,
        grid=(n // tm, n_exp, dff // tf),
        in_specs=[pl.BlockSpec((tm, d), lambda i, e, f, c: (i, 0)),
                  pl.BlockSpec((tm, LANES), lambda i, e, f, c: (i, 0)),
                  pl.BlockSpec((None, n_exp, tm), lambda i, e, f, c: (i, 0, 0)),
                  pl.BlockSpec((tm, LANES), lambda i, e, f, c: (i, 0)),
                  pl.BlockSpec((None, d, tf), lambda i, e, f, c: (e, 0, f)),
                  pl.BlockSpec((None, d, tf), lambda i, e, f, c: (e, 0, f)),
                  pl.BlockSpec((None, tf, d), lambda i, e, f, c: (e, f, 0))],
        out_specs=pl.BlockSpec((tm, d), lambda i, e, f, c: (i, 0)),
        scratch_shapes=[pltpu.VMEM((tm, d), BF16), pltpu.VMEM((tm, d), F32)])
    return pl.pallas_call(
        functools.partial(_moe_kernel, n_exp=n_exp),
        out_shape=jax.ShapeDtypeStruct((n, d), F32),
        grid_spec=grid_spec,
        compiler_params=_params(("arbitrary",) * 3, MOE_VMEM_LIMIT),
        name="moe_swiglu",
    )(counts, h2, rank, rank_t, gates, wg, wu, wd)


def _residual_kernel(x_ref, y_ref, gt_ref, gfin_ref, o_ref, *, final):
    xo = x_ref[...] + gt_ref[...] * y_ref[...]
    if final:
        xo = _rms(xo, gfin_ref[...])
    o_ref[...] = xo


def _residual(x, y, modrows, g_final, final, tm):
    b, s, d = x.shape
    tm = min(tm, s)
    tile = pl.BlockSpec((None, tm, d), lambda bi, i: (bi, i, 0))
    return pl.pallas_call(
        functools.partial(_residual_kernel, final=final),
        out_shape=jax.ShapeDtypeStruct((b, s, d), F32),
        grid=(b, s // tm),
        in_specs=[tile, tile, pl.BlockSpec((None, 1, d), lambda bi, i: (bi, 0, 5)), _const_spec((1, d))],
        out_specs=tile,
        compiler_params=_params(("arbitrary", "arbitrary")),
        name="moe_residual",
    )(x, y, modrows, g_final)


def _moe(h2, x, modrows, gates, wg, wu, wd, g_final, final):
    b, s, d = x.shape
    y = _moe_routed(h2.reshape(b * s, d), gates.reshape(b * s, LANES), wg, wu, wd, 2048, 896)
    return _residual(x, y.reshape(b, s, d), modrows, g_final, final, 512)


def _rope_tables(seq):
    axis_dim = DA_QK // 2
    n_freq = axis_dim // 2
    inv = jnp.exp(-math.log(ROPE_BASE) * jnp.arange(n_freq, dtype=F32) * (2.0 / axis_dim))
    t = jnp.arange(seq, dtype=jnp.int32)
    ar = (t // GRID_W).astype(F32)[:, None] * inv
    ac = (t % GRID_W).astype(F32)[:, None] * inv
    cos = jnp.concatenate([jnp.cos(ar), jnp.cos(ar), jnp.cos(ac), jnp.cos(ac)], axis=-1)
    sin = jnp.concatenate([jnp.sin(ar), jnp.sin(ar), jnp.sin(ac), jnp.sin(ac)], axis=-1)
    first = (jnp.arange(DA_QK) % axis_dim) < n_freq
    sa = jnp.where(first, -sin, 0.0)
    sb = jnp.where(first, 0.0, sin)
    rep = lambda a: jnp.tile(a, (1, LANES // DA_QK))
    return rep(cos), rep(sa), rep(sb)


def _identity_tables(seq):
    return jnp.ones((seq, LANES), F32), jnp.zeros((seq, LANES), F32), jnp.zeros((seq, LANES), F32)


def _pack_layer_weights(w_in, w_uq, w_ukv):
    d = w_in.shape[0]
    zpad = jnp.zeros((d, LANES - MLA_ROPE), w_in.dtype)
    o_kr = 3 * 512 + MLA_Q_RANK + MLA_KV_RANK
    o_pool = o_kr + MLA_ROPE
    w2 = jnp.concatenate([w_in[:, :o_kr + MLA_ROPE], zpad, w_in[:, o_pool:o_pool + 512]], axis=1).astype(BF16)
    wgate = w_in[:, o_pool + 512:].astype(BF16)
    hq = MLA_NOPE + MLA_ROPE
    wq = w_uq.reshape(MLA_Q_RANK, MLA_HEADS, hq)
    wq = jnp.concatenate([wq, jnp.zeros((MLA_Q_RANK, MLA_HEADS, MLA_QK_PAD - hq), w_uq.dtype)], axis=-1)
    wuq = wq.reshape(MLA_Q_RANK, MLA_HEADS * MLA_QK_PAD).astype(BF16)
    wkv = w_ukv.reshape(MLA_KV_RANK, MLA_HEADS, MLA_NOPE + MLA_V)
    wukv = jnp.concatenate([wkv[:, :, :MLA_NOPE].reshape(MLA_KV_RANK, -1),
                            wkv[:, :, MLA_NOPE:].reshape(MLA_KV_RANK, -1)], axis=1).astype(BF16)
    return w2, wgate, wuq, wukv


def kernel(x, c, ctx, c_ctx, w_mod, b_mod, g_mix, w_in, da_lambda, da_subln, mla_gq, w_uq, mla_gkv, w_ukv,
           pool_w, pool_scale, w_branch, w_out, g_ffn, ffn_w_gate, ffn_w_up, ffn_w_down, moe_router,
           moe_w_gate, moe_w_up, moe_w_down, g_final):
    bsz, seq, d = x.shape
    n_ctx = ctx.shape[1]
    depth = w_mod.shape[0]

    cond8 = jnp.zeros((8, d), F32).at[:bsz].set(c).at[bsz].set(c_ctx)
    mod = _mod_rows(cond8, w_mod, b_mod)
    lat_tabs = _rope_tables(seq)
    ctx_tabs = _identity_tables(n_ctx)
    g_fin = g_final.reshape(1, d)

    xc = ctx
    for l in range(depth):
        need_ctx = l < depth - 1
        lam_init = 0.8 - 0.6 * math.exp(-0.3 * l)
        mod_lat = mod[l, :bsz][:, None, :]
        mod_ctx = jnp.broadcast_to(mod[l, bsz][None, None, :], (bsz, 1, N_MOD * d))
        w2, wgate, wuq, wukv = _pack_layer_weights(w_in[l], w_uq[l], w_ukv[l])
        gmix = g_mix[l].reshape(1, d)
        gffn = g_ffn[l].reshape(1, d)
        gq = mla_gq[l].reshape(1, -1)
        gkv = mla_gkv[l].reshape(1, -1)
        subln = da_subln[l].reshape(1, -1)
        lam = da_lambda[l]
        pw = pool_w[l].astype(BF16)
        psc = pool_scale[l].reshape(1, -1)
        wbr = w_branch[l].astype(BF16)
        wout = w_out[l].astype(BF16)
        j = l // 2
        dense = l % 2 == 0
        final = l == depth - 1

        qda, kda, vda, qm, km, vm, pin = _inproj(x, mod_lat, gmix, w2, gq, wuq, gkv, wukv, lat_tabs, 256)
        cqda, ckda, cvda, cqm, ckm, cvm, cpin = _inproj(xc, mod_ctx, gmix, w2, gq, wuq, gkv, wukv, ctx_tabs, 256)

        o_da = _attention(qda, ckda, cvda, kda, vda, DA_HEADS, LANES, DA_V, 256, 512, diff=True,
                          lam=lam, g=subln, lam_init=lam_init, name="diff_attn")
        o_mla = _attention(qm, ckm, cvm, km, vm, MLA_HEADS, MLA_QK_PAD, MLA_V, 512, 512, name="mla_attn")
        o_pool = _pool(pin, pw, psc, 512)

        router = None
        n_exp = moe_router.shape[-1]
        if not dense:
            router = moe_router[j].T
        outs = _merge(x, mod_lat, gmix, wgate, o_da, o_mla, o_pool, wbr, wout, gffn, router, n_exp, 256)

        if dense:
            wg = ffn_w_gate[j].astype(BF16)
            wu = ffn_w_up[j].astype(BF16)
            wd = ffn_w_down[j].astype(BF16)
            x = _ffn(outs[1], outs[0], mod_lat, wg, wu, wd, g_fin, final, 256)
        else:
            wg = moe_w_gate[j].astype(BF16)
            wu = moe_w_up[j].astype(BF16)
            wd = moe_w_down[j].astype(BF16)
            x = _moe(outs[1], outs[0], mod_lat, outs[2], wg, wu, wd, g_fin, final)

        if need_ctx:
            co_da = _attention(cqda, ckda, cvda, None, None, DA_HEADS, LANES, DA_V, 256, 512, diff=True,
                               lam=lam, g=subln, lam_init=lam_init, name="diff_attn_ctx")
            co_mla = _attention(cqm, ckm, cvm, None, None, MLA_HEADS, MLA_QK_PAD, MLA_V, 256, 512,
                                name="mla_attn_ctx")
            co_pool = _pool(cpin, pw, psc, 512)
            couts = _merge(xc, mod_ctx, gmix, wgate, co_da, co_mla, co_pool, wbr, wout, gffn, router, n_exp, 256)
            if dense:
                xc = _ffn(couts[1], couts[0], mod_ctx, wg, wu, wd, g_fin, False, 256)
            else:
                xc = _moe(couts[1], couts[0], mod_ctx, couts[2], wg, wu, wd, g_fin, False)
    return x
```

```python
import functools
import math

import jax
import jax.numpy as jnp
from jax import lax
from jax.experimental import pallas as pl
from jax.experimental.pallas import tpu as pltpu

F32 = jnp.float32
BF16 = jnp.bfloat16

GRID_W = 64
DA_HEADS = 4
DA_QK = 64
DA_V = 128
MLA_HEADS = 4
MLA_NOPE = 128
MLA_ROPE = 64
MLA_V = 128
MLA_Q_RANK = 384
MLA_KV_RANK = 256
POOL_WINDOWS = (2, 4, 8, 16)
POOL_GROUP_W = 128
N_BRANCH = 3
BRANCH_W = 512
ROPE_BASE = 10000.0
EPS = 1e-6
N_MOD = 6
LANES = 128
MLA_QK_PAD = 256
LOG2E = math.log2(math.e)
DA_SCALE = DA_QK ** -0.5 * LOG2E
MLA_SCALE = (MLA_NOPE + MLA_ROPE) ** -0.5 * LOG2E
NEG_BIG = -1e30
ATTN_Q_TILES = 1
VMEM_LIMIT = 56 * 1024 * 1024

C_DAQ, C_DAK, C_DAV = 0, 512, 1024
C_QD = 1536
C_KVD = C_QD + MLA_Q_RANK
C_KR = C_KVD + MLA_KV_RANK
C_POOL = C_KR + LANES
W2_COLS = C_POOL + 512


def _sigmoid(v):
    return 1.0 / (1.0 + jnp.exp(-v))


def _params(sem, vmem=VMEM_LIMIT):
    return pltpu.CompilerParams(dimension_semantics=sem, vmem_limit_bytes=vmem)


def _const_spec(shape):
    nd = len(shape)
    return pl.BlockSpec(shape, lambda *_: (0,) * nd)


def _mod_kernel(cond_ref, w_ref, b_ref, o_ref):
    c = cond_ref[...]
    s = c * _sigmoid(c)
    o_ref[...] = jnp.dot(s, w_ref[...], preferred_element_type=F32,
                         precision=lax.Precision.HIGHEST) + b_ref[...]


def _mod_rows(cond8, w_mod, b_mod):
    depth, d, n = w_mod.shape
    tn = 1536
    return pl.pallas_call(
        _mod_kernel,
        out_shape=jax.ShapeDtypeStruct((depth, 8, n), F32),
        grid=(depth, n // tn),
        in_specs=[pl.BlockSpec((8, d), lambda l, j: (0, 0)),
                  pl.BlockSpec((None, d, tn), lambda l, j: (l, 0, j)),
                  pl.BlockSpec((None, 1, tn), lambda l, j: (l, 0, j))],
        out_specs=pl.BlockSpec((None, 8, tn), lambda l, j: (l, 0, j)),
        compiler_params=_params(("arbitrary", "arbitrary")),
        name="adaln_rows",
    )(cond8, w_mod, b_mod.reshape(depth, 1, n))


def _norm_mod(x, g, sh, sc):
    r = lax.rsqrt(jnp.mean(x * x, axis=-1, keepdims=True) + EPS)
    return (x * r * g) * (1.0 + sc) + sh


def _rms(v, g):
    return v * lax.rsqrt(jnp.mean(v * v, axis=-1, keepdims=True) + EPS) * g


def _inproj_kernel(x_ref, g_ref, sh_ref, sc_ref, w_ref, gq_ref, wuq_ref, gkv_ref, wukv_ref,
                   cos_ref, sa_ref, sb_ref,
                   qda_ref, kda_ref, vda_ref, qm_ref, km_ref, vm_ref, pool_ref):
    h = _norm_mod(x_ref[...], g_ref[...], sh_ref[...], sc_ref[...])
    z = jnp.dot(h.astype(BF16), w_ref[...], preferred_element_type=F32)
    cos = cos_ref[...]
    sa = sa_ref[...]
    sb = sb_ref[...]

    def rope(blk):
        return blk * cos + pltpu.roll(blk, LANES - 16, 1) * sa + pltpu.roll(blk, 16, 1) * sb

    lo = lax.broadcasted_iota(jnp.int32, cos.shape, 1) < MLA_ROPE

    for hh in range(DA_HEADS):
        c0 = hh * LANES
        qda_ref[:, c0:c0 + LANES] = (rope(z[:, C_DAQ + c0:C_DAQ + c0 + LANES]) * DA_SCALE).astype(BF16)
        kda_ref[:, c0:c0 + LANES] = rope(z[:, C_DAK + c0:C_DAK + c0 + LANES]).astype(BF16)

    qn = _rms(z[:, C_QD:C_QD + MLA_Q_RANK], gq_ref[...])
    qf = jnp.dot(qn.astype(BF16), wuq_ref[...], preferred_element_type=F32)
    kvn = _rms(z[:, C_KVD:C_KVD + MLA_KV_RANK], gkv_ref[...])
    kvf = jnp.dot(kvn.astype(BF16), wukv_ref[...], preferred_element_type=F32)
    kr = jnp.where(lo, rope(z[:, C_KR:C_KR + LANES]), 0.0).astype(BF16)
    for hh in range(MLA_HEADS):
        c0 = hh * MLA_QK_PAD
        qm_ref[:, c0:c0 + LANES] = (qf[:, c0:c0 + LANES] * MLA_SCALE).astype(BF16)
        qr = jnp.where(lo, rope(qf[:, c0 + LANES:c0 + 2 * LANES]), 0.0)
        qm_ref[:, c0 + LANES:c0 + 2 * LANES] = (qr * MLA_SCALE).astype(BF16)
        km_ref[:, c0:c0 + LANES] = kvf[:, hh * LANES:(hh + 1) * LANES].astype(BF16)
        km_ref[:, c0 + LANES:c0 + 2 * LANES] = kr
    ones = jnp.ones(cos.shape, BF16)
    for hh in range(MLA_HEADS):
        c0 = 2 * hh * LANES
        vda_ref[:, c0:c0 + LANES] = z[:, C_DAV + hh * LANES:C_DAV + (hh + 1) * LANES].astype(BF16)
        vda_ref[:, c0 + LANES:c0 + 2 * LANES] = ones
        vm_ref[:, c0:c0 + LANES] = kvf[:, 512 + hh * LANES:512 + (hh + 1) * LANES].astype(BF16)
        vm_ref[:, c0 + LANES:c0 + 2 * LANES] = ones
    pool_ref[...] = z[:, C_POOL:C_POOL + 512]


def _inproj(x, modrows, g_mix, w2, gq, wuq, gkv, wukv, tabs, tm):
    b, s, d = x.shape
    tm = min(tm, s)
    cos, sa, sb = tabs
    row = lambda j: pl.BlockSpec((None, 1, d), lambda bi, i: (bi, 0, j))
    tab = pl.BlockSpec((tm, LANES), lambda bi, i: (i, 0))
    out = lambda w, dt: jax.ShapeDtypeStruct((b, s, w), dt)
    ospec = lambda w: pl.BlockSpec((None, tm, w), lambda bi, i: (bi, i, 0))
    return pl.pallas_call(
        _inproj_kernel,
        out_shape=(out(512, BF16), out(512, BF16), out(1024, BF16), out(1024, BF16), out(1024, BF16),
                   out(1024, BF16), out(512, F32)),
        grid=(b, s // tm),
        in_specs=[pl.BlockSpec((None, tm, d), lambda bi, i: (bi, i, 0)),
                  _const_spec((1, d)), row(0), row(1),
                  _const_spec(w2.shape), _const_spec(gq.shape), _const_spec(wuq.shape),
                  _const_spec(gkv.shape), _const_spec(wukv.shape), tab, tab, tab],
        out_specs=(ospec(512), ospec(512), ospec(1024), ospec(1024), ospec(1024), ospec(1024), ospec(512)),
        compiler_params=_params(("arbitrary", "arbitrary")),
        name="in_proj",
    )(x, g_mix, modrows, modrows, w2, gq, wuq, gkv, wukv, cos, sa, sb)


def _attn_kernel(*refs, diff, has_lat, tk, tq, lam_init):
    it = iter(refs)
    q_ref, kc_ref, vc_ref = next(it), next(it), next(it)
    kl_ref = vl_ref = lam_ref = g_ref = None
    if has_lat:
        kl_ref, vl_ref = next(it), next(it)
    if diff:
        lam_ref, g_ref = next(it), next(it)
    o_ref = next(it)
    m_ref, acc_ref = next(it), next(it)
    s_refs = (next(it), next(it))
    p_refs = (next(it), next(it))
    al_refs = (next(it), next(it))
    n_qt = q_ref.shape[0] // tq
    dv = o_ref.shape[-1]

    def stacked_q(qt):
        q = q_ref[qt * tq:(qt + 1) * tq, :]
        if not diff:
            return q
        lane = lax.broadcasted_iota(jnp.int32, q.shape, 1)
        zero = jnp.zeros_like(q)
        return jnp.concatenate([jnp.where(lane < DA_QK, q, zero), jnp.where(lane >= DA_QK, q, zero)], axis=0)

    qs = [stacked_q(qt) for qt in range(n_qt)]
    q_of = lambda qt: qs[qt]
    n_ctx = kc_ref.shape[0]
    n_lat = kl_ref.shape[0] // tk if has_lat else 0
    width = lambda c: n_ctx if c == 0 else tk
    k_of = lambda c: kc_ref[...] if c == 0 else kl_ref[(c - 1) * tk:c * tk, :]
    v_of = lambda c: vc_ref[...] if c == 0 else vl_ref[(c - 1) * tk:c * tk, :]

    def qk(slot, qt, c):
        k = k_of(c)
        s_refs[slot][:, :k.shape[0]] = lax.dot_general(q_of(qt), k, (((1,), (1,)), ((), ())),
                                                       preferred_element_type=F32)

    def sm(slot, qt, c):
        s_ref, p_ref = s_refs[slot], p_refs[slot]
        blocks = [slice(j * LANES, (j + 1) * LANES) for j in range(width(c) // LANES)]
        mx = s_ref[:, blocks[0]]
        for blk in blocks[1:]:
            mx = jnp.maximum(mx, s_ref[:, blk])
        m = m_ref[qt]
        m_new = jnp.maximum(m, jnp.max(mx, axis=-1, keepdims=True))
        al_refs[slot][...] = jnp.exp2(m - m_new)
        m_ref[qt] = m_new
        for blk in blocks:
            p_ref[:, blk] = jnp.exp2(s_ref[:, blk] - m_new).astype(BF16)

    def pv(slot, qt, c):
        v = v_of(c)
        new = jnp.dot(p_refs[slot][:, :v.shape[0]], v, preferred_element_type=F32)
        al = al_refs[slot][...]
        for blk in (slice(0, dv), slice(dv, 2 * dv)):
            acc_ref[qt, :, blk] = al * acc_ref[qt, :, blk] + new[:, blk]

    def finish(qt):
        acc = acc_ref[qt]
        o = acc[:, :dv] / acc[:, dv:]
        if diff:
            lv = lam_ref[...]
            a = jnp.sum(lv[0:1, :] * lv[1:2, :], axis=-1, keepdims=True)
            b = jnp.sum(lv[2:3, :] * lv[3:4, :], axis=-1, keepdims=True)
            lam = jnp.exp(a) - jnp.exp(b) + lam_init
            o = o[:tq] - lam * o[tq:]
            o = _rms(o, g_ref[...]) * (1.0 - lam_init)
        o_ref[qt * tq:(qt + 1) * tq, :] = o.astype(o_ref.dtype)

    m_ref[...] = jnp.full(m_ref.shape, NEG_BIG, F32)
    acc_ref[...] = jnp.zeros(acc_ref.shape, F32)
    items = [(qt, c) for qt in range(n_qt) for c in range(n_lat + 1)]
    for g in range(len(items) + 2):
        if g < len(items):
            qk(g % 2, *items[g])
        if 0 <= g - 2 < len(items):
            pv(g % 2, *items[g - 2])
        if 0 <= g - 1 < len(items):
            sm((g - 1) % 2, *items[g - 1])
        if 0 <= g - 2 < len(items) and items[g - 2][1] == n_lat:
            finish(items[g - 2][0])


def _attention(q, kc, vc, kl, vl, heads, dq, dv, tq, tk, diff=False, lam=None, g=None, lam_init=0.0,
               name="attn"):
    b, s, _ = q.shape
    c = kc.shape[1]
    tq = min(tq, s)
    n_qt = min(ATTN_Q_TILES, s // tq)
    has_lat = kl is not None
    args = [q, kc, vc]
    specs = [pl.BlockSpec((None, n_qt * tq, dq), lambda bi, hi, qi: (bi, qi, hi)),
             pl.BlockSpec((None, c, dq), lambda bi, hi, qi: (bi, 0, hi)),
             pl.BlockSpec((None, c, 2 * dv), lambda bi, hi, qi: (bi, 0, hi))]
    if has_lat:
        sl = kl.shape[1]
        tk = min(tk, sl)
        args += [kl, vl]
        specs += [pl.BlockSpec((None, sl, dq), lambda bi, hi, qi: (bi, 0, hi)),
                  pl.BlockSpec((None, sl, 2 * dv), lambda bi, hi, qi: (bi, 0, hi))]
    if diff:
        args += [lam, g]
        specs += [_const_spec(lam.shape), _const_spec(g.shape)]
    kern = functools.partial(_attn_kernel, diff=diff, has_lat=has_lat, tk=tk, tq=tq, lam_init=lam_init)
    rows = 2 * tq if diff else tq
    wmax = max(tk, c) if has_lat else c
    scratch = [pltpu.VMEM((n_qt, rows, LANES), F32), pltpu.VMEM((n_qt, rows, 2 * dv), F32),
               pltpu.VMEM((rows, wmax), F32), pltpu.VMEM((rows, wmax), F32),
               pltpu.VMEM((rows, wmax), BF16), pltpu.VMEM((rows, wmax), BF16),
               pltpu.VMEM((rows, LANES), F32), pltpu.VMEM((rows, LANES), F32)]
    return pl.pallas_call(
        kern,
        out_shape=jax.ShapeDtypeStruct((b, s, heads * dv), BF16),
        grid=(b, heads, s // (n_qt * tq)),
        in_specs=specs,
        out_specs=pl.BlockSpec((None, n_qt * tq, dv), lambda bi, hi, qi: (bi, qi, hi)),
        scratch_shapes=scratch,
        compiler_params=_params(("arbitrary", "arbitrary", "arbitrary")),
        name=name,
    )(*args)


def _pool_kernel(prev_ref, cur_ref, next_ref, w_ref, sc_ref, o_ref, *, seq_len):
    i = pl.program_id(1)
    n = pl.num_programs(1)
    cur = cur_ref[...]
    tm = cur.shape[0]
    prev = jnp.where(i > 0, prev_ref[...], 0.0)
    nxt = jnp.where(i < n - 1, next_ref[...], 0.0)
    ext = jnp.concatenate([prev, cur, nxt], axis=0)
    ne = tm + 16
    t = i * tm + lax.broadcasted_iota(jnp.int32, (tm, 1), 0)
    for g, w in enumerate(POOL_WINDOWS):
        e = ext[:, g * LANES:(g + 1) * LANES]
        acc = e + pltpu.roll(e, 1, 0)
        half = 1
        while 2 * half < w:
            acc = pltpu.roll(acc, half, 0) + pltpu.roll(acc, ne - half, 0)
            half *= 2
        win = acc[8:8 + tm]
        lo = jnp.clip(t - w // 2, 0, seq_len)
        hi = jnp.clip(t - w // 2 + w, 0, seq_len)
        cnt = (hi - lo).astype(F32)
        dlt = win / cnt - cur[:, g * LANES:(g + 1) * LANES]
        y = jnp.dot(dlt.astype(BF16), w_ref[g], preferred_element_type=F32)
        o_ref[:, g * LANES:(g + 1) * LANES] = (y * sc_ref[:, g * LANES:(g + 1) * LANES]).astype(BF16)


def _pool(u, pool_w, pool_scale, tm):
    b, s, w = u.shape
    tm = min(tm, s)
    nb8 = s // 8
    r8 = tm // 8
    return pl.pallas_call(
        functools.partial(_pool_kernel, seq_len=s),
        out_shape=jax.ShapeDtypeStruct((b, s, w), BF16),
        grid=(b, s // tm),
        in_specs=[pl.BlockSpec((None, 8, w), lambda bi, i: (bi, jnp.maximum(i * r8 - 1, 0), 0)),
                  pl.BlockSpec((None, tm, w), lambda bi, i: (bi, i, 0)),
                  pl.BlockSpec((None, 8, w), lambda bi, i: (bi, jnp.minimum((i + 1) * r8, nb8 - 1), 0)),
                  _const_spec(pool_w.shape), _const_spec(pool_scale.shape)],
        out_specs=pl.BlockSpec((None, tm, w), lambda bi, i: (bi, i, 0)),
        compiler_params=_params(("arbitrary", "arbitrary")),
        name="pool_mixer",
    )(u, u, u, pool_w, pool_scale)


def _merge_kernel(*refs, moe, n_exp):
    (x_ref, gmix_ref, sh1_ref, sc1_ref, wgate_ref, oda_ref, omla_ref, opool_ref, wbr_ref, wout_ref,
     gt1_ref, gffn_ref, sh2_ref, sc2_ref) = refs[:14]
    if moe:
        router_ref, xo_ref, h2_ref, gates_ref = refs[14:]
    else:
        xo_ref, h2_ref = refs[14:]
    x = x_ref[...]
    d = x.shape[-1]
    h = _norm_mod(x, gmix_ref[...], sh1_ref[...], sc1_ref[...]).astype(BF16)
    merged = None
    for n, o_ref in enumerate((oda_ref, omla_ref, opool_ref)):
        gate = _sigmoid(jnp.dot(h, wgate_ref[:, n * d:(n + 1) * d], preferred_element_type=F32))
        proj = jnp.dot(o_ref[...], wbr_ref[n], preferred_element_type=F32)
        merged = gate * proj if merged is None else merged + gate * proj
    mix = jnp.dot(merged.astype(BF16), wout_ref[...], preferred_element_type=F32)
    xn = x + gt1_ref[...] * mix
    xo_ref[...] = xn
    h2 = _norm_mod(xn, gffn_ref[...], sh2_ref[...], sc2_ref[...])
    h2_ref[...] = h2.astype(BF16)
    if moe:
        logit = [jnp.sum(h2 * router_ref[e:e + 1, :], axis=-1, keepdims=True) for e in range(n_exp)]

        def top1(vals):
            best, idx = vals[0], jnp.zeros(vals[0].shape, jnp.int32)
            for e in range(1, n_exp):
                better = vals[e] > best
                best = jnp.where(better, vals[e], best)
                idx = jnp.where(better, e, idx)
            return best, idx

        v1, i1 = top1(logit)
        v2, i2 = top1([jnp.where(i1 == e, NEG_BIG, logit[e]) for e in range(n_exp)])
        w1 = 1.0 / (1.0 + jnp.exp(v2 - v1))
        lane = lax.broadcasted_iota(jnp.int32, gates_ref.shape, 1)
        gates_ref[...] = jnp.where(lane == i1, w1, 0.0) + jnp.where(lane == i2, 1.0 - w1, 0.0)


def _merge(x, modrows, g_mix, wgate, o_da, o_mla, o_pool, wbr, wout, g_ffn, router, n_exp, tm):
    b, s, d = x.shape
    tm = min(tm, s)
    moe = router is not None
    row = lambda j: pl.BlockSpec((None, 1, d), lambda bi, i: (bi, 0, j))
    tile = lambda w: pl.BlockSpec((None, tm, w), lambda bi, i: (bi, i, 0))
    args = [x, g_mix, modrows, modrows, wgate, o_da, o_mla, o_pool, wbr, wout, modrows, g_ffn, modrows, modrows]
    specs = [tile(d), _const_spec((1, d)), row(0), row(1), _const_spec(wgate.shape),
             tile(BRANCH_W), tile(BRANCH_W), tile(BRANCH_W), _const_spec(wbr.shape), _const_spec(wout.shape),
             row(2), _const_spec((1, d)), row(3), row(4)]
    out_shape = [jax.ShapeDtypeStruct((b, s, d), F32), jax.ShapeDtypeStruct((b, s, d), BF16)]
    out_specs = [tile(d), tile(d)]
    if moe:
        args.append(router)
        specs.append(_const_spec(router.shape))
        out_shape.append(jax.ShapeDtypeStruct((b, s, LANES), F32))
        out_specs.append(tile(LANES))
    return pl.pallas_call(
        functools.partial(_merge_kernel, moe=moe, n_exp=n_exp),
        out_shape=tuple(out_shape),
        grid=(b, s // tm),
        in_specs=specs,
        out_specs=tuple(out_specs),
        compiler_params=_params(("arbitrary", "arbitrary")),
        name="merge_out",
    )(*args)


def _ffn_kernel(h_ref, x_ref, gt_ref, wg_ref, wu_ref, wd_ref, gfin_ref, o_ref, *, final):
    h = h_ref[...]
    gte = jnp.dot(h, wg_ref[...], preferred_element_type=F32)
    up = jnp.dot(h, wu_ref[...], preferred_element_type=F32)
    act = (gte * _sigmoid(gte) * up).astype(BF16)
    y = jnp.dot(act, wd_ref[...], preferred_element_type=F32)
    xo = x_ref[...] + gt_ref[...] * y
    if final:
        xo = _rms(xo, gfin_ref[...])
    o_ref[...] = xo


def _ffn(h2, x, modrows, wg, wu, wd, g_final, final, tm):
    b, s, d = x.shape
    tm = min(tm, s)
    tile = pl.BlockSpec((None, tm, d), lambda bi, i: (bi, i, 0))
    return pl.pallas_call(
        functools.partial(_ffn_kernel, final=final),
        out_shape=jax.ShapeDtypeStruct((b, s, d), F32),
        grid=(b, s // tm),
        in_specs=[tile, tile, pl.BlockSpec((None, 1, d), lambda bi, i: (bi, 0, 5)),
                  _const_spec(wg.shape), _const_spec(wu.shape), _const_spec(wd.shape), _const_spec((1, d))],
        out_specs=tile,
        compiler_params=_params(("arbitrary", "arbitrary")),
        name="ffn_swiglu",
    )(h2, x, modrows, wg, wu, wd, g_final)


ROUTE_SUB = 256
MOE_TOKENS = 1024
MOE_ROWS = 288
MOE_ROW_ALIGN = 32


def _route_kernel(g_ref, rank_ref, rankt_ref, cnt_ref, *, n_exp):
    t = g_ref.shape[0]
    r_io = lax.broadcasted_iota(jnp.int32, (ROUTE_SUB, ROUTE_SUB), 0)
    c_io = lax.broadcasted_iota(jnp.int32, (ROUTE_SUB, ROUTE_SUB), 1)
    tri = jnp.where(c_io <= r_io, 1.0, 0.0).astype(BF16)
    carry = jnp.zeros((1, LANES), F32)
    for j in range(t // ROUTE_SUB):
        rows = slice(j * ROUTE_SUB, (j + 1) * ROUTE_SUB)
        routed = g_ref[rows, :] != 0.0
        incl = jnp.dot(tri, jnp.where(routed, 1.0, 0.0).astype(BF16), preferred_element_type=F32)
        rank_ref[rows, :] = jnp.where(routed, incl - 1.0 + carry, -1.0)
        carry = carry + incl[ROUTE_SUB - 1:ROUTE_SUB, :]
    rankt_ref[...] = rank_ref[...].T[0:n_exp, :]
    cnt_ref[...] = jnp.broadcast_to(carry, cnt_ref.shape).astype(jnp.int32)


def _route(gates, n_exp, tm):
    n = gates.shape[0]
    return pl.pallas_call(
        functools.partial(_route_kernel, n_exp=n_exp),
        out_shape=(jax.ShapeDtypeStruct((n, LANES), F32),
                   jax.ShapeDtypeStruct((n // tm, n_exp, tm), F32),
                   jax.ShapeDtypeStruct((n // tm, 8, LANES), jnp.int32)),
        grid=(n // tm,),
        in_specs=[pl.BlockSpec((tm, LANES), lambda i: (i, 0))],
        out_specs=(pl.BlockSpec((tm, LANES), lambda i: (i, 0)),
                   pl.BlockSpec((None, n_exp, tm), lambda i: (i, 0, 0)),
                   pl.BlockSpec((None, 8, LANES), lambda i: (i, 0, 0))),
        compiler_params=_params(("arbitrary",)),
        name="moe_route",
    )(gates)


def _moe_kernel(cnt_ref, h_ref, rank_ref, rankt_ref, gates_ref, wg_ref, wu_ref, wd_ref, o_ref,
                xg_ref, yg_ref, *, n_exp):
    i, e, f = pl.program_id(0), pl.program_id(1), pl.program_id(2)
    nf = pl.num_programs(2)
    t = h_ref.shape[0]
    n_blk = cnt_ref[i * n_exp + e]

    @pl.when(jnp.logical_and(e == 0, f == 0))
    def _():
        o_ref[...] = jnp.zeros_like(o_ref)

    rank_row = rankt_ref[pl.ds(e, 1), :]

    def block(b, carry):
        base = pl.multiple_of(b * MOE_ROWS, MOE_ROW_ALIGN)
        rows = pl.ds(base, MOE_ROWS)

        @pl.when(f == 0)
        def _():
            slot = (base + lax.broadcasted_iota(jnp.int32, (MOE_ROWS, t), 0)).astype(F32)
            onehot = jnp.where(rank_row == slot, 1.0, 0.0).astype(BF16)
            xg_ref[rows, :] = jnp.dot(onehot, h_ref[...], preferred_element_type=F32).astype(BF16)

        xb = xg_ref[rows, :]
        gte = jnp.dot(xb, wg_ref[...], preferred_element_type=F32)
        up = jnp.dot(xb, wu_ref[...], preferred_element_type=F32)
        act = (gte * _sigmoid(gte) * up).astype(BF16)
        part = jnp.dot(act, wd_ref[...], preferred_element_type=F32)

        @pl.when(f == 0)
        def _():
            yg_ref[rows, :] = part

        @pl.when(f > 0)
        def _():
            yg_ref[rows, :] += part

        @pl.when(f == nf - 1)
        def _():
            lane = lax.broadcasted_iota(jnp.int32, (t, LANES), 1)
            rank_col = jnp.sum(jnp.where(lane == e, rank_ref[...], 0.0), axis=-1, keepdims=True)
            gate_col = jnp.sum(jnp.where(lane == e, gates_ref[...], 0.0), axis=-1, keepdims=True)
            slot = (base + lax.broadcasted_iota(jnp.int32, (t, MOE_ROWS), 1)).astype(F32)
            onehot = jnp.where(rank_col == slot, 1.0, 0.0).astype(BF16)
            o_ref[...] += gate_col * jnp.dot(onehot, yg_ref[rows, :].astype(BF16),
                                             preferred_element_type=F32)
        return carry

    lax.fori_loop(0, n_blk, block, 0)


def _moe_routed(h2, gates, wg, wu, wd, tm, tf):
    n, d = h2.shape
    n_exp, _, dff = wg.shape
    tm = min(tm, n)
    rank, rank_t, cnt = _route(gates, n_exp, tm)
    n_blocks = (cnt[:, 0, :n_exp].reshape(-1) + (MOE_ROWS - 1)) // MOE_ROWS
    cap = pl.cdiv(tm, MOE_ROWS) * MOE_ROWS
    grid_spec = pltpu.PrefetchScalarGridSpec(
        num_scalar_prefetch=1,
        grid=(n // tm, n_exp, dff // tf),
        in_specs=[pl.BlockSpec((tm, d), lambda i, e, f, c: (i, 0)),
                  pl.BlockSpec((tm, LANES), lambda i, e, f, c: (i, 0)),
                  pl.BlockSpec((None, n_exp, tm), lambda i, e, f, c: (i, 0, 0)),
                  pl.BlockSpec((tm, LANES), lambda i, e, f, c: (i, 0)),
                  pl.BlockSpec((None, d, tf), lambda i, e, f, c: (e, 0, f)),
                  pl.BlockSpec((None, d, tf), lambda i, e, f, c: (e, 0, f)),
                  pl.BlockSpec((None, tf, d), lambda i, e, f, c: (e, f, 0))],
        out_specs=pl.BlockSpec((tm, d), lambda i, e, f, c: (i, 0)),
        scratch_shapes=[pltpu.VMEM((cap, d), BF16), pltpu.VMEM((cap, d), F32)])
    return pl.pallas_call(
        functools.partial(_moe_kernel, n_exp=n_exp),
        out_shape=jax.ShapeDtypeStruct((n, d), F32),
        grid_spec=grid_spec,
        compiler_params=_params(("arbitrary",) * 3),
        name="moe_swiglu",
    )(n_blocks, h2, rank, rank_t, gates, wg, wu, wd)


def _residual_kernel(x_ref, y_ref, gt_ref, gfin_ref, o_ref, *, final):
    xo = x_ref[...] + gt_ref[...] * y_ref[...]
    if final:
        xo = _rms(xo, gfin_ref[...])
    o_ref[...] = xo


def _residual(x, y, modrows, g_final, final, tm):
    b, s, d = x.shape
    tm = min(tm, s)
    tile = pl.BlockSpec((None, tm, d), lambda bi, i: (bi, i, 0))
    return pl.pallas_call(
        functools.partial(_residual_kernel, final=final),
        out_shape=jax.ShapeDtypeStruct((b, s, d), F32),
        grid=(b, s // tm),
        in_specs=[tile, tile, pl.BlockSpec((None, 1, d), lambda bi, i: (bi, 0, 5)), _const_spec((1, d))],
        out_specs=tile,
        compiler_params=_params(("arbitrary", "arbitrary")),
        name="moe_residual",
    )(x, y, modrows, g_final)


def _moe(h2, x, modrows, gates, wg, wu, wd, g_final, final):
    b, s, d = x.shape
    y = _moe_routed(h2.reshape(b * s, d), gates.reshape(b * s, LANES), wg, wu, wd, MOE_TOKENS, 1792)
    return _residual(x, y.reshape(b, s, d), modrows, g_final, final, 512)


def _rope_tables(seq):
    axis_dim = DA_QK // 2
    n_freq = axis_dim // 2
    inv = jnp.exp(-math.log(ROPE_BASE) * jnp.arange(n_freq, dtype=F32) * (2.0 / axis_dim))
    t = jnp.arange(seq, dtype=jnp.int32)
    ar = (t // GRID_W).astype(F32)[:, None] * inv
    ac = (t % GRID_W).astype(F32)[:, None] * inv
    cos = jnp.concatenate([jnp.cos(ar), jnp.cos(ar), jnp.cos(ac), jnp.cos(ac)], axis=-1)
    sin = jnp.concatenate([jnp.sin(ar), jnp.sin(ar), jnp.sin(ac), jnp.sin(ac)], axis=-1)
    first = (jnp.arange(DA_QK) % axis_dim) < n_freq
    sa = jnp.where(first, -sin, 0.0)
    sb = jnp.where(first, 0.0, sin)
    rep = lambda a: jnp.tile(a, (1, LANES // DA_QK))
    return rep(cos), rep(sa), rep(sb)


def _identity_tables(seq):
    return jnp.ones((seq, LANES), F32), jnp.zeros((seq, LANES), F32), jnp.zeros((seq, LANES), F32)


def _pack_layer_weights(w_in, w_uq, w_ukv):
    d = w_in.shape[0]
    zpad = jnp.zeros((d, LANES - MLA_ROPE), w_in.dtype)
    o_kr = 3 * 512 + MLA_Q_RANK + MLA_KV_RANK
    o_pool = o_kr + MLA_ROPE
    w2 = jnp.concatenate([w_in[:, :o_kr + MLA_ROPE], zpad, w_in[:, o_pool:o_pool + 512]], axis=1).astype(BF16)
    wgate = w_in[:, o_pool + 512:].astype(BF16)
    hq = MLA_NOPE + MLA_ROPE
    wq = w_uq.reshape(MLA_Q_RANK, MLA_HEADS, hq)
    wq = jnp.concatenate([wq, jnp.zeros((MLA_Q_RANK, MLA_HEADS, MLA_QK_PAD - hq), w_uq.dtype)], axis=-1)
    wuq = wq.reshape(MLA_Q_RANK, MLA_HEADS * MLA_QK_PAD).astype(BF16)
    wkv = w_ukv.reshape(MLA_KV_RANK, MLA_HEADS, MLA_NOPE + MLA_V)
    wukv = jnp.concatenate([wkv[:, :, :MLA_NOPE].reshape(MLA_KV_RANK, -1),
                            wkv[:, :, MLA_NOPE:].reshape(MLA_KV_RANK, -1)], axis=1).astype(BF16)
    return w2, wgate, wuq, wukv


def kernel(x, c, ctx, c_ctx, w_mod, b_mod, g_mix, w_in, da_lambda, da_subln, mla_gq, w_uq, mla_gkv, w_ukv,
           pool_w, pool_scale, w_branch, w_out, g_ffn, ffn_w_gate, ffn_w_up, ffn_w_down, moe_router,
           moe_w_gate, moe_w_up, moe_w_down, g_final):
    bsz, seq, d = x.shape
    n_ctx = ctx.shape[1]
    depth = w_mod.shape[0]

    cond8 = jnp.zeros((8, d), F32).at[:bsz].set(c).at[bsz].set(c_ctx)
    mod = _mod_rows(cond8, w_mod, b_mod)
    lat_tabs = _rope_tables(seq)
    ctx_tabs = _identity_tables(n_ctx)
    g_fin = g_final.reshape(1, d)

    xc = ctx
    for l in range(depth):
        need_ctx = l < depth - 1
        lam_init = 0.8 - 0.6 * math.exp(-0.3 * l)
        mod_lat = mod[l, :bsz][:, None, :]
        mod_ctx = jnp.broadcast_to(mod[l, bsz][None, None, :], (bsz, 1, N_MOD * d))
        w2, wgate, wuq, wukv = _pack_layer_weights(w_in[l], w_uq[l], w_ukv[l])
        gmix = g_mix[l].reshape(1, d)
        gffn = g_ffn[l].reshape(1, d)
        gq = mla_gq[l].reshape(1, -1)
        gkv = mla_gkv[l].reshape(1, -1)
        subln = da_subln[l].reshape(1, -1)
        lam = da_lambda[l]
        pw = pool_w[l].astype(BF16)
        psc = pool_scale[l].reshape(1, -1)
        wbr = w_branch[l].astype(BF16)
        wout = w_out[l].astype(BF16)
        j = l // 2
        dense = l % 2 == 0
        final = l == depth - 1

        qda, kda, vda, qm, km, vm, pin = _inproj(x, mod_lat, gmix, w2, gq, wuq, gkv, wukv, lat_tabs, 256)
        cqda, ckda, cvda, cqm, ckm, cvm, cpin = _inproj(xc, mod_ctx, gmix, w2, gq, wuq, gkv, wukv, ctx_tabs, 256)

        o_da = _attention(qda, ckda, cvda, kda, vda, DA_HEADS, LANES, DA_V, 256, 512, diff=True,
                          lam=lam, g=subln, lam_init=lam_init, name="diff_attn")
        o_mla = _attention(qm, ckm, cvm, km, vm, MLA_HEADS, MLA_QK_PAD, MLA_V, 512, 512, name="mla_attn")
        o_pool = _pool(pin, pw, psc, 512)

        router = None
        n_exp = moe_router.shape[-1]
        if not dense:
            router = moe_router[j].T
        outs = _merge(x, mod_lat, gmix, wgate, o_da, o_mla, o_pool, wbr, wout, gffn, router, n_exp, 256)

        if dense:
            wg = ffn_w_gate[j].astype(BF16)
            wu = ffn_w_up[j].astype(BF16)
            wd = ffn_w_down[j].astype(BF16)
            x = _ffn(outs[1], outs[0], mod_lat, wg, wu, wd, g_fin, final, 256)
        else:
            wg = moe_w_gate[j].astype(BF16)
            wu = moe_w_up[j].astype(BF16)
            wd = moe_w_down[j].astype(BF16)
            x = _moe(outs[1], outs[0], mod_lat, outs[2], wg, wu, wd, g_fin, final)

        if need_ctx:
            co_da = _attention(cqda, ckda, cvda, None, None, DA_HEADS, LANES, DA_V, 256, 512, diff=True,
                               lam=lam, g=subln, lam_init=lam_init, name="diff_attn_ctx")
            co_mla = _attention(cqm, ckm, cvm, None, None, MLA_HEADS, MLA_QK_PAD, MLA_V, 256, 512,
                                name="mla_attn_ctx")
            co_pool = _pool(cpin, pw, psc, 512)
            couts = _merge(xc, mod_ctx, gmix, wgate, co_da, co_mla, co_pool, wbr, wout, gffn, router, n_exp, 256)
            if dense:
                xc = _ffn(couts[1], couts[0], mod_ctx, wg, wu, wd, g_fin, False, 256)
            else:
                xc = _moe(couts[1], couts[0], mod_ctx, couts[2], wg, wu, wd, g_fin, False)
    return x
```

```python
import functools
import math

import jax
import jax.numpy as jnp
from jax import lax
from jax.experimental import pallas as pl
from jax.experimental.pallas import tpu as pltpu

F32 = jnp.float32
BF16 = jnp.bfloat16

GRID_W = 64
DA_HEADS = 4
DA_QK = 64
DA_V = 128
MLA_HEADS = 4
MLA_NOPE = 128
MLA_ROPE = 64
MLA_V = 128
MLA_Q_RANK = 384
MLA_KV_RANK = 256
POOL_WINDOWS = (2, 4, 8, 16)
POOL_GROUP_W = 128
N_BRANCH = 3
BRANCH_W = 512
ROPE_BASE = 10000.0
EPS = 1e-6
N_MOD = 6
LANES = 128
MLA_QK_PAD = 256
LOG2E = math.log2(math.e)
DA_SCALE = DA_QK ** -0.5 * LOG2E
MLA_SCALE = (MLA_NOPE + MLA_ROPE) ** -0.5 * LOG2E
NEG_BIG = -1e30
PROJ_ROWS = 512
ATTN_Q_TILES = 1
VMEM_LIMIT = 56 * 1024 * 1024

C_DAQ, C_DAK, C_DAV = 0, 512, 1024
C_QD = 1536
C_KVD = C_QD + MLA_Q_RANK
C_KR = C_KVD + MLA_KV_RANK
C_POOL = C_KR + LANES
W2_COLS = C_POOL + 512


def _sigmoid(v):
    return 1.0 / (1.0 + jnp.exp(-v))


def _params(sem, vmem=VMEM_LIMIT):
    return pltpu.CompilerParams(dimension_semantics=sem, vmem_limit_bytes=vmem)


def _const_spec(shape):
    nd = len(shape)
    return pl.BlockSpec(shape, lambda *_: (0,) * nd)


def _mod_kernel(cond_ref, w_ref, b_ref, o_ref):
    c = cond_ref[...]
    s = c * _sigmoid(c)
    o_ref[...] = jnp.dot(s, w_ref[...], preferred_element_type=F32,
                         precision=lax.Precision.HIGHEST) + b_ref[...]


def _mod_rows(cond8, w_mod, b_mod):
    depth, d, n = w_mod.shape
    tn = 1536
    return pl.pallas_call(
        _mod_kernel,
        out_shape=jax.ShapeDtypeStruct((depth, 8, n), F32),
        grid=(depth, n // tn),
        in_specs=[pl.BlockSpec((8, d), lambda l, j: (0, 0)),
                  pl.BlockSpec((None, d, tn), lambda l, j: (l, 0, j)),
                  pl.BlockSpec((None, 1, tn), lambda l, j: (l, 0, j))],
        out_specs=pl.BlockSpec((None, 8, tn), lambda l, j: (l, 0, j)),
        compiler_params=_params(("arbitrary", "arbitrary")),
        name="adaln_rows",
    )(cond8, w_mod, b_mod.reshape(depth, 1, n))


def _norm_mod(x, g, sh, sc):
    r = lax.rsqrt(jnp.mean(x * x, axis=-1, keepdims=True) + EPS)
    return (x * r * g) * (1.0 + sc) + sh


def _rms(v, g):
    return v * lax.rsqrt(jnp.mean(v * v, axis=-1, keepdims=True) + EPS) * g


def _inproj_kernel(x_ref, g_ref, sh_ref, sc_ref, w_ref, gq_ref, wuq_ref, gkv_ref, wukv_ref,
                   cos_ref, sa_ref, sb_ref,
                   qda_ref, kda_ref, vda_ref, qm_ref, km_ref, vm_ref, pool_ref):
    h = _norm_mod(x_ref[...], g_ref[...], sh_ref[...], sc_ref[...])
    z = jnp.dot(h.astype(BF16), w_ref[...], preferred_element_type=F32)
    cos = cos_ref[...]
    sa = sa_ref[...]
    sb = sb_ref[...]

    def rope(blk):
        return blk * cos + pltpu.roll(blk, LANES - 16, 1) * sa + pltpu.roll(blk, 16, 1) * sb

    lo = lax.broadcasted_iota(jnp.int32, cos.shape, 1) < MLA_ROPE

    for hh in range(DA_HEADS):
        c0 = hh * LANES
        qda_ref[:, c0:c0 + LANES] = (rope(z[:, C_DAQ + c0:C_DAQ + c0 + LANES]) * DA_SCALE).astype(BF16)
        kda_ref[:, c0:c0 + LANES] = rope(z[:, C_DAK + c0:C_DAK + c0 + LANES]).astype(BF16)

    qn = _rms(z[:, C_QD:C_QD + MLA_Q_RANK], gq_ref[...])
    qf = jnp.dot(qn.astype(BF16), wuq_ref[...], preferred_element_type=F32)
    kvn = _rms(z[:, C_KVD:C_KVD + MLA_KV_RANK], gkv_ref[...])
    kvf = jnp.dot(kvn.astype(BF16), wukv_ref[...], preferred_element_type=F32)
    kr = jnp.where(lo, rope(z[:, C_KR:C_KR + LANES]), 0.0).astype(BF16)
    for hh in range(MLA_HEADS):
        c0 = hh * MLA_QK_PAD
        qm_ref[:, c0:c0 + LANES] = (qf[:, c0:c0 + LANES] * MLA_SCALE).astype(BF16)
        qr = jnp.where(lo, rope(qf[:, c0 + LANES:c0 + 2 * LANES]), 0.0)
        qm_ref[:, c0 + LANES:c0 + 2 * LANES] = (qr * MLA_SCALE).astype(BF16)
        km_ref[:, c0:c0 + LANES] = kvf[:, hh * LANES:(hh + 1) * LANES].astype(BF16)
        km_ref[:, c0 + LANES:c0 + 2 * LANES] = kr
    ones = jnp.ones(cos.shape, BF16)
    for hh in range(MLA_HEADS):
        c0 = 2 * hh * LANES
        vda_ref[:, c0:c0 + LANES] = z[:, C_DAV + hh * LANES:C_DAV + (hh + 1) * LANES].astype(BF16)
        vda_ref[:, c0 + LANES:c0 + 2 * LANES] = ones
        vm_ref[:, c0:c0 + LANES] = kvf[:, 512 + hh * LANES:512 + (hh + 1) * LANES].astype(BF16)
        vm_ref[:, c0 + LANES:c0 + 2 * LANES] = ones
    pool_ref[...] = z[:, C_POOL:C_POOL + 512]


def _inproj(x, modrows, g_mix, w2, gq, wuq, gkv, wukv, tabs, tm):
    b, s, d = x.shape
    tm = min(tm, s)
    cos, sa, sb = tabs
    row = lambda j: pl.BlockSpec((None, 1, d), lambda bi, i: (bi, 0, j))
    tab = pl.BlockSpec((tm, LANES), lambda bi, i: (i, 0))
    out = lambda w, dt: jax.ShapeDtypeStruct((b, s, w), dt)
    ospec = lambda w: pl.BlockSpec((None, tm, w), lambda bi, i: (bi, i, 0))
    return pl.pallas_call(
        _inproj_kernel,
        out_shape=(out(512, BF16), out(512, BF16), out(1024, BF16), out(1024, BF16), out(1024, BF16),
                   out(1024, BF16), out(512, F32)),
        grid=(b, s // tm),
        in_specs=[pl.BlockSpec((None, tm, d), lambda bi, i: (bi, i, 0)),
                  _const_spec((1, d)), row(0), row(1),
                  _const_spec(w2.shape), _const_spec(gq.shape), _const_spec(wuq.shape),
                  _const_spec(gkv.shape), _const_spec(wukv.shape), tab, tab, tab],
        out_specs=(ospec(512), ospec(512), ospec(1024), ospec(1024), ospec(1024), ospec(1024), ospec(512)),
        compiler_params=_params(("arbitrary", "arbitrary")),
        name="in_proj",
    )(x, g_mix, modrows, modrows, w2, gq, wuq, gkv, wukv, cos, sa, sb)


def _attn_kernel(*refs, diff, has_lat, tk, tq, lam_init):
    it = iter(refs)
    q_ref, kc_ref, vc_ref = next(it), next(it), next(it)
    kl_ref = vl_ref = lam_ref = g_ref = None
    if has_lat:
        kl_ref, vl_ref = next(it), next(it)
    if diff:
        lam_ref, g_ref = next(it), next(it)
    o_ref = next(it)
    m_ref, acc_ref = next(it), next(it)
    al_refs = (next(it), next(it))
    lat_widths, buf_widths = _attn_chunk_widths(kc_ref.shape[0], kl_ref.shape[0] if has_lat else 0, tk)
    bufs = {w: ((next(it), next(it)), (next(it), next(it))) for w in buf_widths}
    n_qt = q_ref.shape[0] // tq
    dv = o_ref.shape[-1]

    def stacked_q(qt):
        q = q_ref[qt * tq:(qt + 1) * tq, :]
        if not diff:
            return q
        lane = lax.broadcasted_iota(jnp.int32, q.shape, 1)
        zero = jnp.zeros_like(q)
        return jnp.concatenate([jnp.where(lane < DA_QK, q, zero), jnp.where(lane >= DA_QK, q, zero)], axis=0)

    qs = [stacked_q(qt) for qt in range(n_qt)]
    chunks = [(kc_ref, vc_ref, 0, kc_ref.shape[0])]
    start = 0
    for w in lat_widths:
        chunks.append((kl_ref, vl_ref, start, w))
        start += w
    k_of = lambda c: chunks[c][0][chunks[c][2]:chunks[c][2] + chunks[c][3], :]
    v_of = lambda c: chunks[c][1][chunks[c][2]:chunks[c][2] + chunks[c][3], :]

    def qk(item):
        _, qt, c, s_ref, _ = item
        s_ref[...] = lax.dot_general(qs[qt], k_of(c), (((1,), (1,)), ((), ())), preferred_element_type=F32)

    def sm(item):
        g, qt, c, s_ref, p_ref = item
        blocks = [slice(j * LANES, (j + 1) * LANES) for j in range(chunks[c][3] // LANES)]
        mx = s_ref[:, blocks[0]]
        for blk in blocks[1:]:
            mx = jnp.maximum(mx, s_ref[:, blk])
        m = m_ref[qt]
        m_new = jnp.maximum(m, jnp.max(mx, axis=-1, keepdims=True))
        al_refs[g % 2][...] = jnp.exp2(m - m_new)
        m_ref[qt] = m_new
        for blk in blocks:
            p_ref[:, blk] = jnp.exp2(s_ref[:, blk] - m_new).astype(BF16)

    def pv(item):
        g, qt, c, _, p_ref = item
        new = jnp.dot(p_ref[...], v_of(c), preferred_element_type=F32)
        al = al_refs[g % 2][...]
        for blk in (slice(0, dv), slice(dv, 2 * dv)):
            acc_ref[qt, :, blk] = al * acc_ref[qt, :, blk] + new[:, blk]

    def finish(qt):
        acc = acc_ref[qt]
        o = acc[:, :dv] / acc[:, dv:]
        if diff:
            lv = lam_ref[...]
            a = jnp.sum(lv[0:1, :] * lv[1:2, :], axis=-1, keepdims=True)
            b = jnp.sum(lv[2:3, :] * lv[3:4, :], axis=-1, keepdims=True)
            lam = jnp.exp(a) - jnp.exp(b) + lam_init
            o = o[:tq] - lam * o[tq:]
            o = _rms(o, g_ref[...]) * (1.0 - lam_init)
        o_ref[qt * tq:(qt + 1) * tq, :] = o.astype(o_ref.dtype)

    m_ref[...] = jnp.full(m_ref.shape, NEG_BIG, F32)
    acc_ref[...] = jnp.zeros(acc_ref.shape, F32)
    items, used = [], {w: 0 for w in bufs}
    for qt in range(n_qt):
        for c in range(len(chunks)):
            w = chunks[c][3]
            items.append((len(items), qt, c, bufs[w][0][used[w] % 2], bufs[w][1][used[w] % 2]))
            used[w] += 1
    for g in range(len(items) + 2):
        if g < len(items):
            qk(items[g])
        if 0 <= g - 2 < len(items):
            pv(items[g - 2])
        if 0 <= g - 1 < len(items):
            sm(items[g - 1])
        if 0 <= g - 2 < len(items) and items[g - 2][2] == len(chunks) - 1:
            finish(items[g - 2][1])


def _attn_chunk_widths(n_ctx, n_lat_rows, tk):
    n_full = n_lat_rows // tk
    lat = [tk] * n_full
    return lat, sorted(set(lat + [n_ctx]))


def _attention(q, kc, vc, kl, vl, heads, dq, dv, tq, tk, diff=False, lam=None, g=None, lam_init=0.0,
               name="attn"):
    b, s, _ = q.shape
    c = kc.shape[1]
    tq = min(tq, s)
    n_qt = min(ATTN_Q_TILES, s // tq)
    has_lat = kl is not None
    args = [q, kc, vc]
    specs = [pl.BlockSpec((None, n_qt * tq, dq), lambda bi, hi, qi: (bi, qi, hi)),
             pl.BlockSpec((None, c, dq), lambda bi, hi, qi: (bi, 0, hi)),
             pl.BlockSpec((None, c, 2 * dv), lambda bi, hi, qi: (bi, 0, hi))]
    sl = 0
    if has_lat:
        sl = kl.shape[1]
        tk = min(tk, sl)
        args += [kl, vl]
        specs += [pl.BlockSpec((None, sl, dq), lambda bi, hi, qi: (bi, 0, hi)),
                  pl.BlockSpec((None, sl, 2 * dv), lambda bi, hi, qi: (bi, 0, hi))]
    if diff:
        args += [lam, g]
        specs += [_const_spec(lam.shape), _const_spec(g.shape)]
    kern = functools.partial(_attn_kernel, diff=diff, has_lat=has_lat, tk=tk, tq=tq, lam_init=lam_init)
    rows = 2 * tq if diff else tq
    scratch = [pltpu.VMEM((n_qt, rows, LANES), F32), pltpu.VMEM((n_qt, rows, 2 * dv), F32),
               pltpu.VMEM((rows, LANES), F32), pltpu.VMEM((rows, LANES), F32)]
    for w in _attn_chunk_widths(c, sl, tk)[1]:
        scratch += [pltpu.VMEM((rows, w), F32), pltpu.VMEM((rows, w), F32),
                    pltpu.VMEM((rows, w), BF16), pltpu.VMEM((rows, w), BF16)]
    return pl.pallas_call(
        kern,
        out_shape=jax.ShapeDtypeStruct((b, s, heads * dv), BF16),
        grid=(b, heads, s // (n_qt * tq)),
        in_specs=specs,
        out_specs=pl.BlockSpec((None, n_qt * tq, dv), lambda bi, hi, qi: (bi, qi, hi)),
        scratch_shapes=scratch,
        compiler_params=_params(("arbitrary", "arbitrary", "arbitrary")),
        name=name,
    )(*args)


def _pool_kernel(prev_ref, cur_ref, next_ref, w_ref, sc_ref, o_ref, *, seq_len):
    i = pl.program_id(1)
    n = pl.num_programs(1)
    cur = cur_ref[...]
    tm = cur.shape[0]
    prev = jnp.where(i > 0, prev_ref[...], 0.0)
    nxt = jnp.where(i < n - 1, next_ref[...], 0.0)
    ext = jnp.concatenate([prev, cur, nxt], axis=0)
    ne = tm + 16
    t = i * tm + lax.broadcasted_iota(jnp.int32, (tm, 1), 0)
    for g, w in enumerate(POOL_WINDOWS):
        e = ext[:, g * LANES:(g + 1) * LANES]
        acc = e + pltpu.roll(e, 1, 0)
        half = 1
        while 2 * half < w:
            acc = pltpu.roll(acc, half, 0) + pltpu.roll(acc, ne - half, 0)
            half *= 2
        win = acc[8:8 + tm]
        lo = jnp.clip(t - w // 2, 0, seq_len)
        hi = jnp.clip(t - w // 2 + w, 0, seq_len)
        cnt = (hi - lo).astype(F32)
        dlt = win / cnt - cur[:, g * LANES:(g + 1) * LANES]
        y = jnp.dot(dlt.astype(BF16), w_ref[g], preferred_element_type=F32)
        o_ref[:, g * LANES:(g + 1) * LANES] = (y * sc_ref[:, g * LANES:(g + 1) * LANES]).astype(BF16)


def _pool(u, pool_w, pool_scale, tm):
    b, s, w = u.shape
    tm = min(tm, s)
    nb8 = s // 8
    r8 = tm // 8
    return pl.pallas_call(
        functools.partial(_pool_kernel, seq_len=s),
        out_shape=jax.ShapeDtypeStruct((b, s, w), BF16),
        grid=(b, s // tm),
        in_specs=[pl.BlockSpec((None, 8, w), lambda bi, i: (bi, jnp.maximum(i * r8 - 1, 0), 0)),
                  pl.BlockSpec((None, tm, w), lambda bi, i: (bi, i, 0)),
                  pl.BlockSpec((None, 8, w), lambda bi, i: (bi, jnp.minimum((i + 1) * r8, nb8 - 1), 0)),
                  _const_spec(pool_w.shape), _const_spec(pool_scale.shape)],
        out_specs=pl.BlockSpec((None, tm, w), lambda bi, i: (bi, i, 0)),
        compiler_params=_params(("arbitrary", "arbitrary")),
        name="pool_mixer",
    )(u, u, u, pool_w, pool_scale)


def _merge_kernel(*refs, moe, n_exp):
    (x_ref, gmix_ref, sh1_ref, sc1_ref, wgate_ref, oda_ref, omla_ref, opool_ref, wbr_ref, wout_ref,
     gt1_ref, gffn_ref, sh2_ref, sc2_ref) = refs[:14]
    if moe:
        router_ref, xo_ref, h2_ref, gates_ref = refs[14:]
    else:
        xo_ref, h2_ref = refs[14:]
    x = x_ref[...]
    d = x.shape[-1]
    h = _norm_mod(x, gmix_ref[...], sh1_ref[...], sc1_ref[...]).astype(BF16)
    merged = None
    for n, o_ref in enumerate((oda_ref, omla_ref, opool_ref)):
        gate = _sigmoid(jnp.dot(h, wgate_ref[:, n * d:(n + 1) * d], preferred_element_type=F32))
        proj = jnp.dot(o_ref[...], wbr_ref[n], preferred_element_type=F32)
        merged = gate * proj if merged is None else merged + gate * proj
    mix = jnp.dot(merged.astype(BF16), wout_ref[...], preferred_element_type=F32)
    xn = x + gt1_ref[...] * mix
    xo_ref[...] = xn
    h2 = _norm_mod(xn, gffn_ref[...], sh2_ref[...], sc2_ref[...])
    h2_ref[...] = h2.astype(BF16)
    if moe:
        logit = [jnp.sum(h2 * router_ref[e:e + 1, :], axis=-1, keepdims=True) for e in range(n_exp)]

        def top1(vals):
            best, idx = vals[0], jnp.zeros(vals[0].shape, jnp.int32)
            for e in range(1, n_exp):
                better = vals[e] > best
                best = jnp.where(better, vals[e], best)
                idx = jnp.where(better, e, idx)
            return best, idx

        v1, i1 = top1(logit)
        v2, i2 = top1([jnp.where(i1 == e, NEG_BIG, logit[e]) for e in range(n_exp)])
        w1 = 1.0 / (1.0 + jnp.exp(v2 - v1))
        lane = lax.broadcasted_iota(jnp.int32, gates_ref.shape, 1)
        gates_ref[...] = jnp.where(lane == i1, w1, 0.0) + jnp.where(lane == i2, 1.0 - w1, 0.0)


def _merge(x, modrows, g_mix, wgate, o_da, o_mla, o_pool, wbr, wout, g_ffn, router, n_exp, tm):
    b, s, d = x.shape
    tm = min(tm, s)
    moe = router is not None
    row = lambda j: pl.BlockSpec((None, 1, d), lambda bi, i: (bi, 0, j))
    tile = lambda w: pl.BlockSpec((None, tm, w), lambda bi, i: (bi, i, 0))
    args = [x, g_mix, modrows, modrows, wgate, o_da, o_mla, o_pool, wbr, wout, modrows, g_ffn, modrows, modrows]
    specs = [tile(d), _const_spec((1, d)), row(0), row(1), _const_spec(wgate.shape),
             tile(BRANCH_W), tile(BRANCH_W), tile(BRANCH_W), _const_spec(wbr.shape), _const_spec(wout.shape),
             row(2), _const_spec((1, d)), row(3), row(4)]
    out_shape = [jax.ShapeDtypeStruct((b, s, d), F32), jax.ShapeDtypeStruct((b, s, d), BF16)]
    out_specs = [tile(d), tile(d)]
    if moe:
        args.append(router)
        specs.append(_const_spec(router.shape))
        out_shape.append(jax.ShapeDtypeStruct((b, s, LANES), F32))
        out_specs.append(tile(LANES))
    return pl.pallas_call(
        functools.partial(_merge_kernel, moe=moe, n_exp=n_exp),
        out_shape=tuple(out_shape),
        grid=(b, s // tm),
        in_specs=specs,
        out_specs=tuple(out_specs),
        compiler_params=_params(("arbitrary", "arbitrary")),
        name="merge_out",
    )(*args)


def _ffn_kernel(h_ref, x_ref, gt_ref, wg_ref, wu_ref, wd_ref, gfin_ref, o_ref, *, final):
    h = h_ref[...]
    gte = jnp.dot(h, wg_ref[...], preferred_element_type=F32)
    up = jnp.dot(h, wu_ref[...], preferred_element_type=F32)
    act = (gte * _sigmoid(gte) * up).astype(BF16)
    y = jnp.dot(act, wd_ref[...], preferred_element_type=F32)
    xo = x_ref[...] + gt_ref[...] * y
    if final:
        xo = _rms(xo, gfin_ref[...])
    o_ref[...] = xo


def _ffn(h2, x, modrows, wg, wu, wd, g_final, final, tm):
    b, s, d = x.shape
    tm = min(tm, s)
    tile = pl.BlockSpec((None, tm, d), lambda bi, i: (bi, i, 0))
    return pl.pallas_call(
        functools.partial(_ffn_kernel, final=final),
        out_shape=jax.ShapeDtypeStruct((b, s, d), F32),
        grid=(b, s // tm),
        in_specs=[tile, tile, pl.BlockSpec((None, 1, d), lambda bi, i: (bi, 0, 5)),
                  _const_spec(wg.shape), _const_spec(wu.shape), _const_spec(wd.shape), _const_spec((1, d))],
        out_specs=tile,
        compiler_params=_params(("arbitrary", "arbitrary")),
        name="ffn_swiglu",
    )(h2, x, modrows, wg, wu, wd, g_final)


ROUTE_SUB = 256
MOE_TOKENS = 2048
MOE_FF_CHUNK = 896
MOE_VMEM_LIMIT = 60 * 1024 * 1024
MOE_ROWS = 288
MOE_ROW_ALIGN = 32


def _route_kernel(g_ref, rank_ref, rankt_ref, cnt_ref, *, n_exp):
    t = g_ref.shape[0]
    r_io = lax.broadcasted_iota(jnp.int32, (ROUTE_SUB, ROUTE_SUB), 0)
    c_io = lax.broadcasted_iota(jnp.int32, (ROUTE_SUB, ROUTE_SUB), 1)
    tri = jnp.where(c_io <= r_io, 1.0, 0.0).astype(BF16)
    carry = jnp.zeros((1, LANES), F32)
    for j in range(t // ROUTE_SUB):
        rows = slice(j * ROUTE_SUB, (j + 1) * ROUTE_SUB)
        routed = g_ref[rows, :] != 0.0
        incl = jnp.dot(tri, jnp.where(routed, 1.0, 0.0).astype(BF16), preferred_element_type=F32)
        rank_ref[rows, :] = jnp.where(routed, incl - 1.0 + carry, -1.0)
        carry = carry + incl[ROUTE_SUB - 1:ROUTE_SUB, :]
    rankt_ref[...] = rank_ref[...].T[0:n_exp, :]
    cnt_ref[...] = jnp.broadcast_to(carry, cnt_ref.shape).astype(jnp.int32)


def _route(gates, n_exp, tm):
    n = gates.shape[0]
    return pl.pallas_call(
        functools.partial(_route_kernel, n_exp=n_exp),
        out_shape=(jax.ShapeDtypeStruct((n, LANES), F32),
                   jax.ShapeDtypeStruct((n // tm, n_exp, tm), F32),
                   jax.ShapeDtypeStruct((n // tm, 8, LANES), jnp.int32)),
        grid=(n // tm,),
        in_specs=[pl.BlockSpec((tm, LANES), lambda i: (i, 0))],
        out_specs=(pl.BlockSpec((tm, LANES), lambda i: (i, 0)),
                   pl.BlockSpec((None, n_exp, tm), lambda i: (i, 0, 0)),
                   pl.BlockSpec((None, 8, LANES), lambda i: (i, 0, 0))),
        compiler_params=_params(("arbitrary",)),
        name="moe_route",
    )(gates)


def _moe_kernel(cnt_ref, h_ref, rank_ref, rankt_ref, gates_ref, wg_ref, wu_ref, wd_ref, o_ref,
                xg_ref, yg_ref, *, n_exp):
    i, e, f = pl.program_id(0), pl.program_id(1), pl.program_id(2)
    nf = pl.num_programs(2)
    t = h_ref.shape[0]
    n_blk = cnt_ref[i * n_exp + e]

    @pl.when(jnp.logical_and(e == 0, f == 0))
    def _():
        o_ref[...] = jnp.zeros_like(o_ref)

    rank_row = rankt_ref[pl.ds(e, 1), :]

    def block(b, carry):
        base = pl.multiple_of(b * MOE_ROWS, MOE_ROW_ALIGN)
        rows = pl.ds(base, MOE_ROWS)

        @pl.when(f == 0)
        def _():
            slot = (base + lax.broadcasted_iota(jnp.int32, (MOE_ROWS, t), 0)).astype(F32)
            onehot = jnp.where(rank_row == slot, 1.0, 0.0).astype(BF16)
            xg_ref[rows, :] = jnp.dot(onehot, h_ref[...], preferred_element_type=F32).astype(BF16)

        xb = xg_ref[rows, :]
        gte = jnp.dot(xb, wg_ref[...], preferred_element_type=F32)
        up = jnp.dot(xb, wu_ref[...], preferred_element_type=F32)
        act = (gte * _sigmoid(gte) * up).astype(BF16)
        part = jnp.dot(act, wd_ref[...], preferred_element_type=F32)

        @pl.when(f == 0)
        def _():
            yg_ref[rows, :] = part

        @pl.when(f > 0)
        def _():
            yg_ref[rows, :] += part

        @pl.when(f == nf - 1)
        def _():
            lane = lax.broadcasted_iota(jnp.int32, (t, LANES), 1)
            rank_col = jnp.sum(jnp.where(lane == e, rank_ref[...], 0.0), axis=-1, keepdims=True)
            gate_col = jnp.sum(jnp.where(lane == e, gates_ref[...], 0.0), axis=-1, keepdims=True)
            slot = (base + lax.broadcasted_iota(jnp.int32, (t, MOE_ROWS), 1)).astype(F32)
            onehot = jnp.where(rank_col == slot, 1.0, 0.0).astype(BF16)
            o_ref[...] += gate_col * jnp.dot(onehot, yg_ref[rows, :].astype(BF16),
                                             preferred_element_type=F32)
        return carry

    lax.fori_loop(0, n_blk, block, 0)


def _moe_routed(h2, gates, wg, wu, wd, tm, tf):
    n, d = h2.shape
    n_exp, _, dff = wg.shape
    tm = min(tm, n)
    rank, rank_t, cnt = _route(gates, n_exp, tm)
    n_blocks = (cnt[:, 0, :n_exp].reshape(-1) + (MOE_ROWS - 1)) // MOE_ROWS
    cap = pl.cdiv(tm, MOE_ROWS) * MOE_ROWS
    grid_spec = pltpu.PrefetchScalarGridSpec(
        num_scalar_prefetch=1,
        grid=(n // tm, n_exp, dff // tf),
        in_specs=[pl.BlockSpec((tm, d), lambda i, e, f, c: (i, 0)),
                  pl.BlockSpec((tm, LANES), lambda i, e, f, c: (i, 0)),
                  pl.BlockSpec((None, n_exp, tm), lambda i, e, f, c: (i, 0, 0)),
                  pl.BlockSpec((tm, LANES), lambda i, e, f, c: (i, 0)),
                  pl.BlockSpec((None, d, tf), lambda i, e, f, c: (e, 0, f)),
                  pl.BlockSpec((None, d, tf), lambda i, e, f, c: (e, 0, f)),
                  pl.BlockSpec((None, tf, d), lambda i, e, f, c: (e, f, 0))],
        out_specs=pl.BlockSpec((tm, d), lambda i, e, f, c: (i, 0)),
        scratch_shapes=[pltpu.VMEM((cap, d), BF16), pltpu.VMEM((cap, d), F32)])
    return pl.pallas_call(
        functools.partial(_moe_kernel, n_exp=n_exp),
        out_shape=jax.ShapeDtypeStruct((n, d), F32),
        grid_spec=grid_spec,
        compiler_params=_params(("arbitrary",) * 3, MOE_VMEM_LIMIT),
        name="moe_swiglu",
    )(n_blocks, h2, rank, rank_t, gates, wg, wu, wd)


def _residual_kernel(x_ref, y_ref, gt_ref, gfin_ref, o_ref, *, final):
    xo = x_ref[...] + gt_ref[...] * y_ref[...]
    if final:
        xo = _rms(xo, gfin_ref[...])
    o_ref[...] = xo


def _residual(x, y, modrows, g_final, final, tm):
    b, s, d = x.shape
    tm = min(tm, s)
    tile = pl.BlockSpec((None, tm, d), lambda bi, i: (bi, i, 0))
    return pl.pallas_call(
        functools.partial(_residual_kernel, final=final),
        out_shape=jax.ShapeDtypeStruct((b, s, d), F32),
        grid=(b, s // tm),
        in_specs=[tile, tile, pl.BlockSpec((None, 1, d), lambda bi, i: (bi, 0, 5)), _const_spec((1, d))],
        out_specs=tile,
        compiler_params=_params(("arbitrary", "arbitrary")),
        name="moe_residual",
    )(x, y, modrows, g_final)


def _moe(h2, x, modrows, gates, wg, wu, wd, g_final, final):
    b, s, d = x.shape
    y = _moe_routed(h2.reshape(b * s, d), gates.reshape(b * s, LANES), wg, wu, wd, MOE_TOKENS, MOE_FF_CHUNK)
    return _residual(x, y.reshape(b, s, d), modrows, g_final, final, 512)


def _rope_tables(seq):
    axis_dim = DA_QK // 2
    n_freq = axis_dim // 2
    inv = jnp.exp(-math.log(ROPE_BASE) * jnp.arange(n_freq, dtype=F32) * (2.0 / axis_dim))
    t = jnp.arange(seq, dtype=jnp.int32)
    ar = (t // GRID_W).astype(F32)[:, None] * inv
    ac = (t % GRID_W).astype(F32)[:, None] * inv
    cos = jnp.concatenate([jnp.cos(ar), jnp.cos(ar), jnp.cos(ac), jnp.cos(ac)], axis=-1)
    sin = jnp.concatenate([jnp.sin(ar), jnp.sin(ar), jnp.sin(ac), jnp.sin(ac)], axis=-1)
    first = (jnp.arange(DA_QK) % axis_dim) < n_freq
    sa = jnp.where(first, -sin, 0.0)
    sb = jnp.where(first, 0.0, sin)
    rep = lambda a: jnp.tile(a, (1, LANES // DA_QK))
    return rep(cos), rep(sa), rep(sb)


def _identity_tables(seq):
    return jnp.ones((seq, LANES), F32), jnp.zeros((seq, LANES), F32), jnp.zeros((seq, LANES), F32)


def _pack_layer_weights(w_in, w_uq, w_ukv):
    d = w_in.shape[0]
    zpad = jnp.zeros((d, LANES - MLA_ROPE), w_in.dtype)
    o_kr = 3 * 512 + MLA_Q_RANK + MLA_KV_RANK
    o_pool = o_kr + MLA_ROPE
    w2 = jnp.concatenate([w_in[:, :o_kr + MLA_ROPE], zpad, w_in[:, o_pool:o_pool + 512]], axis=1).astype(BF16)
    wgate = w_in[:, o_pool + 512:].astype(BF16)
    hq = MLA_NOPE + MLA_ROPE
    wq = w_uq.reshape(MLA_Q_RANK, MLA_HEADS, hq)
    wq = jnp.concatenate([wq, jnp.zeros((MLA_Q_RANK, MLA_HEADS, MLA_QK_PAD - hq), w_uq.dtype)], axis=-1)
    wuq = wq.reshape(MLA_Q_RANK, MLA_HEADS * MLA_QK_PAD).astype(BF16)
    wkv = w_ukv.reshape(MLA_KV_RANK, MLA_HEADS, MLA_NOPE + MLA_V)
    wukv = jnp.concatenate([wkv[:, :, :MLA_NOPE].reshape(MLA_KV_RANK, -1),
                            wkv[:, :, MLA_NOPE:].reshape(MLA_KV_RANK, -1)], axis=1).astype(BF16)
    return w2, wgate, wuq, wukv


def kernel(x, c, ctx, c_ctx, w_mod, b_mod, g_mix, w_in, da_lambda, da_subln, mla_gq, w_uq, mla_gkv, w_ukv,
           pool_w, pool_scale, w_branch, w_out, g_ffn, ffn_w_gate, ffn_w_up, ffn_w_down, moe_router,
           moe_w_gate, moe_w_up, moe_w_down, g_final):
    bsz, seq, d = x.shape
    n_ctx = ctx.shape[1]
    depth = w_mod.shape[0]

    cond8 = jnp.zeros((8, d), F32).at[:bsz].set(c).at[bsz].set(c_ctx)
    mod = _mod_rows(cond8, w_mod, b_mod)
    lat_tabs = _rope_tables(seq)
    ctx_tabs = _identity_tables(n_ctx)
    g_fin = g_final.reshape(1, d)

    xc = ctx
    for l in range(depth):
        need_ctx = l < depth - 1
        lam_init = 0.8 - 0.6 * math.exp(-0.3 * l)
        mod_lat = mod[l, :bsz][:, None, :]
        mod_ctx = jnp.broadcast_to(mod[l, bsz][None, None, :], (bsz, 1, N_MOD * d))
        w2, wgate, wuq, wukv = _pack_layer_weights(w_in[l], w_uq[l], w_ukv[l])
        gmix = g_mix[l].reshape(1, d)
        gffn = g_ffn[l].reshape(1, d)
        gq = mla_gq[l].reshape(1, -1)
        gkv = mla_gkv[l].reshape(1, -1)
        subln = da_subln[l].reshape(1, -1)
        lam = da_lambda[l]
        pw = pool_w[l].astype(BF16)
        psc = pool_scale[l].reshape(1, -1)
        wbr = w_branch[l].astype(BF16)
        wout = w_out[l].astype(BF16)
        j = l // 2
        dense = l % 2 == 0
        final = l == depth - 1

        qda, kda, vda, qm, km, vm, pin = _inproj(x, mod_lat, gmix, w2, gq, wuq, gkv, wukv, lat_tabs, PROJ_ROWS)
        cqda, ckda, cvda, cqm, ckm, cvm, cpin = _inproj(xc, mod_ctx, gmix, w2, gq, wuq, gkv, wukv, ctx_tabs, PROJ_ROWS)

        o_da = _attention(qda, ckda, cvda, kda, vda, DA_HEADS, LANES, DA_V, 256, 512, diff=True,
                          lam=lam, g=subln, lam_init=lam_init, name="diff_attn")
        o_mla = _attention(qm, ckm, cvm, km, vm, MLA_HEADS, MLA_QK_PAD, MLA_V, 512, 512, name="mla_attn")
        o_pool = _pool(pin, pw, psc, 512)

        router = None
        n_exp = moe_router.shape[-1]
        if not dense:
            router = moe_router[j].T
        outs = _merge(x, mod_lat, gmix, wgate, o_da, o_mla, o_pool, wbr, wout, gffn, router, n_exp, PROJ_ROWS)

        if dense:
            wg = ffn_w_gate[j].astype(BF16)
            wu = ffn_w_up[j].astype(BF16)
            wd = ffn_w_down[j].astype(BF16)
            x = _ffn(outs[1], outs[0], mod_lat, wg, wu, wd, g_fin, final, 256)
        else:
            wg = moe_w_gate[j].astype(BF16)
            wu = moe_w_up[j].astype(BF16)
            wd = moe_w_down[j].astype(BF16)
            x = _moe(outs[1], outs[0], mod_lat, outs[2], wg, wu, wd, g_fin, final)

        if need_ctx:
            co_da = _attention(cqda, ckda, cvda, None, None, DA_HEADS, LANES, DA_V, 256, 512, diff=True,
                               lam=lam, g=subln, lam_init=lam_init, name="diff_attn_ctx")
            co_mla = _attention(cqm, ckm, cvm, None, None, MLA_HEADS, MLA_QK_PAD, MLA_V, 256, 512,
                                name="mla_attn_ctx")
            co_pool = _pool(cpin, pw, psc, 512)
            couts = _merge(xc, mod_ctx, gmix, wgate, co_da, co_mla, co_pool, wbr, wout, gffn, router, n_exp, PROJ_ROWS)
            if dense:
                xc = _ffn(couts[1], couts[0], mod_ctx, wg, wu, wd, g_fin, False, 256)
            else:
                xc = _moe(couts[1], couts[0], mod_ctx, couts[2], wg, wu, wd, g_fin, False)
    return x
```

```python
import functools
import math

import jax
import jax.numpy as jnp
from jax import lax
from jax.experimental import pallas as pl
from jax.experimental.pallas import tpu as pltpu

F32 = jnp.float32
BF16 = jnp.bfloat16

GRID_W = 64
DA_HEADS = 4
DA_QK = 64
DA_V = 128
MLA_HEADS = 4
MLA_NOPE = 128
MLA_ROPE = 64
MLA_V = 128
MLA_Q_RANK = 384
MLA_KV_RANK = 256
POOL_WINDOWS = (2, 4, 8, 16)
POOL_GROUP_W = 128
N_BRANCH = 3
BRANCH_W = 512
ROPE_BASE = 10000.0
EPS = 1e-6
N_MOD = 6
LANES = 128
MLA_QK_PAD = 256
LOG2E = math.log2(math.e)
DA_SCALE = DA_QK ** -0.5 * LOG2E
MLA_SCALE = (MLA_NOPE + MLA_ROPE) ** -0.5 * LOG2E
NEG_BIG = -1e30
PROJ_ROWS = 512
ATTN_Q_TILES = 1
VMEM_LIMIT = 56 * 1024 * 1024

C_DAQ, C_DAK, C_DAV = 0, 512, 1024
C_QD = 1536
C_KVD = C_QD + MLA_Q_RANK
C_KR = C_KVD + MLA_KV_RANK
C_POOL = C_KR + LANES
W2_COLS = C_POOL + 512


def _sigmoid(v):
    return 1.0 / (1.0 + jnp.exp(-v))


def _params(sem, vmem=VMEM_LIMIT):
    return pltpu.CompilerParams(dimension_semantics=sem, vmem_limit_bytes=vmem)


def _const_spec(shape):
    nd = len(shape)
    return pl.BlockSpec(shape, lambda *_: (0,) * nd)


def _mod_kernel(cond_ref, w_ref, b_ref, o_ref):
    c = cond_ref[...]
    s = c * _sigmoid(c)
    o_ref[...] = jnp.dot(s, w_ref[...], preferred_element_type=F32,
                         precision=lax.Precision.HIGHEST) + b_ref[...]


def _mod_rows(cond8, w_mod, b_mod):
    depth, d, n = w_mod.shape
    tn = 1536
    return pl.pallas_call(
        _mod_kernel,
        out_shape=jax.ShapeDtypeStruct((depth, 8, n), F32),
        grid=(depth, n // tn),
        in_specs=[pl.BlockSpec((8, d), lambda l, j: (0, 0)),
                  pl.BlockSpec((None, d, tn), lambda l, j: (l, 0, j)),
                  pl.BlockSpec((None, 1, tn), lambda l, j: (l, 0, j))],
        out_specs=pl.BlockSpec((None, 8, tn), lambda l, j: (l, 0, j)),
        compiler_params=_params(("arbitrary", "arbitrary")),
        name="adaln_rows",
    )(cond8, w_mod, b_mod.reshape(depth, 1, n))


def _norm_mod(x, g, sh, sc):
    r = lax.rsqrt(jnp.mean(x * x, axis=-1, keepdims=True) + EPS)
    return (x * r * g) * (1.0 + sc) + sh


def _rms(v, g):
    return v * lax.rsqrt(jnp.mean(v * v, axis=-1, keepdims=True) + EPS) * g


def _inproj_kernel(x_ref, g_ref, sh_ref, sc_ref, w_ref, gq_ref, wuq_ref, gkv_ref, wukv_ref,
                   cos_ref, sa_ref, sb_ref,
                   qda_ref, kda_ref, vda_ref, qm_ref, km_ref, vm_ref, pool_ref):
    h = _norm_mod(x_ref[...], g_ref[...], sh_ref[...], sc_ref[...])
    z = jnp.dot(h.astype(BF16), w_ref[...], preferred_element_type=F32)
    cos = cos_ref[...]
    sa = sa_ref[...]
    sb = sb_ref[...]

    def rope(blk):
        return blk * cos + pltpu.roll(blk, LANES - 16, 1) * sa + pltpu.roll(blk, 16, 1) * sb

    lo = lax.broadcasted_iota(jnp.int32, cos.shape, 1) < MLA_ROPE

    for hh in range(DA_HEADS):
        c0 = hh * LANES
        qda_ref[:, c0:c0 + LANES] = (rope(z[:, C_DAQ + c0:C_DAQ + c0 + LANES]) * DA_SCALE).astype(BF16)
        kda_ref[:, c0:c0 + LANES] = rope(z[:, C_DAK + c0:C_DAK + c0 + LANES]).astype(BF16)

    qn = _rms(z[:, C_QD:C_QD + MLA_Q_RANK], gq_ref[...])
    qf = jnp.dot(qn.astype(BF16), wuq_ref[...], preferred_element_type=F32)
    kvn = _rms(z[:, C_KVD:C_KVD + MLA_KV_RANK], gkv_ref[...])
    kvf = jnp.dot(kvn.astype(BF16), wukv_ref[...], preferred_element_type=F32)
    kr = jnp.where(lo, rope(z[:, C_KR:C_KR + LANES]), 0.0).astype(BF16)
    for hh in range(MLA_HEADS):
        c0 = hh * MLA_QK_PAD
        qm_ref[:, c0:c0 + LANES] = (qf[:, c0:c0 + LANES] * MLA_SCALE).astype(BF16)
        qr = jnp.where(lo, rope(qf[:, c0 + LANES:c0 + 2 * LANES]), 0.0)
        qm_ref[:, c0 + LANES:c0 + 2 * LANES] = (qr * MLA_SCALE).astype(BF16)
        km_ref[:, c0:c0 + LANES] = kvf[:, hh * LANES:(hh + 1) * LANES].astype(BF16)
        km_ref[:, c0 + LANES:c0 + 2 * LANES] = kr
    ones = jnp.ones(cos.shape, BF16)
    for hh in range(MLA_HEADS):
        c0 = 2 * hh * LANES
        vda_ref[:, c0:c0 + LANES] = z[:, C_DAV + hh * LANES:C_DAV + (hh + 1) * LANES].astype(BF16)
        vda_ref[:, c0 + LANES:c0 + 2 * LANES] = ones
        vm_ref[:, c0:c0 + LANES] = kvf[:, 512 + hh * LANES:512 + (hh + 1) * LANES].astype(BF16)
        vm_ref[:, c0 + LANES:c0 + 2 * LANES] = ones
    pool_ref[...] = z[:, C_POOL:C_POOL + 512]


def _inproj(x, modrows, g_mix, w2, gq, wuq, gkv, wukv, tabs, tm):
    b, s, d = x.shape
    tm = min(tm, s)
    cos, sa, sb = tabs
    row = lambda j: pl.BlockSpec((None, 1, d), lambda bi, i: (bi, 0, j))
    tab = pl.BlockSpec((tm, LANES), lambda bi, i: (i, 0))
    out = lambda w, dt: jax.ShapeDtypeStruct((b, s, w), dt)
    ospec = lambda w: pl.BlockSpec((None, tm, w), lambda bi, i: (bi, i, 0))
    return pl.pallas_call(
        _inproj_kernel,
        out_shape=(out(512, BF16), out(512, BF16), out(1024, BF16), out(1024, BF16), out(1024, BF16),
                   out(1024, BF16), out(512, F32)),
        grid=(b, s // tm),
        in_specs=[pl.BlockSpec((None, tm, d), lambda bi, i: (bi, i, 0)),
                  _const_spec((1, d)), row(0), row(1),
                  _const_spec(w2.shape), _const_spec(gq.shape), _const_spec(wuq.shape),
                  _const_spec(gkv.shape), _const_spec(wukv.shape), tab, tab, tab],
        out_specs=(ospec(512), ospec(512), ospec(1024), ospec(1024), ospec(1024), ospec(1024), ospec(512)),
        compiler_params=_params(("arbitrary", "arbitrary")),
        name="in_proj",
    )(x, g_mix, modrows, modrows, w2, gq, wuq, gkv, wukv, cos, sa, sb)


def _attn_kernel(*refs, diff, has_lat, tk, tq, lam_init):
    it = iter(refs)
    q_ref, kc_ref, vc_ref = next(it), next(it), next(it)
    kl_ref = vl_ref = lam_ref = g_ref = None
    if has_lat:
        kl_ref, vl_ref = next(it), next(it)
    if diff:
        lam_ref, g_ref = next(it), next(it)
    o_ref = next(it)
    m_ref, acc_ref = next(it), next(it)
    s_refs = (next(it), next(it))
    p_refs = (next(it), next(it))
    al_refs = (next(it), next(it))
    n_qt = q_ref.shape[0] // tq
    dv = o_ref.shape[-1]

    def stacked_q(qt):
        q = q_ref[qt * tq:(qt + 1) * tq, :]
        if not diff:
            return q
        lane = lax.broadcasted_iota(jnp.int32, q.shape, 1)
        zero = jnp.zeros_like(q)
        return jnp.concatenate([jnp.where(lane < DA_QK, q, zero), jnp.where(lane >= DA_QK, q, zero)], axis=0)

    qs = [stacked_q(qt) for qt in range(n_qt)]
    n_ctx = kc_ref.shape[0]
    n_lat = kl_ref.shape[0] // tk if has_lat else 0
    width = lambda c: n_ctx if c == 0 else tk
    k_of = lambda c: kc_ref[...] if c == 0 else kl_ref[(c - 1) * tk:c * tk, :]
    v_of = lambda c: vc_ref[...] if c == 0 else vl_ref[(c - 1) * tk:c * tk, :]

    def qk(slot, qt, c):
        k = k_of(c)
        s_refs[slot][:, :k.shape[0]] = lax.dot_general(qs[qt], k, (((1,), (1,)), ((), ())),
                                                       preferred_element_type=F32)

    def sm(slot, qt, c):
        s_ref, p_ref = s_refs[slot], p_refs[slot]
        blocks = [slice(j * LANES, (j + 1) * LANES) for j in range(width(c) // LANES)]
        mx = s_ref[:, blocks[0]]
        for blk in blocks[1:]:
            mx = jnp.maximum(mx, s_ref[:, blk])
        m = m_ref[qt]
        m_new = jnp.maximum(m, jnp.max(mx, axis=-1, keepdims=True))
        al_refs[slot][...] = jnp.exp2(m - m_new)
        m_ref[qt] = m_new
        for blk in blocks:
            p_ref[:, blk] = jnp.exp2(s_ref[:, blk] - m_new).astype(BF16)

    def pv(slot, qt, c):
        v = v_of(c)
        new = jnp.dot(p_refs[slot][:, :v.shape[0]], v, preferred_element_type=F32)
        al = al_refs[slot][...]
        for blk in (slice(0, dv), slice(dv, 2 * dv)):
            acc_ref[qt, :, blk] = al * acc_ref[qt, :, blk] + new[:, blk]

    def finish(qt):
        acc = acc_ref[qt]
        o = acc[:, :dv] / acc[:, dv:]
        if diff:
            lv = lam_ref[...]
            a = jnp.sum(lv[0:1, :] * lv[1:2, :], axis=-1, keepdims=True)
            b = jnp.sum(lv[2:3, :] * lv[3:4, :], axis=-1, keepdims=True)
            lam = jnp.exp(a) - jnp.exp(b) + lam_init
            o = o[:tq] - lam * o[tq:]
            o = _rms(o, g_ref[...]) * (1.0 - lam_init)
        o_ref[qt * tq:(qt + 1) * tq, :] = o.astype(o_ref.dtype)

    m_ref[...] = jnp.full(m_ref.shape, NEG_BIG, F32)
    acc_ref[...] = jnp.zeros(acc_ref.shape, F32)
    items = [(qt, c) for qt in range(n_qt) for c in range(n_lat + 1)]
    for g in range(len(items) + 2):
        if g < len(items):
            qk(g % 2, *items[g])
        if 0 <= g - 2 < len(items):
            pv(g % 2, *items[g - 2])
        if 0 <= g - 1 < len(items):
            sm((g - 1) % 2, *items[g - 1])
        if 0 <= g - 2 < len(items) and items[g - 2][1] == n_lat:
            finish(items[g - 2][0])


def _attention(q, kc, vc, kl, vl, heads, dq, dv, tq, tk, diff=False, lam=None, g=None, lam_init=0.0,
               name="attn"):
    b, s, _ = q.shape
    c = kc.shape[1]
    tq = min(tq, s)
    n_qt = min(ATTN_Q_TILES, s // tq)
    has_lat = kl is not None
    args = [q, kc, vc]
    specs = [pl.BlockSpec((None, n_qt * tq, dq), lambda bi, hi, qi: (bi, qi, hi)),
             pl.BlockSpec((None, c, dq), lambda bi, hi, qi: (bi, 0, hi)),
             pl.BlockSpec((None, c, 2 * dv), lambda bi, hi, qi: (bi, 0, hi))]
    sl = 0
    if has_lat:
        sl = kl.shape[1]
        tk = min(tk, sl)
        args += [kl, vl]
        specs += [pl.BlockSpec((None, sl, dq), lambda bi, hi, qi: (bi, 0, hi)),
                  pl.BlockSpec((None, sl, 2 * dv), lambda bi, hi, qi: (bi, 0, hi))]
    if diff:
        args += [lam, g]
        specs += [_const_spec(lam.shape), _const_spec(g.shape)]
    kern = functools.partial(_attn_kernel, diff=diff, has_lat=has_lat, tk=tk, tq=tq, lam_init=lam_init)
    rows = 2 * tq if diff else tq
    wmax = max(tk, c) if has_lat else c
    scratch = [pltpu.VMEM((n_qt, rows, LANES), F32), pltpu.VMEM((n_qt, rows, 2 * dv), F32),
               pltpu.VMEM((rows, wmax), F32), pltpu.VMEM((rows, wmax), F32),
               pltpu.VMEM((rows, wmax), BF16), pltpu.VMEM((rows, wmax), BF16),
               pltpu.VMEM((rows, LANES), F32), pltpu.VMEM((rows, LANES), F32)]
    return pl.pallas_call(
        kern,
        out_shape=jax.ShapeDtypeStruct((b, s, heads * dv), BF16),
        grid=(b, heads, s // (n_qt * tq)),
        in_specs=specs,
        out_specs=pl.BlockSpec((None, n_qt * tq, dv), lambda bi, hi, qi: (bi, qi, hi)),
        scratch_shapes=scratch,
        compiler_params=_params(("arbitrary", "arbitrary", "arbitrary")),
        name=name,
    )(*args)


def _pool_kernel(prev_ref, cur_ref, next_ref, w_ref, sc_ref, o_ref, *, seq_len):
    i = pl.program_id(1)
    n = pl.num_programs(1)
    cur = cur_ref[...]
    tm = cur.shape[0]
    prev = jnp.where(i > 0, prev_ref[...], 0.0)
    nxt = jnp.where(i < n - 1, next_ref[...], 0.0)
    ext = jnp.concatenate([prev, cur, nxt], axis=0)
    ne = tm + 16
    t = i * tm + lax.broadcasted_iota(jnp.int32, (tm, 1), 0)
    for g, w in enumerate(POOL_WINDOWS):
        e = ext[:, g * LANES:(g + 1) * LANES]
        acc = e + pltpu.roll(e, 1, 0)
        half = 1
        while 2 * half < w:
            acc = pltpu.roll(acc, half, 0) + pltpu.roll(acc, ne - half, 0)
            half *= 2
        win = acc[8:8 + tm]
        lo = jnp.clip(t - w // 2, 0, seq_len)
        hi = jnp.clip(t - w // 2 + w, 0, seq_len)
        cnt = (hi - lo).astype(F32)
        dlt = win / cnt - cur[:, g * LANES:(g + 1) * LANES]
        y = jnp.dot(dlt.astype(BF16), w_ref[g], preferred_element_type=F32)
        o_ref[:, g * LANES:(g + 1) * LANES] = (y * sc_ref[:, g * LANES:(g + 1) * LANES]).astype(BF16)


def _pool(u, pool_w, pool_scale, tm):
    b, s, w = u.shape
    tm = min(tm, s)
    nb8 = s // 8
    r8 = tm // 8
    return pl.pallas_call(
        functools.partial(_pool_kernel, seq_len=s),
        out_shape=jax.ShapeDtypeStruct((b, s, w), BF16),
        grid=(b, s // tm),
        in_specs=[pl.BlockSpec((None, 8, w), lambda bi, i: (bi, jnp.maximum(i * r8 - 1, 0), 0)),
                  pl.BlockSpec((None, tm, w), lambda bi, i: (bi, i, 0)),
                  pl.BlockSpec((None, 8, w), lambda bi, i: (bi, jnp.minimum((i + 1) * r8, nb8 - 1), 0)),
                  _const_spec(pool_w.shape), _const_spec(pool_scale.shape)],
        out_specs=pl.BlockSpec((None, tm, w), lambda bi, i: (bi, i, 0)),
        compiler_params=_params(("arbitrary", "arbitrary")),
        name="pool_mixer",
    )(u, u, u, pool_w, pool_scale)


def _merge_kernel(*refs, moe, n_exp):
    (x_ref, gmix_ref, sh1_ref, sc1_ref, wgate_ref, oda_ref, omla_ref, opool_ref, wbr_ref, wout_ref,
     gt1_ref, gffn_ref, sh2_ref, sc2_ref) = refs[:14]
    if moe:
        router_ref, xo_ref, h2_ref, gates_ref = refs[14:]
    else:
        xo_ref, h2_ref = refs[14:]
    x = x_ref[...]
    d = x.shape[-1]
    h = _norm_mod(x, gmix_ref[...], sh1_ref[...], sc1_ref[...]).astype(BF16)
    merged = None
    for n, o_ref in enumerate((oda_ref, omla_ref, opool_ref)):
        gate = _sigmoid(jnp.dot(h, wgate_ref[:, n * d:(n + 1) * d], preferred_element_type=F32))
        proj = jnp.dot(o_ref[...], wbr_ref[n], preferred_element_type=F32)
        merged = gate * proj if merged is None else merged + gate * proj
    mix = jnp.dot(merged.astype(BF16), wout_ref[...], preferred_element_type=F32)
    xn = x + gt1_ref[...] * mix
    xo_ref[...] = xn
    h2 = _norm_mod(xn, gffn_ref[...], sh2_ref[...], sc2_ref[...])
    h2_ref[...] = h2.astype(BF16)
    if moe:
        logit = [jnp.sum(h2 * router_ref[e:e + 1, :], axis=-1, keepdims=True) for e in range(n_exp)]

        def top1(vals):
            best, idx = vals[0], jnp.zeros(vals[0].shape, jnp.int32)
            for e in range(1, n_exp):
                better = vals[e] > best
                best = jnp.where(better, vals[e], best)
                idx = jnp.where(better, e, idx)
            return best, idx

        v1, i1 = top1(logit)
        v2, i2 = top1([jnp.where(i1 == e, NEG_BIG, logit[e]) for e in range(n_exp)])
        w1 = 1.0 / (1.0 + jnp.exp(v2 - v1))
        lane = lax.broadcasted_iota(jnp.int32, gates_ref.shape, 1)
        gates_ref[...] = jnp.where(lane == i1, w1, 0.0) + jnp.where(lane == i2, 1.0 - w1, 0.0)


def _merge(x, modrows, g_mix, wgate, o_da, o_mla, o_pool, wbr, wout, g_ffn, router, n_exp, tm):
    b, s, d = x.shape
    tm = min(tm, s)
    moe = router is not None
    row = lambda j: pl.BlockSpec((None, 1, d), lambda bi, i: (bi, 0, j))
    tile = lambda w: pl.BlockSpec((None, tm, w), lambda bi, i: (bi, i, 0))
    args = [x, g_mix, modrows, modrows, wgate, o_da, o_mla, o_pool, wbr, wout, modrows, g_ffn, modrows, modrows]
    specs = [tile(d), _const_spec((1, d)), row(0), row(1), _const_spec(wgate.shape),
             tile(BRANCH_W), tile(BRANCH_W), tile(BRANCH_W), _const_spec(wbr.shape), _const_spec(wout.shape),
             row(2), _const_spec((1, d)), row(3), row(4)]
    out_shape = [jax.ShapeDtypeStruct((b, s, d), F32), jax.ShapeDtypeStruct((b, s, d), BF16)]
    out_specs = [tile(d), tile(d)]
    if moe:
        args.append(router)
        specs.append(_const_spec(router.shape))
        out_shape.append(jax.ShapeDtypeStruct((b, s, LANES), F32))
        out_specs.append(tile(LANES))
    return pl.pallas_call(
        functools.partial(_merge_kernel, moe=moe, n_exp=n_exp),
        out_shape=tuple(out_shape),
        grid=(b, s // tm),
        in_specs=specs,
        out_specs=tuple(out_specs),
        compiler_params=_params(("arbitrary", "arbitrary")),
        name="merge_out",
    )(*args)


def _ffn_kernel(h_ref, x_ref, gt_ref, wg_ref, wu_ref, wd_ref, gfin_ref, o_ref, *, final):
    h = h_ref[...]
    gte = jnp.dot(h, wg_ref[...], preferred_element_type=F32)
    up = jnp.dot(h, wu_ref[...], preferred_element_type=F32)
    act = (gte * _sigmoid(gte) * up).astype(BF16)
    y = jnp.dot(act, wd_ref[...], preferred_element_type=F32)
    xo = x_ref[...] + gt_ref[...] * y
    if final:
        xo = _rms(xo, gfin_ref[...])
    o_ref[...] = xo


def _ffn(h2, x, modrows, wg, wu, wd, g_final, final, tm):
    b, s, d = x.shape
    tm = min(tm, s)
    tile = pl.BlockSpec((None, tm, d), lambda bi, i: (bi, i, 0))
    return pl.pallas_call(
        functools.partial(_ffn_kernel, final=final),
        out_shape=jax.ShapeDtypeStruct((b, s, d), F32),
        grid=(b, s // tm),
        in_specs=[tile, tile, pl.BlockSpec((None, 1, d), lambda bi, i: (bi, 0, 5)),
                  _const_spec(wg.shape), _const_spec(wu.shape), _const_spec(wd.shape), _const_spec((1, d))],
        out_specs=tile,
        compiler_params=_params(("arbitrary", "arbitrary")),
        name="ffn_swiglu",
    )(h2, x, modrows, wg, wu, wd, g_final)


ROUTE_SUB = 256
MOE_TOKENS = 1024
MOE_FF_CHUNK = 1792
MOE_ROWS = 288
MOE_ROW_ALIGN = 32


def _route_kernel(g_ref, rank_ref, rankt_ref, cnt_ref, *, n_exp):
    t = g_ref.shape[0]
    r_io = lax.broadcasted_iota(jnp.int32, (ROUTE_SUB, ROUTE_SUB), 0)
    c_io = lax.broadcasted_iota(jnp.int32, (ROUTE_SUB, ROUTE_SUB), 1)
    tri = jnp.where(c_io <= r_io, 1.0, 0.0).astype(BF16)
    carry = jnp.zeros((1, LANES), F32)
    for j in range(t // ROUTE_SUB):
        rows = slice(j * ROUTE_SUB, (j + 1) * ROUTE_SUB)
        routed = g_ref[rows, :] != 0.0
        incl = jnp.dot(tri, jnp.where(routed, 1.0, 0.0).astype(BF16), preferred_element_type=F32)
        rank_ref[rows, :] = jnp.where(routed, incl - 1.0 + carry, -1.0)
        carry = carry + incl[ROUTE_SUB - 1:ROUTE_SUB, :]
    rankt_ref[...] = rank_ref[...].T[0:n_exp, :]
    cnt_ref[...] = jnp.broadcast_to(carry, cnt_ref.shape).astype(jnp.int32)


def _route(gates, n_exp, tm):
    n = gates.shape[0]
    return pl.pallas_call(
        functools.partial(_route_kernel, n_exp=n_exp),
        out_shape=(jax.ShapeDtypeStruct((n, LANES), F32),
                   jax.ShapeDtypeStruct((n // tm, n_exp, tm), F32),
                   jax.ShapeDtypeStruct((n // tm, 8, LANES), jnp.int32)),
        grid=(n // tm,),
        in_specs=[pl.BlockSpec((tm, LANES), lambda i: (i, 0))],
        out_specs=(pl.BlockSpec((tm, LANES), lambda i: (i, 0)),
                   pl.BlockSpec((None, n_exp, tm), lambda i: (i, 0, 0)),
                   pl.BlockSpec((None, 8, LANES), lambda i: (i, 0, 0))),
        compiler_params=_params(("arbitrary",)),
        name="moe_route",
    )(gates)


def _moe_kernel(cnt_ref, h_ref, rank_ref, rankt_ref, gates_ref, wg_ref, wu_ref, wd_ref, o_ref,
                xg_ref, yg_ref, *, n_exp):
    i, e, f = pl.program_id(0), pl.program_id(1), pl.program_id(2)
    nf = pl.num_programs(2)
    t = h_ref.shape[0]
    n_blk = cnt_ref[i * n_exp + e]

    @pl.when(jnp.logical_and(e == 0, f == 0))
    def _():
        o_ref[...] = jnp.zeros_like(o_ref)

    rank_row = rankt_ref[pl.ds(e, 1), :]

    def block(b, carry):
        base = pl.multiple_of(b * MOE_ROWS, MOE_ROW_ALIGN)
        rows = pl.ds(base, MOE_ROWS)

        @pl.when(f == 0)
        def _():
            slot = (base + lax.broadcasted_iota(jnp.int32, (MOE_ROWS, t), 0)).astype(F32)
            onehot = jnp.where(rank_row == slot, 1.0, 0.0).astype(BF16)
            xg_ref[rows, :] = jnp.dot(onehot, h_ref[...], preferred_element_type=F32).astype(BF16)

        xb = xg_ref[rows, :]
        gte = jnp.dot(xb, wg_ref[...], preferred_element_type=F32)
        up = jnp.dot(xb, wu_ref[...], preferred_element_type=F32)
        act = (gte * _sigmoid(gte) * up).astype(BF16)
        part = jnp.dot(act, wd_ref[...], preferred_element_type=F32)

        @pl.when(f == 0)
        def _():
            yg_ref[rows, :] = part

        @pl.when(f > 0)
        def _():
            yg_ref[rows, :] += part

        @pl.when(f == nf - 1)
        def _():
            lane = lax.broadcasted_iota(jnp.int32, (t, LANES), 1)
            rank_col = jnp.sum(jnp.where(lane == e, rank_ref[...], 0.0), axis=-1, keepdims=True)
            gate_col = jnp.sum(jnp.where(lane == e, gates_ref[...], 0.0), axis=-1, keepdims=True)
            slot = (base + lax.broadcasted_iota(jnp.int32, (t, MOE_ROWS), 1)).astype(F32)
            onehot = jnp.where(rank_col == slot, 1.0, 0.0).astype(BF16)
            o_ref[...] += gate_col * jnp.dot(onehot, yg_ref[rows, :].astype(BF16),
                                             preferred_element_type=F32)
        return carry

    lax.fori_loop(0, n_blk, block, 0)


def _moe_routed(h2, gates, wg, wu, wd, tm, tf):
    n, d = h2.shape
    n_exp, _, dff = wg.shape
    tm = min(tm, n)
    chunked = lambda w: w.reshape(n_exp, d, dff // tf, tf).transpose(0, 2, 1, 3)
    wg, wu = chunked(wg), chunked(wu)
    rank, rank_t, cnt = _route(gates, n_exp, tm)
    n_blocks = (cnt[:, 0, :n_exp].reshape(-1) + (MOE_ROWS - 1)) // MOE_ROWS
    cap = pl.cdiv(tm, MOE_ROWS) * MOE_ROWS
    grid_spec = pltpu.PrefetchScalarGridSpec(
        num_scalar_prefetch=1,
        grid=(n // tm, n_exp, dff // tf),
        in_specs=[pl.BlockSpec((tm, d), lambda i, e, f, c: (i, 0)),
                  pl.BlockSpec((tm, LANES), lambda i, e, f, c: (i, 0)),
                  pl.BlockSpec((None, n_exp, tm), lambda i, e, f, c: (i, 0, 0)),
                  pl.BlockSpec((tm, LANES), lambda i, e, f, c: (i, 0)),
                  pl.BlockSpec((None, None, d, tf), lambda i, e, f, c: (e, f, 0, 0)),
                  pl.BlockSpec((None, None, d, tf), lambda i, e, f, c: (e, f, 0, 0)),
                  pl.BlockSpec((None, tf, d), lambda i, e, f, c: (e, f, 0))],
        out_specs=pl.BlockSpec((tm, d), lambda i, e, f, c: (i, 0)),
        scratch_shapes=[pltpu.VMEM((cap, d), BF16), pltpu.VMEM((cap, d), F32)])
    return pl.pallas_call(
        functools.partial(_moe_kernel, n_exp=n_exp),
        out_shape=jax.ShapeDtypeStruct((n, d), F32),
        grid_spec=grid_spec,
        compiler_params=_params(("arbitrary",) * 3),
        name="moe_swiglu",
    )(n_blocks, h2, rank, rank_t, gates, wg, wu, wd)


def _residual_kernel(x_ref, y_ref, gt_ref, gfin_ref, o_ref, *, final):
    xo = x_ref[...] + gt_ref[...] * y_ref[...]
    if final:
        xo = _rms(xo, gfin_ref[...])
    o_ref[...] = xo


def _residual(x, y, modrows, g_final, final, tm):
    b, s, d = x.shape
    tm = min(tm, s)
    tile = pl.BlockSpec((None, tm, d), lambda bi, i: (bi, i, 0))
    return pl.pallas_call(
        functools.partial(_residual_kernel, final=final),
        out_shape=jax.ShapeDtypeStruct((b, s, d), F32),
        grid=(b, s // tm),
        in_specs=[tile, tile, pl.BlockSpec((None, 1, d), lambda bi, i: (bi, 0, 5)), _const_spec((1, d))],
        out_specs=tile,
        compiler_params=_params(("arbitrary", "arbitrary")),
        name="moe_residual",
    )(x, y, modrows, g_final)


def _moe(h2, x, modrows, gates, wg, wu, wd, g_final, final):
    b, s, d = x.shape
    y = _moe_routed(h2.reshape(b * s, d), gates.reshape(b * s, LANES), wg, wu, wd, MOE_TOKENS, MOE_FF_CHUNK)
    return _residual(x, y.reshape(b, s, d), modrows, g_final, final, 512)


def _rope_tables(seq):
    axis_dim = DA_QK // 2
    n_freq = axis_dim // 2
    inv = jnp.exp(-math.log(ROPE_BASE) * jnp.arange(n_freq, dtype=F32) * (2.0 / axis_dim))
    t = jnp.arange(seq, dtype=jnp.int32)
    ar = (t // GRID_W).astype(F32)[:, None] * inv
    ac = (t % GRID_W).astype(F32)[:, None] * inv
    cos = jnp.concatenate([jnp.cos(ar), jnp.cos(ar), jnp.cos(ac), jnp.cos(ac)], axis=-1)
    sin = jnp.concatenate([jnp.sin(ar), jnp.sin(ar), jnp.sin(ac), jnp.sin(ac)], axis=-1)
    first = (jnp.arange(DA_QK) % axis_dim) < n_freq
    sa = jnp.where(first, -sin, 0.0)
    sb = jnp.where(first, 0.0, sin)
    rep = lambda a: jnp.tile(a, (1, LANES // DA_QK))
    return rep(cos), rep(sa), rep(sb)


def _identity_tables(seq):
    return jnp.ones((seq, LANES), F32), jnp.zeros((seq, LANES), F32), jnp.zeros((seq, LANES), F32)


def _pack_layer_weights(w_in, w_uq, w_ukv):
    d = w_in.shape[0]
    zpad = jnp.zeros((d, LANES - MLA_ROPE), w_in.dtype)
    o_kr = 3 * 512 + MLA_Q_RANK + MLA_KV_RANK
    o_pool = o_kr + MLA_ROPE
    w2 = jnp.concatenate([w_in[:, :o_kr + MLA_ROPE], zpad, w_in[:, o_pool:o_pool + 512]], axis=1).astype(BF16)
    wgate = w_in[:, o_pool + 512:].astype(BF16)
    hq = MLA_NOPE + MLA_ROPE
    wq = w_uq.reshape(MLA_Q_RANK, MLA_HEADS, hq)
    wq = jnp.concatenate([wq, jnp.zeros((MLA_Q_RANK, MLA_HEADS, MLA_QK_PAD - hq), w_uq.dtype)], axis=-1)
    wuq = wq.reshape(MLA_Q_RANK, MLA_HEADS * MLA_QK_PAD).astype(BF16)
    wkv = w_ukv.reshape(MLA_KV_RANK, MLA_HEADS, MLA_NOPE + MLA_V)
    wukv = jnp.concatenate([wkv[:, :, :MLA_NOPE].reshape(MLA_KV_RANK, -1),
                            wkv[:, :, MLA_NOPE:].reshape(MLA_KV_RANK, -1)], axis=1).astype(BF16)
    return w2, wgate, wuq, wukv


def kernel(x, c, ctx, c_ctx, w_mod, b_mod, g_mix, w_in, da_lambda, da_subln, mla_gq, w_uq, mla_gkv, w_ukv,
           pool_w, pool_scale, w_branch, w_out, g_ffn, ffn_w_gate, ffn_w_up, ffn_w_down, moe_router,
           moe_w_gate, moe_w_up, moe_w_down, g_final):
    bsz, seq, d = x.shape
    n_ctx = ctx.shape[1]
    depth = w_mod.shape[0]

    cond8 = jnp.zeros((8, d), F32).at[:bsz].set(c).at[bsz].set(c_ctx)
    mod = _mod_rows(cond8, w_mod, b_mod)
    lat_tabs = _rope_tables(seq)
    ctx_tabs = _identity_tables(n_ctx)
    g_fin = g_final.reshape(1, d)

    xc = ctx
    for l in range(depth):
        need_ctx = l < depth - 1
        lam_init = 0.8 - 0.6 * math.exp(-0.3 * l)
        mod_lat = mod[l, :bsz][:, None, :]
        mod_ctx = jnp.broadcast_to(mod[l, bsz][None, None, :], (bsz, 1, N_MOD * d))
        w2, wgate, wuq, wukv = _pack_layer_weights(w_in[l], w_uq[l], w_ukv[l])
        gmix = g_mix[l].reshape(1, d)
        gffn = g_ffn[l].reshape(1, d)
        gq = mla_gq[l].reshape(1, -1)
        gkv = mla_gkv[l].reshape(1, -1)
        subln = da_subln[l].reshape(1, -1)
        lam = da_lambda[l]
        pw = pool_w[l].astype(BF16)
        psc = pool_scale[l].reshape(1, -1)
        wbr = w_branch[l].astype(BF16)
        wout = w_out[l].astype(BF16)
        j = l // 2
        dense = l % 2 == 0
        final = l == depth - 1

        qda, kda, vda, qm, km, vm, pin = _inproj(x, mod_lat, gmix, w2, gq, wuq, gkv, wukv, lat_tabs, PROJ_ROWS)
        cqda, ckda, cvda, cqm, ckm, cvm, cpin = _inproj(xc, mod_ctx, gmix, w2, gq, wuq, gkv, wukv, ctx_tabs, PROJ_ROWS)

        o_da = _attention(qda, ckda, cvda, kda, vda, DA_HEADS, LANES, DA_V, 256, 512, diff=True,
                          lam=lam, g=subln, lam_init=lam_init, name="diff_attn")
        o_mla = _attention(qm, ckm, cvm, km, vm, MLA_HEADS, MLA_QK_PAD, MLA_V, 512, 512, name="mla_attn")
        o_pool = _pool(pin, pw, psc, 512)

        router = None
        n_exp = moe_router.shape[-1]
        if not dense:
            router = moe_router[j].T
        outs = _merge(x, mod_lat, gmix, wgate, o_da, o_mla, o_pool, wbr, wout, gffn, router, n_exp, PROJ_ROWS)

        if dense:
            wg = ffn_w_gate[j].astype(BF16)
            wu = ffn_w_up[j].astype(BF16)
            wd = ffn_w_down[j].astype(BF16)
            x = _ffn(outs[1], outs[0], mod_lat, wg, wu, wd, g_fin, final, 256)
        else:
            wg = moe_w_gate[j].astype(BF16)
            wu = moe_w_up[j].astype(BF16)
            wd = moe_w_down[j].astype(BF16)
            x = _moe(outs[1], outs[0], mod_lat, outs[2], wg, wu, wd, g_fin, final)

        if need_ctx:
            co_da = _attention(cqda, ckda, cvda, None, None, DA_HEADS, LANES, DA_V, 256, 512, diff=True,
                               lam=lam, g=subln, lam_init=lam_init, name="diff_attn_ctx")
            co_mla = _attention(cqm, ckm, cvm, None, None, MLA_HEADS, MLA_QK_PAD, MLA_V, 256, 512,
                                name="mla_attn_ctx")
            co_pool = _pool(cpin, pw, psc, 512)
            couts = _merge(xc, mod_ctx, gmix, wgate, co_da, co_mla, co_pool, wbr, wout, gffn, router, n_exp, PROJ_ROWS)
            if dense:
                xc = _ffn(couts[1], couts[0], mod_ctx, wg, wu, wd, g_fin, False, 256)
            else:
                xc = _moe(couts[1], couts[0], mod_ctx, couts[2], wg, wu, wd, g_fin, False)
    return x
```

```python
import functools
import math

import jax
import jax.numpy as jnp
from jax import lax
from jax.experimental import pallas as pl
from jax.experimental.pallas import tpu as pltpu

F32 = jnp.float32
BF16 = jnp.bfloat16

GRID_W = 64
DA_HEADS = 4
DA_QK = 64
DA_V = 128
MLA_HEADS = 4
MLA_NOPE = 128
MLA_ROPE = 64
MLA_V = 128
MLA_Q_RANK = 384
MLA_KV_RANK = 256
POOL_WINDOWS = (2, 4, 8, 16)
POOL_GROUP_W = 128
N_BRANCH = 3
BRANCH_W = 512
ROPE_BASE = 10000.0
EPS = 1e-6
N_MOD = 6
LANES = 128
MLA_QK_PAD = 256
LOG2E = math.log2(math.e)
DA_SCALE = DA_QK ** -0.5 * LOG2E
MLA_SCALE = (MLA_NOPE + MLA_ROPE) ** -0.5 * LOG2E
NEG_BIG = -1e30
PROJ_ROWS = 512
ATTN_Q_TILES = 1
VMEM_LIMIT = 56 * 1024 * 1024

C_DAQ, C_DAK, C_DAV = 0, 512, 1024
C_QD = 1536
C_KVD = C_QD + MLA_Q_RANK
C_KR = C_KVD + MLA_KV_RANK
C_POOL = C_KR + LANES
W2_COLS = C_POOL + 512


def _sigmoid(v):
    return 1.0 / (1.0 + jnp.exp(-v))


def _params(sem, vmem=VMEM_LIMIT):
    return pltpu.CompilerParams(dimension_semantics=sem, vmem_limit_bytes=vmem)


def _const_spec(shape):
    nd = len(shape)
    return pl.BlockSpec(shape, lambda *_: (0,) * nd)


def _mod_kernel(cond_ref, w_ref, b_ref, o_ref):
    c = cond_ref[...]
    s = c * _sigmoid(c)
    o_ref[...] = jnp.dot(s, w_ref[...], preferred_element_type=F32,
                         precision=lax.Precision.HIGHEST) + b_ref[...]


def _mod_rows(cond8, w_mod, b_mod):
    depth, d, n = w_mod.shape
    tn = 1536
    return pl.pallas_call(
        _mod_kernel,
        out_shape=jax.ShapeDtypeStruct((depth, 8, n), F32),
        grid=(depth, n // tn),
        in_specs=[pl.BlockSpec((8, d), lambda l, j: (0, 0)),
                  pl.BlockSpec((None, d, tn), lambda l, j: (l, 0, j)),
                  pl.BlockSpec((None, 1, tn), lambda l, j: (l, 0, j))],
        out_specs=pl.BlockSpec((None, 8, tn), lambda l, j: (l, 0, j)),
        compiler_params=_params(("arbitrary", "arbitrary")),
        name="adaln_rows",
    )(cond8, w_mod, b_mod.reshape(depth, 1, n))


def _norm_mod(x, g, sh, sc):
    r = lax.rsqrt(jnp.mean(x * x, axis=-1, keepdims=True) + EPS)
    return (x * r * g) * (1.0 + sc) + sh


def _rms(v, g):
    return v * lax.rsqrt(jnp.mean(v * v, axis=-1, keepdims=True) + EPS) * g


def _inproj_kernel(x_ref, g_ref, sh_ref, sc_ref, w_ref, gq_ref, wuq_ref, gkv_ref, wukv_ref,
                   cos_ref, sa_ref, sb_ref,
                   qda_ref, kda_ref, vda_ref, qm_ref, km_ref, vm_ref, pool_ref):
    h = _norm_mod(x_ref[...], g_ref[...], sh_ref[...], sc_ref[...])
    z = jnp.dot(h.astype(BF16), w_ref[...], preferred_element_type=F32)
    cos = cos_ref[...]
    sa = sa_ref[...]
    sb = sb_ref[...]

    def rope(blk):
        return blk * cos + pltpu.roll(blk, LANES - 16, 1) * sa + pltpu.roll(blk, 16, 1) * sb

    lo = lax.broadcasted_iota(jnp.int32, cos.shape, 1) < MLA_ROPE

    for hh in range(DA_HEADS):
        c0 = hh * LANES
        qda_ref[:, c0:c0 + LANES] = (rope(z[:, C_DAQ + c0:C_DAQ + c0 + LANES]) * DA_SCALE).astype(BF16)
        kda_ref[:, c0:c0 + LANES] = rope(z[:, C_DAK + c0:C_DAK + c0 + LANES]).astype(BF16)

    qn = _rms(z[:, C_QD:C_QD + MLA_Q_RANK], gq_ref[...])
    qf = jnp.dot(qn.astype(BF16), wuq_ref[...], preferred_element_type=F32)
    kvn = _rms(z[:, C_KVD:C_KVD + MLA_KV_RANK], gkv_ref[...])
    kvf = jnp.dot(kvn.astype(BF16), wukv_ref[...], preferred_element_type=F32)
    kr = jnp.where(lo, rope(z[:, C_KR:C_KR + LANES]), 0.0).astype(BF16)
    for hh in range(MLA_HEADS):
        c0 = hh * MLA_QK_PAD
        qm_ref[:, c0:c0 + LANES] = (qf[:, c0:c0 + LANES] * MLA_SCALE).astype(BF16)
        qr = jnp.where(lo, rope(qf[:, c0 + LANES:c0 + 2 * LANES]), 0.0)
        qm_ref[:, c0 + LANES:c0 + 2 * LANES] = (qr * MLA_SCALE).astype(BF16)
        km_ref[:, c0:c0 + LANES] = kvf[:, hh * LANES:(hh + 1) * LANES].astype(BF16)
        km_ref[:, c0 + LANES:c0 + 2 * LANES] = kr
    ones = jnp.ones(cos.shape, BF16)
    for hh in range(MLA_HEADS):
        c0 = 2 * hh * LANES
        vda_ref[:, c0:c0 + LANES] = z[:, C_DAV + hh * LANES:C_DAV + (hh + 1) * LANES].astype(BF16)
        vda_ref[:, c0 + LANES:c0 + 2 * LANES] = ones
        vm_ref[:, c0:c0 + LANES] = kvf[:, 512 + hh * LANES:512 + (hh + 1) * LANES].astype(BF16)
        vm_ref[:, c0 + LANES:c0 + 2 * LANES] = ones
    pool_ref[...] = z[:, C_POOL:C_POOL + 512]


def _inproj(x, modrows, g_mix, w2, gq, wuq, gkv, wukv, tabs, tm):
    b, s, d = x.shape
    tm = min(tm, s)
    cos, sa, sb = tabs
    row = lambda j: pl.BlockSpec((None, 1, d), lambda bi, i: (bi, 0, j))
    tab = pl.BlockSpec((tm, LANES), lambda bi, i: (i, 0))
    out = lambda w, dt: jax.ShapeDtypeStruct((b, s, w), dt)
    ospec = lambda w: pl.BlockSpec((None, tm, w), lambda bi, i: (bi, i, 0))
    return pl.pallas_call(
        _inproj_kernel,
        out_shape=(out(512, BF16), out(512, BF16), out(1024, BF16), out(1024, BF16), out(1024, BF16),
                   out(1024, BF16), out(512, F32)),
        grid=(b, s // tm),
        in_specs=[pl.BlockSpec((None, tm, d), lambda bi, i: (bi, i, 0)),
                  _const_spec((1, d)), row(0), row(1),
                  _const_spec(w2.shape), _const_spec(gq.shape), _const_spec(wuq.shape),
                  _const_spec(gkv.shape), _const_spec(wukv.shape), tab, tab, tab],
        out_specs=(ospec(512), ospec(512), ospec(1024), ospec(1024), ospec(1024), ospec(1024), ospec(512)),
        compiler_params=_params(("arbitrary", "arbitrary")),
        name="in_proj",
    )(x, g_mix, modrows, modrows, w2, gq, wuq, gkv, wukv, cos, sa, sb)


def _attn_kernel(*refs, diff, has_lat, tk, tq, lam_init):
    it = iter(refs)
    q_ref, kc_ref, vc_ref = next(it), next(it), next(it)
    kl_ref = vl_ref = lam_ref = g_ref = None
    if has_lat:
        kl_ref, vl_ref = next(it), next(it)
    if diff:
        lam_ref, g_ref = next(it), next(it)
    o_ref = next(it)
    m_ref, acc_ref = next(it), next(it)
    s_refs = (next(it), next(it))
    p_refs = (next(it), next(it))
    al_refs = (next(it), next(it))
    n_qt = q_ref.shape[0] // tq
    dv = o_ref.shape[-1]

    def stacked_q(qt):
        q = q_ref[qt * tq:(qt + 1) * tq, :]
        if not diff:
            return q
        lane = lax.broadcasted_iota(jnp.int32, q.shape, 1)
        zero = jnp.zeros_like(q)
        return jnp.concatenate([jnp.where(lane < DA_QK, q, zero), jnp.where(lane >= DA_QK, q, zero)], axis=0)

    qs = [stacked_q(qt) for qt in range(n_qt)]
    n_ctx = kc_ref.shape[0]
    n_lat = kl_ref.shape[0] // tk if has_lat else 0
    width = lambda c: n_ctx if c == 0 else tk
    k_of = lambda c: kc_ref[...] if c == 0 else kl_ref[(c - 1) * tk:c * tk, :]
    v_of = lambda c: vc_ref[...] if c == 0 else vl_ref[(c - 1) * tk:c * tk, :]

    def qk(slot, qt, c):
        k = k_of(c)
        s_refs[slot][:, :k.shape[0]] = lax.dot_general(qs[qt], k, (((1,), (1,)), ((), ())),
                                                       preferred_element_type=F32)

    def sm(slot, qt, c):
        s_ref, p_ref = s_refs[slot], p_refs[slot]
        blocks = [slice(j * LANES, (j + 1) * LANES) for j in range(width(c) // LANES)]
        mx = s_ref[:, blocks[0]]
        for blk in blocks[1:]:
            mx = jnp.maximum(mx, s_ref[:, blk])
        m = m_ref[qt]
        m_new = jnp.maximum(m, jnp.max(mx, axis=-1, keepdims=True))
        al_refs[slot][...] = jnp.exp2(m - m_new)
        m_ref[qt] = m_new
        for blk in blocks:
            p_ref[:, blk] = jnp.exp2(s_ref[:, blk] - m_new).astype(BF16)

    def pv(slot, qt, c):
        v = v_of(c)
        new = jnp.dot(p_refs[slot][:, :v.shape[0]], v, preferred_element_type=F32)
        al = al_refs[slot][...]
        for blk in (slice(0, dv), slice(dv, 2 * dv)):
            acc_ref[qt, :, blk] = al * acc_ref[qt, :, blk] + new[:, blk]

    def finish(qt):
        acc = acc_ref[qt]
        o = acc[:, :dv] / acc[:, dv:]
        if diff:
            lv = lam_ref[...]
            a = jnp.sum(lv[0:1, :] * lv[1:2, :], axis=-1, keepdims=True)
            b = jnp.sum(lv[2:3, :] * lv[3:4, :], axis=-1, keepdims=True)
            lam = jnp.exp(a) - jnp.exp(b) + lam_init
            o = o[:tq] - lam * o[tq:]
            o = _rms(o, g_ref[...]) * (1.0 - lam_init)
        o_ref[qt * tq:(qt + 1) * tq, :] = o.astype(o_ref.dtype)

    m_ref[...] = jnp.full(m_ref.shape, NEG_BIG, F32)
    acc_ref[...] = jnp.zeros(acc_ref.shape, F32)
    items = [(qt, c) for qt in range(n_qt) for c in range(n_lat + 1)]
    for g in range(len(items) + 2):
        if g < len(items):
            qk(g % 2, *items[g])
        if 0 <= g - 2 < len(items):
            pv(g % 2, *items[g - 2])
        if 0 <= g - 1 < len(items):
            sm((g - 1) % 2, *items[g - 1])
        if 0 <= g - 2 < len(items) and items[g - 2][1] == n_lat:
            finish(items[g - 2][0])


def _attention(q, kc, vc, kl, vl, heads, dq, dv, tq, tk, diff=False, lam=None, g=None, lam_init=0.0,
               name="attn"):
    b, s, _ = q.shape
    c = kc.shape[1]
    tq = min(tq, s)
    n_qt = min(ATTN_Q_TILES, s // tq)
    has_lat = kl is not None
    args = [q, kc, vc]
    specs = [pl.BlockSpec((None, n_qt * tq, dq), lambda bi, hi, qi: (bi, qi, hi)),
             pl.BlockSpec((None, c, dq), lambda bi, hi, qi: (bi, 0, hi)),
             pl.BlockSpec((None, c, 2 * dv), lambda bi, hi, qi: (bi, 0, hi))]
    sl = 0
    if has_lat:
        sl = kl.shape[1]
        tk = min(tk, sl)
        args += [kl, vl]
        specs += [pl.BlockSpec((None, sl, dq), lambda bi, hi, qi: (bi, 0, hi)),
                  pl.BlockSpec((None, sl, 2 * dv), lambda bi, hi, qi: (bi, 0, hi))]
    if diff:
        args += [lam, g]
        specs += [_const_spec(lam.shape), _const_spec(g.shape)]
    kern = functools.partial(_attn_kernel, diff=diff, has_lat=has_lat, tk=tk, tq=tq, lam_init=lam_init)
    rows = 2 * tq if diff else tq
    wmax = max(tk, c) if has_lat else c
    scratch = [pltpu.VMEM((n_qt, rows, LANES), F32), pltpu.VMEM((n_qt, rows, 2 * dv), F32),
               pltpu.VMEM((rows, wmax), F32), pltpu.VMEM((rows, wmax), F32),
               pltpu.VMEM((rows, wmax), BF16), pltpu.VMEM((rows, wmax), BF16),
               pltpu.VMEM((rows, LANES), F32), pltpu.VMEM((rows, LANES), F32)]
    return pl.pallas_call(
        kern,
        out_shape=jax.ShapeDtypeStruct((b, s, heads * dv), BF16),
        grid=(b, heads, s // (n_qt * tq)),
        in_specs=specs,
        out_specs=pl.BlockSpec((None, n_qt * tq, dv), lambda bi, hi, qi: (bi, qi, hi)),
        scratch_shapes=scratch,
        compiler_params=_params(("arbitrary", "arbitrary", "arbitrary")),
        name=name,
    )(*args)


def _pool_kernel(prev_ref, cur_ref, next_ref, w_ref, sc_ref, o_ref, *, seq_len):
    i = pl.program_id(1)
    n = pl.num_programs(1)
    cur = cur_ref[...]
    tm = cur.shape[0]
    prev = jnp.where(i > 0, prev_ref[...], 0.0)
    nxt = jnp.where(i < n - 1, next_ref[...], 0.0)
    ext = jnp.concatenate([prev, cur, nxt], axis=0)
    ne = tm + 16
    t = i * tm + lax.broadcasted_iota(jnp.int32, (tm, 1), 0)
    for g, w in enumerate(POOL_WINDOWS):
        e = ext[:, g * LANES:(g + 1) * LANES]
        acc = e + pltpu.roll(e, 1, 0)
        half = 1
        while 2 * half < w:
            acc = pltpu.roll(acc, half, 0) + pltpu.roll(acc, ne - half, 0)
            half *= 2
        win = acc[8:8 + tm]
        lo = jnp.clip(t - w // 2, 0, seq_len)
        hi = jnp.clip(t - w // 2 + w, 0, seq_len)
        cnt = (hi - lo).astype(F32)
        dlt = win / cnt - cur[:, g * LANES:(g + 1) * LANES]
        y = jnp.dot(dlt.astype(BF16), w_ref[g], preferred_element_type=F32)
        o_ref[:, g * LANES:(g + 1) * LANES] = (y * sc_ref[:, g * LANES:(g + 1) * LANES]).astype(BF16)


def _pool(u, pool_w, pool_scale, tm):
    b, s, w = u.shape
    tm = min(tm, s)
    nb8 = s // 8
    r8 = tm // 8
    return pl.pallas_call(
        functools.partial(_pool_kernel, seq_len=s),
        out_shape=jax.ShapeDtypeStruct((b, s, w), BF16),
        grid=(b, s // tm),
        in_specs=[pl.BlockSpec((None, 8, w), lambda bi, i: (bi, jnp.maximum(i * r8 - 1, 0), 0)),
                  pl.BlockSpec((None, tm, w), lambda bi, i: (bi, i, 0)),
                  pl.BlockSpec((None, 8, w), lambda bi, i: (bi, jnp.minimum((i + 1) * r8, nb8 - 1), 0)),
                  _const_spec(pool_w.shape), _const_spec(pool_scale.shape)],
        out_specs=pl.BlockSpec((None, tm, w), lambda bi, i: (bi, i, 0)),
        compiler_params=_params(("arbitrary", "arbitrary")),
        name="pool_mixer",
    )(u, u, u, pool_w, pool_scale)


def _merge_kernel(*refs, moe, n_exp):
    (x_ref, gmix_ref, sh1_ref, sc1_ref, wgate_ref, oda_ref, omla_ref, opool_ref, wbr_ref, wout_ref,
     gt1_ref, gffn_ref, sh2_ref, sc2_ref) = refs[:14]
    if moe:
        router_ref, xo_ref, h2_ref, gates_ref = refs[14:]
    else:
        xo_ref, h2_ref = refs[14:]
    x = x_ref[...]
    d = x.shape[-1]
    h = _norm_mod(x, gmix_ref[...], sh1_ref[...], sc1_ref[...]).astype(BF16)
    merged = None
    for n, o_ref in enumerate((oda_ref, omla_ref, opool_ref)):
        gate = _sigmoid(jnp.dot(h, wgate_ref[:, n * d:(n + 1) * d], preferred_element_type=F32))
        proj = jnp.dot(o_ref[...], wbr_ref[n], preferred_element_type=F32)
        merged = gate * proj if merged is None else merged + gate * proj
    mix = jnp.dot(merged.astype(BF16), wout_ref[...], preferred_element_type=F32)
    xn = x + gt1_ref[...] * mix
    xo_ref[...] = xn
    h2 = _norm_mod(xn, gffn_ref[...], sh2_ref[...], sc2_ref[...])
    h2_ref[...] = h2.astype(BF16)
    if moe:
        logit = [jnp.sum(h2 * router_ref[e:e + 1, :], axis=-1, keepdims=True) for e in range(n_exp)]

        def top1(vals):
            best, idx = vals[0], jnp.zeros(vals[0].shape, jnp.int32)
            for e in range(1, n_exp):
                better = vals[e] > best
                best = jnp.where(better, vals[e], best)
                idx = jnp.where(better, e, idx)
            return best, idx

        v1, i1 = top1(logit)
        v2, i2 = top1([jnp.where(i1 == e, NEG_BIG, logit[e]) for e in range(n_exp)])
        w1 = 1.0 / (1.0 + jnp.exp(v2 - v1))
        lane = lax.broadcasted_iota(jnp.int32, gates_ref.shape, 1)
        gates_ref[...] = jnp.where(lane == i1, w1, 0.0) + jnp.where(lane == i2, 1.0 - w1, 0.0)


def _merge(x, modrows, g_mix, wgate, o_da, o_mla, o_pool, wbr, wout, g_ffn, router, n_exp, tm):
    b, s, d = x.shape
    tm = min(tm, s)
    moe = router is not None
    row = lambda j: pl.BlockSpec((None, 1, d), lambda bi, i: (bi, 0, j))
    tile = lambda w: pl.BlockSpec((None, tm, w), lambda bi, i: (bi, i, 0))
    args = [x, g_mix, modrows, modrows, wgate, o_da, o_mla, o_pool, wbr, wout, modrows, g_ffn, modrows, modrows]
    specs = [tile(d), _const_spec((1, d)), row(0), row(1), _const_spec(wgate.shape),
             tile(BRANCH_W), tile(BRANCH_W), tile(BRANCH_W), _const_spec(wbr.shape), _const_spec(wout.shape),
             row(2), _const_spec((1, d)), row(3), row(4)]
    out_shape = [jax.ShapeDtypeStruct((b, s, d), F32), jax.ShapeDtypeStruct((b, s, d), BF16)]
    out_specs = [tile(d), tile(d)]
    if moe:
        args.append(router)
        specs.append(_const_spec(router.shape))
        out_shape.append(jax.ShapeDtypeStruct((b, s, LANES), F32))
        out_specs.append(tile(LANES))
    return pl.pallas_call(
        functools.partial(_merge_kernel, moe=moe, n_exp=n_exp),
        out_shape=tuple(out_shape),
        grid=(b, s // tm),
        in_specs=specs,
        out_specs=tuple(out_specs),
        compiler_params=_params(("arbitrary", "arbitrary")),
        name="merge_out",
    )(*args)


def _ffn_kernel(h_ref, x_ref, gt_ref, wg_ref, wu_ref, wd_ref, gfin_ref, o_ref, *, final):
    h = h_ref[...]
    gte = jnp.dot(h, wg_ref[...], preferred_element_type=F32)
    up = jnp.dot(h, wu_ref[...], preferred_element_type=F32)
    act = (gte * _sigmoid(gte) * up).astype(BF16)
    y = jnp.dot(act, wd_ref[...], preferred_element_type=F32)
    xo = x_ref[...] + gt_ref[...] * y
    if final:
        xo = _rms(xo, gfin_ref[...])
    o_ref[...] = xo


def _ffn(h2, x, modrows, wg, wu, wd, g_final, final, tm):
    b, s, d = x.shape
    tm = min(tm, s)
    tile = pl.BlockSpec((None, tm, d), lambda bi, i: (bi, i, 0))
    return pl.pallas_call(
        functools.partial(_ffn_kernel, final=final),
        out_shape=jax.ShapeDtypeStruct((b, s, d), F32),
        grid=(b, s // tm),
        in_specs=[tile, tile, pl.BlockSpec((None, 1, d), lambda bi, i: (bi, 0, 5)),
                  _const_spec(wg.shape), _const_spec(wu.shape), _const_spec(wd.shape), _const_spec((1, d))],
        out_specs=tile,
        compiler_params=_params(("arbitrary", "arbitrary")),
        name="ffn_swiglu",
    )(h2, x, modrows, wg, wu, wd, g_final)


ROUTE_SUB = 256
MOE_TOKENS = 1024
MOE_FF_CHUNK = 1792
MOE_ROWS = 256
MOE_TAIL_ROWS = 64


def _route_kernel(g_ref, rank_ref, rankt_ref, cnt_ref, *, n_exp):
    t = g_ref.shape[0]
    r_io = lax.broadcasted_iota(jnp.int32, (ROUTE_SUB, ROUTE_SUB), 0)
    c_io = lax.broadcasted_iota(jnp.int32, (ROUTE_SUB, ROUTE_SUB), 1)
    tri = jnp.where(c_io <= r_io, 1.0, 0.0).astype(BF16)
    carry = jnp.zeros((1, LANES), F32)
    for j in range(t // ROUTE_SUB):
        rows = slice(j * ROUTE_SUB, (j + 1) * ROUTE_SUB)
        routed = g_ref[rows, :] != 0.0
        incl = jnp.dot(tri, jnp.where(routed, 1.0, 0.0).astype(BF16), preferred_element_type=F32)
        rank_ref[rows, :] = jnp.where(routed, incl - 1.0 + carry, -1.0)
        carry = carry + incl[ROUTE_SUB - 1:ROUTE_SUB, :]
    rankt_ref[...] = rank_ref[...].T[0:n_exp, :]
    cnt_ref[...] = jnp.broadcast_to(carry, cnt_ref.shape).astype(jnp.int32)


def _route(gates, n_exp, tm):
    n = gates.shape[0]
    return pl.pallas_call(
        functools.partial(_route_kernel, n_exp=n_exp),
        out_shape=(jax.ShapeDtypeStruct((n, LANES), F32),
                   jax.ShapeDtypeStruct((n // tm, n_exp, tm), F32),
                   jax.ShapeDtypeStruct((n // tm, 8, LANES), jnp.int32)),
        grid=(n // tm,),
        in_specs=[pl.BlockSpec((tm, LANES), lambda i: (i, 0))],
        out_specs=(pl.BlockSpec((tm, LANES), lambda i: (i, 0)),
                   pl.BlockSpec((None, n_exp, tm), lambda i: (i, 0, 0)),
                   pl.BlockSpec((None, 8, LANES), lambda i: (i, 0, 0))),
        compiler_params=_params(("arbitrary",)),
        name="moe_route",
    )(gates)


def _moe_kernel(cnt_ref, h_ref, rank_ref, rankt_ref, gates_ref, wg_ref, wu_ref, wd_ref, o_ref,
                xg_ref, yg_ref, *, n_exp):
    i, e, f = pl.program_id(0), pl.program_id(1), pl.program_id(2)
    nf = pl.num_programs(2)
    t = h_ref.shape[0]
    code = cnt_ref[i * n_exp + e]
    n_full = lax.shift_right_logical(code, 1)

    @pl.when(jnp.logical_and(e == 0, f == 0))
    def _():
        o_ref[...] = jnp.zeros_like(o_ref)

    rank_row = rankt_ref[pl.ds(e, 1), :]

    def block(base, n_rows):
        rows = pl.ds(base, n_rows)

        @pl.when(f == 0)
        def _():
            slot = (base + lax.broadcasted_iota(jnp.int32, (n_rows, t), 0)).astype(F32)
            onehot = jnp.where(rank_row == slot, 1.0, 0.0).astype(BF16)
            xg_ref[rows, :] = jnp.dot(onehot, h_ref[...], preferred_element_type=F32).astype(BF16)

        xb = xg_ref[rows, :]
        gte = jnp.dot(xb, wg_ref[...], preferred_element_type=F32)
        up = jnp.dot(xb, wu_ref[...], preferred_element_type=F32)
        act = (gte * _sigmoid(gte) * up).astype(BF16)
        part = jnp.dot(act, wd_ref[...], preferred_element_type=F32)

        @pl.when(f == 0)
        def _():
            yg_ref[rows, :] = part

        @pl.when(f > 0)
        def _():
            yg_ref[rows, :] += part

        @pl.when(f == nf - 1)
        def _():
            lane = lax.broadcasted_iota(jnp.int32, (t, LANES), 1)
            rank_col = jnp.sum(jnp.where(lane == e, rank_ref[...], 0.0), axis=-1, keepdims=True)
            gate_col = jnp.sum(jnp.where(lane == e, gates_ref[...], 0.0), axis=-1, keepdims=True)
            slot = (base + lax.broadcasted_iota(jnp.int32, (t, n_rows), 1)).astype(F32)
            onehot = jnp.where(rank_col == slot, 1.0, 0.0).astype(BF16)
            o_ref[...] += gate_col * jnp.dot(onehot, yg_ref[rows, :].astype(BF16),
                                             preferred_element_type=F32)

    def full_block(b, carry):
        block(pl.multiple_of(b * MOE_ROWS, MOE_ROWS), MOE_ROWS)
        return carry

    lax.fori_loop(0, n_full, full_block, 0)

    @pl.when(jnp.bitwise_and(code, 1) == 1)
    def _():
        block(pl.multiple_of(n_full * MOE_ROWS, MOE_ROWS), MOE_TAIL_ROWS)


def _moe_routed(h2, gates, wg, wu, wd, tm, tf):
    n, d = h2.shape
    n_exp, _, dff = wg.shape
    tm = min(tm, n)
    rank, rank_t, cnt = _route(gates, n_exp, tm)
    counts = cnt[:, 0, :n_exp].reshape(-1)
    rem = counts % MOE_ROWS
    tail = jnp.logical_and(rem > 0, rem <= MOE_TAIL_ROWS)
    n_blocks = 2 * (counts // MOE_ROWS + (rem > MOE_TAIL_ROWS)) + tail
    cap = tm
    grid_spec = pltpu.PrefetchScalarGridSpec(
        num_scalar_prefetch=1,
        grid=(n // tm, n_exp, dff // tf),
        in_specs=[pl.BlockSpec((tm, d), lambda i, e, f, c: (i, 0)),
                  pl.BlockSpec((tm, LANES), lambda i, e, f, c: (i, 0)),
                  pl.BlockSpec((None, n_exp, tm), lambda i, e, f, c: (i, 0, 0)),
                  pl.BlockSpec((tm, LANES), lambda i, e, f, c: (i, 0)),
                  pl.BlockSpec((None, d, tf), lambda i, e, f, c: (e, 0, f)),
                  pl.BlockSpec((None, d, tf), lambda i, e, f, c: (e, 0, f)),
                  pl.BlockSpec((None, tf, d), lambda i, e, f, c: (e, f, 0))],
        out_specs=pl.BlockSpec((tm, d), lambda i, e, f, c: (i, 0)),
        scratch_shapes=[pltpu.VMEM((cap, d), BF16), pltpu.VMEM((cap, d), F32)])
    return pl.pallas_call(
        functools.partial(_moe_kernel, n_exp=n_exp),
        out_shape=jax.ShapeDtypeStruct((n, d), F32),
        grid_spec=grid_spec,
        compiler_params=_params(("arbitrary",) * 3),
        name="moe_swiglu",
    )(n_blocks, h2, rank, rank_t, gates, wg, wu, wd)


def _residual_kernel(x_ref, y_ref, gt_ref, gfin_ref, o_ref, *, final):
    xo = x_ref[...] + gt_ref[...] * y_ref[...]
    if final:
        xo = _rms(xo, gfin_ref[...])
    o_ref[...] = xo


def _residual(x, y, modrows, g_final, final, tm):
    b, s, d = x.shape
    tm = min(tm, s)
    tile = pl.BlockSpec((None, tm, d), lambda bi, i: (bi, i, 0))
    return pl.pallas_call(
        functools.partial(_residual_kernel, final=final),
        out_shape=jax.ShapeDtypeStruct((b, s, d), F32),
        grid=(b, s // tm),
        in_specs=[tile, tile, pl.BlockSpec((None, 1, d), lambda bi, i: (bi, 0, 5)), _const_spec((1, d))],
        out_specs=tile,
        compiler_params=_params(("arbitrary", "arbitrary")),
        name="moe_residual",
    )(x, y, modrows, g_final)


def _moe(h2, x, modrows, gates, wg, wu, wd, g_final, final):
    b, s, d = x.shape
    y = _moe_routed(h2.reshape(b * s, d), gates.reshape(b * s, LANES), wg, wu, wd, MOE_TOKENS, MOE_FF_CHUNK)
    return _residual(x, y.reshape(b, s, d), modrows, g_final, final, 512)


def _rope_tables(seq):
    axis_dim = DA_QK // 2
    n_freq = axis_dim // 2
    inv = jnp.exp(-math.log(ROPE_BASE) * jnp.arange(n_freq, dtype=F32) * (2.0 / axis_dim))
    t = jnp.arange(seq, dtype=jnp.int32)
    ar = (t // GRID_W).astype(F32)[:, None] * inv
    ac = (t % GRID_W).astype(F32)[:, None] * inv
    cos = jnp.concatenate([jnp.cos(ar), jnp.cos(ar), jnp.cos(ac), jnp.cos(ac)], axis=-1)
    sin = jnp.concatenate([jnp.sin(ar), jnp.sin(ar), jnp.sin(ac), jnp.sin(ac)], axis=-1)
    first = (jnp.arange(DA_QK) % axis_dim) < n_freq
    sa = jnp.where(first, -sin, 0.0)
    sb = jnp.where(first, 0.0, sin)
    rep = lambda a: jnp.tile(a, (1, LANES // DA_QK))
    return rep(cos), rep(sa), rep(sb)


def _identity_tables(seq):
    return jnp.ones((seq, LANES), F32), jnp.zeros((seq, LANES), F32), jnp.zeros((seq, LANES), F32)


def _pack_layer_weights(w_in, w_uq, w_ukv):
    d = w_in.shape[0]
    zpad = jnp.zeros((d, LANES - MLA_ROPE), w_in.dtype)
    o_kr = 3 * 512 + MLA_Q_RANK + MLA_KV_RANK
    o_pool = o_kr + MLA_ROPE
    w2 = jnp.concatenate([w_in[:, :o_kr + MLA_ROPE], zpad, w_in[:, o_pool:o_pool + 512]], axis=1).astype(BF16)
    wgate = w_in[:, o_pool + 512:].astype(BF16)
    hq = MLA_NOPE + MLA_ROPE
    wq = w_uq.reshape(MLA_Q_RANK, MLA_HEADS, hq)
    wq = jnp.concatenate([wq, jnp.zeros((MLA_Q_RANK, MLA_HEADS, MLA_QK_PAD - hq), w_uq.dtype)], axis=-1)
    wuq = wq.reshape(MLA_Q_RANK, MLA_HEADS * MLA_QK_PAD).astype(BF16)
    wkv = w_ukv.reshape(MLA_KV_RANK, MLA_HEADS, MLA_NOPE + MLA_V)
    wukv = jnp.concatenate([wkv[:, :, :MLA_NOPE].reshape(MLA_KV_RANK, -1),
                            wkv[:, :, MLA_NOPE:].reshape(MLA_KV_RANK, -1)], axis=1).astype(BF16)
    return w2, wgate, wuq, wukv


def kernel(x, c, ctx, c_ctx, w_mod, b_mod, g_mix, w_in, da_lambda, da_subln, mla_gq, w_uq, mla_gkv, w_ukv,
           pool_w, pool_scale, w_branch, w_out, g_ffn, ffn_w_gate, ffn_w_up, ffn_w_down, moe_router,
           moe_w_gate, moe_w_up, moe_w_down, g_final):
    bsz, seq, d = x.shape
    n_ctx = ctx.shape[1]
    depth = w_mod.shape[0]

    cond8 = jnp.zeros((8, d), F32).at[:bsz].set(c).at[bsz].set(c_ctx)
    mod = _mod_rows(cond8, w_mod, b_mod)
    lat_tabs = _rope_tables(seq)
    ctx_tabs = _identity_tables(n_ctx)
    g_fin = g_final.reshape(1, d)

    xc = ctx
    for l in range(depth):
        need_ctx = l < depth - 1
        lam_init = 0.8 - 0.6 * math.exp(-0.3 * l)
        mod_lat = mod[l, :bsz][:, None, :]
        mod_ctx = jnp.broadcast_to(mod[l, bsz][None, None, :], (bsz, 1, N_MOD * d))
        w2, wgate, wuq, wukv = _pack_layer_weights(w_in[l], w_uq[l], w_ukv[l])
        gmix = g_mix[l].reshape(1, d)
        gffn = g_ffn[l].reshape(1, d)
        gq = mla_gq[l].reshape(1, -1)
        gkv = mla_gkv[l].reshape(1, -1)
        subln = da_subln[l].reshape(1, -1)
        lam = da_lambda[l]
        pw = pool_w[l].astype(BF16)
        psc = pool_scale[l].reshape(1, -1)
        wbr = w_branch[l].astype(BF16)
        wout = w_out[l].astype(BF16)
        j = l // 2
        dense = l % 2 == 0
        final = l == depth - 1

        qda, kda, vda, qm, km, vm, pin = _inproj(x, mod_lat, gmix, w2, gq, wuq, gkv, wukv, lat_tabs, PROJ_ROWS)
        cqda, ckda, cvda, cqm, ckm, cvm, cpin = _inproj(xc, mod_ctx, gmix, w2, gq, wuq, gkv, wukv, ctx_tabs, PROJ_ROWS)

        o_da = _attention(qda, ckda, cvda, kda, vda, DA_HEADS, LANES, DA_V, 256, 512, diff=True,
                          lam=lam, g=subln, lam_init=lam_init, name="diff_attn")
        o_mla = _attention(qm, ckm, cvm, km, vm, MLA_HEADS, MLA_QK_PAD, MLA_V, 512, 512, name="mla_attn")
        o_pool = _pool(pin, pw, psc, 512)

        router = None
        n_exp = moe_router.shape[-1]
        if not dense:
            router = moe_router[j].T
        outs = _merge(x, mod_lat, gmix, wgate, o_da, o_mla, o_pool, wbr, wout, gffn, router, n_exp, PROJ_ROWS)

        if dense:
            wg = ffn_w_gate[j].astype(BF16)
            wu = ffn_w_up[j].astype(BF16)
            wd = ffn_w_down[j].astype(BF16)
            x = _ffn(outs[1], outs[0], mod_lat, wg, wu, wd, g_fin, final, 256)
        else:
            wg = moe_w_gate[j].astype(BF16)
            wu = moe_w_up[j].astype(BF16)
            wd = moe_w_down[j].astype(BF16)
            x = _moe(outs[1], outs[0], mod_lat, outs[2], wg, wu, wd, g_fin, final)

        if need_ctx:
            co_da = _attention(cqda, ckda, cvda, None, None, DA_HEADS, LANES, DA_V, 256, 512, diff=True,
                               lam=lam, g=subln, lam_init=lam_init, name="diff_attn_ctx")
            co_mla = _attention(cqm, ckm, cvm, None, None, MLA_HEADS, MLA_QK_PAD, MLA_V, 256, 512,
                                name="mla_attn_ctx")
            co_pool = _pool(cpin, pw, psc, 512)
            couts = _merge(xc, mod_ctx, gmix, wgate, co_da, co_mla, co_pool, wbr, wout, gffn, router, n_exp, PROJ_ROWS)
            if dense:
                xc = _ffn(couts[1], couts[0], mod_ctx, wg, wu, wd, g_fin, False, 256)
            else:
                xc = _moe(couts[1], couts[0], mod_ctx, couts[2], wg, wu, wd, g_fin, False)
    return x
```

```python
import functools
import math

import jax
import jax.numpy as jnp
from jax import lax
from jax.experimental import pallas as pl
from jax.experimental.pallas import tpu as pltpu

F32 = jnp.float32
BF16 = jnp.bfloat16

GRID_W = 64
DA_HEADS = 4
DA_QK = 64
DA_V = 128
MLA_HEADS = 4
MLA_NOPE = 128
MLA_ROPE = 64
MLA_V = 128
MLA_Q_RANK = 384
MLA_KV_RANK = 256
POOL_WINDOWS = (2, 4, 8, 16)
POOL_GROUP_W = 128
N_BRANCH = 3
BRANCH_W = 512
ROPE_BASE = 10000.0
EPS = 1e-6
N_MOD = 6
LANES = 128
MLA_QK_PAD = 256
LOG2E = math.log2(math.e)
DA_SCALE = DA_QK ** -0.5 * LOG2E
MLA_SCALE = (MLA_NOPE + MLA_ROPE) ** -0.5 * LOG2E
NEG_BIG = -1e30
PROJ_ROWS = 512
ATTN_KEYS = 512
ATTN_Q_TILES = 2
VMEM_LIMIT = 56 * 1024 * 1024

C_DAQ, C_DAK, C_DAV = 0, 512, 1024
C_QD = 1536
C_KVD = C_QD + MLA_Q_RANK
C_KR = C_KVD + MLA_KV_RANK
C_POOL = C_KR + LANES
W2_COLS = C_POOL + 512


def _sigmoid(v):
    return 1.0 / (1.0 + jnp.exp(-v))


def _params(sem, vmem=VMEM_LIMIT):
    return pltpu.CompilerParams(dimension_semantics=sem, vmem_limit_bytes=vmem)


def _const_spec(shape):
    nd = len(shape)
    return pl.BlockSpec(shape, lambda *_: (0,) * nd)


def _mod_kernel(cond_ref, w_ref, b_ref, o_ref):
    c = cond_ref[...]
    s = c * _sigmoid(c)
    o_ref[...] = jnp.dot(s, w_ref[...], preferred_element_type=F32,
                         precision=lax.Precision.HIGHEST) + b_ref[...]


def _mod_rows(cond8, w_mod, b_mod):
    depth, d, n = w_mod.shape
    tn = 1536
    return pl.pallas_call(
        _mod_kernel,
        out_shape=jax.ShapeDtypeStruct((depth, 8, n), F32),
        grid=(depth, n // tn),
        in_specs=[pl.BlockSpec((8, d), lambda l, j: (0, 0)),
                  pl.BlockSpec((None, d, tn), lambda l, j: (l, 0, j)),
                  pl.BlockSpec((None, 1, tn), lambda l, j: (l, 0, j))],
        out_specs=pl.BlockSpec((None, 8, tn), lambda l, j: (l, 0, j)),
        compiler_params=_params(("arbitrary", "arbitrary")),
        name="adaln_rows",
    )(cond8, w_mod, b_mod.reshape(depth, 1, n))


def _norm_mod(x, g, sh, sc):
    r = lax.rsqrt(jnp.mean(x * x, axis=-1, keepdims=True) + EPS)
    return (x * r * g) * (1.0 + sc) + sh


def _rms(v, g):
    return v * lax.rsqrt(jnp.mean(v * v, axis=-1, keepdims=True) + EPS) * g


def _inproj_kernel(x_ref, g_ref, sh_ref, sc_ref, w_ref, gq_ref, wuq_ref, gkv_ref, wukv_ref,
                   cos_ref, sa_ref, sb_ref,
                   qda_ref, kda_ref, vda_ref, qm_ref, km_ref, vm_ref, pool_ref):
    h = _norm_mod(x_ref[...], g_ref[...], sh_ref[...], sc_ref[...])
    z = jnp.dot(h.astype(BF16), w_ref[...], preferred_element_type=F32)
    cos = cos_ref[...]
    sa = sa_ref[...]
    sb = sb_ref[...]

    def rope(blk):
        return blk * cos + pltpu.roll(blk, LANES - 16, 1) * sa + pltpu.roll(blk, 16, 1) * sb

    lo = lax.broadcasted_iota(jnp.int32, cos.shape, 1) < MLA_ROPE

    for hh in range(DA_HEADS):
        c0 = hh * LANES
        qda_ref[:, c0:c0 + LANES] = (rope(z[:, C_DAQ + c0:C_DAQ + c0 + LANES]) * DA_SCALE).astype(BF16)
        kda_ref[:, c0:c0 + LANES] = rope(z[:, C_DAK + c0:C_DAK + c0 + LANES]).astype(BF16)

    qn = _rms(z[:, C_QD:C_QD + MLA_Q_RANK], gq_ref[...])
    qf = jnp.dot(qn.astype(BF16), wuq_ref[...], preferred_element_type=F32)
    kvn = _rms(z[:, C_KVD:C_KVD + MLA_KV_RANK], gkv_ref[...])
    kvf = jnp.dot(kvn.astype(BF16), wukv_ref[...], preferred_element_type=F32)
    kr = jnp.where(lo, rope(z[:, C_KR:C_KR + LANES]), 0.0).astype(BF16)
    for hh in range(MLA_HEADS):
        c0 = hh * MLA_QK_PAD
        qm_ref[:, c0:c0 + LANES] = (qf[:, c0:c0 + LANES] * MLA_SCALE).astype(BF16)
        qr = jnp.where(lo, rope(qf[:, c0 + LANES:c0 + 2 * LANES]), 0.0)
        qm_ref[:, c0 + LANES:c0 + 2 * LANES] = (qr * MLA_SCALE).astype(BF16)
        km_ref[:, c0:c0 + LANES] = kvf[:, hh * LANES:(hh + 1) * LANES].astype(BF16)
        km_ref[:, c0 + LANES:c0 + 2 * LANES] = kr
    ones = jnp.ones(cos.shape, BF16)
    for hh in range(MLA_HEADS):
        c0 = 2 * hh * LANES
        vda_ref[:, c0:c0 + LANES] = z[:, C_DAV + hh * LANES:C_DAV + (hh + 1) * LANES].astype(BF16)
        vda_ref[:, c0 + LANES:c0 + 2 * LANES] = ones
        vm_ref[:, c0:c0 + LANES] = kvf[:, 512 + hh * LANES:512 + (hh + 1) * LANES].astype(BF16)
        vm_ref[:, c0 + LANES:c0 + 2 * LANES] = ones
    pool_ref[...] = z[:, C_POOL:C_POOL + 512]


def _inproj(x, modrows, g_mix, w2, gq, wuq, gkv, wukv, tabs, tm):
    b, s, d = x.shape
    tm = min(tm, s)
    cos, sa, sb = tabs
    row = lambda j: pl.BlockSpec((None, 1, d), lambda bi, i: (bi, 0, j))
    tab = pl.BlockSpec((tm, LANES), lambda bi, i: (i, 0))
    out = lambda w, dt: jax.ShapeDtypeStruct((b, s, w), dt)
    ospec = lambda w: pl.BlockSpec((None, tm, w), lambda bi, i: (bi, i, 0))
    return pl.pallas_call(
        _inproj_kernel,
        out_shape=(out(512, BF16), out(512, BF16), out(1024, BF16), out(1024, BF16), out(1024, BF16),
                   out(1024, BF16), out(512, F32)),
        grid=(b, s // tm),
        in_specs=[pl.BlockSpec((None, tm, d), lambda bi, i: (bi, i, 0)),
                  _const_spec((1, d)), row(0), row(1),
                  _const_spec(w2.shape), _const_spec(gq.shape), _const_spec(wuq.shape),
                  _const_spec(gkv.shape), _const_spec(wukv.shape), tab, tab, tab],
        out_specs=(ospec(512), ospec(512), ospec(1024), ospec(1024), ospec(1024), ospec(1024), ospec(512)),
        compiler_params=_params(("arbitrary", "arbitrary")),
        name="in_proj",
    )(x, g_mix, modrows, modrows, w2, gq, wuq, gkv, wukv, cos, sa, sb)


def _attn_kernel(*refs, diff, has_lat, tk, tq, lam_init):
    it = iter(refs)
    q_ref, kc_ref, vc_ref = next(it), next(it), next(it)
    kl_ref = vl_ref = lam_ref = g_ref = None
    if has_lat:
        kl_ref, vl_ref = next(it), next(it)
    if diff:
        lam_ref, g_ref = next(it), next(it)
    o_ref = next(it)
    m_ref, acc_ref = next(it), next(it)
    s_refs = (next(it), next(it))
    p_refs = (next(it), next(it))
    al_refs = (next(it), next(it))
    n_qt = q_ref.shape[0] // tq
    dv = o_ref.shape[-1]

    def stacked_q(qt):
        q = q_ref[qt * tq:(qt + 1) * tq, :]
        if not diff:
            return q
        lane = lax.broadcasted_iota(jnp.int32, q.shape, 1)
        zero = jnp.zeros_like(q)
        return jnp.concatenate([jnp.where(lane < DA_QK, q, zero), jnp.where(lane >= DA_QK, q, zero)], axis=0)

    qs = [stacked_q(qt) for qt in range(n_qt)]
    n_ctx = kc_ref.shape[0]
    n_lat = kl_ref.shape[0] // tk if has_lat else 0
    width = lambda c: n_ctx if c == 0 else tk
    k_of = lambda c: kc_ref[...] if c == 0 else kl_ref[(c - 1) * tk:c * tk, :]
    v_of = lambda c: vc_ref[...] if c == 0 else vl_ref[(c - 1) * tk:c * tk, :]

    def qk(slot, qt, c):
        k = k_of(c)
        s_refs[slot][:, :k.shape[0]] = lax.dot_general(qs[qt], k, (((1,), (1,)), ((), ())),
                                                       preferred_element_type=F32)

    def sm(slot, qt, c):
        s_ref, p_ref = s_refs[slot], p_refs[slot]
        blocks = [slice(j * LANES, (j + 1) * LANES) for j in range(width(c) // LANES)]
        mx = s_ref[:, blocks[0]]
        for blk in blocks[1:]:
            mx = jnp.maximum(mx, s_ref[:, blk])
        m = m_ref[qt]
        m_new = jnp.maximum(m, jnp.max(mx, axis=-1, keepdims=True))
        al_refs[slot][...] = jnp.exp2(m - m_new)
        m_ref[qt] = m_new
        for blk in blocks:
            p_ref[:, blk] = jnp.exp2(s_ref[:, blk] - m_new).astype(BF16)

    def pv(slot, qt, c):
        v = v_of(c)
        new = jnp.dot(p_refs[slot][:, :v.shape[0]], v, preferred_element_type=F32)
        al = al_refs[slot][...]
        for blk in (slice(0, dv), slice(dv, 2 * dv)):
            acc_ref[qt, :, blk] = al * acc_ref[qt, :, blk] + new[:, blk]

    def finish(qt):
        acc = acc_ref[qt]
        o = acc[:, :dv] / acc[:, dv:]
        if diff:
            lv = lam_ref[...]
            a = jnp.sum(lv[0:1, :] * lv[1:2, :], axis=-1, keepdims=True)
            b = jnp.sum(lv[2:3, :] * lv[3:4, :], axis=-1, keepdims=True)
            lam = jnp.exp(a) - jnp.exp(b) + lam_init
            o = o[:tq] - lam * o[tq:]
            o = _rms(o, g_ref[...]) * (1.0 - lam_init)
        o_ref[qt * tq:(qt + 1) * tq, :] = o.astype(o_ref.dtype)

    m_ref[...] = jnp.full(m_ref.shape, NEG_BIG, F32)
    acc_ref[...] = jnp.zeros(acc_ref.shape, F32)
    items = [(qt, 0) for qt in range(n_qt)] + [(qt, c) for qt in range(n_qt) for c in range(1, n_lat + 1)]
    for g in range(len(items) + 2):
        if g < len(items):
            qk(g % 2, *items[g])
        if 0 <= g - 2 < len(items):
            pv(g % 2, *items[g - 2])
        if 0 <= g - 1 < len(items):
            sm((g - 1) % 2, *items[g - 1])
        if 0 <= g - 2 < len(items) and items[g - 2][1] == n_lat:
            finish(items[g - 2][0])


def _attention(q, kc, vc, kl, vl, heads, dq, dv, tq, tk, diff=False, lam=None, g=None, lam_init=0.0,
               name="attn"):
    b, s, _ = q.shape
    c = kc.shape[1]
    tq = min(tq, s)
    n_qt = min(ATTN_Q_TILES, s // tq)
    has_lat = kl is not None
    args = [q, kc, vc]
    specs = [pl.BlockSpec((None, n_qt * tq, dq), lambda bi, hi, qi: (bi, qi, hi)),
             pl.BlockSpec((None, c, dq), lambda bi, hi, qi: (bi, 0, hi)),
             pl.BlockSpec((None, c, 2 * dv), lambda bi, hi, qi: (bi, 0, hi))]
    sl = 0
    if has_lat:
        sl = kl.shape[1]
        tk = min(tk, sl)
        args += [kl, vl]
        specs += [pl.BlockSpec((None, sl, dq), lambda bi, hi, qi: (bi, 0, hi)),
                  pl.BlockSpec((None, sl, 2 * dv), lambda bi, hi, qi: (bi, 0, hi))]
    if diff:
        args += [lam, g]
        specs += [_const_spec(lam.shape), _const_spec(g.shape)]
    kern = functools.partial(_attn_kernel, diff=diff, has_lat=has_lat, tk=tk, tq=tq, lam_init=lam_init)
    rows = 2 * tq if diff else tq
    wmax = max(tk, c) if has_lat else c
    scratch = [pltpu.VMEM((n_qt, rows, LANES), F32), pltpu.VMEM((n_qt, rows, 2 * dv), F32),
               pltpu.VMEM((rows, wmax), F32), pltpu.VMEM((rows, wmax), F32),
               pltpu.VMEM((rows, wmax), BF16), pltpu.VMEM((rows, wmax), BF16),
               pltpu.VMEM((rows, LANES), F32), pltpu.VMEM((rows, LANES), F32)]
    return pl.pallas_call(
        kern,
        out_shape=jax.ShapeDtypeStruct((b, s, heads * dv), BF16),
        grid=(b, heads, s // (n_qt * tq)),
        in_specs=specs,
        out_specs=pl.BlockSpec((None, n_qt * tq, dv), lambda bi, hi, qi: (bi, qi, hi)),
        scratch_shapes=scratch,
        compiler_params=_params(("arbitrary", "arbitrary", "arbitrary")),
        name=name,
    )(*args)


def _pool_kernel(prev_ref, cur_ref, next_ref, w_ref, sc_ref, o_ref, *, seq_len):
    i = pl.program_id(1)
    n = pl.num_programs(1)
    cur = cur_ref[...]
    tm = cur.shape[0]
    prev = jnp.where(i > 0, prev_ref[...], 0.0)
    nxt = jnp.where(i < n - 1, next_ref[...], 0.0)
    ext = jnp.concatenate([prev, cur, nxt], axis=0)
    ne = tm + 16
    t = i * tm + lax.broadcasted_iota(jnp.int32, (tm, 1), 0)
    for g, w in enumerate(POOL_WINDOWS):
        e = ext[:, g * LANES:(g + 1) * LANES]
        acc = e + pltpu.roll(e, 1, 0)
        half = 1
        while 2 * half < w:
            acc = pltpu.roll(acc, half, 0) + pltpu.roll(acc, ne - half, 0)
            half *= 2
        win = acc[8:8 + tm]
        lo = jnp.clip(t - w // 2, 0, seq_len)
        hi = jnp.clip(t - w // 2 + w, 0, seq_len)
        cnt = (hi - lo).astype(F32)
        dlt = win / cnt - cur[:, g * LANES:(g + 1) * LANES]
        y = jnp.dot(dlt.astype(BF16), w_ref[g], preferred_element_type=F32)
        o_ref[:, g * LANES:(g + 1) * LANES] = (y * sc_ref[:, g * LANES:(g + 1) * LANES]).astype(BF16)


def _pool(u, pool_w, pool_scale, tm):
    b, s, w = u.shape
    tm = min(tm, s)
    nb8 = s // 8
    r8 = tm // 8
    return pl.pallas_call(
        functools.partial(_pool_kernel, seq_len=s),
        out_shape=jax.ShapeDtypeStruct((b, s, w), BF16),
        grid=(b, s // tm),
        in_specs=[pl.BlockSpec((None, 8, w), lambda bi, i: (bi, jnp.maximum(i * r8 - 1, 0), 0)),
                  pl.BlockSpec((None, tm, w), lambda bi, i: (bi, i, 0)),
                  pl.BlockSpec((None, 8, w), lambda bi, i: (bi, jnp.minimum((i + 1) * r8, nb8 - 1), 0)),
                  _const_spec(pool_w.shape), _const_spec(pool_scale.shape)],
        out_specs=pl.BlockSpec((None, tm, w), lambda bi, i: (bi, i, 0)),
        compiler_params=_params(("arbitrary", "arbitrary")),
        name="pool_mixer",
    )(u, u, u, pool_w, pool_scale)


def _merge_kernel(*refs, moe, n_exp):
    (x_ref, gmix_ref, sh1_ref, sc1_ref, wgate_ref, oda_ref, omla_ref, opool_ref, wbr_ref, wout_ref,
     gt1_ref, gffn_ref, sh2_ref, sc2_ref) = refs[:14]
    if moe:
        router_ref, xo_ref, h2_ref, gates_ref = refs[14:]
    else:
        xo_ref, h2_ref = refs[14:]
    x = x_ref[...]
    d = x.shape[-1]
    h = _norm_mod(x, gmix_ref[...], sh1_ref[...], sc1_ref[...]).astype(BF16)
    merged = None
    for n, o_ref in enumerate((oda_ref, omla_ref, opool_ref)):
        gate = _sigmoid(jnp.dot(h, wgate_ref[:, n * d:(n + 1) * d], preferred_element_type=F32))
        proj = jnp.dot(o_ref[...], wbr_ref[n], preferred_element_type=F32)
        merged = gate * proj if merged is None else merged + gate * proj
    mix = jnp.dot(merged.astype(BF16), wout_ref[...], preferred_element_type=F32)
    xn = x + gt1_ref[...] * mix
    xo_ref[...] = xn
    h2 = _norm_mod(xn, gffn_ref[...], sh2_ref[...], sc2_ref[...])
    h2_ref[...] = h2.astype(BF16)
    if moe:
        logit = [jnp.sum(h2 * router_ref[e:e + 1, :], axis=-1, keepdims=True) for e in range(n_exp)]

        def top1(vals):
            best, idx = vals[0], jnp.zeros(vals[0].shape, jnp.int32)
            for e in range(1, n_exp):
                better = vals[e] > best
                best = jnp.where(better, vals[e], best)
                idx = jnp.where(better, e, idx)
            return best, idx

        v1, i1 = top1(logit)
        v2, i2 = top1([jnp.where(i1 == e, NEG_BIG, logit[e]) for e in range(n_exp)])
        w1 = 1.0 / (1.0 + jnp.exp(v2 - v1))
        lane = lax.broadcasted_iota(jnp.int32, gates_ref.shape, 1)
        gates_ref[...] = jnp.where(lane == i1, w1, 0.0) + jnp.where(lane == i2, 1.0 - w1, 0.0)


def _merge(x, modrows, g_mix, wgate, o_da, o_mla, o_pool, wbr, wout, g_ffn, router, n_exp, tm):
    b, s, d = x.shape
    tm = min(tm, s)
    moe = router is not None
    row = lambda j: pl.BlockSpec((None, 1, d), lambda bi, i: (bi, 0, j))
    tile = lambda w: pl.BlockSpec((None, tm, w), lambda bi, i: (bi, i, 0))
    args = [x, g_mix, modrows, modrows, wgate, o_da, o_mla, o_pool, wbr, wout, modrows, g_ffn, modrows, modrows]
    specs = [tile(d), _const_spec((1, d)), row(0), row(1), _const_spec(wgate.shape),
             tile(BRANCH_W), tile(BRANCH_W), tile(BRANCH_W), _const_spec(wbr.shape), _const_spec(wout.shape),
             row(2), _const_spec((1, d)), row(3), row(4)]
    out_shape = [jax.ShapeDtypeStruct((b, s, d), F32), jax.ShapeDtypeStruct((b, s, d), BF16)]
    out_specs = [tile(d), tile(d)]
    if moe:
        args.append(router)
        specs.append(_const_spec(router.shape))
        out_shape.append(jax.ShapeDtypeStruct((b, s, LANES), F32))
        out_specs.append(tile(LANES))
    return pl.pallas_call(
        functools.partial(_merge_kernel, moe=moe, n_exp=n_exp),
        out_shape=tuple(out_shape),
        grid=(b, s // tm),
        in_specs=specs,
        out_specs=tuple(out_specs),
        compiler_params=_params(("arbitrary", "arbitrary")),
        name="merge_out",
    )(*args)


def _ffn_kernel(h_ref, x_ref, gt_ref, wg_ref, wu_ref, wd_ref, gfin_ref, o_ref, *, final):
    h = h_ref[...]
    gte = jnp.dot(h, wg_ref[...], preferred_element_type=F32)
    up = jnp.dot(h, wu_ref[...], preferred_element_type=F32)
    act = (gte * _sigmoid(gte) * up).astype(BF16)
    y = jnp.dot(act, wd_ref[...], preferred_element_type=F32)
    xo = x_ref[...] + gt_ref[...] * y
    if final:
        xo = _rms(xo, gfin_ref[...])
    o_ref[...] = xo


def _ffn(h2, x, modrows, wg, wu, wd, g_final, final, tm):
    b, s, d = x.shape
    tm = min(tm, s)
    tile = pl.BlockSpec((None, tm, d), lambda bi, i: (bi, i, 0))
    return pl.pallas_call(
        functools.partial(_ffn_kernel, final=final),
        out_shape=jax.ShapeDtypeStruct((b, s, d), F32),
        grid=(b, s // tm),
        in_specs=[tile, tile, pl.BlockSpec((None, 1, d), lambda bi, i: (bi, 0, 5)),
                  _const_spec(wg.shape), _const_spec(wu.shape), _const_spec(wd.shape), _const_spec((1, d))],
        out_specs=tile,
        compiler_params=_params(("arbitrary", "arbitrary")),
        name="ffn_swiglu",
    )(h2, x, modrows, wg, wu, wd, g_final)


ROUTE_SUB = 256
MOE_TOKENS = 1024
MOE_FF_CHUNK = 1792
MOE_ROWS = 256
MOE_TAIL_ROWS = 64


def _route_kernel(g_ref, rank_ref, rankt_ref, cnt_ref, *, n_exp):
    t = g_ref.shape[0]
    r_io = lax.broadcasted_iota(jnp.int32, (ROUTE_SUB, ROUTE_SUB), 0)
    c_io = lax.broadcasted_iota(jnp.int32, (ROUTE_SUB, ROUTE_SUB), 1)
    tri = jnp.where(c_io <= r_io, 1.0, 0.0).astype(BF16)
    carry = jnp.zeros((1, LANES), F32)
    for j in range(t // ROUTE_SUB):
        rows = slice(j * ROUTE_SUB, (j + 1) * ROUTE_SUB)
        routed = g_ref[rows, :] != 0.0
        incl = jnp.dot(tri, jnp.where(routed, 1.0, 0.0).astype(BF16), preferred_element_type=F32)
        rank_ref[rows, :] = jnp.where(routed, incl - 1.0 + carry, -1.0)
        carry = carry + incl[ROUTE_SUB - 1:ROUTE_SUB, :]
    rankt_ref[...] = rank_ref[...].T[0:n_exp, :]
    cnt_ref[...] = jnp.broadcast_to(carry, cnt_ref.shape).astype(jnp.int32)


def _route(gates, n_exp, tm):
    n = gates.shape[0]
    return pl.pallas_call(
        functools.partial(_route_kernel, n_exp=n_exp),
        out_shape=(jax.ShapeDtypeStruct((n, LANES), F32),
                   jax.ShapeDtypeStruct((n // tm, n_exp, tm), F32),
                   jax.ShapeDtypeStruct((n // tm, 8, LANES), jnp.int32)),
        grid=(n // tm,),
        in_specs=[pl.BlockSpec((tm, LANES), lambda i: (i, 0))],
        out_specs=(pl.BlockSpec((tm, LANES), lambda i: (i, 0)),
                   pl.BlockSpec((None, n_exp, tm), lambda i: (i, 0, 0)),
                   pl.BlockSpec((None, 8, LANES), lambda i: (i, 0, 0))),
        compiler_params=_params(("arbitrary",)),
        name="moe_route",
    )(gates)


def _moe_kernel(cnt_ref, h_ref, rank_ref, rankt_ref, gates_ref, wg_ref, wu_ref, wd_ref, o_ref,
                xg_ref, yg_ref, *, n_exp):
    i, e, f = pl.program_id(0), pl.program_id(1), pl.program_id(2)
    nf = pl.num_programs(2)
    t = h_ref.shape[0]
    code = cnt_ref[i * n_exp + e]
    n_full = lax.shift_right_logical(code, 1)

    @pl.when(jnp.logical_and(e == 0, f == 0))
    def _():
        o_ref[...] = jnp.zeros_like(o_ref)

    rank_row = rankt_ref[pl.ds(e, 1), :]

    def block(base, n_rows):
        rows = pl.ds(base, n_rows)

        @pl.when(f == 0)
        def _():
            slot = (base + lax.broadcasted_iota(jnp.int32, (n_rows, t), 0)).astype(F32)
            onehot = jnp.where(rank_row == slot, 1.0, 0.0).astype(BF16)
            xg_ref[rows, :] = jnp.dot(onehot, h_ref[...], preferred_element_type=F32).astype(BF16)

        xb = xg_ref[rows, :]
        gte = jnp.dot(xb, wg_ref[...], preferred_element_type=F32)
        up = jnp.dot(xb, wu_ref[...], preferred_element_type=F32)
        act = (gte * _sigmoid(gte) * up).astype(BF16)
        part = jnp.dot(act, wd_ref[...], preferred_element_type=F32)

        @pl.when(f == 0)
        def _():
            yg_ref[rows, :] = part

        @pl.when(f > 0)
        def _():
            yg_ref[rows, :] += part

        @pl.when(f == nf - 1)
        def _():
            lane = lax.broadcasted_iota(jnp.int32, (t, LANES), 1)
            rank_col = jnp.sum(jnp.where(lane == e, rank_ref[...], 0.0), axis=-1, keepdims=True)
            gate_col = jnp.sum(jnp.where(lane == e, gates_ref[...], 0.0), axis=-1, keepdims=True)
            slot = (base + lax.broadcasted_iota(jnp.int32, (t, n_rows), 1)).astype(F32)
            onehot = jnp.where(rank_col == slot, 1.0, 0.0).astype(BF16)
            o_ref[...] += gate_col * jnp.dot(onehot, yg_ref[rows, :].astype(BF16),
                                             preferred_element_type=F32)

    def full_block(b, carry):
        block(pl.multiple_of(b * MOE_ROWS, MOE_ROWS), MOE_ROWS)
        return carry

    lax.fori_loop(0, n_full, full_block, 0)

    @pl.when(jnp.bitwise_and(code, 1) == 1)
    def _():
        block(pl.multiple_of(n_full * MOE_ROWS, MOE_ROWS), MOE_TAIL_ROWS)


def _moe_routed(h2, gates, wg, wu, wd, tm, tf):
    n, d = h2.shape
    n_exp, _, dff = wg.shape
    tm = min(tm, n)
    rank, rank_t, cnt = _route(gates, n_exp, tm)
    counts = cnt[:, 0, :n_exp].reshape(-1)
    rem = counts % MOE_ROWS
    tail = jnp.logical_and(rem > 0, rem <= MOE_TAIL_ROWS)
    n_blocks = 2 * (counts // MOE_ROWS + (rem > MOE_TAIL_ROWS)) + tail
    cap = tm
    grid_spec = pltpu.PrefetchScalarGridSpec(
        num_scalar_prefetch=1,
        grid=(n // tm, n_exp, dff // tf),
        in_specs=[pl.BlockSpec((tm, d), lambda i, e, f, c: (i, 0)),
                  pl.BlockSpec((tm, LANES), lambda i, e, f, c: (i, 0)),
                  pl.BlockSpec((None, n_exp, tm), lambda i, e, f, c: (i, 0, 0)),
                  pl.BlockSpec((tm, LANES), lambda i, e, f, c: (i, 0)),
                  pl.BlockSpec((None, d, tf), lambda i, e, f, c: (e, 0, f)),
                  pl.BlockSpec((None, d, tf), lambda i, e, f, c: (e, 0, f)),
                  pl.BlockSpec((None, tf, d), lambda i, e, f, c: (e, f, 0))],
        out_specs=pl.BlockSpec((tm, d), lambda i, e, f, c: (i, 0)),
        scratch_shapes=[pltpu.VMEM((cap, d), BF16), pltpu.VMEM((cap, d), F32)])
    return pl.pallas_call(
        functools.partial(_moe_kernel, n_exp=n_exp),
        out_shape=jax.ShapeDtypeStruct((n, d), F32),
        grid_spec=grid_spec,
        compiler_params=_params(("arbitrary",) * 3),
        name="moe_swiglu",
    )(n_blocks, h2, rank, rank_t, gates, wg, wu, wd)


def _residual_kernel(x_ref, y_ref, gt_ref, gfin_ref, o_ref, *, final):
    xo = x_ref[...] + gt_ref[...] * y_ref[...]
    if final:
        xo = _rms(xo, gfin_ref[...])
    o_ref[...] = xo


def _residual(x, y, modrows, g_final, final, tm):
    b, s, d = x.shape
    tm = min(tm, s)
    tile = pl.BlockSpec((None, tm, d), lambda bi, i: (bi, i, 0))
    return pl.pallas_call(
        functools.partial(_residual_kernel, final=final),
        out_shape=jax.ShapeDtypeStruct((b, s, d), F32),
        grid=(b, s // tm),
        in_specs=[tile, tile, pl.BlockSpec((None, 1, d), lambda bi, i: (bi, 0, 5)), _const_spec((1, d))],
        out_specs=tile,
        compiler_params=_params(("arbitrary", "arbitrary")),
        name="moe_residual",
    )(x, y, modrows, g_final)


def _moe(h2, x, modrows, gates, wg, wu, wd, g_final, final):
    b, s, d = x.shape
    y = _moe_routed(h2.reshape(b * s, d), gates.reshape(b * s, LANES), wg, wu, wd, MOE_TOKENS, MOE_FF_CHUNK)
    return _residual(x, y.reshape(b, s, d), modrows, g_final, final, 512)


def _rope_tables(seq):
    axis_dim = DA_QK // 2
    n_freq = axis_dim // 2
    inv = jnp.exp(-math.log(ROPE_BASE) * jnp.arange(n_freq, dtype=F32) * (2.0 / axis_dim))
    t = jnp.arange(seq, dtype=jnp.int32)
    ar = (t // GRID_W).astype(F32)[:, None] * inv
    ac = (t % GRID_W).astype(F32)[:, None] * inv
    cos = jnp.concatenate([jnp.cos(ar), jnp.cos(ar), jnp.cos(ac), jnp.cos(ac)], axis=-1)
    sin = jnp.concatenate([jnp.sin(ar), jnp.sin(ar), jnp.sin(ac), jnp.sin(ac)], axis=-1)
    first = (jnp.arange(DA_QK) % axis_dim) < n_freq
    sa = jnp.where(first, -sin, 0.0)
    sb = jnp.where(first, 0.0, sin)
    rep = lambda a: jnp.tile(a, (1, LANES // DA_QK))
    return rep(cos), rep(sa), rep(sb)


def _identity_tables(seq):
    return jnp.ones((seq, LANES), F32), jnp.zeros((seq, LANES), F32), jnp.zeros((seq, LANES), F32)


def _pack_layer_weights(w_in, w_uq, w_ukv):
    d = w_in.shape[0]
    zpad = jnp.zeros((d, LANES - MLA_ROPE), w_in.dtype)
    o_kr = 3 * 512 + MLA_Q_RANK + MLA_KV_RANK
    o_pool = o_kr + MLA_ROPE
    w2 = jnp.concatenate([w_in[:, :o_kr + MLA_ROPE], zpad, w_in[:, o_pool:o_pool + 512]], axis=1).astype(BF16)
    wgate = w_in[:, o_pool + 512:].astype(BF16)
    hq = MLA_NOPE + MLA_ROPE
    wq = w_uq.reshape(MLA_Q_RANK, MLA_HEADS, hq)
    wq = jnp.concatenate([wq, jnp.zeros((MLA_Q_RANK, MLA_HEADS, MLA_QK_PAD - hq), w_uq.dtype)], axis=-1)
    wuq = wq.reshape(MLA_Q_RANK, MLA_HEADS * MLA_QK_PAD).astype(BF16)
    wkv = w_ukv.reshape(MLA_KV_RANK, MLA_HEADS, MLA_NOPE + MLA_V)
    wukv = jnp.concatenate([wkv[:, :, :MLA_NOPE].reshape(MLA_KV_RANK, -1),
                            wkv[:, :, MLA_NOPE:].reshape(MLA_KV_RANK, -1)], axis=1).astype(BF16)
    return w2, wgate, wuq, wukv


def kernel(x, c, ctx, c_ctx, w_mod, b_mod, g_mix, w_in, da_lambda, da_subln, mla_gq, w_uq, mla_gkv, w_ukv,
           pool_w, pool_scale, w_branch, w_out, g_ffn, ffn_w_gate, ffn_w_up, ffn_w_down, moe_router,
           moe_w_gate, moe_w_up, moe_w_down, g_final):
    bsz, seq, d = x.shape
    n_ctx = ctx.shape[1]
    depth = w_mod.shape[0]

    cond8 = jnp.zeros((8, d), F32).at[:bsz].set(c).at[bsz].set(c_ctx)
    mod = _mod_rows(cond8, w_mod, b_mod)
    lat_tabs = _rope_tables(seq)
    ctx_tabs = _identity_tables(n_ctx)
    g_fin = g_final.reshape(1, d)

    xc = ctx
    for l in range(depth):
        need_ctx = l < depth - 1
        lam_init = 0.8 - 0.6 * math.exp(-0.3 * l)
        mod_lat = mod[l, :bsz][:, None, :]
        mod_ctx = jnp.broadcast_to(mod[l, bsz][None, None, :], (bsz, 1, N_MOD * d))
        w2, wgate, wuq, wukv = _pack_layer_weights(w_in[l], w_uq[l], w_ukv[l])
        gmix = g_mix[l].reshape(1, d)
        gffn = g_ffn[l].reshape(1, d)
        gq = mla_gq[l].reshape(1, -1)
        gkv = mla_gkv[l].reshape(1, -1)
        subln = da_subln[l].reshape(1, -1)
        lam = da_lambda[l]
        pw = pool_w[l].astype(BF16)
        psc = pool_scale[l].reshape(1, -1)
        wbr = w_branch[l].astype(BF16)
        wout = w_out[l].astype(BF16)
        j = l // 2
        dense = l % 2 == 0
        final = l == depth - 1

        qda, kda, vda, qm, km, vm, pin = _inproj(x, mod_lat, gmix, w2, gq, wuq, gkv, wukv, lat_tabs, PROJ_ROWS)
        cqda, ckda, cvda, cqm, ckm, cvm, cpin = _inproj(xc, mod_ctx, gmix, w2, gq, wuq, gkv, wukv, ctx_tabs, PROJ_ROWS)

        o_da = _attention(qda, ckda, cvda, kda, vda, DA_HEADS, LANES, DA_V, 256, ATTN_KEYS, diff=True,
                          lam=lam, g=subln, lam_init=lam_init, name="diff_attn")
        o_mla = _attention(qm, ckm, cvm, km, vm, MLA_HEADS, MLA_QK_PAD, MLA_V, 512, ATTN_KEYS, name="mla_attn")
        o_pool = _pool(pin, pw, psc, 512)

        router = None
        n_exp = moe_router.shape[-1]
        if not dense:
            router = moe_router[j].T
        outs = _merge(x, mod_lat, gmix, wgate, o_da, o_mla, o_pool, wbr, wout, gffn, router, n_exp, PROJ_ROWS)

        if dense:
            wg = ffn_w_gate[j].astype(BF16)
            wu = ffn_w_up[j].astype(BF16)
            wd = ffn_w_down[j].astype(BF16)
            x = _ffn(outs[1], outs[0], mod_lat, wg, wu, wd, g_fin, final, 256)
        else:
            wg = moe_w_gate[j].astype(BF16)
            wu = moe_w_up[j].astype(BF16)
            wd = moe_w_down[j].astype(BF16)
            x = _moe(outs[1], outs[0], mod_lat, outs[2], wg, wu, wd, g_fin, final)

        if need_ctx:
            co_da = _attention(cqda, ckda, cvda, None, None, DA_HEADS, LANES, DA_V, 256, ATTN_KEYS, diff=True,
                               lam=lam, g=subln, lam_init=lam_init, name="diff_attn_ctx")
            co_mla = _attention(cqm, ckm, cvm, None, None, MLA_HEADS, MLA_QK_PAD, MLA_V, 256, ATTN_KEYS,
                                name="mla_attn_ctx")
            co_pool = _pool(cpin, pw, psc, 512)
            couts = _merge(xc, mod_ctx, gmix, wgate, co_da, co_mla, co_pool, wbr, wout, gffn, router, n_exp, PROJ_ROWS)
            if dense:
                xc = _ffn(couts[1], couts[0], mod_ctx, wg, wu, wd, g_fin, False, 256)
            else:
                xc = _moe(couts[1], couts[0], mod_ctx, couts[2], wg, wu, wd, g_fin, False)
    return x
```

```python
import functools
import math

import jax
import jax.numpy as jnp
from jax import lax
from jax.experimental import pallas as pl
from jax.experimental.pallas import tpu as pltpu

F32 = jnp.float32
BF16 = jnp.bfloat16

GRID_W = 64
DA_HEADS = 4
DA_QK = 64
DA_V = 128
MLA_HEADS = 4
MLA_NOPE = 128
MLA_ROPE = 64
MLA_V = 128
MLA_Q_RANK = 384
MLA_KV_RANK = 256
POOL_WINDOWS = (2, 4, 8, 16)
POOL_GROUP_W = 128
N_BRANCH = 3
BRANCH_W = 512
ROPE_BASE = 10000.0
EPS = 1e-6
N_MOD = 6
LANES = 128
MLA_QK_PAD = 256
LOG2E = math.log2(math.e)
DA_SCALE = DA_QK ** -0.5 * LOG2E
MLA_SCALE = (MLA_NOPE + MLA_ROPE) ** -0.5 * LOG2E
NEG_BIG = -1e30
PROJ_ROWS = 512
ATTN_ROWS = 1024
ATTN_KEYS = 512
ATTN_Q_TILES = 1
VMEM_LIMIT = 56 * 1024 * 1024

C_DAQ, C_DAK, C_DAV = 0, 512, 1024
C_QD = 1536
C_KVD = C_QD + MLA_Q_RANK
C_KR = C_KVD + MLA_KV_RANK
C_POOL = C_KR + LANES
W2_COLS = C_POOL + 512


def _sigmoid(v):
    return 1.0 / (1.0 + jnp.exp(-v))


def _params(sem, vmem=VMEM_LIMIT):
    return pltpu.CompilerParams(dimension_semantics=sem, vmem_limit_bytes=vmem)


def _const_spec(shape):
    nd = len(shape)
    return pl.BlockSpec(shape, lambda *_: (0,) * nd)


def _mod_kernel(cond_ref, w_ref, b_ref, o_ref):
    c = cond_ref[...]
    s = c * _sigmoid(c)
    o_ref[...] = jnp.dot(s, w_ref[...], preferred_element_type=F32,
                         precision=lax.Precision.HIGHEST) + b_ref[...]


def _mod_rows(cond8, w_mod, b_mod):
    depth, d, n = w_mod.shape
    tn = 1536
    return pl.pallas_call(
        _mod_kernel,
        out_shape=jax.ShapeDtypeStruct((depth, 8, n), F32),
        grid=(depth, n // tn),
        in_specs=[pl.BlockSpec((8, d), lambda l, j: (0, 0)),
                  pl.BlockSpec((None, d, tn), lambda l, j: (l, 0, j)),
                  pl.BlockSpec((None, 1, tn), lambda l, j: (l, 0, j))],
        out_specs=pl.BlockSpec((None, 8, tn), lambda l, j: (l, 0, j)),
        compiler_params=_params(("arbitrary", "arbitrary")),
        name="adaln_rows",
    )(cond8, w_mod, b_mod.reshape(depth, 1, n))


def _norm_mod(x, g, sh, sc):
    r = lax.rsqrt(jnp.mean(x * x, axis=-1, keepdims=True) + EPS)
    return (x * r * g) * (1.0 + sc) + sh


def _rms(v, g):
    return v * lax.rsqrt(jnp.mean(v * v, axis=-1, keepdims=True) + EPS) * g


def _inproj_kernel(x_ref, g_ref, sh_ref, sc_ref, w_ref, gq_ref, wuq_ref, gkv_ref, wukv_ref,
                   cos_ref, sa_ref, sb_ref,
                   qda_ref, kda_ref, vda_ref, qm_ref, km_ref, vm_ref, pool_ref):
    h = _norm_mod(x_ref[...], g_ref[...], sh_ref[...], sc_ref[...])
    z = jnp.dot(h.astype(BF16), w_ref[...], preferred_element_type=F32)
    cos = cos_ref[...]
    sa = sa_ref[...]
    sb = sb_ref[...]

    def rope(blk):
        return blk * cos + pltpu.roll(blk, LANES - 16, 1) * sa + pltpu.roll(blk, 16, 1) * sb

    lo = lax.broadcasted_iota(jnp.int32, cos.shape, 1) < MLA_ROPE

    for hh in range(DA_HEADS):
        c0 = hh * LANES
        qda_ref[:, c0:c0 + LANES] = (rope(z[:, C_DAQ + c0:C_DAQ + c0 + LANES]) * DA_SCALE).astype(BF16)
        kda_ref[:, c0:c0 + LANES] = rope(z[:, C_DAK + c0:C_DAK + c0 + LANES]).astype(BF16)

    qn = _rms(z[:, C_QD:C_QD + MLA_Q_RANK], gq_ref[...])
    qf = jnp.dot(qn.astype(BF16), wuq_ref[...], preferred_element_type=F32)
    kvn = _rms(z[:, C_KVD:C_KVD + MLA_KV_RANK], gkv_ref[...])
    kvf = jnp.dot(kvn.astype(BF16), wukv_ref[...], preferred_element_type=F32)
    kr = jnp.where(lo, rope(z[:, C_KR:C_KR + LANES]), 0.0).astype(BF16)
    for hh in range(MLA_HEADS):
        c0 = hh * MLA_QK_PAD
        qm_ref[:, c0:c0 + LANES] = (qf[:, c0:c0 + LANES] * MLA_SCALE).astype(BF16)
        qr = jnp.where(lo, rope(qf[:, c0 + LANES:c0 + 2 * LANES]), 0.0)
        qm_ref[:, c0 + LANES:c0 + 2 * LANES] = (qr * MLA_SCALE).astype(BF16)
        km_ref[:, c0:c0 + LANES] = kvf[:, hh * LANES:(hh + 1) * LANES].astype(BF16)
        km_ref[:, c0 + LANES:c0 + 2 * LANES] = kr
    ones = jnp.ones(cos.shape, BF16)
    for hh in range(MLA_HEADS):
        c0 = 2 * hh * LANES
        vda_ref[:, c0:c0 + LANES] = z[:, C_DAV + hh * LANES:C_DAV + (hh + 1) * LANES].astype(BF16)
        vda_ref[:, c0 + LANES:c0 + 2 * LANES] = ones
        vm_ref[:, c0:c0 + LANES] = kvf[:, 512 + hh * LANES:512 + (hh + 1) * LANES].astype(BF16)
        vm_ref[:, c0 + LANES:c0 + 2 * LANES] = ones
    pool_ref[...] = z[:, C_POOL:C_POOL + 512]


def _inproj(x, modrows, g_mix, w2, gq, wuq, gkv, wukv, tabs, tm):
    b, s, d = x.shape
    tm = min(tm, s)
    cos, sa, sb = tabs
    row = lambda j: pl.BlockSpec((None, 1, d), lambda bi, i: (bi, 0, j))
    tab = pl.BlockSpec((tm, LANES), lambda bi, i: (i, 0))
    out = lambda w, dt: jax.ShapeDtypeStruct((b, s, w), dt)
    ospec = lambda w: pl.BlockSpec((None, tm, w), lambda bi, i: (bi, i, 0))
    return pl.pallas_call(
        _inproj_kernel,
        out_shape=(out(512, BF16), out(512, BF16), out(1024, BF16), out(1024, BF16), out(1024, BF16),
                   out(1024, BF16), out(512, F32)),
        grid=(b, s // tm),
        in_specs=[pl.BlockSpec((None, tm, d), lambda bi, i: (bi, i, 0)),
                  _const_spec((1, d)), row(0), row(1),
                  _const_spec(w2.shape), _const_spec(gq.shape), _const_spec(wuq.shape),
                  _const_spec(gkv.shape), _const_spec(wukv.shape), tab, tab, tab],
        out_specs=(ospec(512), ospec(512), ospec(1024), ospec(1024), ospec(1024), ospec(1024), ospec(512)),
        compiler_params=_params(("arbitrary", "arbitrary")),
        name="in_proj",
    )(x, g_mix, modrows, modrows, w2, gq, wuq, gkv, wukv, cos, sa, sb)


def _attn_kernel(*refs, diff, has_lat, tk, tq, lam_init):
    it = iter(refs)
    q_ref, kc_ref, vc_ref = next(it), next(it), next(it)
    kl_ref = vl_ref = lam_ref = g_ref = None
    if has_lat:
        kl_ref, vl_ref = next(it), next(it)
    if diff:
        lam_ref, g_ref = next(it), next(it)
    o_ref = next(it)
    m_ref, acc_ref = next(it), next(it)
    s_refs = (next(it), next(it))
    p_refs = (next(it), next(it))
    al_refs = (next(it), next(it))
    n_qt = q_ref.shape[0] // tq
    dv = o_ref.shape[-1]

    def stacked_q(qt):
        q = q_ref[qt * tq:(qt + 1) * tq, :]
        if not diff:
            return q
        lane = lax.broadcasted_iota(jnp.int32, q.shape, 1)
        zero = jnp.zeros_like(q)
        return jnp.concatenate([jnp.where(lane < DA_QK, q, zero), jnp.where(lane >= DA_QK, q, zero)], axis=0)

    qs = [stacked_q(qt) for qt in range(n_qt)]
    n_ctx = kc_ref.shape[0]
    n_lat = kl_ref.shape[0] // tk if has_lat else 0
    width = lambda c: n_ctx if c == 0 else tk
    k_of = lambda c: kc_ref[...] if c == 0 else kl_ref[(c - 1) * tk:c * tk, :]
    v_of = lambda c: vc_ref[...] if c == 0 else vl_ref[(c - 1) * tk:c * tk, :]

    def qk(slot, qt, c):
        k = k_of(c)
        s_refs[slot][:, :k.shape[0]] = lax.dot_general(qs[qt], k, (((1,), (1,)), ((), ())),
                                                       preferred_element_type=F32)

    def sm(slot, qt, c):
        s_ref, p_ref = s_refs[slot], p_refs[slot]
        blocks = [slice(j * LANES, (j + 1) * LANES) for j in range(width(c) // LANES)]
        mx = s_ref[:, blocks[0]]
        for blk in blocks[1:]:
            mx = jnp.maximum(mx, s_ref[:, blk])
        m = m_ref[qt]
        m_new = jnp.maximum(m, jnp.max(mx, axis=-1, keepdims=True))
        al_refs[slot][...] = jnp.exp2(m - m_new)
        m_ref[qt] = m_new
        for blk in blocks:
            p_ref[:, blk] = jnp.exp2(s_ref[:, blk] - m_new).astype(BF16)

    def pv(slot, qt, c):
        v = v_of(c)
        new = jnp.dot(p_refs[slot][:, :v.shape[0]], v, preferred_element_type=F32)
        al = al_refs[slot][...]
        for blk in (slice(0, dv), slice(dv, 2 * dv)):
            acc_ref[qt, :, blk] = al * acc_ref[qt, :, blk] + new[:, blk]

    def finish(qt):
        acc = acc_ref[qt]
        o = acc[:, :dv] / acc[:, dv:]
        if diff:
            lv = lam_ref[...]
            a = jnp.sum(lv[0:1, :] * lv[1:2, :], axis=-1, keepdims=True)
            b = jnp.sum(lv[2:3, :] * lv[3:4, :], axis=-1, keepdims=True)
            lam = jnp.exp(a) - jnp.exp(b) + lam_init
            o = o[:tq] - lam * o[tq:]
            o = _rms(o, g_ref[...]) * (1.0 - lam_init)
        o_ref[qt * tq:(qt + 1) * tq, :] = o.astype(o_ref.dtype)

    m_ref[...] = jnp.full(m_ref.shape, NEG_BIG, F32)
    acc_ref[...] = jnp.zeros(acc_ref.shape, F32)
    items = [(qt, 0) for qt in range(n_qt)] + [(qt, c) for qt in range(n_qt) for c in range(1, n_lat + 1)]
    for g in range(len(items) + 2):
        if g < len(items):
            qk(g % 2, *items[g])
        if 0 <= g - 2 < len(items):
            pv(g % 2, *items[g - 2])
        if 0 <= g - 1 < len(items):
            sm((g - 1) % 2, *items[g - 1])
        if 0 <= g - 2 < len(items) and items[g - 2][1] == n_lat:
            finish(items[g - 2][0])


def _attention(q, kc, vc, kl, vl, heads, dq, dv, tq, tk, diff=False, lam=None, g=None, lam_init=0.0,
               name="attn"):
    b, s, _ = q.shape
    c = kc.shape[1]
    tq = min(tq, s)
    n_qt = min(ATTN_Q_TILES, s // tq)
    has_lat = kl is not None
    args = [q, kc, vc]
    specs = [pl.BlockSpec((None, n_qt * tq, dq), lambda bi, hi, qi: (bi, qi, hi)),
             pl.BlockSpec((None, c, dq), lambda bi, hi, qi: (bi, 0, hi)),
             pl.BlockSpec((None, c, 2 * dv), lambda bi, hi, qi: (bi, 0, hi))]
    sl = 0
    if has_lat:
        sl = kl.shape[1]
        tk = min(tk, sl)
        args += [kl, vl]
        specs += [pl.BlockSpec((None, sl, dq), lambda bi, hi, qi: (bi, 0, hi)),
                  pl.BlockSpec((None, sl, 2 * dv), lambda bi, hi, qi: (bi, 0, hi))]
    if diff:
        args += [lam, g]
        specs += [_const_spec(lam.shape), _const_spec(g.shape)]
    kern = functools.partial(_attn_kernel, diff=diff, has_lat=has_lat, tk=tk, tq=tq, lam_init=lam_init)
    rows = 2 * tq if diff else tq
    wmax = max(tk, c) if has_lat else c
    scratch = [pltpu.VMEM((n_qt, rows, LANES), F32), pltpu.VMEM((n_qt, rows, 2 * dv), F32),
               pltpu.VMEM((rows, wmax), F32), pltpu.VMEM((rows, wmax), F32),
               pltpu.VMEM((rows, wmax), BF16), pltpu.VMEM((rows, wmax), BF16),
               pltpu.VMEM((rows, LANES), F32), pltpu.VMEM((rows, LANES), F32)]
    return pl.pallas_call(
        kern,
        out_shape=jax.ShapeDtypeStruct((b, s, heads * dv), BF16),
        grid=(b, heads, s // (n_qt * tq)),
        in_specs=specs,
        out_specs=pl.BlockSpec((None, n_qt * tq, dv), lambda bi, hi, qi: (bi, qi, hi)),
        scratch_shapes=scratch,
        compiler_params=_params(("arbitrary", "arbitrary", "arbitrary")),
        name=name,
    )(*args)


def _pool_kernel(prev_ref, cur_ref, next_ref, w_ref, sc_ref, o_ref, *, seq_len):
    i = pl.program_id(1)
    n = pl.num_programs(1)
    cur = cur_ref[...]
    tm = cur.shape[0]
    prev = jnp.where(i > 0, prev_ref[...], 0.0)
    nxt = jnp.where(i < n - 1, next_ref[...], 0.0)
    ext = jnp.concatenate([prev, cur, nxt], axis=0)
    ne = tm + 16
    t = i * tm + lax.broadcasted_iota(jnp.int32, (tm, 1), 0)
    for g, w in enumerate(POOL_WINDOWS):
        e = ext[:, g * LANES:(g + 1) * LANES]
        acc = e + pltpu.roll(e, 1, 0)
        half = 1
        while 2 * half < w:
            acc = pltpu.roll(acc, half, 0) + pltpu.roll(acc, ne - half, 0)
            half *= 2
        win = acc[8:8 + tm]
        lo = jnp.clip(t - w // 2, 0, seq_len)
        hi = jnp.clip(t - w // 2 + w, 0, seq_len)
        cnt = (hi - lo).astype(F32)
        dlt = win / cnt - cur[:, g * LANES:(g + 1) * LANES]
        y = jnp.dot(dlt.astype(BF16), w_ref[g], preferred_element_type=F32)
        o_ref[:, g * LANES:(g + 1) * LANES] = (y * sc_ref[:, g * LANES:(g + 1) * LANES]).astype(BF16)


def _pool(u, pool_w, pool_scale, tm):
    b, s, w = u.shape
    tm = min(tm, s)
    nb8 = s // 8
    r8 = tm // 8
    return pl.pallas_call(
        functools.partial(_pool_kernel, seq_len=s),
        out_shape=jax.ShapeDtypeStruct((b, s, w), BF16),
        grid=(b, s // tm),
        in_specs=[pl.BlockSpec((None, 8, w), lambda bi, i: (bi, jnp.maximum(i * r8 - 1, 0), 0)),
                  pl.BlockSpec((None, tm, w), lambda bi, i: (bi, i, 0)),
                  pl.BlockSpec((None, 8, w), lambda bi, i: (bi, jnp.minimum((i + 1) * r8, nb8 - 1), 0)),
                  _const_spec(pool_w.shape), _const_spec(pool_scale.shape)],
        out_specs=pl.BlockSpec((None, tm, w), lambda bi, i: (bi, i, 0)),
        compiler_params=_params(("arbitrary", "arbitrary")),
        name="pool_mixer",
    )(u, u, u, pool_w, pool_scale)


def _merge_kernel(*refs, moe, n_exp):
    (x_ref, gmix_ref, sh1_ref, sc1_ref, wgate_ref, oda_ref, omla_ref, opool_ref, wbr_ref, wout_ref,
     gt1_ref, gffn_ref, sh2_ref, sc2_ref) = refs[:14]
    if moe:
        router_ref, xo_ref, h2_ref, gates_ref = refs[14:]
    else:
        xo_ref, h2_ref = refs[14:]
    x = x_ref[...]
    d = x.shape[-1]
    h = _norm_mod(x, gmix_ref[...], sh1_ref[...], sc1_ref[...]).astype(BF16)
    merged = None
    for n, o_ref in enumerate((oda_ref, omla_ref, opool_ref)):
        gate = _sigmoid(jnp.dot(h, wgate_ref[:, n * d:(n + 1) * d], preferred_element_type=F32))
        proj = jnp.dot(o_ref[...], wbr_ref[n], preferred_element_type=F32)
        merged = gate * proj if merged is None else merged + gate * proj
    mix = jnp.dot(merged.astype(BF16), wout_ref[...], preferred_element_type=F32)
    xn = x + gt1_ref[...] * mix
    xo_ref[...] = xn
    h2 = _norm_mod(xn, gffn_ref[...], sh2_ref[...], sc2_ref[...])
    h2_ref[...] = h2.astype(BF16)
    if moe:
        logit = [jnp.sum(h2 * router_ref[e:e + 1, :], axis=-1, keepdims=True) for e in range(n_exp)]

        def top1(vals):
            best, idx = vals[0], jnp.zeros(vals[0].shape, jnp.int32)
            for e in range(1, n_exp):
                better = vals[e] > best
                best = jnp.where(better, vals[e], best)
                idx = jnp.where(better, e, idx)
            return best, idx

        v1, i1 = top1(logit)
        v2, i2 = top1([jnp.where(i1 == e, NEG_BIG, logit[e]) for e in range(n_exp)])
        w1 = 1.0 / (1.0 + jnp.exp(v2 - v1))
        lane = lax.broadcasted_iota(jnp.int32, gates_ref.shape, 1)
        gates_ref[...] = jnp.where(lane == i1, w1, 0.0) + jnp.where(lane == i2, 1.0 - w1, 0.0)


def _merge(x, modrows, g_mix, wgate, o_da, o_mla, o_pool, wbr, wout, g_ffn, router, n_exp, tm):
    b, s, d = x.shape
    tm = min(tm, s)
    moe = router is not None
    row = lambda j: pl.BlockSpec((None, 1, d), lambda bi, i: (bi, 0, j))
    tile = lambda w: pl.BlockSpec((None, tm, w), lambda bi, i: (bi, i, 0))
    args = [x, g_mix, modrows, modrows, wgate, o_da, o_mla, o_pool, wbr, wout, modrows, g_ffn, modrows, modrows]
    specs = [tile(d), _const_spec((1, d)), row(0), row(1), _const_spec(wgate.shape),
             tile(BRANCH_W), tile(BRANCH_W), tile(BRANCH_W), _const_spec(wbr.shape), _const_spec(wout.shape),
             row(2), _const_spec((1, d)), row(3), row(4)]
    out_shape = [jax.ShapeDtypeStruct((b, s, d), F32), jax.ShapeDtypeStruct((b, s, d), BF16)]
    out_specs = [tile(d), tile(d)]
    if moe:
        args.append(router)
        specs.append(_const_spec(router.shape))
        out_shape.append(jax.ShapeDtypeStruct((b, s, LANES), F32))
        out_specs.append(tile(LANES))
    return pl.pallas_call(
        functools.partial(_merge_kernel, moe=moe, n_exp=n_exp),
        out_shape=tuple(out_shape),
        grid=(b, s // tm),
        in_specs=specs,
        out_specs=tuple(out_specs),
        compiler_params=_params(("arbitrary", "arbitrary")),
        name="merge_out",
    )(*args)


def _ffn_kernel(h_ref, x_ref, gt_ref, wg_ref, wu_ref, wd_ref, gfin_ref, o_ref, *, final):
    h = h_ref[...]
    gte = jnp.dot(h, wg_ref[...], preferred_element_type=F32)
    up = jnp.dot(h, wu_ref[...], preferred_element_type=F32)
    act = (gte * _sigmoid(gte) * up).astype(BF16)
    y = jnp.dot(act, wd_ref[...], preferred_element_type=F32)
    xo = x_ref[...] + gt_ref[...] * y
    if final:
        xo = _rms(xo, gfin_ref[...])
    o_ref[...] = xo


def _ffn(h2, x, modrows, wg, wu, wd, g_final, final, tm):
    b, s, d = x.shape
    tm = min(tm, s)
    tile = pl.BlockSpec((None, tm, d), lambda bi, i: (bi, i, 0))
    return pl.pallas_call(
        functools.partial(_ffn_kernel, final=final),
        out_shape=jax.ShapeDtypeStruct((b, s, d), F32),
        grid=(b, s // tm),
        in_specs=[tile, tile, pl.BlockSpec((None, 1, d), lambda bi, i: (bi, 0, 5)),
                  _const_spec(wg.shape), _const_spec(wu.shape), _const_spec(wd.shape), _const_spec((1, d))],
        out_specs=tile,
        compiler_params=_params(("arbitrary", "arbitrary")),
        name="ffn_swiglu",
    )(h2, x, modrows, wg, wu, wd, g_final)


ROUTE_SUB = 256
MOE_TOKENS = 1024
MOE_FF_CHUNK = 1792
MOE_ROWS = 256
MOE_TAIL_ROWS = 64


def _route_kernel(g_ref, rank_ref, rankt_ref, cnt_ref, *, n_exp):
    t = g_ref.shape[0]
    r_io = lax.broadcasted_iota(jnp.int32, (ROUTE_SUB, ROUTE_SUB), 0)
    c_io = lax.broadcasted_iota(jnp.int32, (ROUTE_SUB, ROUTE_SUB), 1)
    tri = jnp.where(c_io <= r_io, 1.0, 0.0).astype(BF16)
    carry = jnp.zeros((1, LANES), F32)
    for j in range(t // ROUTE_SUB):
        rows = slice(j * ROUTE_SUB, (j + 1) * ROUTE_SUB)
        routed = g_ref[rows, :] != 0.0
        incl = jnp.dot(tri, jnp.where(routed, 1.0, 0.0).astype(BF16), preferred_element_type=F32)
        rank_ref[rows, :] = jnp.where(routed, incl - 1.0 + carry, -1.0)
        carry = carry + incl[ROUTE_SUB - 1:ROUTE_SUB, :]
    rankt_ref[...] = rank_ref[...].T[0:n_exp, :]
    cnt_ref[...] = jnp.broadcast_to(carry, cnt_ref.shape).astype(jnp.int32)


def _route(gates, n_exp, tm):
    n = gates.shape[0]
    return pl.pallas_call(
        functools.partial(_route_kernel, n_exp=n_exp),
        out_shape=(jax.ShapeDtypeStruct((n, LANES), F32),
                   jax.ShapeDtypeStruct((n // tm, n_exp, tm), F32),
                   jax.ShapeDtypeStruct((n // tm, 8, LANES), jnp.int32)),
        grid=(n // tm,),
        in_specs=[pl.BlockSpec((tm, LANES), lambda i: (i, 0))],
        out_specs=(pl.BlockSpec((tm, LANES), lambda i: (i, 0)),
                   pl.BlockSpec((None, n_exp, tm), lambda i: (i, 0, 0)),
                   pl.BlockSpec((None, 8, LANES), lambda i: (i, 0, 0))),
        compiler_params=_params(("arbitrary",)),
        name="moe_route",
    )(gates)


def _moe_kernel(cnt_ref, h_ref, rank_ref, rankt_ref, gates_ref, wg_ref, wu_ref, wd_ref, o_ref,
                xg_ref, yg_ref, *, n_exp):
    i, e, f = pl.program_id(0), pl.program_id(1), pl.program_id(2)
    nf = pl.num_programs(2)
    t = h_ref.shape[0]
    code = cnt_ref[i * n_exp + e]
    n_full = lax.shift_right_logical(code, 1)

    @pl.when(jnp.logical_and(e == 0, f == 0))
    def _():
        o_ref[...] = jnp.zeros_like(o_ref)

    rank_row = rankt_ref[pl.ds(e, 1), :]

    def block(base, n_rows):
        rows = pl.ds(base, n_rows)

        @pl.when(f == 0)
        def _():
            slot = (base + lax.broadcasted_iota(jnp.int32, (n_rows, t), 0)).astype(F32)
            onehot = jnp.where(rank_row == slot, 1.0, 0.0).astype(BF16)
            xg_ref[rows, :] = jnp.dot(onehot, h_ref[...], preferred_element_type=F32).astype(BF16)

        xb = xg_ref[rows, :]
        gte = jnp.dot(xb, wg_ref[...], preferred_element_type=F32)
        up = jnp.dot(xb, wu_ref[...], preferred_element_type=F32)
        act = (gte * _sigmoid(gte) * up).astype(BF16)
        part = jnp.dot(act, wd_ref[...], preferred_element_type=F32)

        @pl.when(f == 0)
        def _():
            yg_ref[rows, :] = part

        @pl.when(f > 0)
        def _():
            yg_ref[rows, :] += part

        @pl.when(f == nf - 1)
        def _():
            lane = lax.broadcasted_iota(jnp.int32, (t, LANES), 1)
            rank_col = jnp.sum(jnp.where(lane == e, rank_ref[...], 0.0), axis=-1, keepdims=True)
            gate_col = jnp.sum(jnp.where(lane == e, gates_ref[...], 0.0), axis=-1, keepdims=True)
            slot = (base + lax.broadcasted_iota(jnp.int32, (t, n_rows), 1)).astype(F32)
            onehot = jnp.where(rank_col == slot, 1.0, 0.0).astype(BF16)
            o_ref[...] += gate_col * jnp.dot(onehot, yg_ref[rows, :].astype(BF16),
                                             preferred_element_type=F32)

    def full_block(b, carry):
        block(pl.multiple_of(b * MOE_ROWS, MOE_ROWS), MOE_ROWS)
        return carry

    lax.fori_loop(0, n_full, full_block, 0)

    @pl.when(jnp.bitwise_and(code, 1) == 1)
    def _():
        block(pl.multiple_of(n_full * MOE_ROWS, MOE_ROWS), MOE_TAIL_ROWS)


def _moe_routed(h2, gates, wg, wu, wd, tm, tf):
    n, d = h2.shape
    n_exp, _, dff = wg.shape
    tm = min(tm, n)
    rank, rank_t, cnt = _route(gates, n_exp, tm)
    counts = cnt[:, 0, :n_exp].reshape(-1)
    rem = counts % MOE_ROWS
    tail = jnp.logical_and(rem > 0, rem <= MOE_TAIL_ROWS)
    n_blocks = 2 * (counts // MOE_ROWS + (rem > MOE_TAIL_ROWS)) + tail
    cap = tm
    grid_spec = pltpu.PrefetchScalarGridSpec(
        num_scalar_prefetch=1,
        grid=(n // tm, n_exp, dff // tf),
        in_specs=[pl.BlockSpec((tm, d), lambda i, e, f, c: (i, 0)),
                  pl.BlockSpec((tm, LANES), lambda i, e, f, c: (i, 0)),
                  pl.BlockSpec((None, n_exp, tm), lambda i, e, f, c: (i, 0, 0)),
                  pl.BlockSpec((tm, LANES), lambda i, e, f, c: (i, 0)),
                  pl.BlockSpec((None, d, tf), lambda i, e, f, c: (e, 0, f)),
                  pl.BlockSpec((None, d, tf), lambda i, e, f, c: (e, 0, f)),
                  pl.BlockSpec((None, tf, d), lambda i, e, f, c: (e, f, 0))],
        out_specs=pl.BlockSpec((tm, d), lambda i, e, f, c: (i, 0)),
        scratch_shapes=[pltpu.VMEM((cap, d), BF16), pltpu.VMEM((cap, d), F32)])
    return pl.pallas_call(
        functools.partial(_moe_kernel, n_exp=n_exp),
        out_shape=jax.ShapeDtypeStruct((n, d), F32),
        grid_spec=grid_spec,
        compiler_params=_params(("arbitrary",) * 3),
        name="moe_swiglu",
    )(n_blocks, h2, rank, rank_t, gates, wg, wu, wd)


def _residual_kernel(x_ref, y_ref, gt_ref, gfin_ref, o_ref, *, final):
    xo = x_ref[...] + gt_ref[...] * y_ref[...]
    if final:
        xo = _rms(xo, gfin_ref[...])
    o_ref[...] = xo


def _residual(x, y, modrows, g_final, final, tm):
    b, s, d = x.shape
    tm = min(tm, s)
    tile = pl.BlockSpec((None, tm, d), lambda bi, i: (bi, i, 0))
    return pl.pallas_call(
        functools.partial(_residual_kernel, final=final),
        out_shape=jax.ShapeDtypeStruct((b, s, d), F32),
        grid=(b, s // tm),
        in_specs=[tile, tile, pl.BlockSpec((None, 1, d), lambda bi, i: (bi, 0, 5)), _const_spec((1, d))],
        out_specs=tile,
        compiler_params=_params(("arbitrary", "arbitrary")),
        name="moe_residual",
    )(x, y, modrows, g_final)


def _moe(h2, x, modrows, gates, wg, wu, wd, g_final, final):
    b, s, d = x.shape
    y = _moe_routed(h2.reshape(b * s, d), gates.reshape(b * s, LANES), wg, wu, wd, MOE_TOKENS, MOE_FF_CHUNK)
    return _residual(x, y.reshape(b, s, d), modrows, g_final, final, 512)


def _rope_tables(seq):
    axis_dim = DA_QK // 2
    n_freq = axis_dim // 2
    inv = jnp.exp(-math.log(ROPE_BASE) * jnp.arange(n_freq, dtype=F32) * (2.0 / axis_dim))
    t = jnp.arange(seq, dtype=jnp.int32)
    ar = (t // GRID_W).astype(F32)[:, None] * inv
    ac = (t % GRID_W).astype(F32)[:, None] * inv
    cos = jnp.concatenate([jnp.cos(ar), jnp.cos(ar), jnp.cos(ac), jnp.cos(ac)], axis=-1)
    sin = jnp.concatenate([jnp.sin(ar), jnp.sin(ar), jnp.sin(ac), jnp.sin(ac)], axis=-1)
    first = (jnp.arange(DA_QK) % axis_dim) < n_freq
    sa = jnp.where(first, -sin, 0.0)
    sb = jnp.where(first, 0.0, sin)
    rep = lambda a: jnp.tile(a, (1, LANES // DA_QK))
    return rep(cos), rep(sa), rep(sb)


def _identity_tables(seq):
    return jnp.ones((seq, LANES), F32), jnp.zeros((seq, LANES), F32), jnp.zeros((seq, LANES), F32)


def _pack_layer_weights(w_in, w_uq, w_ukv):
    d = w_in.shape[0]
    zpad = jnp.zeros((d, LANES - MLA_ROPE), w_in.dtype)
    o_kr = 3 * 512 + MLA_Q_RANK + MLA_KV_RANK
    o_pool = o_kr + MLA_ROPE
    w2 = jnp.concatenate([w_in[:, :o_kr + MLA_ROPE], zpad, w_in[:, o_pool:o_pool + 512]], axis=1).astype(BF16)
    wgate = w_in[:, o_pool + 512:].astype(BF16)
    hq = MLA_NOPE + MLA_ROPE
    wq = w_uq.reshape(MLA_Q_RANK, MLA_HEADS, hq)
    wq = jnp.concatenate([wq, jnp.zeros((MLA_Q_RANK, MLA_HEADS, MLA_QK_PAD - hq), w_uq.dtype)], axis=-1)
    wuq = wq.reshape(MLA_Q_RANK, MLA_HEADS * MLA_QK_PAD).astype(BF16)
    wkv = w_ukv.reshape(MLA_KV_RANK, MLA_HEADS, MLA_NOPE + MLA_V)
    wukv = jnp.concatenate([wkv[:, :, :MLA_NOPE].reshape(MLA_KV_RANK, -1),
                            wkv[:, :, MLA_NOPE:].reshape(MLA_KV_RANK, -1)], axis=1).astype(BF16)
    return w2, wgate, wuq, wukv


def kernel(x, c, ctx, c_ctx, w_mod, b_mod, g_mix, w_in, da_lambda, da_subln, mla_gq, w_uq, mla_gkv, w_ukv,
           pool_w, pool_scale, w_branch, w_out, g_ffn, ffn_w_gate, ffn_w_up, ffn_w_down, moe_router,
           moe_w_gate, moe_w_up, moe_w_down, g_final):
    bsz, seq, d = x.shape
    n_ctx = ctx.shape[1]
    depth = w_mod.shape[0]

    cond8 = jnp.zeros((8, d), F32).at[:bsz].set(c).at[bsz].set(c_ctx)
    mod = _mod_rows(cond8, w_mod, b_mod)
    lat_tabs = _rope_tables(seq)
    ctx_tabs = _identity_tables(n_ctx)
    g_fin = g_final.reshape(1, d)

    xc = ctx
    for l in range(depth):
        need_ctx = l < depth - 1
        lam_init = 0.8 - 0.6 * math.exp(-0.3 * l)
        mod_lat = mod[l, :bsz][:, None, :]
        mod_ctx = jnp.broadcast_to(mod[l, bsz][None, None, :], (bsz, 1, N_MOD * d))
        w2, wgate, wuq, wukv = _pack_layer_weights(w_in[l], w_uq[l], w_ukv[l])
        gmix = g_mix[l].reshape(1, d)
        gffn = g_ffn[l].reshape(1, d)
        gq = mla_gq[l].reshape(1, -1)
        gkv = mla_gkv[l].reshape(1, -1)
        subln = da_subln[l].reshape(1, -1)
        lam = da_lambda[l]
        pw = pool_w[l].astype(BF16)
        psc = pool_scale[l].reshape(1, -1)
        wbr = w_branch[l].astype(BF16)
        wout = w_out[l].astype(BF16)
        j = l // 2
        dense = l % 2 == 0
        final = l == depth - 1

        qda, kda, vda, qm, km, vm, pin = _inproj(x, mod_lat, gmix, w2, gq, wuq, gkv, wukv, lat_tabs, PROJ_ROWS)
        cqda, ckda, cvda, cqm, ckm, cvm, cpin = _inproj(xc, mod_ctx, gmix, w2, gq, wuq, gkv, wukv, ctx_tabs, PROJ_ROWS)

        o_da = _attention(qda, ckda, cvda, kda, vda, DA_HEADS, LANES, DA_V, ATTN_ROWS // 2, ATTN_KEYS, diff=True,
                          lam=lam, g=subln, lam_init=lam_init, name="diff_attn")
        o_mla = _attention(qm, ckm, cvm, km, vm, MLA_HEADS, MLA_QK_PAD, MLA_V, ATTN_ROWS, ATTN_KEYS, name="mla_attn")
        o_pool = _pool(pin, pw, psc, 512)

        router = None
        n_exp = moe_router.shape[-1]
        if not dense:
            router = moe_router[j].T
        outs = _merge(x, mod_lat, gmix, wgate, o_da, o_mla, o_pool, wbr, wout, gffn, router, n_exp, PROJ_ROWS)

        if dense:
            wg = ffn_w_gate[j].astype(BF16)
            wu = ffn_w_up[j].astype(BF16)
            wd = ffn_w_down[j].astype(BF16)
            x = _ffn(outs[1], outs[0], mod_lat, wg, wu, wd, g_fin, final, 256)
        else:
            wg = moe_w_gate[j].astype(BF16)
            wu = moe_w_up[j].astype(BF16)
            wd = moe_w_down[j].astype(BF16)
            x = _moe(outs[1], outs[0], mod_lat, outs[2], wg, wu, wd, g_fin, final)

        if need_ctx:
            co_da = _attention(cqda, ckda, cvda, None, None, DA_HEADS, LANES, DA_V, ATTN_ROWS // 2, ATTN_KEYS, diff=True,
                               lam=lam, g=subln, lam_init=lam_init, name="diff_attn_ctx")
            co_mla = _attention(cqm, ckm, cvm, None, None, MLA_HEADS, MLA_QK_PAD, MLA_V, ATTN_ROWS, ATTN_KEYS,
                                name="mla_attn_ctx")
            co_pool = _pool(cpin, pw, psc, 512)
            couts = _merge(xc, mod_ctx, gmix, wgate, co_da, co_mla, co_pool, wbr, wout, gffn, router, n_exp, PROJ_ROWS)
            if dense:
                xc = _ffn(couts[1], couts[0], mod_ctx, wg, wu, wd, g_fin, False, 256)
            else:
                xc = _moe(couts[1], couts[0], mod_ctx, couts[2], wg, wu, wd, g_fin, False)
    return x
```

```python
import functools
import math

import jax
import jax.numpy as jnp
from jax import lax
from jax.experimental import pallas as pl
from jax.experimental.pallas import tpu as pltpu

F32 = jnp.float32
BF16 = jnp.bfloat16

GRID_W = 64
DA_HEADS = 4
DA_QK = 64
DA_V = 128
MLA_HEADS = 4
MLA_NOPE = 128
MLA_ROPE = 64
MLA_V = 128
MLA_Q_RANK = 384
MLA_KV_RANK = 256
POOL_WINDOWS = (2, 4, 8, 16)
POOL_GROUP_W = 128
N_BRANCH = 3
BRANCH_W = 512
ROPE_BASE = 10000.0
EPS = 1e-6
N_MOD = 6
LANES = 128
MLA_QK_PAD = 256
LOG2E = math.log2(math.e)
DA_SCALE = DA_QK ** -0.5 * LOG2E
MLA_SCALE = (MLA_NOPE + MLA_ROPE) ** -0.5 * LOG2E
NEG_BIG = -1e30
PROJ_ROWS = 512
ATTN_ROWS = 2048
ATTN_KEYS = 512
ATTN_Q_TILES = 1
VMEM_LIMIT = 56 * 1024 * 1024

C_DAQ, C_DAK, C_DAV = 0, 512, 1024
C_QD = 1536
C_KVD = C_QD + MLA_Q_RANK
C_KR = C_KVD + MLA_KV_RANK
C_POOL = C_KR + LANES
W2_COLS = C_POOL + 512


def _sigmoid(v):
    return 1.0 / (1.0 + jnp.exp(-v))


def _params(sem, vmem=VMEM_LIMIT):
    return pltpu.CompilerParams(dimension_semantics=sem, vmem_limit_bytes=vmem)


def _const_spec(shape):
    nd = len(shape)
    return pl.BlockSpec(shape, lambda *_: (0,) * nd)


def _mod_kernel(cond_ref, w_ref, b_ref, o_ref):
    c = cond_ref[...]
    s = c * _sigmoid(c)
    o_ref[...] = jnp.dot(s, w_ref[...], preferred_element_type=F32,
                         precision=lax.Precision.HIGHEST) + b_ref[...]


def _mod_rows(cond8, w_mod, b_mod):
    depth, d, n = w_mod.shape
    tn = 1536
    return pl.pallas_call(
        _mod_kernel,
        out_shape=jax.ShapeDtypeStruct((depth, 8, n), F32),
        grid=(depth, n // tn),
        in_specs=[pl.BlockSpec((8, d), lambda l, j: (0, 0)),
                  pl.BlockSpec((None, d, tn), lambda l, j: (l, 0, j)),
                  pl.BlockSpec((None, 1, tn), lambda l, j: (l, 0, j))],
        out_specs=pl.BlockSpec((None, 8, tn), lambda l, j: (l, 0, j)),
        compiler_params=_params(("arbitrary", "arbitrary")),
        name="adaln_rows",
    )(cond8, w_mod, b_mod.reshape(depth, 1, n))


def _norm_mod(x, g, sh, sc):
    r = lax.rsqrt(jnp.mean(x * x, axis=-1, keepdims=True) + EPS)
    return (x * r * g) * (1.0 + sc) + sh


def _rms(v, g):
    return v * lax.rsqrt(jnp.mean(v * v, axis=-1, keepdims=True) + EPS) * g


def _inproj_kernel(x_ref, g_ref, sh_ref, sc_ref, w_ref, gq_ref, wuq_ref, gkv_ref, wukv_ref,
                   cos_ref, sa_ref, sb_ref,
                   qda_ref, kda_ref, vda_ref, qm_ref, km_ref, vm_ref, pool_ref):
    h = _norm_mod(x_ref[...], g_ref[...], sh_ref[...], sc_ref[...])
    z = jnp.dot(h.astype(BF16), w_ref[...], preferred_element_type=F32)
    cos = cos_ref[...]
    sa = sa_ref[...]
    sb = sb_ref[...]

    def rope(blk):
        return blk * cos + pltpu.roll(blk, LANES - 16, 1) * sa + pltpu.roll(blk, 16, 1) * sb

    lo = lax.broadcasted_iota(jnp.int32, cos.shape, 1) < MLA_ROPE

    for hh in range(DA_HEADS):
        c0 = hh * LANES
        qda_ref[:, c0:c0 + LANES] = (rope(z[:, C_DAQ + c0:C_DAQ + c0 + LANES]) * DA_SCALE).astype(BF16)
        kda_ref[:, c0:c0 + LANES] = rope(z[:, C_DAK + c0:C_DAK + c0 + LANES]).astype(BF16)

    qn = _rms(z[:, C_QD:C_QD + MLA_Q_RANK], gq_ref[...])
    qf = jnp.dot(qn.astype(BF16), wuq_ref[...], preferred_element_type=F32)
    kvn = _rms(z[:, C_KVD:C_KVD + MLA_KV_RANK], gkv_ref[...])
    kvf = jnp.dot(kvn.astype(BF16), wukv_ref[...], preferred_element_type=F32)
    kr = jnp.where(lo, rope(z[:, C_KR:C_KR + LANES]), 0.0).astype(BF16)
    for hh in range(MLA_HEADS):
        c0 = hh * MLA_QK_PAD
        qm_ref[:, c0:c0 + LANES] = (qf[:, c0:c0 + LANES] * MLA_SCALE).astype(BF16)
        qr = jnp.where(lo, rope(qf[:, c0 + LANES:c0 + 2 * LANES]), 0.0)
        qm_ref[:, c0 + LANES:c0 + 2 * LANES] = (qr * MLA_SCALE).astype(BF16)
        km_ref[:, c0:c0 + LANES] = kvf[:, hh * LANES:(hh + 1) * LANES].astype(BF16)
        km_ref[:, c0 + LANES:c0 + 2 * LANES] = kr
    ones = jnp.ones(cos.shape, BF16)
    for hh in range(MLA_HEADS):
        c0 = 2 * hh * LANES
        vda_ref[:, c0:c0 + LANES] = z[:, C_DAV + hh * LANES:C_DAV + (hh + 1) * LANES].astype(BF16)
        vda_ref[:, c0 + LANES:c0 + 2 * LANES] = ones
        vm_ref[:, c0:c0 + LANES] = kvf[:, 512 + hh * LANES:512 + (hh + 1) * LANES].astype(BF16)
        vm_ref[:, c0 + LANES:c0 + 2 * LANES] = ones
    pool_ref[...] = z[:, C_POOL:C_POOL + 512]


def _inproj(x, modrows, g_mix, w2, gq, wuq, gkv, wukv, tabs, tm):
    b, s, d = x.shape
    tm = min(tm, s)
    cos, sa, sb = tabs
    row = lambda j: pl.BlockSpec((None, 1, d), lambda bi, i: (bi, 0, j))
    tab = pl.BlockSpec((tm, LANES), lambda bi, i: (i, 0))
    out = lambda w, dt: jax.ShapeDtypeStruct((b, s, w), dt)
    ospec = lambda w: pl.BlockSpec((None, tm, w), lambda bi, i: (bi, i, 0))
    return pl.pallas_call(
        _inproj_kernel,
        out_shape=(out(512, BF16), out(512, BF16), out(1024, BF16), out(1024, BF16), out(1024, BF16),
                   out(1024, BF16), out(512, F32)),
        grid=(b, s // tm),
        in_specs=[pl.BlockSpec((None, tm, d), lambda bi, i: (bi, i, 0)),
                  _const_spec((1, d)), row(0), row(1),
                  _const_spec(w2.shape), _const_spec(gq.shape), _const_spec(wuq.shape),
                  _const_spec(gkv.shape), _const_spec(wukv.shape), tab, tab, tab],
        out_specs=(ospec(512), ospec(512), ospec(1024), ospec(1024), ospec(1024), ospec(1024), ospec(512)),
        compiler_params=_params(("arbitrary", "arbitrary")),
        name="in_proj",
    )(x, g_mix, modrows, modrows, w2, gq, wuq, gkv, wukv, cos, sa, sb)


def _attn_kernel(*refs, diff, has_lat, tk, tq, lam_init):
    it = iter(refs)
    q_ref, kc_ref, vc_ref = next(it), next(it), next(it)
    kl_ref = vl_ref = lam_ref = g_ref = None
    if has_lat:
        kl_ref, vl_ref = next(it), next(it)
    if diff:
        lam_ref, g_ref = next(it), next(it)
    o_ref = next(it)
    m_ref, acc_ref = next(it), next(it)
    s_refs = (next(it), next(it))
    p_refs = (next(it), next(it))
    al_refs = (next(it), next(it))
    n_qt = q_ref.shape[0] // tq
    dv = o_ref.shape[-1]

    def stacked_q(qt):
        q = q_ref[qt * tq:(qt + 1) * tq, :]
        if not diff:
            return q
        lane = lax.broadcasted_iota(jnp.int32, q.shape, 1)
        zero = jnp.zeros_like(q)
        return jnp.concatenate([jnp.where(lane < DA_QK, q, zero), jnp.where(lane >= DA_QK, q, zero)], axis=0)

    qs = [stacked_q(qt) for qt in range(n_qt)]
    n_ctx = kc_ref.shape[0]
    n_lat = kl_ref.shape[0] // tk if has_lat else 0
    width = lambda c: n_ctx if c == 0 else tk
    k_of = lambda c: kc_ref[...] if c == 0 else kl_ref[(c - 1) * tk:c * tk, :]
    v_of = lambda c: vc_ref[...] if c == 0 else vl_ref[(c - 1) * tk:c * tk, :]

    def qk(slot, qt, c):
        k = k_of(c)
        s_refs[slot][:, :k.shape[0]] = lax.dot_general(qs[qt], k, (((1,), (1,)), ((), ())),
                                                       preferred_element_type=F32)

    def sm(slot, qt, c):
        s_ref, p_ref = s_refs[slot], p_refs[slot]
        blocks = [slice(j * LANES, (j + 1) * LANES) for j in range(width(c) // LANES)]
        mx = s_ref[:, blocks[0]]
        for blk in blocks[1:]:
            mx = jnp.maximum(mx, s_ref[:, blk])
        m = m_ref[qt]
        m_new = jnp.maximum(m, jnp.max(mx, axis=-1, keepdims=True))
        al_refs[slot][...] = jnp.exp2(m - m_new)
        m_ref[qt] = m_new
        for blk in blocks:
            p_ref[:, blk] = jnp.exp2(s_ref[:, blk] - m_new).astype(BF16)

    def pv(slot, qt, c):
        v = v_of(c)
        new = jnp.dot(p_refs[slot][:, :v.shape[0]], v, preferred_element_type=F32)
        al = al_refs[slot][...]
        for blk in (slice(0, dv), slice(dv, 2 * dv)):
            acc_ref[qt, :, blk] = al * acc_ref[qt, :, blk] + new[:, blk]

    def finish(qt):
        acc = acc_ref[qt]
        o = acc[:, :dv] / acc[:, dv:]
        if diff:
            lv = lam_ref[...]
            a = jnp.sum(lv[0:1, :] * lv[1:2, :], axis=-1, keepdims=True)
            b = jnp.sum(lv[2:3, :] * lv[3:4, :], axis=-1, keepdims=True)
            lam = jnp.exp(a) - jnp.exp(b) + lam_init
            o = o[:tq] - lam * o[tq:]
            o = _rms(o, g_ref[...]) * (1.0 - lam_init)
        o_ref[qt * tq:(qt + 1) * tq, :] = o.astype(o_ref.dtype)

    m_ref[...] = jnp.full(m_ref.shape, NEG_BIG, F32)
    acc_ref[...] = jnp.zeros(acc_ref.shape, F32)
    items = [(qt, 0) for qt in range(n_qt)] + [(qt, c) for qt in range(n_qt) for c in range(1, n_lat + 1)]
    for g in range(len(items) + 2):
        if g < len(items):
            qk(g % 2, *items[g])
        if 0 <= g - 2 < len(items):
            pv(g % 2, *items[g - 2])
        if 0 <= g - 1 < len(items):
            sm((g - 1) % 2, *items[g - 1])
        if 0 <= g - 2 < len(items) and items[g - 2][1] == n_lat:
            finish(items[g - 2][0])


def _attention(q, kc, vc, kl, vl, heads, dq, dv, tq, tk, diff=False, lam=None, g=None, lam_init=0.0,
               name="attn"):
    b, s, _ = q.shape
    c = kc.shape[1]
    tq = min(tq, s)
    n_qt = min(ATTN_Q_TILES, s // tq)
    has_lat = kl is not None
    args = [q, kc, vc]
    specs = [pl.BlockSpec((None, n_qt * tq, dq), lambda bi, hi, qi: (bi, qi, hi)),
             pl.BlockSpec((None, c, dq), lambda bi, hi, qi: (bi, 0, hi)),
             pl.BlockSpec((None, c, 2 * dv), lambda bi, hi, qi: (bi, 0, hi))]
    sl = 0
    if has_lat:
        sl = kl.shape[1]
        tk = min(tk, sl)
        args += [kl, vl]
        specs += [pl.BlockSpec((None, sl, dq), lambda bi, hi, qi: (bi, 0, hi)),
                  pl.BlockSpec((None, sl, 2 * dv), lambda bi, hi, qi: (bi, 0, hi))]
    if diff:
        args += [lam, g]
        specs += [_const_spec(lam.shape), _const_spec(g.shape)]
    kern = functools.partial(_attn_kernel, diff=diff, has_lat=has_lat, tk=tk, tq=tq, lam_init=lam_init)
    rows = 2 * tq if diff else tq
    wmax = max(tk, c) if has_lat else c
    scratch = [pltpu.VMEM((n_qt, rows, LANES), F32), pltpu.VMEM((n_qt, rows, 2 * dv), F32),
               pltpu.VMEM((rows, wmax), F32), pltpu.VMEM((rows, wmax), F32),
               pltpu.VMEM((rows, wmax), BF16), pltpu.VMEM((rows, wmax), BF16),
               pltpu.VMEM((rows, LANES), F32), pltpu.VMEM((rows, LANES), F32)]
    return pl.pallas_call(
        kern,
        out_shape=jax.ShapeDtypeStruct((b, s, heads * dv), BF16),
        grid=(b, heads, s // (n_qt * tq)),
        in_specs=specs,
        out_specs=pl.BlockSpec((None, n_qt * tq, dv), lambda bi, hi, qi: (bi, qi, hi)),
        scratch_shapes=scratch,
        compiler_params=_params(("arbitrary", "arbitrary", "arbitrary")),
        name=name,
    )(*args)


def _pool_kernel(prev_ref, cur_ref, next_ref, w_ref, sc_ref, o_ref, *, seq_len):
    i = pl.program_id(1)
    n = pl.num_programs(1)
    cur = cur_ref[...]
    tm = cur.shape[0]
    prev = jnp.where(i > 0, prev_ref[...], 0.0)
    nxt = jnp.where(i < n - 1, next_ref[...], 0.0)
    ext = jnp.concatenate([prev, cur, nxt], axis=0)
    ne = tm + 16
    t = i * tm + lax.broadcasted_iota(jnp.int32, (tm, 1), 0)
    for g, w in enumerate(POOL_WINDOWS):
        e = ext[:, g * LANES:(g + 1) * LANES]
        acc = e + pltpu.roll(e, 1, 0)
        half = 1
        while 2 * half < w:
            acc = pltpu.roll(acc, half, 0) + pltpu.roll(acc, ne - half, 0)
            half *= 2
        win = acc[8:8 + tm]
        lo = jnp.clip(t - w // 2, 0, seq_len)
        hi = jnp.clip(t - w // 2 + w, 0, seq_len)
        cnt = (hi - lo).astype(F32)
        dlt = win / cnt - cur[:, g * LANES:(g + 1) * LANES]
        y = jnp.dot(dlt.astype(BF16), w_ref[g], preferred_element_type=F32)
        o_ref[:, g * LANES:(g + 1) * LANES] = (y * sc_ref[:, g * LANES:(g + 1) * LANES]).astype(BF16)


def _pool(u, pool_w, pool_scale, tm):
    b, s, w = u.shape
    tm = min(tm, s)
    nb8 = s // 8
    r8 = tm // 8
    return pl.pallas_call(
        functools.partial(_pool_kernel, seq_len=s),
        out_shape=jax.ShapeDtypeStruct((b, s, w), BF16),
        grid=(b, s // tm),
        in_specs=[pl.BlockSpec((None, 8, w), lambda bi, i: (bi, jnp.maximum(i * r8 - 1, 0), 0)),
                  pl.BlockSpec((None, tm, w), lambda bi, i: (bi, i, 0)),
                  pl.BlockSpec((None, 8, w), lambda bi, i: (bi, jnp.minimum((i + 1) * r8, nb8 - 1), 0)),
                  _const_spec(pool_w.shape), _const_spec(pool_scale.shape)],
        out_specs=pl.BlockSpec((None, tm, w), lambda bi, i: (bi, i, 0)),
        compiler_params=_params(("arbitrary", "arbitrary")),
        name="pool_mixer",
    )(u, u, u, pool_w, pool_scale)


def _merge_kernel(*refs, moe, n_exp):
    (x_ref, gmix_ref, sh1_ref, sc1_ref, wgate_ref, oda_ref, omla_ref, opool_ref, wbr_ref, wout_ref,
     gt1_ref, gffn_ref, sh2_ref, sc2_ref) = refs[:14]
    if moe:
        router_ref, xo_ref, h2_ref, gates_ref = refs[14:]
    else:
        xo_ref, h2_ref = refs[14:]
    x = x_ref[...]
    d = x.shape[-1]
    h = _norm_mod(x, gmix_ref[...], sh1_ref[...], sc1_ref[...]).astype(BF16)
    merged = None
    for n, o_ref in enumerate((oda_ref, omla_ref, opool_ref)):
        gate = _sigmoid(jnp.dot(h, wgate_ref[:, n * d:(n + 1) * d], preferred_element_type=F32))
        proj = jnp.dot(o_ref[...], wbr_ref[n], preferred_element_type=F32)
        merged = gate * proj if merged is None else merged + gate * proj
    mix = jnp.dot(merged.astype(BF16), wout_ref[...], preferred_element_type=F32)
    xn = x + gt1_ref[...] * mix
    xo_ref[...] = xn
    h2 = _norm_mod(xn, gffn_ref[...], sh2_ref[...], sc2_ref[...])
    h2_ref[...] = h2.astype(BF16)
    if moe:
        logit = [jnp.sum(h2 * router_ref[e:e + 1, :], axis=-1, keepdims=True) for e in range(n_exp)]

        def top1(vals):
            best, idx = vals[0], jnp.zeros(vals[0].shape, jnp.int32)
            for e in range(1, n_exp):
                better = vals[e] > best
                best = jnp.where(better, vals[e], best)
                idx = jnp.where(better, e, idx)
            return best, idx

        v1, i1 = top1(logit)
        v2, i2 = top1([jnp.where(i1 == e, NEG_BIG, logit[e]) for e in range(n_exp)])
        w1 = 1.0 / (1.0 + jnp.exp(v2 - v1))
        lane = lax.broadcasted_iota(jnp.int32, gates_ref.shape, 1)
        gates_ref[...] = jnp.where(lane == i1, w1, 0.0) + jnp.where(lane == i2, 1.0 - w1, 0.0)


def _merge(x, modrows, g_mix, wgate, o_da, o_mla, o_pool, wbr, wout, g_ffn, router, n_exp, tm):
    b, s, d = x.shape
    tm = min(tm, s)
    moe = router is not None
    row = lambda j: pl.BlockSpec((None, 1, d), lambda bi, i: (bi, 0, j))
    tile = lambda w: pl.BlockSpec((None, tm, w), lambda bi, i: (bi, i, 0))
    args = [x, g_mix, modrows, modrows, wgate, o_da, o_mla, o_pool, wbr, wout, modrows, g_ffn, modrows, modrows]
    specs = [tile(d), _const_spec((1, d)), row(0), row(1), _const_spec(wgate.shape),
             tile(BRANCH_W), tile(BRANCH_W), tile(BRANCH_W), _const_spec(wbr.shape), _const_spec(wout.shape),
             row(2), _const_spec((1, d)), row(3), row(4)]
    out_shape = [jax.ShapeDtypeStruct((b, s, d), F32), jax.ShapeDtypeStruct((b, s, d), BF16)]
    out_specs = [tile(d), tile(d)]
    if moe:
        args.append(router)
        specs.append(_const_spec(router.shape))
        out_shape.append(jax.ShapeDtypeStruct((b, s, LANES), F32))
        out_specs.append(tile(LANES))
    return pl.pallas_call(
        functools.partial(_merge_kernel, moe=moe, n_exp=n_exp),
        out_shape=tuple(out_shape),
        grid=(b, s // tm),
        in_specs=specs,
        out_specs=tuple(out_specs),
        compiler_params=_params(("arbitrary", "arbitrary")),
        name="merge_out",
    )(*args)


def _ffn_kernel(h_ref, x_ref, gt_ref, wg_ref, wu_ref, wd_ref, gfin_ref, o_ref, *, final):
    h = h_ref[...]
    gte = jnp.dot(h, wg_ref[...], preferred_element_type=F32)
    up = jnp.dot(h, wu_ref[...], preferred_element_type=F32)
    act = (gte * _sigmoid(gte) * up).astype(BF16)
    y = jnp.dot(act, wd_ref[...], preferred_element_type=F32)
    xo = x_ref[...] + gt_ref[...] * y
    if final:
        xo = _rms(xo, gfin_ref[...])
    o_ref[...] = xo


def _ffn(h2, x, modrows, wg, wu, wd, g_final, final, tm):
    b, s, d = x.shape
    tm = min(tm, s)
    tile = pl.BlockSpec((None, tm, d), lambda bi, i: (bi, i, 0))
    return pl.pallas_call(
        functools.partial(_ffn_kernel, final=final),
        out_shape=jax.ShapeDtypeStruct((b, s, d), F32),
        grid=(b, s // tm),
        in_specs=[tile, tile, pl.BlockSpec((None, 1, d), lambda bi, i: (bi, 0, 5)),
                  _const_spec(wg.shape), _const_spec(wu.shape), _const_spec(wd.shape), _const_spec((1, d))],
        out_specs=tile,
        compiler_params=_params(("arbitrary", "arbitrary")),
        name="ffn_swiglu",
    )(h2, x, modrows, wg, wu, wd, g_final)


ROUTE_SUB = 256
MOE_TOKENS = 1024
MOE_FF_CHUNK = 1792
MOE_ROWS = 256
MOE_TAIL_ROWS = 64


def _route_kernel(g_ref, rank_ref, rankt_ref, cnt_ref, *, n_exp):
    t = g_ref.shape[0]
    r_io = lax.broadcasted_iota(jnp.int32, (ROUTE_SUB, ROUTE_SUB), 0)
    c_io = lax.broadcasted_iota(jnp.int32, (ROUTE_SUB, ROUTE_SUB), 1)
    tri = jnp.where(c_io <= r_io, 1.0, 0.0).astype(BF16)
    carry = jnp.zeros((1, LANES), F32)
    for j in range(t // ROUTE_SUB):
        rows = slice(j * ROUTE_SUB, (j + 1) * ROUTE_SUB)
        routed = g_ref[rows, :] != 0.0
        incl = jnp.dot(tri, jnp.where(routed, 1.0, 0.0).astype(BF16), preferred_element_type=F32)
        rank_ref[rows, :] = jnp.where(routed, incl - 1.0 + carry, -1.0)
        carry = carry + incl[ROUTE_SUB - 1:ROUTE_SUB, :]
    rankt_ref[...] = rank_ref[...].T[0:n_exp, :]
    cnt_ref[...] = jnp.broadcast_to(carry, cnt_ref.shape).astype(jnp.int32)


def _route(gates, n_exp, tm):
    n = gates.shape[0]
    return pl.pallas_call(
        functools.partial(_route_kernel, n_exp=n_exp),
        out_shape=(jax.ShapeDtypeStruct((n, LANES), F32),
                   jax.ShapeDtypeStruct((n // tm, n_exp, tm), F32),
                   jax.ShapeDtypeStruct((n // tm, 8, LANES), jnp.int32)),
        grid=(n // tm,),
        in_specs=[pl.BlockSpec((tm, LANES), lambda i: (i, 0))],
        out_specs=(pl.BlockSpec((tm, LANES), lambda i: (i, 0)),
                   pl.BlockSpec((None, n_exp, tm), lambda i: (i, 0, 0)),
                   pl.BlockSpec((None, 8, LANES), lambda i: (i, 0, 0))),
        compiler_params=_params(("arbitrary",)),
        name="moe_route",
    )(gates)


def _moe_kernel(cnt_ref, h_ref, rank_ref, rankt_ref, gates_ref, wg_ref, wu_ref, wd_ref, o_ref,
                xg_ref, yg_ref, *, n_exp):
    i, e, f = pl.program_id(0), pl.program_id(1), pl.program_id(2)
    nf = pl.num_programs(2)
    t = h_ref.shape[0]
    code = cnt_ref[i * n_exp + e]
    n_full = lax.shift_right_logical(code, 1)

    @pl.when(jnp.logical_and(e == 0, f == 0))
    def _():
        o_ref[...] = jnp.zeros_like(o_ref)

    rank_row = rankt_ref[pl.ds(e, 1), :]

    def block(base, n_rows):
        rows = pl.ds(base, n_rows)

        @pl.when(f == 0)
        def _():
            slot = (base + lax.broadcasted_iota(jnp.int32, (n_rows, t), 0)).astype(F32)
            onehot = jnp.where(rank_row == slot, 1.0, 0.0).astype(BF16)
            xg_ref[rows, :] = jnp.dot(onehot, h_ref[...], preferred_element_type=F32).astype(BF16)

        xb = xg_ref[rows, :]
        gte = jnp.dot(xb, wg_ref[...], preferred_element_type=F32)
        up = jnp.dot(xb, wu_ref[...], preferred_element_type=F32)
        act = (gte * _sigmoid(gte) * up).astype(BF16)
        part = jnp.dot(act, wd_ref[...], preferred_element_type=F32)

        @pl.when(f == 0)
        def _():
            yg_ref[rows, :] = part

        @pl.when(f > 0)
        def _():
            yg_ref[rows, :] += part

        @pl.when(f == nf - 1)
        def _():
            lane = lax.broadcasted_iota(jnp.int32, (t, LANES), 1)
            rank_col = jnp.sum(jnp.where(lane == e, rank_ref[...], 0.0), axis=-1, keepdims=True)
            gate_col = jnp.sum(jnp.where(lane == e, gates_ref[...], 0.0), axis=-1, keepdims=True)
            slot = (base + lax.broadcasted_iota(jnp.int32, (t, n_rows), 1)).astype(F32)
            onehot = jnp.where(rank_col == slot, 1.0, 0.0).astype(BF16)
            o_ref[...] += gate_col * jnp.dot(onehot, yg_ref[rows, :].astype(BF16),
                                             preferred_element_type=F32)

    def full_block(b, carry):
        block(pl.multiple_of(b * MOE_ROWS, MOE_ROWS), MOE_ROWS)
        return carry

    lax.fori_loop(0, n_full, full_block, 0)

    @pl.when(jnp.bitwise_and(code, 1) == 1)
    def _():
        block(pl.multiple_of(n_full * MOE_ROWS, MOE_ROWS), MOE_TAIL_ROWS)


def _moe_routed(h2, gates, wg, wu, wd, tm, tf):
    n, d = h2.shape
    n_exp, _, dff = wg.shape
    tm = min(tm, n)
    rank, rank_t, cnt = _route(gates, n_exp, tm)
    counts = cnt[:, 0, :n_exp].reshape(-1)
    rem = counts % MOE_ROWS
    tail = jnp.logical_and(rem > 0, rem <= MOE_TAIL_ROWS)
    n_blocks = 2 * (counts // MOE_ROWS + (rem > MOE_TAIL_ROWS)) + tail
    cap = tm
    grid_spec = pltpu.PrefetchScalarGridSpec(
        num_scalar_prefetch=1,
        grid=(n // tm, n_exp, dff // tf),
        in_specs=[pl.BlockSpec((tm, d), lambda i, e, f, c: (i, 0)),
                  pl.BlockSpec((tm, LANES), lambda i, e, f, c: (i, 0)),
                  pl.BlockSpec((None, n_exp, tm), lambda i, e, f, c: (i, 0, 0)),
                  pl.BlockSpec((tm, LANES), lambda i, e, f, c: (i, 0)),
                  pl.BlockSpec((None, d, tf), lambda i, e, f, c: (e, 0, f)),
                  pl.BlockSpec((None, d, tf), lambda i, e, f, c: (e, 0, f)),
                  pl.BlockSpec((None, tf, d), lambda i, e, f, c: (e, f, 0))],
        out_specs=pl.BlockSpec((tm, d), lambda i, e, f, c: (i, 0)),
        scratch_shapes=[pltpu.VMEM((cap, d), BF16), pltpu.VMEM((cap, d), F32)])
    return pl.pallas_call(
        functools.partial(_moe_kernel, n_exp=n_exp),
        out_shape=jax.ShapeDtypeStruct((n, d), F32),
        grid_spec=grid_spec,
        compiler_params=_params(("arbitrary",) * 3),
        name="moe_swiglu",
    )(n_blocks, h2, rank, rank_t, gates, wg, wu, wd)


def _residual_kernel(x_ref, y_ref, gt_ref, gfin_ref, o_ref, *, final):
    xo = x_ref[...] + gt_ref[...] * y_ref[...]
    if final:
        xo = _rms(xo, gfin_ref[...])
    o_ref[...] = xo


def _residual(x, y, modrows, g_final, final, tm):
    b, s, d = x.shape
    tm = min(tm, s)
    tile = pl.BlockSpec((None, tm, d), lambda bi, i: (bi, i, 0))
    return pl.pallas_call(
        functools.partial(_residual_kernel, final=final),
        out_shape=jax.ShapeDtypeStruct((b, s, d), F32),
        grid=(b, s // tm),
        in_specs=[tile, tile, pl.BlockSpec((None, 1, d), lambda bi, i: (bi, 0, 5)), _const_spec((1, d))],
        out_specs=tile,
        compiler_params=_params(("arbitrary", "arbitrary")),
        name="moe_residual",
    )(x, y, modrows, g_final)


def _moe(h2, x, modrows, gates, wg, wu, wd, g_final, final):
    b, s, d = x.shape
    y = _moe_routed(h2.reshape(b * s, d), gates.reshape(b * s, LANES), wg, wu, wd, MOE_TOKENS, MOE_FF_CHUNK)
    return _residual(x, y.reshape(b, s, d), modrows, g_final, final, 512)


def _rope_tables(seq):
    axis_dim = DA_QK // 2
    n_freq = axis_dim // 2
    inv = jnp.exp(-math.log(ROPE_BASE) * jnp.arange(n_freq, dtype=F32) * (2.0 / axis_dim))
    t = jnp.arange(seq, dtype=jnp.int32)
    ar = (t // GRID_W).astype(F32)[:, None] * inv
    ac = (t % GRID_W).astype(F32)[:, None] * inv
    cos = jnp.concatenate([jnp.cos(ar), jnp.cos(ar), jnp.cos(ac), jnp.cos(ac)], axis=-1)
    sin = jnp.concatenate([jnp.sin(ar), jnp.sin(ar), jnp.sin(ac), jnp.sin(ac)], axis=-1)
    first = (jnp.arange(DA_QK) % axis_dim) < n_freq
    sa = jnp.where(first, -sin, 0.0)
    sb = jnp.where(first, 0.0, sin)
    rep = lambda a: jnp.tile(a, (1, LANES // DA_QK))
    return rep(cos), rep(sa), rep(sb)


def _identity_tables(seq):
    return jnp.ones((seq, LANES), F32), jnp.zeros((seq, LANES), F32), jnp.zeros((seq, LANES), F32)


def _pack_layer_weights(w_in, w_uq, w_ukv):
    d = w_in.shape[0]
    zpad = jnp.zeros((d, LANES - MLA_ROPE), w_in.dtype)
    o_kr = 3 * 512 + MLA_Q_RANK + MLA_KV_RANK
    o_pool = o_kr + MLA_ROPE
    w2 = jnp.concatenate([w_in[:, :o_kr + MLA_ROPE], zpad, w_in[:, o_pool:o_pool + 512]], axis=1).astype(BF16)
    wgate = w_in[:, o_pool + 512:].astype(BF16)
    hq = MLA_NOPE + MLA_ROPE
    wq = w_uq.reshape(MLA_Q_RANK, MLA_HEADS, hq)
    wq = jnp.concatenate([wq, jnp.zeros((MLA_Q_RANK, MLA_HEADS, MLA_QK_PAD - hq), w_uq.dtype)], axis=-1)
    wuq = wq.reshape(MLA_Q_RANK, MLA_HEADS * MLA_QK_PAD).astype(BF16)
    wkv = w_ukv.reshape(MLA_KV_RANK, MLA_HEADS, MLA_NOPE + MLA_V)
    wukv = jnp.concatenate([wkv[:, :, :MLA_NOPE].reshape(MLA_KV_RANK, -1),
                            wkv[:, :, MLA_NOPE:].reshape(MLA_KV_RANK, -1)], axis=1).astype(BF16)
    return w2, wgate, wuq, wukv


def kernel(x, c, ctx, c_ctx, w_mod, b_mod, g_mix, w_in, da_lambda, da_subln, mla_gq, w_uq, mla_gkv, w_ukv,
           pool_w, pool_scale, w_branch, w_out, g_ffn, ffn_w_gate, ffn_w_up, ffn_w_down, moe_router,
           moe_w_gate, moe_w_up, moe_w_down, g_final):
    bsz, seq, d = x.shape
    n_ctx = ctx.shape[1]
    depth = w_mod.shape[0]

    cond8 = jnp.zeros((8, d), F32).at[:bsz].set(c).at[bsz].set(c_ctx)
    mod = _mod_rows(cond8, w_mod, b_mod)
    lat_tabs = _rope_tables(seq)
    ctx_tabs = _identity_tables(n_ctx)
    g_fin = g_final.reshape(1, d)

    xc = ctx
    for l in range(depth):
        need_ctx = l < depth - 1
        lam_init = 0.8 - 0.6 * math.exp(-0.3 * l)
        mod_lat = mod[l, :bsz][:, None, :]
        mod_ctx = jnp.broadcast_to(mod[l, bsz][None, None, :], (bsz, 1, N_MOD * d))
        w2, wgate, wuq, wukv = _pack_layer_weights(w_in[l], w_uq[l], w_ukv[l])
        gmix = g_mix[l].reshape(1, d)
        gffn = g_ffn[l].reshape(1, d)
        gq = mla_gq[l].reshape(1, -1)
        gkv = mla_gkv[l].reshape(1, -1)
        subln = da_subln[l].reshape(1, -1)
        lam = da_lambda[l]
        pw = pool_w[l].astype(BF16)
        psc = pool_scale[l].reshape(1, -1)
        wbr = w_branch[l].astype(BF16)
        wout = w_out[l].astype(BF16)
        j = l // 2
        dense = l % 2 == 0
        final = l == depth - 1

        qda, kda, vda, qm, km, vm, pin = _inproj(x, mod_lat, gmix, w2, gq, wuq, gkv, wukv, lat_tabs, PROJ_ROWS)
        cqda, ckda, cvda, cqm, ckm, cvm, cpin = _inproj(xc, mod_ctx, gmix, w2, gq, wuq, gkv, wukv, ctx_tabs, PROJ_ROWS)

        o_da = _attention(qda, ckda, cvda, kda, vda, DA_HEADS, LANES, DA_V, ATTN_ROWS // 2, ATTN_KEYS, diff=True,
                          lam=lam, g=subln, lam_init=lam_init, name="diff_attn")
        o_mla = _attention(qm, ckm, cvm, km, vm, MLA_HEADS, MLA_QK_PAD, MLA_V, ATTN_ROWS, ATTN_KEYS, name="mla_attn")
        o_pool = _pool(pin, pw, psc, 512)

        router = None
        n_exp = moe_router.shape[-1]
        if not dense:
            router = moe_router[j].T
        outs = _merge(x, mod_lat, gmix, wgate, o_da, o_mla, o_pool, wbr, wout, gffn, router, n_exp, PROJ_ROWS)

        if dense:
            wg = ffn_w_gate[j].astype(BF16)
            wu = ffn_w_up[j].astype(BF16)
            wd = ffn_w_down[j].astype(BF16)
            x = _ffn(outs[1], outs[0], mod_lat, wg, wu, wd, g_fin, final, 256)
        else:
            wg = moe_w_gate[j].astype(BF16)
            wu = moe_w_up[j].astype(BF16)
            wd = moe_w_down[j].astype(BF16)
            x = _moe(outs[1], outs[0], mod_lat, outs[2], wg, wu, wd, g_fin, final)

        if need_ctx:
            co_da = _attention(cqda, ckda, cvda, None, None, DA_HEADS, LANES, DA_V, ATTN_ROWS // 2, ATTN_KEYS, diff=True,
                               lam=lam, g=subln, lam_init=lam_init, name="diff_attn_ctx")
            co_mla = _attention(cqm, ckm, cvm, None, None, MLA_HEADS, MLA_QK_PAD, MLA_V, ATTN_ROWS, ATTN_KEYS,
                                name="mla_attn_ctx")
            co_pool = _pool(cpin, pw, psc, 512)
            couts = _merge(xc, mod_ctx, gmix, wgate, co_da, co_mla, co_pool, wbr, wout, gffn, router, n_exp, PROJ_ROWS)
            if dense:
                xc = _ffn(couts[1], couts[0], mod_ctx, wg, wu, wd, g_fin, False, 256)
            else:
                xc = _moe(couts[1], couts[0], mod_ctx, couts[2], wg, wu, wd, g_fin, False)
    return x
```

```python
import functools
import math

import jax
import jax.numpy as jnp
from jax import lax
from jax.experimental import pallas as pl
from jax.experimental.pallas import tpu as pltpu

F32 = jnp.float32
BF16 = jnp.bfloat16

GRID_W = 64
DA_HEADS = 4
DA_QK = 64
DA_V = 128
MLA_HEADS = 4
MLA_NOPE = 128
MLA_ROPE = 64
MLA_V = 128
MLA_Q_RANK = 384
MLA_KV_RANK = 256
POOL_WINDOWS = (2, 4, 8, 16)
POOL_GROUP_W = 128
N_BRANCH = 3
BRANCH_W = 512
ROPE_BASE = 10000.0
EPS = 1e-6
N_MOD = 6
LANES = 128
MLA_QK_PAD = 256
LOG2E = math.log2(math.e)
DA_SCALE = DA_QK ** -0.5 * LOG2E
MLA_SCALE = (MLA_NOPE + MLA_ROPE) ** -0.5 * LOG2E
NEG_BIG = -1e30
PROJ_ROWS = 512
ATTN_ROWS = 1024
ATTN_KEYS = 512
ATTN_Q_TILES = 2
VMEM_LIMIT = 56 * 1024 * 1024

C_DAQ, C_DAK, C_DAV = 0, 512, 1024
C_QD = 1536
C_KVD = C_QD + MLA_Q_RANK
C_KR = C_KVD + MLA_KV_RANK
C_POOL = C_KR + LANES
W2_COLS = C_POOL + 512


def _sigmoid(v):
    return 1.0 / (1.0 + jnp.exp(-v))


def _params(sem, vmem=VMEM_LIMIT):
    return pltpu.CompilerParams(dimension_semantics=sem, vmem_limit_bytes=vmem)


def _const_spec(shape, single=False):
    nd = len(shape)
    if single:
        return pl.BlockSpec(shape, lambda *_: (0,) * nd, pipeline_mode=pl.Buffered(1))
    return pl.BlockSpec(shape, lambda *_: (0,) * nd)


def _mod_kernel(cond_ref, w_ref, b_ref, o_ref):
    c = cond_ref[...]
    s = c * _sigmoid(c)
    o_ref[...] = jnp.dot(s, w_ref[...], preferred_element_type=F32,
                         precision=lax.Precision.HIGHEST) + b_ref[...]


def _mod_rows(cond8, w_mod, b_mod):
    depth, d, n = w_mod.shape
    tn = 1536
    return pl.pallas_call(
        _mod_kernel,
        out_shape=jax.ShapeDtypeStruct((depth, 8, n), F32),
        grid=(depth, n // tn),
        in_specs=[pl.BlockSpec((8, d), lambda l, j: (0, 0)),
                  pl.BlockSpec((None, d, tn), lambda l, j: (l, 0, j)),
                  pl.BlockSpec((None, 1, tn), lambda l, j: (l, 0, j))],
        out_specs=pl.BlockSpec((None, 8, tn), lambda l, j: (l, 0, j)),
        compiler_params=_params(("arbitrary", "arbitrary")),
        name="adaln_rows",
    )(cond8, w_mod, b_mod.reshape(depth, 1, n))


def _norm_mod(x, g, sh, sc):
    r = lax.rsqrt(jnp.mean(x * x, axis=-1, keepdims=True) + EPS)
    return (x * r * g) * (1.0 + sc) + sh


def _rms(v, g):
    return v * lax.rsqrt(jnp.mean(v * v, axis=-1, keepdims=True) + EPS) * g


def _inproj_kernel(x_ref, g_ref, sh_ref, sc_ref, w_ref, gq_ref, wuq_ref, gkv_ref, wukv_ref,
                   cos_ref, sa_ref, sb_ref,
                   qda_ref, kda_ref, vda_ref, qm_ref, km_ref, vm_ref, pool_ref):
    h = _norm_mod(x_ref[...], g_ref[...], sh_ref[...], sc_ref[...])
    z = jnp.dot(h.astype(BF16), w_ref[...], preferred_element_type=F32)
    cos = cos_ref[...]
    sa = sa_ref[...]
    sb = sb_ref[...]

    def rope(blk):
        return blk * cos + pltpu.roll(blk, LANES - 16, 1) * sa + pltpu.roll(blk, 16, 1) * sb

    lo = lax.broadcasted_iota(jnp.int32, cos.shape, 1) < MLA_ROPE

    for hh in range(DA_HEADS):
        c0 = hh * LANES
        qda_ref[:, c0:c0 + LANES] = (rope(z[:, C_DAQ + c0:C_DAQ + c0 + LANES]) * DA_SCALE).astype(BF16)
        kda_ref[:, c0:c0 + LANES] = rope(z[:, C_DAK + c0:C_DAK + c0 + LANES]).astype(BF16)

    qn = _rms(z[:, C_QD:C_QD + MLA_Q_RANK], gq_ref[...])
    qf = jnp.dot(qn.astype(BF16), wuq_ref[...], preferred_element_type=F32)
    kvn = _rms(z[:, C_KVD:C_KVD + MLA_KV_RANK], gkv_ref[...])
    kvf = jnp.dot(kvn.astype(BF16), wukv_ref[...], preferred_element_type=F32)
    kr = jnp.where(lo, rope(z[:, C_KR:C_KR + LANES]), 0.0).astype(BF16)
    for hh in range(MLA_HEADS):
        c0 = hh * MLA_QK_PAD
        qm_ref[:, c0:c0 + LANES] = (qf[:, c0:c0 + LANES] * MLA_SCALE).astype(BF16)
        qr = jnp.where(lo, rope(qf[:, c0 + LANES:c0 + 2 * LANES]), 0.0)
        qm_ref[:, c0 + LANES:c0 + 2 * LANES] = (qr * MLA_SCALE).astype(BF16)
        km_ref[:, c0:c0 + LANES] = kvf[:, hh * LANES:(hh + 1) * LANES].astype(BF16)
        km_ref[:, c0 + LANES:c0 + 2 * LANES] = kr
    ones = jnp.ones(cos.shape, BF16)
    for hh in range(MLA_HEADS):
        c0 = 2 * hh * LANES
        vda_ref[:, c0:c0 + LANES] = z[:, C_DAV + hh * LANES:C_DAV + (hh + 1) * LANES].astype(BF16)
        vda_ref[:, c0 + LANES:c0 + 2 * LANES] = ones
        vm_ref[:, c0:c0 + LANES] = kvf[:, 512 + hh * LANES:512 + (hh + 1) * LANES].astype(BF16)
        vm_ref[:, c0 + LANES:c0 + 2 * LANES] = ones
    pool_ref[...] = z[:, C_POOL:C_POOL + 512]


def _inproj(x, modrows, g_mix, w2, gq, wuq, gkv, wukv, tabs, tm):
    b, s, d = x.shape
    tm = min(tm, s)
    cos, sa, sb = tabs
    row = lambda j: pl.BlockSpec((None, 1, d), lambda bi, i: (bi, 0, j))
    tab = pl.BlockSpec((tm, LANES), lambda bi, i: (i, 0))
    out = lambda w, dt: jax.ShapeDtypeStruct((b, s, w), dt)
    ospec = lambda w: pl.BlockSpec((None, tm, w), lambda bi, i: (bi, i, 0))
    return pl.pallas_call(
        _inproj_kernel,
        out_shape=(out(512, BF16), out(512, BF16), out(1024, BF16), out(1024, BF16), out(1024, BF16),
                   out(1024, BF16), out(512, F32)),
        grid=(b, s // tm),
        in_specs=[pl.BlockSpec((None, tm, d), lambda bi, i: (bi, i, 0)),
                  _const_spec((1, d)), row(0), row(1),
                  _const_spec(w2.shape), _const_spec(gq.shape), _const_spec(wuq.shape),
                  _const_spec(gkv.shape), _const_spec(wukv.shape), tab, tab, tab],
        out_specs=(ospec(512), ospec(512), ospec(1024), ospec(1024), ospec(1024), ospec(1024), ospec(512)),
        compiler_params=_params(("arbitrary", "arbitrary")),
        name="in_proj",
    )(x, g_mix, modrows, modrows, w2, gq, wuq, gkv, wukv, cos, sa, sb)


def _attn_kernel(*refs, diff, has_lat, tk, tq, lam_init):
    it = iter(refs)
    q_ref, kc_ref, vc_ref = next(it), next(it), next(it)
    kl_ref = vl_ref = lam_ref = g_ref = None
    if has_lat:
        kl_ref, vl_ref = next(it), next(it)
    if diff:
        lam_ref, g_ref = next(it), next(it)
    o_ref = next(it)
    m_ref, acc_ref = next(it), next(it)
    s_refs = (next(it), next(it))
    p_refs = (next(it), next(it))
    al_refs = (next(it), next(it))
    n_qt = q_ref.shape[0] // tq
    dv = o_ref.shape[-1]

    def stacked_q(qt):
        q = q_ref[qt * tq:(qt + 1) * tq, :]
        if not diff:
            return q
        lane = lax.broadcasted_iota(jnp.int32, q.shape, 1)
        zero = jnp.zeros_like(q)
        return jnp.concatenate([jnp.where(lane < DA_QK, q, zero), jnp.where(lane >= DA_QK, q, zero)], axis=0)

    qs = [stacked_q(qt) for qt in range(n_qt)]
    n_ctx = kc_ref.shape[0]
    n_lat = kl_ref.shape[0] // tk if has_lat else 0
    width = lambda c: n_ctx if c == 0 else tk
    k_of = lambda c: kc_ref[...] if c == 0 else kl_ref[(c - 1) * tk:c * tk, :]
    v_of = lambda c: vc_ref[...] if c == 0 else vl_ref[(c - 1) * tk:c * tk, :]

    def qk(slot, qt, c):
        k = k_of(c)
        s_refs[slot][:, :k.shape[0]] = lax.dot_general(qs[qt], k, (((1,), (1,)), ((), ())),
                                                       preferred_element_type=F32)

    def sm(slot, qt, c):
        s_ref, p_ref = s_refs[slot], p_refs[slot]
        blocks = [slice(j * LANES, (j + 1) * LANES) for j in range(width(c) // LANES)]
        mx = s_ref[:, blocks[0]]
        for blk in blocks[1:]:
            mx = jnp.maximum(mx, s_ref[:, blk])
        m = m_ref[qt]
        m_new = jnp.maximum(m, jnp.max(mx, axis=-1, keepdims=True))
        al_refs[slot][...] = jnp.exp2(m - m_new)
        m_ref[qt] = m_new
        for blk in blocks:
            p_ref[:, blk] = jnp.exp2(s_ref[:, blk] - m_new).astype(BF16)

    def pv(slot, qt, c):
        v = v_of(c)
        new = jnp.dot(p_refs[slot][:, :v.shape[0]], v, preferred_element_type=F32)
        al = al_refs[slot][...]
        for blk in (slice(0, dv), slice(dv, 2 * dv)):
            acc_ref[qt, :, blk] = al * acc_ref[qt, :, blk] + new[:, blk]

    def finish(qt):
        acc = acc_ref[qt]
        o = acc[:, :dv] / acc[:, dv:]
        if diff:
            lv = lam_ref[...]
            a = jnp.sum(lv[0:1, :] * lv[1:2, :], axis=-1, keepdims=True)
            b = jnp.sum(lv[2:3, :] * lv[3:4, :], axis=-1, keepdims=True)
            lam = jnp.exp(a) - jnp.exp(b) + lam_init
            o = o[:tq] - lam * o[tq:]
            o = _rms(o, g_ref[...]) * (1.0 - lam_init)
        o_ref[qt * tq:(qt + 1) * tq, :] = o.astype(o_ref.dtype)

    m_ref[...] = jnp.full(m_ref.shape, NEG_BIG, F32)
    acc_ref[...] = jnp.zeros(acc_ref.shape, F32)
    items = [(qt, 0) for qt in range(n_qt)] + [(qt, c) for qt in range(n_qt) for c in range(1, n_lat + 1)]
    for g in range(len(items) + 2):
        if g < len(items):
            qk(g % 2, *items[g])
        if 0 <= g - 2 < len(items):
            pv(g % 2, *items[g - 2])
        if 0 <= g - 1 < len(items):
            sm((g - 1) % 2, *items[g - 1])
        if 0 <= g - 2 < len(items) and items[g - 2][1] == n_lat:
            finish(items[g - 2][0])


def _attention(q, kc, vc, kl, vl, heads, dq, dv, tq, tk, diff=False, lam=None, g=None, lam_init=0.0,
               name="attn"):
    b, s, _ = q.shape
    c = kc.shape[1]
    tq = min(tq, s)
    n_qt = min(ATTN_Q_TILES, s // tq)
    has_lat = kl is not None
    args = [q, kc, vc]
    specs = [pl.BlockSpec((None, n_qt * tq, dq), lambda bi, hi, qi: (bi, qi, hi)),
             pl.BlockSpec((None, c, dq), lambda bi, hi, qi: (bi, 0, hi)),
             pl.BlockSpec((None, c, 2 * dv), lambda bi, hi, qi: (bi, 0, hi))]
    sl = 0
    if has_lat:
        sl = kl.shape[1]
        tk = min(tk, sl)
        args += [kl, vl]
        specs += [pl.BlockSpec((None, sl, dq), lambda bi, hi, qi: (bi, 0, hi)),
                  pl.BlockSpec((None, sl, 2 * dv), lambda bi, hi, qi: (bi, 0, hi))]
    if diff:
        args += [lam, g]
        specs += [_const_spec(lam.shape), _const_spec(g.shape)]
    kern = functools.partial(_attn_kernel, diff=diff, has_lat=has_lat, tk=tk, tq=tq, lam_init=lam_init)
    rows = 2 * tq if diff else tq
    wmax = max(tk, c) if has_lat else c
    scratch = [pltpu.VMEM((n_qt, rows, LANES), F32), pltpu.VMEM((n_qt, rows, 2 * dv), F32),
               pltpu.VMEM((rows, wmax), F32), pltpu.VMEM((rows, wmax), F32),
               pltpu.VMEM((rows, wmax), BF16), pltpu.VMEM((rows, wmax), BF16),
               pltpu.VMEM((rows, LANES), F32), pltpu.VMEM((rows, LANES), F32)]
    return pl.pallas_call(
        kern,
        out_shape=jax.ShapeDtypeStruct((b, s, heads * dv), BF16),
        grid=(b, heads, s // (n_qt * tq)),
        in_specs=specs,
        out_specs=pl.BlockSpec((None, n_qt * tq, dv), lambda bi, hi, qi: (bi, qi, hi)),
        scratch_shapes=scratch,
        compiler_params=_params(("arbitrary", "arbitrary", "arbitrary")),
        name=name,
    )(*args)


def _pool_kernel(prev_ref, cur_ref, next_ref, w_ref, sc_ref, o_ref, *, seq_len):
    i = pl.program_id(1)
    n = pl.num_programs(1)
    cur = cur_ref[...]
    tm = cur.shape[0]
    prev = jnp.where(i > 0, prev_ref[...], 0.0)
    nxt = jnp.where(i < n - 1, next_ref[...], 0.0)
    ext = jnp.concatenate([prev, cur, nxt], axis=0)
    ne = tm + 16
    t = i * tm + lax.broadcasted_iota(jnp.int32, (tm, 1), 0)
    for g, w in enumerate(POOL_WINDOWS):
        e = ext[:, g * LANES:(g + 1) * LANES]
        acc = e + pltpu.roll(e, 1, 0)
        half = 1
        while 2 * half < w:
            acc = pltpu.roll(acc, half, 0) + pltpu.roll(acc, ne - half, 0)
            half *= 2
        win = acc[8:8 + tm]
        lo = jnp.clip(t - w // 2, 0, seq_len)
        hi = jnp.clip(t - w // 2 + w, 0, seq_len)
        cnt = (hi - lo).astype(F32)
        dlt = win / cnt - cur[:, g * LANES:(g + 1) * LANES]
        y = jnp.dot(dlt.astype(BF16), w_ref[g], preferred_element_type=F32)
        o_ref[:, g * LANES:(g + 1) * LANES] = (y * sc_ref[:, g * LANES:(g + 1) * LANES]).astype(BF16)


def _pool(u, pool_w, pool_scale, tm):
    b, s, w = u.shape
    tm = min(tm, s)
    nb8 = s // 8
    r8 = tm // 8
    return pl.pallas_call(
        functools.partial(_pool_kernel, seq_len=s),
        out_shape=jax.ShapeDtypeStruct((b, s, w), BF16),
        grid=(b, s // tm),
        in_specs=[pl.BlockSpec((None, 8, w), lambda bi, i: (bi, jnp.maximum(i * r8 - 1, 0), 0)),
                  pl.BlockSpec((None, tm, w), lambda bi, i: (bi, i, 0)),
                  pl.BlockSpec((None, 8, w), lambda bi, i: (bi, jnp.minimum((i + 1) * r8, nb8 - 1), 0)),
                  _const_spec(pool_w.shape), _const_spec(pool_scale.shape)],
        out_specs=pl.BlockSpec((None, tm, w), lambda bi, i: (bi, i, 0)),
        compiler_params=_params(("arbitrary", "arbitrary")),
        name="pool_mixer",
    )(u, u, u, pool_w, pool_scale)


def _merge_kernel(*refs, moe, n_exp):
    (x_ref, gmix_ref, sh1_ref, sc1_ref, wgate_ref, oda_ref, omla_ref, opool_ref, wbr_ref, wout_ref,
     gt1_ref, gffn_ref, sh2_ref, sc2_ref) = refs[:14]
    if moe:
        router_ref, xo_ref, h2_ref, gates_ref = refs[14:]
    else:
        xo_ref, h2_ref = refs[14:]
    x = x_ref[...]
    d = x.shape[-1]
    h = _norm_mod(x, gmix_ref[...], sh1_ref[...], sc1_ref[...]).astype(BF16)
    merged = None
    for n, o_ref in enumerate((oda_ref, omla_ref, opool_ref)):
        gate = _sigmoid(jnp.dot(h, wgate_ref[:, n * d:(n + 1) * d], preferred_element_type=F32))
        proj = jnp.dot(o_ref[...], wbr_ref[n], preferred_element_type=F32)
        merged = gate * proj if merged is None else merged + gate * proj
    mix = jnp.dot(merged.astype(BF16), wout_ref[...], preferred_element_type=F32)
    xn = x + gt1_ref[...] * mix
    xo_ref[...] = xn
    h2 = _norm_mod(xn, gffn_ref[...], sh2_ref[...], sc2_ref[...])
    h2_ref[...] = h2.astype(BF16)
    if moe:
        logit = [jnp.sum(h2 * router_ref[e:e + 1, :], axis=-1, keepdims=True) for e in range(n_exp)]

        def top1(vals):
            best, idx = vals[0], jnp.zeros(vals[0].shape, jnp.int32)
            for e in range(1, n_exp):
                better = vals[e] > best
                best = jnp.where(better, vals[e], best)
                idx = jnp.where(better, e, idx)
            return best, idx

        v1, i1 = top1(logit)
        v2, i2 = top1([jnp.where(i1 == e, NEG_BIG, logit[e]) for e in range(n_exp)])
        w1 = 1.0 / (1.0 + jnp.exp(v2 - v1))
        lane = lax.broadcasted_iota(jnp.int32, gates_ref.shape, 1)
        gates_ref[...] = jnp.where(lane == i1, w1, 0.0) + jnp.where(lane == i2, 1.0 - w1, 0.0)


def _merge(x, modrows, g_mix, wgate, o_da, o_mla, o_pool, wbr, wout, g_ffn, router, n_exp, tm):
    b, s, d = x.shape
    tm = min(tm, s)
    moe = router is not None
    row = lambda j: pl.BlockSpec((None, 1, d), lambda bi, i: (bi, 0, j))
    tile = lambda w: pl.BlockSpec((None, tm, w), lambda bi, i: (bi, i, 0))
    args = [x, g_mix, modrows, modrows, wgate, o_da, o_mla, o_pool, wbr, wout, modrows, g_ffn, modrows, modrows]
    specs = [tile(d), _const_spec((1, d)), row(0), row(1), _const_spec(wgate.shape),
             tile(BRANCH_W), tile(BRANCH_W), tile(BRANCH_W), _const_spec(wbr.shape), _const_spec(wout.shape),
             row(2), _const_spec((1, d)), row(3), row(4)]
    out_shape = [jax.ShapeDtypeStruct((b, s, d), F32), jax.ShapeDtypeStruct((b, s, d), BF16)]
    out_specs = [tile(d), tile(d)]
    if moe:
        args.append(router)
        specs.append(_const_spec(router.shape))
        out_shape.append(jax.ShapeDtypeStruct((b, s, LANES), F32))
        out_specs.append(tile(LANES))
    return pl.pallas_call(
        functools.partial(_merge_kernel, moe=moe, n_exp=n_exp),
        out_shape=tuple(out_shape),
        grid=(b, s // tm),
        in_specs=specs,
        out_specs=tuple(out_specs),
        compiler_params=_params(("arbitrary", "arbitrary")),
        name="merge_out",
    )(*args)


def _ffn_kernel(h_ref, x_ref, gt_ref, wg_ref, wu_ref, wd_ref, gfin_ref, o_ref, *, final):
    h = h_ref[...]
    gte = jnp.dot(h, wg_ref[...], preferred_element_type=F32)
    up = jnp.dot(h, wu_ref[...], preferred_element_type=F32)
    act = (gte * _sigmoid(gte) * up).astype(BF16)
    y = jnp.dot(act, wd_ref[...], preferred_element_type=F32)
    xo = x_ref[...] + gt_ref[...] * y
    if final:
        xo = _rms(xo, gfin_ref[...])
    o_ref[...] = xo


def _ffn(h2, x, modrows, wg, wu, wd, g_final, final, tm):
    b, s, d = x.shape
    tm = min(tm, s)
    tile = pl.BlockSpec((None, tm, d), lambda bi, i: (bi, i, 0))
    return pl.pallas_call(
        functools.partial(_ffn_kernel, final=final),
        out_shape=jax.ShapeDtypeStruct((b, s, d), F32),
        grid=(b, s // tm),
        in_specs=[tile, tile, pl.BlockSpec((None, 1, d), lambda bi, i: (bi, 0, 5)),
                  _const_spec(wg.shape, True), _const_spec(wu.shape, True), _const_spec(wd.shape, True),
                  _const_spec((1, d))],
        out_specs=tile,
        compiler_params=_params(("arbitrary", "arbitrary")),
        name="ffn_swiglu",
    )(h2, x, modrows, wg, wu, wd, g_final)


ROUTE_SUB = 256
MOE_TOKENS = 1024
MOE_FF_CHUNK = 1792
MOE_ROWS = 256
MOE_TAIL_ROWS = 64


def _route_kernel(g_ref, rank_ref, rankt_ref, cnt_ref, *, n_exp):
    t = g_ref.shape[0]
    r_io = lax.broadcasted_iota(jnp.int32, (ROUTE_SUB, ROUTE_SUB), 0)
    c_io = lax.broadcasted_iota(jnp.int32, (ROUTE_SUB, ROUTE_SUB), 1)
    tri = jnp.where(c_io <= r_io, 1.0, 0.0).astype(BF16)
    carry = jnp.zeros((1, LANES), F32)
    for j in range(t // ROUTE_SUB):
        rows = slice(j * ROUTE_SUB, (j + 1) * ROUTE_SUB)
        routed = g_ref[rows, :] != 0.0
        incl = jnp.dot(tri, jnp.where(routed, 1.0, 0.0).astype(BF16), preferred_element_type=F32)
        rank_ref[rows, :] = jnp.where(routed, incl - 1.0 + carry, -1.0)
        carry = carry + incl[ROUTE_SUB - 1:ROUTE_SUB, :]
    rankt_ref[...] = rank_ref[...].T[0:n_exp, :]
    cnt_ref[...] = jnp.broadcast_to(carry, cnt_ref.shape).astype(jnp.int32)


def _route(gates, n_exp, tm):
    n = gates.shape[0]
    return pl.pallas_call(
        functools.partial(_route_kernel, n_exp=n_exp),
        out_shape=(jax.ShapeDtypeStruct((n, LANES), F32),
                   jax.ShapeDtypeStruct((n // tm, n_exp, tm), F32),
                   jax.ShapeDtypeStruct((n // tm, 8, LANES), jnp.int32)),
        grid=(n // tm,),
        in_specs=[pl.BlockSpec((tm, LANES), lambda i: (i, 0))],
        out_specs=(pl.BlockSpec((tm, LANES), lambda i: (i, 0)),
                   pl.BlockSpec((None, n_exp, tm), lambda i: (i, 0, 0)),
                   pl.BlockSpec((None, 8, LANES), lambda i: (i, 0, 0))),
        compiler_params=_params(("arbitrary",)),
        name="moe_route",
    )(gates)


def _moe_kernel(cnt_ref, h_ref, rank_ref, rankt_ref, gates_ref, wg_ref, wu_ref, wd_ref, o_ref,
                xg_ref, yg_ref, *, n_exp):
    i, e, f = pl.program_id(0), pl.program_id(1), pl.program_id(2)
    nf = pl.num_programs(2)
    t = h_ref.shape[0]
    code = cnt_ref[i * n_exp + e]
    n_full = lax.shift_right_logical(code, 1)

    @pl.when(jnp.logical_and(e == 0, f == 0))
    def _():
        o_ref[...] = jnp.zeros_like(o_ref)

    rank_row = rankt_ref[pl.ds(e, 1), :]

    def block(base, n_rows):
        rows = pl.ds(base, n_rows)

        @pl.when(f == 0)
        def _():
            slot = (base + lax.broadcasted_iota(jnp.int32, (n_rows, t), 0)).astype(F32)
            onehot = jnp.where(rank_row == slot, 1.0, 0.0).astype(BF16)
            xg_ref[rows, :] = jnp.dot(onehot, h_ref[...], preferred_element_type=F32).astype(BF16)

        xb = xg_ref[rows, :]
        gte = jnp.dot(xb, wg_ref[...], preferred_element_type=F32)
        up = jnp.dot(xb, wu_ref[...], preferred_element_type=F32)
        act = (gte * _sigmoid(gte) * up).astype(BF16)
        part = jnp.dot(act, wd_ref[...], preferred_element_type=F32)

        @pl.when(f == 0)
        def _():
            yg_ref[rows, :] = part

        @pl.when(f > 0)
        def _():
            yg_ref[rows, :] += part

        @pl.when(f == nf - 1)
        def _():
            lane = lax.broadcasted_iota(jnp.int32, (t, LANES), 1)
            rank_col = jnp.sum(jnp.where(lane == e, rank_ref[...], 0.0), axis=-1, keepdims=True)
            gate_col = jnp.sum(jnp.where(lane == e, gates_ref[...], 0.0), axis=-1, keepdims=True)
            slot = (base + lax.broadcasted_iota(jnp.int32, (t, n_rows), 1)).astype(F32)
            onehot = jnp.where(rank_col == slot, 1.0, 0.0).astype(BF16)
            o_ref[...] += gate_col * jnp.dot(onehot, yg_ref[rows, :].astype(BF16),
                                             preferred_element_type=F32)

    def full_block(b, carry):
        block(pl.multiple_of(b * MOE_ROWS, MOE_ROWS), MOE_ROWS)
        return carry

    lax.fori_loop(0, n_full, full_block, 0)

    @pl.when(jnp.bitwise_and(code, 1) == 1)
    def _():
        block(pl.multiple_of(n_full * MOE_ROWS, MOE_ROWS), MOE_TAIL_ROWS)


def _moe_routed(h2, gates, wg, wu, wd, tm, tf):
    n, d = h2.shape
    n_exp, _, dff = wg.shape
    tm = min(tm, n)
    rank, rank_t, cnt = _route(gates, n_exp, tm)
    counts = cnt[:, 0, :n_exp].reshape(-1)
    rem = counts % MOE_ROWS
    tail = jnp.logical_and(rem > 0, rem <= MOE_TAIL_ROWS)
    n_blocks = 2 * (counts // MOE_ROWS + (rem > MOE_TAIL_ROWS)) + tail
    cap = tm
    grid_spec = pltpu.PrefetchScalarGridSpec(
        num_scalar_prefetch=1,
        grid=(n // tm, n_exp, dff // tf),
        in_specs=[pl.BlockSpec((tm, d), lambda i, e, f, c: (i, 0)),
                  pl.BlockSpec((tm, LANES), lambda i, e, f, c: (i, 0)),
                  pl.BlockSpec((None, n_exp, tm), lambda i, e, f, c: (i, 0, 0)),
                  pl.BlockSpec((tm, LANES), lambda i, e, f, c: (i, 0)),
                  pl.BlockSpec((None, d, tf), lambda i, e, f, c: (e, 0, f)),
                  pl.BlockSpec((None, d, tf), lambda i, e, f, c: (e, 0, f)),
                  pl.BlockSpec((None, tf, d), lambda i, e, f, c: (e, f, 0))],
        out_specs=pl.BlockSpec((tm, d), lambda i, e, f, c: (i, 0)),
        scratch_shapes=[pltpu.VMEM((cap, d), BF16), pltpu.VMEM((cap, d), F32)])
    return pl.pallas_call(
        functools.partial(_moe_kernel, n_exp=n_exp),
        out_shape=jax.ShapeDtypeStruct((n, d), F32),
        grid_spec=grid_spec,
        compiler_params=_params(("arbitrary",) * 3),
        name="moe_swiglu",
    )(n_blocks, h2, rank, rank_t, gates, wg, wu, wd)


def _residual_kernel(x_ref, y_ref, gt_ref, gfin_ref, o_ref, *, final):
    xo = x_ref[...] + gt_ref[...] * y_ref[...]
    if final:
        xo = _rms(xo, gfin_ref[...])
    o_ref[...] = xo


def _residual(x, y, modrows, g_final, final, tm):
    b, s, d = x.shape
    tm = min(tm, s)
    tile = pl.BlockSpec((None, tm, d), lambda bi, i: (bi, i, 0))
    return pl.pallas_call(
        functools.partial(_residual_kernel, final=final),
        out_shape=jax.ShapeDtypeStruct((b, s, d), F32),
        grid=(b, s // tm),
        in_specs=[tile, tile, pl.BlockSpec((None, 1, d), lambda bi, i: (bi, 0, 5)), _const_spec((1, d))],
        out_specs=tile,
        compiler_params=_params(("arbitrary", "arbitrary")),
        name="moe_residual",
    )(x, y, modrows, g_final)


def _moe(h2, x, modrows, gates, wg, wu, wd, g_final, final):
    b, s, d = x.shape
    y = _moe_routed(h2.reshape(b * s, d), gates.reshape(b * s, LANES), wg, wu, wd, MOE_TOKENS, MOE_FF_CHUNK)
    return _residual(x, y.reshape(b, s, d), modrows, g_final, final, 512)


def _rope_tables(seq):
    axis_dim = DA_QK // 2
    n_freq = axis_dim // 2
    inv = jnp.exp(-math.log(ROPE_BASE) * jnp.arange(n_freq, dtype=F32) * (2.0 / axis_dim))
    t = jnp.arange(seq, dtype=jnp.int32)
    ar = (t // GRID_W).astype(F32)[:, None] * inv
    ac = (t % GRID_W).astype(F32)[:, None] * inv
    cos = jnp.concatenate([jnp.cos(ar), jnp.cos(ar), jnp.cos(ac), jnp.cos(ac)], axis=-1)
    sin = jnp.concatenate([jnp.sin(ar), jnp.sin(ar), jnp.sin(ac), jnp.sin(ac)], axis=-1)
    first = (jnp.arange(DA_QK) % axis_dim) < n_freq
    sa = jnp.where(first, -sin, 0.0)
    sb = jnp.where(first, 0.0, sin)
    rep = lambda a: jnp.tile(a, (1, LANES // DA_QK))
    return rep(cos), rep(sa), rep(sb)


def _identity_tables(seq):
    return jnp.ones((seq, LANES), F32), jnp.zeros((seq, LANES), F32), jnp.zeros((seq, LANES), F32)


def _pack_layer_weights(w_in, w_uq, w_ukv):
    d = w_in.shape[0]
    zpad = jnp.zeros((d, LANES - MLA_ROPE), w_in.dtype)
    o_kr = 3 * 512 + MLA_Q_RANK + MLA_KV_RANK
    o_pool = o_kr + MLA_ROPE
    w2 = jnp.concatenate([w_in[:, :o_kr + MLA_ROPE], zpad, w_in[:, o_pool:o_pool + 512]], axis=1).astype(BF16)
    wgate = w_in[:, o_pool + 512:].astype(BF16)
    hq = MLA_NOPE + MLA_ROPE
    wq = w_uq.reshape(MLA_Q_RANK, MLA_HEADS, hq)
    wq = jnp.concatenate([wq, jnp.zeros((MLA_Q_RANK, MLA_HEADS, MLA_QK_PAD - hq), w_uq.dtype)], axis=-1)
    wuq = wq.reshape(MLA_Q_RANK, MLA_HEADS * MLA_QK_PAD).astype(BF16)
    wkv = w_ukv.reshape(MLA_KV_RANK, MLA_HEADS, MLA_NOPE + MLA_V)
    wukv = jnp.concatenate([wkv[:, :, :MLA_NOPE].reshape(MLA_KV_RANK, -1),
                            wkv[:, :, MLA_NOPE:].reshape(MLA_KV_RANK, -1)], axis=1).astype(BF16)
    return w2, wgate, wuq, wukv


def kernel(x, c, ctx, c_ctx, w_mod, b_mod, g_mix, w_in, da_lambda, da_subln, mla_gq, w_uq, mla_gkv, w_ukv,
           pool_w, pool_scale, w_branch, w_out, g_ffn, ffn_w_gate, ffn_w_up, ffn_w_down, moe_router,
           moe_w_gate, moe_w_up, moe_w_down, g_final):
    bsz, seq, d = x.shape
    n_ctx = ctx.shape[1]
    depth = w_mod.shape[0]

    cond8 = jnp.zeros((8, d), F32).at[:bsz].set(c).at[bsz].set(c_ctx)
    mod = _mod_rows(cond8, w_mod, b_mod)
    lat_tabs = _rope_tables(seq)
    ctx_tabs = _identity_tables(n_ctx)
    g_fin = g_final.reshape(1, d)

    xc = ctx
    for l in range(depth):
        need_ctx = l < depth - 1
        lam_init = 0.8 - 0.6 * math.exp(-0.3 * l)
        mod_lat = mod[l, :bsz][:, None, :]
        mod_ctx = jnp.broadcast_to(mod[l, bsz][None, None, :], (bsz, 1, N_MOD * d))
        w2, wgate, wuq, wukv = _pack_layer_weights(w_in[l], w_uq[l], w_ukv[l])
        gmix = g_mix[l].reshape(1, d)
        gffn = g_ffn[l].reshape(1, d)
        gq = mla_gq[l].reshape(1, -1)
        gkv = mla_gkv[l].reshape(1, -1)
        subln = da_subln[l].reshape(1, -1)
        lam = da_lambda[l]
        pw = pool_w[l].astype(BF16)
        psc = pool_scale[l].reshape(1, -1)
        wbr = w_branch[l].astype(BF16)
        wout = w_out[l].astype(BF16)
        j = l // 2
        dense = l % 2 == 0
        final = l == depth - 1

        qda, kda, vda, qm, km, vm, pin = _inproj(x, mod_lat, gmix, w2, gq, wuq, gkv, wukv, lat_tabs, PROJ_ROWS)
        cqda, ckda, cvda, cqm, ckm, cvm, cpin = _inproj(xc, mod_ctx, gmix, w2, gq, wuq, gkv, wukv, ctx_tabs, PROJ_ROWS)

        o_da = _attention(qda, ckda, cvda, kda, vda, DA_HEADS, LANES, DA_V, ATTN_ROWS // 2, ATTN_KEYS, diff=True,
                          lam=lam, g=subln, lam_init=lam_init, name="diff_attn")
        o_mla = _attention(qm, ckm, cvm, km, vm, MLA_HEADS, MLA_QK_PAD, MLA_V, ATTN_ROWS, ATTN_KEYS, name="mla_attn")
        o_pool = _pool(pin, pw, psc, 512)

        router = None
        n_exp = moe_router.shape[-1]
        if not dense:
            router = moe_router[j].T
        outs = _merge(x, mod_lat, gmix, wgate, o_da, o_mla, o_pool, wbr, wout, gffn, router, n_exp, PROJ_ROWS)

        if dense:
            wg = ffn_w_gate[j].astype(BF16)
            wu = ffn_w_up[j].astype(BF16)
            wd = ffn_w_down[j].astype(BF16)
            x = _ffn(outs[1], outs[0], mod_lat, wg, wu, wd, g_fin, final, PROJ_ROWS)
        else:
            wg = moe_w_gate[j].astype(BF16)
            wu = moe_w_up[j].astype(BF16)
            wd = moe_w_down[j].astype(BF16)
            x = _moe(outs[1], outs[0], mod_lat, outs[2], wg, wu, wd, g_fin, final)

        if need_ctx:
            co_da = _attention(cqda, ckda, cvda, None, None, DA_HEADS, LANES, DA_V, ATTN_ROWS // 2, ATTN_KEYS, diff=True,
                               lam=lam, g=subln, lam_init=lam_init, name="diff_attn_ctx")
            co_mla = _attention(cqm, ckm, cvm, None, None, MLA_HEADS, MLA_QK_PAD, MLA_V, ATTN_ROWS, ATTN_KEYS,
                                name="mla_attn_ctx")
            co_pool = _pool(cpin, pw, psc, 512)
            couts = _merge(xc, mod_ctx, gmix, wgate, co_da, co_mla, co_pool, wbr, wout, gffn, router, n_exp, PROJ_ROWS)
            if dense:
                xc = _ffn(couts[1], couts[0], mod_ctx, wg, wu, wd, g_fin, False, PROJ_ROWS)
            else:
                xc = _moe(couts[1], couts[0], mod_ctx, couts[2], wg, wu, wd, g_fin, False)
    return x
```

```python
import functools
import math

import jax
import jax.numpy as jnp
from jax import lax
from jax.experimental import pallas as pl
from jax.experimental.pallas import tpu as pltpu

F32 = jnp.float32
BF16 = jnp.bfloat16

GRID_W = 64
DA_HEADS = 4
DA_QK = 64
DA_V = 128
MLA_HEADS = 4
MLA_NOPE = 128
MLA_ROPE = 64
MLA_V = 128
MLA_Q_RANK = 384
MLA_KV_RANK = 256
POOL_WINDOWS = (2, 4, 8, 16)
BRANCH_W = 512
ROPE_BASE = 10000.0
EPS = 1e-6
N_MOD = 6
LANES = 128
MLA_QK_PAD = 256
LOG2E = math.log2(math.e)
DA_SCALE = DA_QK ** -0.5 * LOG2E
MLA_SCALE = (MLA_NOPE + MLA_ROPE) ** -0.5 * LOG2E
NEG_BIG = -1e30
PROJ_ROWS = 512
ATTN_ROWS = 1024
ATTN_KEYS = 512
ATTN_Q_TILES = 2
VMEM_LIMIT = 56 * 1024 * 1024

C_DAQ, C_DAK, C_DAV = 0, 512, 1024
C_QD = 1536
C_KVD = C_QD + MLA_Q_RANK
C_KR = C_KVD + MLA_KV_RANK
C_POOL = C_KR + LANES


def _sigmoid(v):
    return 1.0 / (1.0 + jnp.exp(-v))


def _params(sem, vmem=VMEM_LIMIT):
    return pltpu.CompilerParams(dimension_semantics=sem, vmem_limit_bytes=vmem)


def _const_spec(shape, single=False):
    nd = len(shape)
    if single:
        return pl.BlockSpec(shape, lambda *_: (0,) * nd, pipeline_mode=pl.Buffered(1))
    return pl.BlockSpec(shape, lambda *_: (0,) * nd)


def _mod_kernel(cond_ref, w_ref, b_ref, o_ref):
    c = cond_ref[...]
    s = c * _sigmoid(c)
    o_ref[...] = jnp.dot(s, w_ref[...], preferred_element_type=F32,
                         precision=lax.Precision.HIGHEST) + b_ref[...]


def _mod_rows(cond8, w_mod, b_mod):
    depth, d, n = w_mod.shape
    tn = 1536
    return pl.pallas_call(
        _mod_kernel,
        out_shape=jax.ShapeDtypeStruct((depth, 8, n), F32),
        grid=(depth, n // tn),
        in_specs=[pl.BlockSpec((8, d), lambda l, j: (0, 0)),
                  pl.BlockSpec((None, d, tn), lambda l, j: (l, 0, j)),
                  pl.BlockSpec((None, 1, tn), lambda l, j: (l, 0, j))],
        out_specs=pl.BlockSpec((None, 8, tn), lambda l, j: (l, 0, j)),
        compiler_params=_params(("arbitrary", "arbitrary")),
        name="adaln_rows",
    )(cond8, w_mod, b_mod.reshape(depth, 1, n))


def _norm_mod(x, g, sh, sc):
    r = lax.rsqrt(jnp.mean(x * x, axis=-1, keepdims=True) + EPS)
    return (x * r * g) * (1.0 + sc) + sh


def _rms(v, g):
    return v * lax.rsqrt(jnp.mean(v * v, axis=-1, keepdims=True) + EPS) * g


def _inproj_kernel(x_ref, g_ref, sh_ref, sc_ref, w_ref, gq_ref, wuq_ref, gkv_ref, wukv_ref,
                   cos_ref, sa_ref, sb_ref,
                   qda_ref, kda_ref, vda_ref, qm_ref, km_ref, vm_ref, pool_ref):
    h = _norm_mod(x_ref[...], g_ref[...], sh_ref[...], sc_ref[...])
    z = jnp.dot(h.astype(BF16), w_ref[...], preferred_element_type=F32)
    cos = cos_ref[...]
    sa = sa_ref[...]
    sb = sb_ref[...]

    def rope(blk):
        return blk * cos + pltpu.roll(blk, LANES - 16, 1) * sa + pltpu.roll(blk, 16, 1) * sb

    lo = lax.broadcasted_iota(jnp.int32, cos.shape, 1) < MLA_ROPE

    for hh in range(DA_HEADS):
        c0 = hh * LANES
        qda_ref[:, c0:c0 + LANES] = (rope(z[:, C_DAQ + c0:C_DAQ + c0 + LANES]) * DA_SCALE).astype(BF16)
        kda_ref[:, c0:c0 + LANES] = rope(z[:, C_DAK + c0:C_DAK + c0 + LANES]).astype(BF16)

    qn = _rms(z[:, C_QD:C_QD + MLA_Q_RANK], gq_ref[...])
    qf = jnp.dot(qn.astype(BF16), wuq_ref[...], preferred_element_type=F32)
    kvn = _rms(z[:, C_KVD:C_KVD + MLA_KV_RANK], gkv_ref[...])
    kvf = jnp.dot(kvn.astype(BF16), wukv_ref[...], preferred_element_type=F32)
    kr = jnp.where(lo, rope(z[:, C_KR:C_KR + LANES]), 0.0).astype(BF16)
    for hh in range(MLA_HEADS):
        c0 = hh * MLA_QK_PAD
        qm_ref[:, c0:c0 + LANES] = (qf[:, c0:c0 + LANES] * MLA_SCALE).astype(BF16)
        qr = jnp.where(lo, rope(qf[:, c0 + LANES:c0 + 2 * LANES]), 0.0)
        qm_ref[:, c0 + LANES:c0 + 2 * LANES] = (qr * MLA_SCALE).astype(BF16)
        km_ref[:, c0:c0 + LANES] = kvf[:, hh * LANES:(hh + 1) * LANES].astype(BF16)
        km_ref[:, c0 + LANES:c0 + 2 * LANES] = kr
    ones = jnp.ones(cos.shape, BF16)
    for hh in range(MLA_HEADS):
        c0 = 2 * hh * LANES
        vda_ref[:, c0:c0 + LANES] = z[:, C_DAV + hh * LANES:C_DAV + (hh + 1) * LANES].astype(BF16)
        vda_ref[:, c0 + LANES:c0 + 2 * LANES] = ones
        vm_ref[:, c0:c0 + LANES] = kvf[:, 512 + hh * LANES:512 + (hh + 1) * LANES].astype(BF16)
        vm_ref[:, c0 + LANES:c0 + 2 * LANES] = ones
    pool_ref[...] = z[:, C_POOL:C_POOL + 512]


def _inproj(x, modrows, g_mix, w2, gq, wuq, gkv, wukv, tabs, tm):
    b, s, d = x.shape
    tm = min(tm, s)
    cos, sa, sb = tabs
    row = lambda j: pl.BlockSpec((None, 1, d), lambda bi, i: (bi, 0, j))
    tab = pl.BlockSpec((tm, LANES), lambda bi, i: (i, 0))
    out = lambda w, dt: jax.ShapeDtypeStruct((b, s, w), dt)
    ospec = lambda w: pl.BlockSpec((None, tm, w), lambda bi, i: (bi, i, 0))
    return pl.pallas_call(
        _inproj_kernel,
        out_shape=(out(512, BF16), out(512, BF16), out(1024, BF16), out(1024, BF16), out(1024, BF16),
                   out(1024, BF16), out(512, F32)),
        grid=(b, s // tm),
        in_specs=[pl.BlockSpec((None, tm, d), lambda bi, i: (bi, i, 0)),
                  _const_spec((1, d)), row(0), row(1),
                  _const_spec(w2.shape), _const_spec(gq.shape), _const_spec(wuq.shape),
                  _const_spec(gkv.shape), _const_spec(wukv.shape), tab, tab, tab],
        out_specs=(ospec(512), ospec(512), ospec(1024), ospec(1024), ospec(1024), ospec(1024), ospec(512)),
        compiler_params=_params(("arbitrary", "arbitrary")),
        name="in_proj",
    )(x, g_mix, modrows, modrows, w2, gq, wuq, gkv, wukv, cos, sa, sb)


def _attn_kernel(*refs, diff, has_lat, tk, tq, lam_init):
    it = iter(refs)
    q_ref, kc_ref, vc_ref = next(it), next(it), next(it)
    kl_ref = vl_ref = lam_ref = g_ref = None
    if has_lat:
        kl_ref, vl_ref = next(it), next(it)
    if diff:
        lam_ref, g_ref = next(it), next(it)
    o_ref = next(it)
    m_ref, acc_ref = next(it), next(it)
    s_refs = (next(it), next(it))
    p_refs = (next(it), next(it))
    al_refs = (next(it), next(it))
    n_qt = q_ref.shape[0] // tq
    dv = o_ref.shape[-1]

    def stacked_q(qt):
        q = q_ref[qt * tq:(qt + 1) * tq, :]
        if not diff:
            return q
        lane = lax.broadcasted_iota(jnp.int32, q.shape, 1)
        zero = jnp.zeros_like(q)
        return jnp.concatenate([jnp.where(lane < DA_QK, q, zero), jnp.where(lane >= DA_QK, q, zero)], axis=0)

    qs = [stacked_q(qt) for qt in range(n_qt)]
    n_ctx = kc_ref.shape[0]
    n_lat = kl_ref.shape[0] // tk if has_lat else 0
    width = lambda c: n_ctx if c == 0 else tk
    k_of = lambda c: kc_ref[...] if c == 0 else kl_ref[(c - 1) * tk:c * tk, :]
    v_of = lambda c: vc_ref[...] if c == 0 else vl_ref[(c - 1) * tk:c * tk, :]

    def qk(slot, qt, c):
        k = k_of(c)
        s_refs[slot][:, :k.shape[0]] = lax.dot_general(qs[qt], k, (((1,), (1,)), ((), ())),
                                                       preferred_element_type=F32)

    def sm(slot, qt, c):
        s_ref, p_ref = s_refs[slot], p_refs[slot]
        blocks = [slice(j * LANES, (j + 1) * LANES) for j in range(width(c) // LANES)]
        mx = s_ref[:, blocks[0]]
        for blk in blocks[1:]:
            mx = jnp.maximum(mx, s_ref[:, blk])
        m = m_ref[qt]
        m_new = jnp.maximum(m, jnp.max(mx, axis=-1, keepdims=True))
        al_refs[slot][...] = jnp.exp2(m - m_new)
        m_ref[qt] = m_new
        for blk in blocks:
            p_ref[:, blk] = jnp.exp2(s_ref[:, blk] - m_new).astype(BF16)

    def pv(slot, qt, c):
        v = v_of(c)
        new = jnp.dot(p_refs[slot][:, :v.shape[0]], v, preferred_element_type=F32)
        al = al_refs[slot][...]
        for blk in (slice(0, dv), slice(dv, 2 * dv)):
            acc_ref[qt, :, blk] = al * acc_ref[qt, :, blk] + new[:, blk]

    def finish(qt):
        acc = acc_ref[qt]
        o = acc[:, :dv] / acc[:, dv:]
        if diff:
            lv = lam_ref[...]
            a = jnp.sum(lv[0:1, :] * lv[1:2, :], axis=-1, keepdims=True)
            b = jnp.sum(lv[2:3, :] * lv[3:4, :], axis=-1, keepdims=True)
            lam = jnp.exp(a) - jnp.exp(b) + lam_init
            o = o[:tq] - lam * o[tq:]
            o = _rms(o, g_ref[...]) * (1.0 - lam_init)
        o_ref[qt * tq:(qt + 1) * tq, :] = o.astype(o_ref.dtype)

    m_ref[...] = jnp.full(m_ref.shape, NEG_BIG, F32)
    acc_ref[...] = jnp.zeros(acc_ref.shape, F32)
    items = [(qt, 0) for qt in range(n_qt)] + [(qt, c) for qt in range(n_qt) for c in range(1, n_lat + 1)]
    for g in range(len(items) + 2):
        if g < len(items):
            qk(g % 2, *items[g])
        if 0 <= g - 2 < len(items):
            pv(g % 2, *items[g - 2])
        if 0 <= g - 1 < len(items):
            sm((g - 1) % 2, *items[g - 1])
        if 0 <= g - 2 < len(items) and items[g - 2][1] == n_lat:
            finish(items[g - 2][0])


def _attention(q, kc, vc, kl, vl, heads, dq, dv, tq, tk, diff=False, lam=None, g=None, lam_init=0.0,
               name="attn"):
    b, s, _ = q.shape
    c = kc.shape[1]
    tq = min(tq, s)
    n_qt = min(ATTN_Q_TILES, s // tq)
    has_lat = kl is not None
    args = [q, kc, vc]
    specs = [pl.BlockSpec((None, n_qt * tq, dq), lambda bi, hi, qi: (bi, qi, hi)),
             pl.BlockSpec((None, c, dq), lambda bi, hi, qi: (bi, 0, hi)),
             pl.BlockSpec((None, c, 2 * dv), lambda bi, hi, qi: (bi, 0, hi))]
    sl = 0
    if has_lat:
        sl = kl.shape[1]
        tk = min(tk, sl)
        args += [kl, vl]
        specs += [pl.BlockSpec((None, sl, dq), lambda bi, hi, qi: (bi, 0, hi)),
                  pl.BlockSpec((None, sl, 2 * dv), lambda bi, hi, qi: (bi, 0, hi))]
    if diff:
        args += [lam, g]
        specs += [_const_spec(lam.shape), _const_spec(g.shape)]
    kern = functools.partial(_attn_kernel, diff=diff, has_lat=has_lat, tk=tk, tq=tq, lam_init=lam_init)
    rows = 2 * tq if diff else tq
    wmax = max(tk, c) if has_lat else c
    scratch = [pltpu.VMEM((n_qt, rows, LANES), F32), pltpu.VMEM((n_qt, rows, 2 * dv), F32),
               pltpu.VMEM((rows, wmax), F32), pltpu.VMEM((rows, wmax), F32),
               pltpu.VMEM((rows, wmax), BF16), pltpu.VMEM((rows, wmax), BF16),
               pltpu.VMEM((rows, LANES), F32), pltpu.VMEM((rows, LANES), F32)]
    return pl.pallas_call(
        kern,
        out_shape=jax.ShapeDtypeStruct((b, s, heads * dv), BF16),
        grid=(b, heads, s // (n_qt * tq)),
        in_specs=specs,
        out_specs=pl.BlockSpec((None, n_qt * tq, dv), lambda bi, hi, qi: (bi, qi, hi)),
        scratch_shapes=scratch,
        compiler_params=_params(("arbitrary", "arbitrary", "arbitrary")),
        name=name,
    )(*args)


def _pool_kernel(prev_ref, cur_ref, next_ref, w_ref, sc_ref, o_ref, *, seq_len):
    i = pl.program_id(1)
    n = pl.num_programs(1)
    cur = cur_ref[...]
    tm = cur.shape[0]
    prev = jnp.where(i > 0, prev_ref[...], 0.0)
    nxt = jnp.where(i < n - 1, next_ref[...], 0.0)
    ext = jnp.concatenate([prev, cur, nxt], axis=0)
    ne = tm + 16
    t = i * tm + lax.broadcasted_iota(jnp.int32, (tm, 1), 0)
    for g, w in enumerate(POOL_WINDOWS):
        e = ext[:, g * LANES:(g + 1) * LANES]
        acc = e + pltpu.roll(e, 1, 0)
        half = 1
        while 2 * half < w:
            acc = pltpu.roll(acc, half, 0) + pltpu.roll(acc, ne - half, 0)
            half *= 2
        win = acc[8:8 + tm]
        lo = jnp.clip(t - w // 2, 0, seq_len)
        hi = jnp.clip(t - w // 2 + w, 0, seq_len)
        cnt = (hi - lo).astype(F32)
        dlt = win / cnt - cur[:, g * LANES:(g + 1) * LANES]
        y = jnp.dot(dlt.astype(BF16), w_ref[g], preferred_element_type=F32)
        o_ref[:, g * LANES:(g + 1) * LANES] = (y * sc_ref[:, g * LANES:(g + 1) * LANES]).astype(BF16)


def _pool(u, pool_w, pool_scale, tm):
    b, s, w = u.shape
    tm = min(tm, s)
    nb8 = s // 8
    r8 = tm // 8
    return pl.pallas_call(
        functools.partial(_pool_kernel, seq_len=s),
        out_shape=jax.ShapeDtypeStruct((b, s, w), BF16),
        grid=(b, s // tm),
        in_specs=[pl.BlockSpec((None, 8, w), lambda bi, i: (bi, jnp.maximum(i * r8 - 1, 0), 0)),
                  pl.BlockSpec((None, tm, w), lambda bi, i: (bi, i, 0)),
                  pl.BlockSpec((None, 8, w), lambda bi, i: (bi, jnp.minimum((i + 1) * r8, nb8 - 1), 0)),
                  _const_spec(pool_w.shape), _const_spec(pool_scale.shape)],
        out_specs=pl.BlockSpec((None, tm, w), lambda bi, i: (bi, i, 0)),
        compiler_params=_params(("arbitrary", "arbitrary")),
        name="pool_mixer",
    )(u, u, u, pool_w, pool_scale)


def _merge_kernel(*refs, moe, n_exp):
    (x_ref, gmix_ref, sh1_ref, sc1_ref, wgate_ref, oda_ref, omla_ref, opool_ref, wbr_ref, wout_ref,
     gt1_ref, gffn_ref, sh2_ref, sc2_ref) = refs[:14]
    if moe:
        router_ref, xo_ref, h2_ref, gates_ref = refs[14:]
    else:
        xo_ref, h2_ref = refs[14:]
    x = x_ref[...]
    d = x.shape[-1]
    h = _norm_mod(x, gmix_ref[...], sh1_ref[...], sc1_ref[...]).astype(BF16)
    merged = None
    for n, o_ref in enumerate((oda_ref, omla_ref, opool_ref)):
        gate = _sigmoid(jnp.dot(h, wgate_ref[:, n * d:(n + 1) * d], preferred_element_type=F32))
        proj = jnp.dot(o_ref[...], wbr_ref[n], preferred_element_type=F32)
        merged = gate * proj if merged is None else merged + gate * proj
    mix = jnp.dot(merged.astype(BF16), wout_ref[...], preferred_element_type=F32)
    xn = x + gt1_ref[...] * mix
    xo_ref[...] = xn
    h2 = _norm_mod(xn, gffn_ref[...], sh2_ref[...], sc2_ref[...])
    h2_ref[...] = h2.astype(BF16)
    if moe:
        logit = [jnp.sum(h2 * router_ref[e:e + 1, :], axis=-1, keepdims=True) for e in range(n_exp)]

        def top1(vals):
            best, idx = vals[0], jnp.zeros(vals[0].shape, jnp.int32)
            for e in range(1, n_exp):
                better = vals[e] > best
                best = jnp.where(better, vals[e], best)
                idx = jnp.where(better, e, idx)
            return best, idx

        v1, i1 = top1(logit)
        v2, i2 = top1([jnp.where(i1 == e, NEG_BIG, logit[e]) for e in range(n_exp)])
        w1 = 1.0 / (1.0 + jnp.exp(v2 - v1))
        lane = lax.broadcasted_iota(jnp.int32, gates_ref.shape, 1)
        gates_ref[...] = jnp.where(lane == i1, w1, 0.0) + jnp.where(lane == i2, 1.0 - w1, 0.0)


def _merge(x, modrows, g_mix, wgate, o_da, o_mla, o_pool, wbr, wout, g_ffn, router, n_exp, tm):
    b, s, d = x.shape
    tm = min(tm, s)
    moe = router is not None
    row = lambda j: pl.BlockSpec((None, 1, d), lambda bi, i: (bi, 0, j))
    tile = lambda w: pl.BlockSpec((None, tm, w), lambda bi, i: (bi, i, 0))
    args = [x, g_mix, modrows, modrows, wgate, o_da, o_mla, o_pool, wbr, wout, modrows, g_ffn, modrows, modrows]
    specs = [tile(d), _const_spec((1, d)), row(0), row(1), _const_spec(wgate.shape),
             tile(BRANCH_W), tile(BRANCH_W), tile(BRANCH_W), _const_spec(wbr.shape), _const_spec(wout.shape),
             row(2), _const_spec((1, d)), row(3), row(4)]
    out_shape = [jax.ShapeDtypeStruct((b, s, d), F32), jax.ShapeDtypeStruct((b, s, d), BF16)]
    out_specs = [tile(d), tile(d)]
    if moe:
        args.append(router)
        specs.append(_const_spec(router.shape))
        out_shape.append(jax.ShapeDtypeStruct((b, s, LANES), F32))
        out_specs.append(tile(LANES))
    return pl.pallas_call(
        functools.partial(_merge_kernel, moe=moe, n_exp=n_exp),
        out_shape=tuple(out_shape),
        grid=(b, s // tm),
        in_specs=specs,
        out_specs=tuple(out_specs),
        compiler_params=_params(("arbitrary", "arbitrary")),
        name="merge_out",
    )(*args)


def _ffn_kernel(h_ref, x_ref, gt_ref, wg_ref, wu_ref, wd_ref, gfin_ref, o_ref, *, final):
    h = h_ref[...]
    gte = jnp.dot(h, wg_ref[...], preferred_element_type=F32)
    up = jnp.dot(h, wu_ref[...], preferred_element_type=F32)
    act = (gte * _sigmoid(gte) * up).astype(BF16)
    y = jnp.dot(act, wd_ref[...], preferred_element_type=F32)
    xo = x_ref[...] + gt_ref[...] * y
    if final:
        xo = _rms(xo, gfin_ref[...])
    o_ref[...] = xo


def _ffn(h2, x, modrows, wg, wu, wd, g_final, final, tm):
    b, s, d = x.shape
    tm = min(tm, s)
    tile = pl.BlockSpec((None, tm, d), lambda bi, i: (bi, i, 0))
    return pl.pallas_call(
        functools.partial(_ffn_kernel, final=final),
        out_shape=jax.ShapeDtypeStruct((b, s, d), F32),
        grid=(b, s // tm),
        in_specs=[tile, tile, pl.BlockSpec((None, 1, d), lambda bi, i: (bi, 0, 5)),
                  _const_spec(wg.shape, True), _const_spec(wu.shape, True), _const_spec(wd.shape, True),
                  _const_spec((1, d))],
        out_specs=tile,
        compiler_params=_params(("arbitrary", "arbitrary")),
        name="ffn_swiglu",
    )(h2, x, modrows, wg, wu, wd, g_final)


ROUTE_SUB = 256
MOE_TOKENS = 1024
MOE_FF_CHUNK = 1792
MOE_ROWS = 256
MOE_TAIL_ROWS = 64


def _route_kernel(g_ref, rank_ref, rankt_ref, cnt_ref, *, n_exp):
    t = g_ref.shape[0]
    r_io = lax.broadcasted_iota(jnp.int32, (ROUTE_SUB, ROUTE_SUB), 0)
    c_io = lax.broadcasted_iota(jnp.int32, (ROUTE_SUB, ROUTE_SUB), 1)
    tri = jnp.where(c_io <= r_io, 1.0, 0.0).astype(BF16)
    carry = jnp.zeros((1, LANES), F32)
    for j in range(t // ROUTE_SUB):
        rows = slice(j * ROUTE_SUB, (j + 1) * ROUTE_SUB)
        routed = g_ref[rows, :] != 0.0
        incl = jnp.dot(tri, jnp.where(routed, 1.0, 0.0).astype(BF16), preferred_element_type=F32)
        rank_ref[rows, :] = jnp.where(routed, incl - 1.0 + carry, -1.0)
        carry = carry + incl[ROUTE_SUB - 1:ROUTE_SUB, :]
    rankt_ref[...] = rank_ref[...].T[0:n_exp, :]
    cnt_ref[...] = jnp.broadcast_to(carry, cnt_ref.shape).astype(jnp.int32)


def _route(gates, n_exp, tm):
    n = gates.shape[0]
    return pl.pallas_call(
        functools.partial(_route_kernel, n_exp=n_exp),
        out_shape=(jax.ShapeDtypeStruct((n, LANES), F32),
                   jax.ShapeDtypeStruct((n // tm, n_exp, tm), F32),
                   jax.ShapeDtypeStruct((n // tm, 8, LANES), jnp.int32)),
        grid=(n // tm,),
        in_specs=[pl.BlockSpec((tm, LANES), lambda i: (i, 0))],
        out_specs=(pl.BlockSpec((tm, LANES), lambda i: (i, 0)),
                   pl.BlockSpec((None, n_exp, tm), lambda i: (i, 0, 0)),
                   pl.BlockSpec((None, 8, LANES), lambda i: (i, 0, 0))),
        compiler_params=_params(("arbitrary",)),
        name="moe_route",
    )(gates)


def _moe_kernel(cnt_ref, h_ref, rank_ref, rankt_ref, gates_ref, wg_ref, wu_ref, wd_ref, x_ref, gt_ref,
                gfin_ref, o_ref, xg_ref, yg_ref, *, n_exp, final):
    i, e, f = pl.program_id(0), pl.program_id(1), pl.program_id(2)
    nf = pl.num_programs(2)
    t = h_ref.shape[0]
    code = cnt_ref[i * n_exp + e]
    n_full = lax.shift_right_logical(code, 1)

    @pl.when(jnp.logical_and(e == 0, f == 0))
    def _():
        o_ref[...] = jnp.zeros_like(o_ref)

    rank_row = rankt_ref[pl.ds(e, 1), :]

    def block(base, n_rows):
        rows = pl.ds(base, n_rows)

        @pl.when(f == 0)
        def _():
            slot = (base + lax.broadcasted_iota(jnp.int32, (n_rows, t), 0)).astype(F32)
            onehot = jnp.where(rank_row == slot, 1.0, 0.0).astype(BF16)
            xg_ref[rows, :] = jnp.dot(onehot, h_ref[...], preferred_element_type=F32).astype(BF16)

        xb = xg_ref[rows, :]
        gte = jnp.dot(xb, wg_ref[...], preferred_element_type=F32)
        up = jnp.dot(xb, wu_ref[...], preferred_element_type=F32)
        act = (gte * _sigmoid(gte) * up).astype(BF16)
        part = jnp.dot(act, wd_ref[...], preferred_element_type=F32)

        @pl.when(f == 0)
        def _():
            yg_ref[rows, :] = part

        @pl.when(f > 0)
        def _():
            yg_ref[rows, :] += part

        @pl.when(f == nf - 1)
        def _():
            lane = lax.broadcasted_iota(jnp.int32, (t, LANES), 1)
            rank_col = jnp.sum(jnp.where(lane == e, rank_ref[...], 0.0), axis=-1, keepdims=True)
            gate_col = jnp.sum(jnp.where(lane == e, gates_ref[...], 0.0), axis=-1, keepdims=True)
            slot = (base + lax.broadcasted_iota(jnp.int32, (t, n_rows), 1)).astype(F32)
            onehot = jnp.where(rank_col == slot, 1.0, 0.0).astype(BF16)
            o_ref[...] += gate_col * jnp.dot(onehot, yg_ref[rows, :].astype(BF16),
                                             preferred_element_type=F32)

    def full_block(b, carry):
        block(pl.multiple_of(b * MOE_ROWS, MOE_ROWS), MOE_ROWS)
        return carry

    lax.fori_loop(0, n_full, full_block, 0)

    @pl.when(jnp.bitwise_and(code, 1) == 1)
    def _():
        block(pl.multiple_of(n_full * MOE_ROWS, MOE_ROWS), MOE_TAIL_ROWS)

    @pl.when(jnp.logical_and(e == n_exp - 1, f == nf - 1))
    def _():
        xo = x_ref[...] + gt_ref[...] * o_ref[...]
        if final:
            xo = _rms(xo, gfin_ref[...])
        o_ref[...] = xo


def _moe(h2, x, modrows, gates, wg, wu, wd, g_final, final):
    bsz, seq, d = x.shape
    n = bsz * seq
    h2, x, gates = h2.reshape(n, d), x.reshape(n, d), gates.reshape(n, LANES)
    n_exp, _, dff = wg.shape
    tm, tf = min(MOE_TOKENS, seq), MOE_FF_CHUNK
    tiles_per_row = seq // tm
    rank, rank_t, cnt = _route(gates, n_exp, tm)
    counts = cnt[:, 0, :n_exp].reshape(-1)
    rem = counts % MOE_ROWS
    tail = jnp.logical_and(rem > 0, rem <= MOE_TAIL_ROWS)
    n_blocks = 2 * (counts // MOE_ROWS + (rem > MOE_TAIL_ROWS)) + tail
    grid_spec = pltpu.PrefetchScalarGridSpec(
        num_scalar_prefetch=1,
        grid=(n // tm, n_exp, dff // tf),
        in_specs=[pl.BlockSpec((tm, d), lambda i, e, f, c: (i, 0)),
                  pl.BlockSpec((tm, LANES), lambda i, e, f, c: (i, 0)),
                  pl.BlockSpec((None, n_exp, tm), lambda i, e, f, c: (i, 0, 0)),
                  pl.BlockSpec((tm, LANES), lambda i, e, f, c: (i, 0)),
                  pl.BlockSpec((None, d, tf), lambda i, e, f, c: (e, 0, f)),
                  pl.BlockSpec((None, d, tf), lambda i, e, f, c: (e, 0, f)),
                  pl.BlockSpec((None, tf, d), lambda i, e, f, c: (e, f, 0)),
                  pl.BlockSpec((tm, d), lambda i, e, f, c: (i, 0)),
                  pl.BlockSpec((None, 1, d), lambda i, e, f, c: (i // tiles_per_row, 0, N_MOD - 1)),
                  pl.BlockSpec((1, d), lambda i, e, f, c: (0, 0))],
        out_specs=pl.BlockSpec((tm, d), lambda i, e, f, c: (i, 0)),
        scratch_shapes=[pltpu.VMEM((tm, d), BF16), pltpu.VMEM((tm, d), F32)])
    out = pl.pallas_call(
        functools.partial(_moe_kernel, n_exp=n_exp, final=final),
        out_shape=jax.ShapeDtypeStruct((n, d), F32),
        grid_spec=grid_spec,
        compiler_params=_params(("arbitrary",) * 3),
        name="moe_swiglu",
    )(n_blocks, h2, rank, rank_t, gates, wg, wu, wd, x, modrows, g_final)
    return out.reshape(bsz, seq, d)


def _rope_tables(seq):
    axis_dim = DA_QK // 2
    n_freq = axis_dim // 2
    inv = jnp.exp(-math.log(ROPE_BASE) * jnp.arange(n_freq, dtype=F32) * (2.0 / axis_dim))
    t = jnp.arange(seq, dtype=jnp.int32)
    ar = (t // GRID_W).astype(F32)[:, None] * inv
    ac = (t % GRID_W).astype(F32)[:, None] * inv
    cos = jnp.concatenate([jnp.cos(ar), jnp.cos(ar), jnp.cos(ac), jnp.cos(ac)], axis=-1)
    sin = jnp.concatenate([jnp.sin(ar), jnp.sin(ar), jnp.sin(ac), jnp.sin(ac)], axis=-1)
    first = (jnp.arange(DA_QK) % axis_dim) < n_freq
    sa = jnp.where(first, -sin, 0.0)
    sb = jnp.where(first, 0.0, sin)
    rep = lambda a: jnp.tile(a, (1, LANES // DA_QK))
    return rep(cos), rep(sa), rep(sb)


def _identity_tables(seq):
    return jnp.ones((seq, LANES), F32), jnp.zeros((seq, LANES), F32), jnp.zeros((seq, LANES), F32)


def _pack_layer_weights(w_in, w_uq, w_ukv):
    d = w_in.shape[0]
    zpad = jnp.zeros((d, LANES - MLA_ROPE), w_in.dtype)
    o_kr = 3 * 512 + MLA_Q_RANK + MLA_KV_RANK
    o_pool = o_kr + MLA_ROPE
    w2 = jnp.concatenate([w_in[:, :o_kr + MLA_ROPE], zpad, w_in[:, o_pool:o_pool + 512]], axis=1).astype(BF16)
    wgate = w_in[:, o_pool + 512:].astype(BF16)
    hq = MLA_NOPE + MLA_ROPE
    wq = w_uq.reshape(MLA_Q_RANK, MLA_HEADS, hq)
    wq = jnp.concatenate([wq, jnp.zeros((MLA_Q_RANK, MLA_HEADS, MLA_QK_PAD - hq), w_uq.dtype)], axis=-1)
    wuq = wq.reshape(MLA_Q_RANK, MLA_HEADS * MLA_QK_PAD).astype(BF16)
    wkv = w_ukv.reshape(MLA_KV_RANK, MLA_HEADS, MLA_NOPE + MLA_V)
    wukv = jnp.concatenate([wkv[:, :, :MLA_NOPE].reshape(MLA_KV_RANK, -1),
                            wkv[:, :, MLA_NOPE:].reshape(MLA_KV_RANK, -1)], axis=1).astype(BF16)
    return w2, wgate, wuq, wukv


def kernel(x, c, ctx, c_ctx, w_mod, b_mod, g_mix, w_in, da_lambda, da_subln, mla_gq, w_uq, mla_gkv, w_ukv,
           pool_w, pool_scale, w_branch, w_out, g_ffn, ffn_w_gate, ffn_w_up, ffn_w_down, moe_router,
           moe_w_gate, moe_w_up, moe_w_down, g_final):
    bsz, seq, d = x.shape
    n_ctx = ctx.shape[1]
    depth = w_mod.shape[0]

    cond8 = jnp.zeros((8, d), F32).at[:bsz].set(c).at[bsz].set(c_ctx)
    mod = _mod_rows(cond8, w_mod, b_mod)
    lat_tabs = _rope_tables(seq)
    ctx_tabs = _identity_tables(n_ctx)
    g_fin = g_final.reshape(1, d)

    xc = ctx
    for l in range(depth):
        need_ctx = l < depth - 1
        lam_init = 0.8 - 0.6 * math.exp(-0.3 * l)
        mod_lat = mod[l, :bsz][:, None, :]
        mod_ctx = jnp.broadcast_to(mod[l, bsz][None, None, :], (bsz, 1, N_MOD * d))
        w2, wgate, wuq, wukv = _pack_layer_weights(w_in[l], w_uq[l], w_ukv[l])
        gmix = g_mix[l].reshape(1, d)
        gffn = g_ffn[l].reshape(1, d)
        gq = mla_gq[l].reshape(1, -1)
        gkv = mla_gkv[l].reshape(1, -1)
        subln = da_subln[l].reshape(1, -1)
        lam = da_lambda[l]
        pw = pool_w[l].astype(BF16)
        psc = pool_scale[l].reshape(1, -1)
        wbr = w_branch[l].astype(BF16)
        wout = w_out[l].astype(BF16)
        j = l // 2
        dense = l % 2 == 0
        final = l == depth - 1

        qda, kda, vda, qm, km, vm, pin = _inproj(x, mod_lat, gmix, w2, gq, wuq, gkv, wukv, lat_tabs, PROJ_ROWS)
        cqda, ckda, cvda, cqm, ckm, cvm, cpin = _inproj(xc, mod_ctx, gmix, w2, gq, wuq, gkv, wukv, ctx_tabs, PROJ_ROWS)

        o_da = _attention(qda, ckda, cvda, kda, vda, DA_HEADS, LANES, DA_V, ATTN_ROWS // 2, ATTN_KEYS, diff=True,
                          lam=lam, g=subln, lam_init=lam_init, name="diff_attn")
        o_mla = _attention(qm, ckm, cvm, km, vm, MLA_HEADS, MLA_QK_PAD, MLA_V, ATTN_ROWS, ATTN_KEYS, name="mla_attn")
        o_pool = _pool(pin, pw, psc, 512)

        router = None
        n_exp = moe_router.shape[-1]
        if not dense:
            router = moe_router[j].T
        outs = _merge(x, mod_lat, gmix, wgate, o_da, o_mla, o_pool, wbr, wout, gffn, router, n_exp, PROJ_ROWS)

        if dense:
            wg = ffn_w_gate[j].astype(BF16)
            wu = ffn_w_up[j].astype(BF16)
            wd = ffn_w_down[j].astype(BF16)
            x = _ffn(outs[1], outs[0], mod_lat, wg, wu, wd, g_fin, final, PROJ_ROWS)
        else:
            wg = moe_w_gate[j].astype(BF16)
            wu = moe_w_up[j].astype(BF16)
            wd = moe_w_down[j].astype(BF16)
            x = _moe(outs[1], outs[0], mod_lat, outs[2], wg, wu, wd, g_fin, final)

        if need_ctx:
            co_da = _attention(cqda, ckda, cvda, None, None, DA_HEADS, LANES, DA_V, ATTN_ROWS // 2, ATTN_KEYS, diff=True,
                               lam=lam, g=subln, lam_init=lam_init, name="diff_attn_ctx")
            co_mla = _attention(cqm, ckm, cvm, None, None, MLA_HEADS, MLA_QK_PAD, MLA_V, ATTN_ROWS, ATTN_KEYS,
                                name="mla_attn_ctx")
            co_pool = _pool(cpin, pw, psc, 512)
            couts = _merge(xc, mod_ctx, gmix, wgate, co_da, co_mla, co_pool, wbr, wout, gffn, router, n_exp, PROJ_ROWS)
            if dense:
                xc = _ffn(couts[1], couts[0], mod_ctx, wg, wu, wd, g_fin, False, PROJ_ROWS)
            else:
                xc = _moe(couts[1], couts[0], mod_ctx, couts[2], wg, wu, wd, g_fin, False)
    return x
```

```python
import functools
import math

import jax
import jax.numpy as jnp
from jax import lax
from jax.experimental import pallas as pl
from jax.experimental.pallas import tpu as pltpu

F32 = jnp.float32
BF16 = jnp.bfloat16

GRID_W = 64
DA_HEADS = 4
DA_QK = 64
DA_V = 128
MLA_HEADS = 4
MLA_NOPE = 128
MLA_ROPE = 64
MLA_V = 128
MLA_Q_RANK = 384
MLA_KV_RANK = 256
POOL_WINDOWS = (2, 4, 8, 16)
BRANCH_W = 512
ROPE_BASE = 10000.0
EPS = 1e-6
N_MOD = 6
LANES = 128
MLA_QK_PAD = 256
LOG2E = math.log2(math.e)
DA_SCALE = DA_QK ** -0.5 * LOG2E
MLA_SCALE = (MLA_NOPE + MLA_ROPE) ** -0.5 * LOG2E
NEG_BIG = -1e30
PROJ_ROWS = 512
ATTN_ROWS = 1024
ATTN_KEYS = 1024
ATTN_Q_TILES = 2
VMEM_LIMIT = 56 * 1024 * 1024

C_DAQ, C_DAK, C_DAV = 0, 512, 1024
C_QD = 1536
C_KVD = C_QD + MLA_Q_RANK
C_KR = C_KVD + MLA_KV_RANK
C_POOL = C_KR + LANES


def _sigmoid(v):
    return 1.0 / (1.0 + jnp.exp(-v))


def _params(sem, vmem=VMEM_LIMIT):
    return pltpu.CompilerParams(dimension_semantics=sem, vmem_limit_bytes=vmem)


def _const_spec(shape, single=False):
    nd = len(shape)
    if single:
        return pl.BlockSpec(shape, lambda *_: (0,) * nd, pipeline_mode=pl.Buffered(1))
    return pl.BlockSpec(shape, lambda *_: (0,) * nd)


def _mod_kernel(cond_ref, w_ref, b_ref, o_ref):
    c = cond_ref[...]
    s = c * _sigmoid(c)
    o_ref[...] = jnp.dot(s, w_ref[...], preferred_element_type=F32,
                         precision=lax.Precision.HIGHEST) + b_ref[...]


def _mod_rows(cond8, w_mod, b_mod):
    depth, d, n = w_mod.shape
    tn = 1536
    return pl.pallas_call(
        _mod_kernel,
        out_shape=jax.ShapeDtypeStruct((depth, 8, n), F32),
        grid=(depth, n // tn),
        in_specs=[pl.BlockSpec((8, d), lambda l, j: (0, 0)),
                  pl.BlockSpec((None, d, tn), lambda l, j: (l, 0, j)),
                  pl.BlockSpec((None, 1, tn), lambda l, j: (l, 0, j))],
        out_specs=pl.BlockSpec((None, 8, tn), lambda l, j: (l, 0, j)),
        compiler_params=_params(("arbitrary", "arbitrary")),
        name="adaln_rows",
    )(cond8, w_mod, b_mod.reshape(depth, 1, n))


def _norm_mod(x, g, sh, sc):
    r = lax.rsqrt(jnp.mean(x * x, axis=-1, keepdims=True) + EPS)
    return (x * r * g) * (1.0 + sc) + sh


def _rms(v, g):
    return v * lax.rsqrt(jnp.mean(v * v, axis=-1, keepdims=True) + EPS) * g


def _inproj_kernel(x_ref, g_ref, sh_ref, sc_ref, w_ref, gq_ref, wuq_ref, gkv_ref, wukv_ref,
                   cos_ref, sa_ref, sb_ref,
                   qda_ref, kda_ref, vda_ref, qm_ref, km_ref, vm_ref, pool_ref):
    h = _norm_mod(x_ref[...], g_ref[...], sh_ref[...], sc_ref[...])
    z = jnp.dot(h.astype(BF16), w_ref[...], preferred_element_type=F32)
    cos = cos_ref[...]
    sa = sa_ref[...]
    sb = sb_ref[...]

    def rope(blk):
        return blk * cos + pltpu.roll(blk, LANES - 16, 1) * sa + pltpu.roll(blk, 16, 1) * sb

    lo = lax.broadcasted_iota(jnp.int32, cos.shape, 1) < MLA_ROPE

    for hh in range(DA_HEADS):
        c0 = hh * LANES
        qda_ref[:, c0:c0 + LANES] = (rope(z[:, C_DAQ + c0:C_DAQ + c0 + LANES]) * DA_SCALE).astype(BF16)
        kda_ref[:, c0:c0 + LANES] = rope(z[:, C_DAK + c0:C_DAK + c0 + LANES]).astype(BF16)

    qn = _rms(z[:, C_QD:C_QD + MLA_Q_RANK], gq_ref[...])
    qf = jnp.dot(qn.astype(BF16), wuq_ref[...], preferred_element_type=F32)
    kvn = _rms(z[:, C_KVD:C_KVD + MLA_KV_RANK], gkv_ref[...])
    kvf = jnp.dot(kvn.astype(BF16), wukv_ref[...], preferred_element_type=F32)
    kr = jnp.where(lo, rope(z[:, C_KR:C_KR + LANES]), 0.0).astype(BF16)
    for hh in range(MLA_HEADS):
        c0 = hh * MLA_QK_PAD
        qm_ref[:, c0:c0 + LANES] = (qf[:, c0:c0 + LANES] * MLA_SCALE).astype(BF16)
        qr = jnp.where(lo, rope(qf[:, c0 + LANES:c0 + 2 * LANES]), 0.0)
        qm_ref[:, c0 + LANES:c0 + 2 * LANES] = (qr * MLA_SCALE).astype(BF16)
        km_ref[:, c0:c0 + LANES] = kvf[:, hh * LANES:(hh + 1) * LANES].astype(BF16)
        km_ref[:, c0 + LANES:c0 + 2 * LANES] = kr
    ones = jnp.ones(cos.shape, BF16)
    for hh in range(MLA_HEADS):
        c0 = 2 * hh * LANES
        vda_ref[:, c0:c0 + LANES] = z[:, C_DAV + hh * LANES:C_DAV + (hh + 1) * LANES].astype(BF16)
        vda_ref[:, c0 + LANES:c0 + 2 * LANES] = ones
        vm_ref[:, c0:c0 + LANES] = kvf[:, 512 + hh * LANES:512 + (hh + 1) * LANES].astype(BF16)
        vm_ref[:, c0 + LANES:c0 + 2 * LANES] = ones
    pool_ref[...] = z[:, C_POOL:C_POOL + 512]


def _inproj(x, modrows, g_mix, w2, gq, wuq, gkv, wukv, tabs, tm):
    b, s, d = x.shape
    tm = min(tm, s)
    cos, sa, sb = tabs
    row = lambda j: pl.BlockSpec((None, 1, d), lambda bi, i: (bi, 0, j))
    tab = pl.BlockSpec((tm, LANES), lambda bi, i: (i, 0))
    out = lambda w, dt: jax.ShapeDtypeStruct((b, s, w), dt)
    ospec = lambda w: pl.BlockSpec((None, tm, w), lambda bi, i: (bi, i, 0))
    return pl.pallas_call(
        _inproj_kernel,
        out_shape=(out(512, BF16), out(512, BF16), out(1024, BF16), out(1024, BF16), out(1024, BF16),
                   out(1024, BF16), out(512, F32)),
        grid=(b, s // tm),
        in_specs=[pl.BlockSpec((None, tm, d), lambda bi, i: (bi, i, 0)),
                  _const_spec((1, d)), row(0), row(1),
                  _const_spec(w2.shape), _const_spec(gq.shape), _const_spec(wuq.shape),
                  _const_spec(gkv.shape), _const_spec(wukv.shape), tab, tab, tab],
        out_specs=(ospec(512), ospec(512), ospec(1024), ospec(1024), ospec(1024), ospec(1024), ospec(512)),
        compiler_params=_params(("arbitrary", "arbitrary")),
        name="in_proj",
    )(x, g_mix, modrows, modrows, w2, gq, wuq, gkv, wukv, cos, sa, sb)


def _attn_kernel(*refs, diff, has_lat, tk, tq, lam_init):
    it = iter(refs)
    q_ref, kc_ref, vc_ref = next(it), next(it), next(it)
    kl_ref = vl_ref = lam_ref = g_ref = None
    if has_lat:
        kl_ref, vl_ref = next(it), next(it)
    if diff:
        lam_ref, g_ref = next(it), next(it)
    o_ref = next(it)
    m_ref, acc_ref = next(it), next(it)
    s_refs = (next(it), next(it))
    p_refs = (next(it), next(it))
    al_refs = (next(it), next(it))
    n_qt = q_ref.shape[0] // tq
    dv = o_ref.shape[-1]

    def stacked_q(qt):
        q = q_ref[qt * tq:(qt + 1) * tq, :]
        if not diff:
            return q
        lane = lax.broadcasted_iota(jnp.int32, q.shape, 1)
        zero = jnp.zeros_like(q)
        return jnp.concatenate([jnp.where(lane < DA_QK, q, zero), jnp.where(lane >= DA_QK, q, zero)], axis=0)

    qs = [stacked_q(qt) for qt in range(n_qt)]
    n_ctx = kc_ref.shape[0]
    n_lat = kl_ref.shape[0] // tk if has_lat else 0
    width = lambda c: n_ctx if c == 0 else tk
    k_of = lambda c: kc_ref[...] if c == 0 else kl_ref[(c - 1) * tk:c * tk, :]
    v_of = lambda c: vc_ref[...] if c == 0 else vl_ref[(c - 1) * tk:c * tk, :]

    def qk(slot, qt, c):
        k = k_of(c)
        s_refs[slot][:, :k.shape[0]] = lax.dot_general(qs[qt], k, (((1,), (1,)), ((), ())),
                                                       preferred_element_type=F32)

    def sm(slot, qt, c):
        s_ref, p_ref = s_refs[slot], p_refs[slot]
        blocks = [slice(j * LANES, (j + 1) * LANES) for j in range(width(c) // LANES)]
        mx = s_ref[:, blocks[0]]
        for blk in blocks[1:]:
            mx = jnp.maximum(mx, s_ref[:, blk])
        m = m_ref[qt]
        m_new = jnp.maximum(m, jnp.max(mx, axis=-1, keepdims=True))
        al_refs[slot][...] = jnp.exp2(m - m_new)
        m_ref[qt] = m_new
        for blk in blocks:
            p_ref[:, blk] = jnp.exp2(s_ref[:, blk] - m_new).astype(BF16)

    def pv(slot, qt, c):
        v = v_of(c)
        new = jnp.dot(p_refs[slot][:, :v.shape[0]], v, preferred_element_type=F32)
        al = al_refs[slot][...]
        for blk in (slice(0, dv), slice(dv, 2 * dv)):
            acc_ref[qt, :, blk] = al * acc_ref[qt, :, blk] + new[:, blk]

    def finish(qt):
        acc = acc_ref[qt]
        o = acc[:, :dv] / acc[:, dv:]
        if diff:
            lv = lam_ref[...]
            a = jnp.sum(lv[0:1, :] * lv[1:2, :], axis=-1, keepdims=True)
            b = jnp.sum(lv[2:3, :] * lv[3:4, :], axis=-1, keepdims=True)
            lam = jnp.exp(a) - jnp.exp(b) + lam_init
            o = o[:tq] - lam * o[tq:]
            o = _rms(o, g_ref[...]) * (1.0 - lam_init)
        o_ref[qt * tq:(qt + 1) * tq, :] = o.astype(o_ref.dtype)

    m_ref[...] = jnp.full(m_ref.shape, NEG_BIG, F32)
    acc_ref[...] = jnp.zeros(acc_ref.shape, F32)
    items = [(qt, 0) for qt in range(n_qt)] + [(qt, c) for qt in range(n_qt) for c in range(1, n_lat + 1)]
    for g in range(len(items) + 2):
        if g < len(items):
            qk(g % 2, *items[g])
        if 0 <= g - 2 < len(items):
            pv(g % 2, *items[g - 2])
        if 0 <= g - 1 < len(items):
            sm((g - 1) % 2, *items[g - 1])
        if 0 <= g - 2 < len(items) and items[g - 2][1] == n_lat:
            finish(items[g - 2][0])


def _attention(q, kc, vc, kl, vl, heads, dq, dv, tq, tk, diff=False, lam=None, g=None, lam_init=0.0,
               name="attn"):
    b, s, _ = q.shape
    c = kc.shape[1]
    tq = min(tq, s)
    n_qt = min(ATTN_Q_TILES, s // tq)
    has_lat = kl is not None
    args = [q, kc, vc]
    specs = [pl.BlockSpec((None, n_qt * tq, dq), lambda bi, hi, qi: (bi, qi, hi)),
             pl.BlockSpec((None, c, dq), lambda bi, hi, qi: (bi, 0, hi)),
             pl.BlockSpec((None, c, 2 * dv), lambda bi, hi, qi: (bi, 0, hi))]
    sl = 0
    if has_lat:
        sl = kl.shape[1]
        tk = min(tk, sl)
        args += [kl, vl]
        specs += [pl.BlockSpec((None, sl, dq), lambda bi, hi, qi: (bi, 0, hi)),
                  pl.BlockSpec((None, sl, 2 * dv), lambda bi, hi, qi: (bi, 0, hi))]
    if diff:
        args += [lam, g]
        specs += [_const_spec(lam.shape), _const_spec(g.shape)]
    kern = functools.partial(_attn_kernel, diff=diff, has_lat=has_lat, tk=tk, tq=tq, lam_init=lam_init)
    rows = 2 * tq if diff else tq
    wmax = max(tk, c) if has_lat else c
    scratch = [pltpu.VMEM((n_qt, rows, LANES), F32), pltpu.VMEM((n_qt, rows, 2 * dv), F32),
               pltpu.VMEM((rows, wmax), F32), pltpu.VMEM((rows, wmax), F32),
               pltpu.VMEM((rows, wmax), BF16), pltpu.VMEM((rows, wmax), BF16),
               pltpu.VMEM((rows, LANES), F32), pltpu.VMEM((rows, LANES), F32)]
    return pl.pallas_call(
        kern,
        out_shape=jax.ShapeDtypeStruct((b, s, heads * dv), BF16),
        grid=(b, heads, s // (n_qt * tq)),
        in_specs=specs,
        out_specs=pl.BlockSpec((None, n_qt * tq, dv), lambda bi, hi, qi: (bi, qi, hi)),
        scratch_shapes=scratch,
        compiler_params=_params(("arbitrary", "arbitrary", "arbitrary")),
        name=name,
    )(*args)


def _pool_kernel(prev_ref, cur_ref, next_ref, w_ref, sc_ref, o_ref, *, seq_len):
    i = pl.program_id(1)
    n = pl.num_programs(1)
    cur = cur_ref[...]
    tm = cur.shape[0]
    prev = jnp.where(i > 0, prev_ref[...], 0.0)
    nxt = jnp.where(i < n - 1, next_ref[...], 0.0)
    ext = jnp.concatenate([prev, cur, nxt], axis=0)
    ne = tm + 16
    t = i * tm + lax.broadcasted_iota(jnp.int32, (tm, 1), 0)
    for g, w in enumerate(POOL_WINDOWS):
        e = ext[:, g * LANES:(g + 1) * LANES]
        acc = e + pltpu.roll(e, 1, 0)
        half = 1
        while 2 * half < w:
            acc = pltpu.roll(acc, half, 0) + pltpu.roll(acc, ne - half, 0)
            half *= 2
        win = acc[8:8 + tm]
        lo = jnp.clip(t - w // 2, 0, seq_len)
        hi = jnp.clip(t - w // 2 + w, 0, seq_len)
        cnt = (hi - lo).astype(F32)
        dlt = win / cnt - cur[:, g * LANES:(g + 1) * LANES]
        y = jnp.dot(dlt.astype(BF16), w_ref[g], preferred_element_type=F32)
        o_ref[:, g * LANES:(g + 1) * LANES] = (y * sc_ref[:, g * LANES:(g + 1) * LANES]).astype(BF16)


def _pool(u, pool_w, pool_scale, tm):
    b, s, w = u.shape
    tm = min(tm, s)
    nb8 = s // 8
    r8 = tm // 8
    return pl.pallas_call(
        functools.partial(_pool_kernel, seq_len=s),
        out_shape=jax.ShapeDtypeStruct((b, s, w), BF16),
        grid=(b, s // tm),
        in_specs=[pl.BlockSpec((None, 8, w), lambda bi, i: (bi, jnp.maximum(i * r8 - 1, 0), 0)),
                  pl.BlockSpec((None, tm, w), lambda bi, i: (bi, i, 0)),
                  pl.BlockSpec((None, 8, w), lambda bi, i: (bi, jnp.minimum((i + 1) * r8, nb8 - 1), 0)),
                  _const_spec(pool_w.shape), _const_spec(pool_scale.shape)],
        out_specs=pl.BlockSpec((None, tm, w), lambda bi, i: (bi, i, 0)),
        compiler_params=_params(("arbitrary", "arbitrary")),
        name="pool_mixer",
    )(u, u, u, pool_w, pool_scale)


def _merge_kernel(*refs, moe, n_exp):
    (x_ref, gmix_ref, sh1_ref, sc1_ref, wgate_ref, oda_ref, omla_ref, opool_ref, wbr_ref, wout_ref,
     gt1_ref, gffn_ref, sh2_ref, sc2_ref) = refs[:14]
    if moe:
        router_ref, xo_ref, h2_ref, gates_ref = refs[14:]
    else:
        xo_ref, h2_ref = refs[14:]
    x = x_ref[...]
    d = x.shape[-1]
    h = _norm_mod(x, gmix_ref[...], sh1_ref[...], sc1_ref[...]).astype(BF16)
    merged = None
    for n, o_ref in enumerate((oda_ref, omla_ref, opool_ref)):
        gate = _sigmoid(jnp.dot(h, wgate_ref[:, n * d:(n + 1) * d], preferred_element_type=F32))
        proj = jnp.dot(o_ref[...], wbr_ref[n], preferred_element_type=F32)
        merged = gate * proj if merged is None else merged + gate * proj
    mix = jnp.dot(merged.astype(BF16), wout_ref[...], preferred_element_type=F32)
    xn = x + gt1_ref[...] * mix
    xo_ref[...] = xn
    h2 = _norm_mod(xn, gffn_ref[...], sh2_ref[...], sc2_ref[...])
    h2_ref[...] = h2.astype(BF16)
    if moe:
        logit = [jnp.sum(h2 * router_ref[e:e + 1, :], axis=-1, keepdims=True) for e in range(n_exp)]

        def top1(vals):
            best, idx = vals[0], jnp.zeros(vals[0].shape, jnp.int32)
            for e in range(1, n_exp):
                better = vals[e] > best
                best = jnp.where(better, vals[e], best)
                idx = jnp.where(better, e, idx)
            return best, idx

        v1, i1 = top1(logit)
        v2, i2 = top1([jnp.where(i1 == e, NEG_BIG, logit[e]) for e in range(n_exp)])
        w1 = 1.0 / (1.0 + jnp.exp(v2 - v1))
        lane = lax.broadcasted_iota(jnp.int32, gates_ref.shape, 1)
        gates_ref[...] = jnp.where(lane == i1, w1, 0.0) + jnp.where(lane == i2, 1.0 - w1, 0.0)


def _merge(x, modrows, g_mix, wgate, o_da, o_mla, o_pool, wbr, wout, g_ffn, router, n_exp, tm):
    b, s, d = x.shape
    tm = min(tm, s)
    moe = router is not None
    row = lambda j: pl.BlockSpec((None, 1, d), lambda bi, i: (bi, 0, j))
    tile = lambda w: pl.BlockSpec((None, tm, w), lambda bi, i: (bi, i, 0))
    args = [x, g_mix, modrows, modrows, wgate, o_da, o_mla, o_pool, wbr, wout, modrows, g_ffn, modrows, modrows]
    specs = [tile(d), _const_spec((1, d)), row(0), row(1), _const_spec(wgate.shape),
             tile(BRANCH_W), tile(BRANCH_W), tile(BRANCH_W), _const_spec(wbr.shape), _const_spec(wout.shape),
             row(2), _const_spec((1, d)), row(3), row(4)]
    out_shape = [jax.ShapeDtypeStruct((b, s, d), F32), jax.ShapeDtypeStruct((b, s, d), BF16)]
    out_specs = [tile(d), tile(d)]
    if moe:
        args.append(router)
        specs.append(_const_spec(router.shape))
        out_shape.append(jax.ShapeDtypeStruct((b, s, LANES), F32))
        out_specs.append(tile(LANES))
    return pl.pallas_call(
        functools.partial(_merge_kernel, moe=moe, n_exp=n_exp),
        out_shape=tuple(out_shape),
        grid=(b, s // tm),
        in_specs=specs,
        out_specs=tuple(out_specs),
        compiler_params=_params(("arbitrary", "arbitrary")),
        name="merge_out",
    )(*args)


def _ffn_kernel(h_ref, x_ref, gt_ref, wg_ref, wu_ref, wd_ref, gfin_ref, o_ref, *, final):
    h = h_ref[...]
    gte = jnp.dot(h, wg_ref[...], preferred_element_type=F32)
    up = jnp.dot(h, wu_ref[...], preferred_element_type=F32)
    act = (gte * _sigmoid(gte) * up).astype(BF16)
    y = jnp.dot(act, wd_ref[...], preferred_element_type=F32)
    xo = x_ref[...] + gt_ref[...] * y
    if final:
        xo = _rms(xo, gfin_ref[...])
    o_ref[...] = xo


def _ffn(h2, x, modrows, wg, wu, wd, g_final, final, tm):
    b, s, d = x.shape
    tm = min(tm, s)
    tile = pl.BlockSpec((None, tm, d), lambda bi, i: (bi, i, 0))
    return pl.pallas_call(
        functools.partial(_ffn_kernel, final=final),
        out_shape=jax.ShapeDtypeStruct((b, s, d), F32),
        grid=(b, s // tm),
        in_specs=[tile, tile, pl.BlockSpec((None, 1, d), lambda bi, i: (bi, 0, 5)),
                  _const_spec(wg.shape, True), _const_spec(wu.shape, True), _const_spec(wd.shape, True),
                  _const_spec((1, d))],
        out_specs=tile,
        compiler_params=_params(("arbitrary", "arbitrary")),
        name="ffn_swiglu",
    )(h2, x, modrows, wg, wu, wd, g_final)


ROUTE_SUB = 256
MOE_TOKENS = 1024
MOE_FF_CHUNK = 1792
MOE_ROWS = 256
MOE_TAIL_ROWS = 64


def _route_kernel(g_ref, rank_ref, rankt_ref, cnt_ref, *, n_exp):
    t = g_ref.shape[0]
    r_io = lax.broadcasted_iota(jnp.int32, (ROUTE_SUB, ROUTE_SUB), 0)
    c_io = lax.broadcasted_iota(jnp.int32, (ROUTE_SUB, ROUTE_SUB), 1)
    tri = jnp.where(c_io <= r_io, 1.0, 0.0).astype(BF16)
    carry = jnp.zeros((1, LANES), F32)
    for j in range(t // ROUTE_SUB):
        rows = slice(j * ROUTE_SUB, (j + 1) * ROUTE_SUB)
        routed = g_ref[rows, :] != 0.0
        incl = jnp.dot(tri, jnp.where(routed, 1.0, 0.0).astype(BF16), preferred_element_type=F32)
        rank_ref[rows, :] = jnp.where(routed, incl - 1.0 + carry, -1.0)
        carry = carry + incl[ROUTE_SUB - 1:ROUTE_SUB, :]
    rankt_ref[...] = rank_ref[...].T[0:n_exp, :]
    cnt_ref[...] = jnp.broadcast_to(carry, cnt_ref.shape).astype(jnp.int32)


def _route(gates, n_exp, tm):
    n = gates.shape[0]
    return pl.pallas_call(
        functools.partial(_route_kernel, n_exp=n_exp),
        out_shape=(jax.ShapeDtypeStruct((n, LANES), F32),
                   jax.ShapeDtypeStruct((n // tm, n_exp, tm), F32),
                   jax.ShapeDtypeStruct((n // tm, 8, LANES), jnp.int32)),
        grid=(n // tm,),
        in_specs=[pl.BlockSpec((tm, LANES), lambda i: (i, 0))],
        out_specs=(pl.BlockSpec((tm, LANES), lambda i: (i, 0)),
                   pl.BlockSpec((None, n_exp, tm), lambda i: (i, 0, 0)),
                   pl.BlockSpec((None, 8, LANES), lambda i: (i, 0, 0))),
        compiler_params=_params(("arbitrary",)),
        name="moe_route",
    )(gates)


def _moe_kernel(cnt_ref, h_ref, rank_ref, rankt_ref, gates_ref, wg_ref, wu_ref, wd_ref, x_ref, gt_ref,
                gfin_ref, o_ref, xg_ref, yg_ref, *, n_exp, final):
    i, e, f = pl.program_id(0), pl.program_id(1), pl.program_id(2)
    nf = pl.num_programs(2)
    t = h_ref.shape[0]
    code = cnt_ref[i * n_exp + e]
    n_full = lax.shift_right_logical(code, 1)

    @pl.when(jnp.logical_and(e == 0, f == 0))
    def _():
        o_ref[...] = jnp.zeros_like(o_ref)

    rank_row = rankt_ref[pl.ds(e, 1), :]

    def block(base, n_rows):
        rows = pl.ds(base, n_rows)

        @pl.when(f == 0)
        def _():
            slot = (base + lax.broadcasted_iota(jnp.int32, (n_rows, t), 0)).astype(F32)
            onehot = jnp.where(rank_row == slot, 1.0, 0.0).astype(BF16)
            xg_ref[rows, :] = jnp.dot(onehot, h_ref[...], preferred_element_type=F32).astype(BF16)

        xb = xg_ref[rows, :]
        gte = jnp.dot(xb, wg_ref[...], preferred_element_type=F32)
        up = jnp.dot(xb, wu_ref[...], preferred_element_type=F32)
        act = (gte * _sigmoid(gte) * up).astype(BF16)
        part = jnp.dot(act, wd_ref[...], preferred_element_type=F32)

        @pl.when(f == 0)
        def _():
            yg_ref[rows, :] = part

        @pl.when(f > 0)
        def _():
            yg_ref[rows, :] += part

        @pl.when(f == nf - 1)
        def _():
            lane = lax.broadcasted_iota(jnp.int32, (t, LANES), 1)
            rank_col = jnp.sum(jnp.where(lane == e, rank_ref[...], 0.0), axis=-1, keepdims=True)
            gate_col = jnp.sum(jnp.where(lane == e, gates_ref[...], 0.0), axis=-1, keepdims=True)
            slot = (base + lax.broadcasted_iota(jnp.int32, (t, n_rows), 1)).astype(F32)
            onehot = jnp.where(rank_col == slot, 1.0, 0.0).astype(BF16)
            o_ref[...] += gate_col * jnp.dot(onehot, yg_ref[rows, :].astype(BF16),
                                             preferred_element_type=F32)

    def full_block(b, carry):
        block(pl.multiple_of(b * MOE_ROWS, MOE_ROWS), MOE_ROWS)
        return carry

    lax.fori_loop(0, n_full, full_block, 0)

    @pl.when(jnp.bitwise_and(code, 1) == 1)
    def _():
        block(pl.multiple_of(n_full * MOE_ROWS, MOE_ROWS), MOE_TAIL_ROWS)

    @pl.when(jnp.logical_and(e == n_exp - 1, f == nf - 1))
    def _():
        xo = x_ref[...] + gt_ref[...] * o_ref[...]
        if final:
            xo = _rms(xo, gfin_ref[...])
        o_ref[...] = xo


def _moe(h2, x, modrows, gates, wg, wu, wd, g_final, final):
    bsz, seq, d = x.shape
    n = bsz * seq
    h2, x, gates = h2.reshape(n, d), x.reshape(n, d), gates.reshape(n, LANES)
    n_exp, _, dff = wg.shape
    tm, tf = min(MOE_TOKENS, seq), MOE_FF_CHUNK
    tiles_per_row = seq // tm
    rank, rank_t, cnt = _route(gates, n_exp, tm)
    counts = cnt[:, 0, :n_exp].reshape(-1)
    rem = counts % MOE_ROWS
    tail = jnp.logical_and(rem > 0, rem <= MOE_TAIL_ROWS)
    n_blocks = 2 * (counts // MOE_ROWS + (rem > MOE_TAIL_ROWS)) + tail
    grid_spec = pltpu.PrefetchScalarGridSpec(
        num_scalar_prefetch=1,
        grid=(n // tm, n_exp, dff // tf),
        in_specs=[pl.BlockSpec((tm, d), lambda i, e, f, c: (i, 0)),
                  pl.BlockSpec((tm, LANES), lambda i, e, f, c: (i, 0)),
                  pl.BlockSpec((None, n_exp, tm), lambda i, e, f, c: (i, 0, 0)),
                  pl.BlockSpec((tm, LANES), lambda i, e, f, c: (i, 0)),
                  pl.BlockSpec((None, d, tf), lambda i, e, f, c: (e, 0, f)),
                  pl.BlockSpec((None, d, tf), lambda i, e, f, c: (e, 0, f)),
                  pl.BlockSpec((None, tf, d), lambda i, e, f, c: (e, f, 0)),
                  pl.BlockSpec((tm, d), lambda i, e, f, c: (i, 0)),
                  pl.BlockSpec((None, 1, d), lambda i, e, f, c: (i // tiles_per_row, 0, N_MOD - 1)),
                  pl.BlockSpec((1, d), lambda i, e, f, c: (0, 0))],
        out_specs=pl.BlockSpec((tm, d), lambda i, e, f, c: (i, 0)),
        scratch_shapes=[pltpu.VMEM((tm, d), BF16), pltpu.VMEM((tm, d), F32)])
    out = pl.pallas_call(
        functools.partial(_moe_kernel, n_exp=n_exp, final=final),
        out_shape=jax.ShapeDtypeStruct((n, d), F32),
        grid_spec=grid_spec,
        compiler_params=_params(("arbitrary",) * 3),
        name="moe_swiglu",
    )(n_blocks, h2, rank, rank_t, gates, wg, wu, wd, x, modrows, g_final)
    return out.reshape(bsz, seq, d)


def _rope_tables(seq):
    axis_dim = DA_QK // 2
    n_freq = axis_dim // 2
    inv = jnp.exp(-math.log(ROPE_BASE) * jnp.arange(n_freq, dtype=F32) * (2.0 / axis_dim))
    t = jnp.arange(seq, dtype=jnp.int32)
    ar = (t // GRID_W).astype(F32)[:, None] * inv
    ac = (t % GRID_W).astype(F32)[:, None] * inv
    cos = jnp.concatenate([jnp.cos(ar), jnp.cos(ar), jnp.cos(ac), jnp.cos(ac)], axis=-1)
    sin = jnp.concatenate([jnp.sin(ar), jnp.sin(ar), jnp.sin(ac), jnp.sin(ac)], axis=-1)
    first = (jnp.arange(DA_QK) % axis_dim) < n_freq
    sa = jnp.where(first, -sin, 0.0)
    sb = jnp.where(first, 0.0, sin)
    rep = lambda a: jnp.tile(a, (1, LANES // DA_QK))
    return rep(cos), rep(sa), rep(sb)


def _identity_tables(seq):
    return jnp.ones((seq, LANES), F32), jnp.zeros((seq, LANES), F32), jnp.zeros((seq, LANES), F32)


def _pack_layer_weights(w_in, w_uq, w_ukv):
    d = w_in.shape[0]
    zpad = jnp.zeros((d, LANES - MLA_ROPE), w_in.dtype)
    o_kr = 3 * 512 + MLA_Q_RANK + MLA_KV_RANK
    o_pool = o_kr + MLA_ROPE
    w2 = jnp.concatenate([w_in[:, :o_kr + MLA_ROPE], zpad, w_in[:, o_pool:o_pool + 512]], axis=1).astype(BF16)
    wgate = w_in[:, o_pool + 512:].astype(BF16)
    hq = MLA_NOPE + MLA_ROPE
    wq = w_uq.reshape(MLA_Q_RANK, MLA_HEADS, hq)
    wq = jnp.concatenate([wq, jnp.zeros((MLA_Q_RANK, MLA_HEADS, MLA_QK_PAD - hq), w_uq.dtype)], axis=-1)
    wuq = wq.reshape(MLA_Q_RANK, MLA_HEADS * MLA_QK_PAD).astype(BF16)
    wkv = w_ukv.reshape(MLA_KV_RANK, MLA_HEADS, MLA_NOPE + MLA_V)
    wukv = jnp.concatenate([wkv[:, :, :MLA_NOPE].reshape(MLA_KV_RANK, -1),
                            wkv[:, :, MLA_NOPE:].reshape(MLA_KV_RANK, -1)], axis=1).astype(BF16)
    return w2, wgate, wuq, wukv


def kernel(x, c, ctx, c_ctx, w_mod, b_mod, g_mix, w_in, da_lambda, da_subln, mla_gq, w_uq, mla_gkv, w_ukv,
           pool_w, pool_scale, w_branch, w_out, g_ffn, ffn_w_gate, ffn_w_up, ffn_w_down, moe_router,
           moe_w_gate, moe_w_up, moe_w_down, g_final):
    bsz, seq, d = x.shape
    n_ctx = ctx.shape[1]
    depth = w_mod.shape[0]

    cond8 = jnp.zeros((8, d), F32).at[:bsz].set(c).at[bsz].set(c_ctx)
    mod = _mod_rows(cond8, w_mod, b_mod)
    lat_tabs = _rope_tables(seq)
    ctx_tabs = _identity_tables(n_ctx)
    g_fin = g_final.reshape(1, d)

    xc = ctx
    for l in range(depth):
        need_ctx = l < depth - 1
        lam_init = 0.8 - 0.6 * math.exp(-0.3 * l)
        mod_lat = mod[l, :bsz][:, None, :]
        mod_ctx = jnp.broadcast_to(mod[l, bsz][None, None, :], (bsz, 1, N_MOD * d))
        w2, wgate, wuq, wukv = _pack_layer_weights(w_in[l], w_uq[l], w_ukv[l])
        gmix = g_mix[l].reshape(1, d)
        gffn = g_ffn[l].reshape(1, d)
        gq = mla_gq[l].reshape(1, -1)
        gkv = mla_gkv[l].reshape(1, -1)
        subln = da_subln[l].reshape(1, -1)
        lam = da_lambda[l]
        pw = pool_w[l].astype(BF16)
        psc = pool_scale[l].reshape(1, -1)
        wbr = w_branch[l].astype(BF16)
        wout = w_out[l].astype(BF16)
        j = l // 2
        dense = l % 2 == 0
        final = l == depth - 1

        qda, kda, vda, qm, km, vm, pin = _inproj(x, mod_lat, gmix, w2, gq, wuq, gkv, wukv, lat_tabs, PROJ_ROWS)
        cqda, ckda, cvda, cqm, ckm, cvm, cpin = _inproj(xc, mod_ctx, gmix, w2, gq, wuq, gkv, wukv, ctx_tabs, PROJ_ROWS)

        o_da = _attention(qda, ckda, cvda, kda, vda, DA_HEADS, LANES, DA_V, ATTN_ROWS // 2, ATTN_KEYS, diff=True,
                          lam=lam, g=subln, lam_init=lam_init, name="diff_attn")
        o_mla = _attention(qm, ckm, cvm, km, vm, MLA_HEADS, MLA_QK_PAD, MLA_V, ATTN_ROWS, ATTN_KEYS, name="mla_attn")
        o_pool = _pool(pin, pw, psc, 512)

        router = None
        n_exp = moe_router.shape[-1]
        if not dense:
            router = moe_router[j].T
        outs = _merge(x, mod_lat, gmix, wgate, o_da, o_mla, o_pool, wbr, wout, gffn, router, n_exp, PROJ_ROWS)

        if dense:
            wg = ffn_w_gate[j].astype(BF16)
            wu = ffn_w_up[j].astype(BF16)
            wd = ffn_w_down[j].astype(BF16)
            x = _ffn(outs[1], outs[0], mod_lat, wg, wu, wd, g_fin, final, PROJ_ROWS)
        else:
            wg = moe_w_gate[j].astype(BF16)
            wu = moe_w_up[j].astype(BF16)
            wd = moe_w_down[j].astype(BF16)
            x = _moe(outs[1], outs[0], mod_lat, outs[2], wg, wu, wd, g_fin, final)

        if need_ctx:
            co_da = _attention(cqda, ckda, cvda, None, None, DA_HEADS, LANES, DA_V, ATTN_ROWS // 2, ATTN_KEYS, diff=True,
                               lam=lam, g=subln, lam_init=lam_init, name="diff_attn_ctx")
            co_mla = _attention(cqm, ckm, cvm, None, None, MLA_HEADS, MLA_QK_PAD, MLA_V, ATTN_ROWS, ATTN_KEYS,
                                name="mla_attn_ctx")
            co_pool = _pool(cpin, pw, psc, 512)
            couts = _merge(xc, mod_ctx, gmix, wgate, co_da, co_mla, co_pool, wbr, wout, gffn, router, n_exp, PROJ_ROWS)
            if dense:
                xc = _ffn(couts[1], couts[0], mod_ctx, wg, wu, wd, g_fin, False, PROJ_ROWS)
            else:
                xc = _moe(couts[1], couts[0], mod_ctx, couts[2], wg, wu, wd, g_fin, False)
    return x
```

```python
import functools
import math

import jax
import jax.numpy as jnp
from jax import lax
from jax.experimental import pallas as pl
from jax.experimental.pallas import tpu as pltpu

F32 = jnp.float32
BF16 = jnp.bfloat16

GRID_W = 64
DA_HEADS = 4
DA_QK = 64
DA_V = 128
MLA_HEADS = 4
MLA_NOPE = 128
MLA_ROPE = 64
MLA_V = 128
MLA_Q_RANK = 384
MLA_KV_RANK = 256
POOL_WINDOWS = (2, 4, 8, 16)
BRANCH_W = 512
ROPE_BASE = 10000.0
EPS = 1e-6
N_MOD = 6
LANES = 128
MLA_QK_PAD = 256
LOG2E = math.log2(math.e)
DA_SCALE = DA_QK ** -0.5 * LOG2E
MLA_SCALE = (MLA_NOPE + MLA_ROPE) ** -0.5 * LOG2E
NEG_BIG = -1e30
PROJ_ROWS = 512
POOL_ROWS = 2048
ATTN_ROWS = 1024
ATTN_KEYS = 512
ATTN_Q_TILES = 2
VMEM_LIMIT = 56 * 1024 * 1024

C_DAQ, C_DAK, C_DAV = 0, 512, 1024
C_QD = 1536
C_KVD = C_QD + MLA_Q_RANK
C_KR = C_KVD + MLA_KV_RANK
C_POOL = C_KR + LANES


def _sigmoid(v):
    return 1.0 / (1.0 + jnp.exp(-v))


def _params(sem, vmem=VMEM_LIMIT):
    return pltpu.CompilerParams(dimension_semantics=sem, vmem_limit_bytes=vmem)


def _const_spec(shape, single=False):
    nd = len(shape)
    if single:
        return pl.BlockSpec(shape, lambda *_: (0,) * nd, pipeline_mode=pl.Buffered(1))
    return pl.BlockSpec(shape, lambda *_: (0,) * nd)


def _mod_kernel(cond_ref, w_ref, b_ref, o_ref):
    c = cond_ref[...]
    s = c * _sigmoid(c)
    o_ref[...] = jnp.dot(s, w_ref[...], preferred_element_type=F32,
                         precision=lax.Precision.HIGHEST) + b_ref[...]


def _mod_rows(cond8, w_mod, b_mod):
    depth, d, n = w_mod.shape
    tn = n // 2
    return pl.pallas_call(
        _mod_kernel,
        out_shape=jax.ShapeDtypeStruct((depth, 8, n), F32),
        grid=(depth, n // tn),
        in_specs=[pl.BlockSpec((8, d), lambda l, j: (0, 0)),
                  pl.BlockSpec((None, d, tn), lambda l, j: (l, 0, j)),
                  pl.BlockSpec((None, 1, tn), lambda l, j: (l, 0, j))],
        out_specs=pl.BlockSpec((None, 8, tn), lambda l, j: (l, 0, j)),
        compiler_params=_params(("arbitrary", "arbitrary")),
        name="adaln_rows",
    )(cond8, w_mod, b_mod.reshape(depth, 1, n))


def _norm_mod(x, g, sh, sc):
    r = lax.rsqrt(jnp.mean(x * x, axis=-1, keepdims=True) + EPS)
    return (x * r * g) * (1.0 + sc) + sh


def _rms(v, g):
    return v * lax.rsqrt(jnp.mean(v * v, axis=-1, keepdims=True) + EPS) * g


def _inproj_kernel(x_ref, g_ref, sh_ref, sc_ref, w_ref, gq_ref, wuq_ref, gkv_ref, wukv_ref,
                   cos_ref, sa_ref, sb_ref,
                   qda_ref, kda_ref, vda_ref, qm_ref, km_ref, vm_ref, pool_ref):
    h = _norm_mod(x_ref[...], g_ref[...], sh_ref[...], sc_ref[...])
    z = jnp.dot(h.astype(BF16), w_ref[...], preferred_element_type=F32)
    cos = cos_ref[...]
    sa = sa_ref[...]
    sb = sb_ref[...]

    def rope(blk):
        return blk * cos + pltpu.roll(blk, LANES - 16, 1) * sa + pltpu.roll(blk, 16, 1) * sb

    lo = lax.broadcasted_iota(jnp.int32, cos.shape, 1) < MLA_ROPE

    for hh in range(DA_HEADS):
        c0 = hh * LANES
        qda_ref[:, c0:c0 + LANES] = (rope(z[:, C_DAQ + c0:C_DAQ + c0 + LANES]) * DA_SCALE).astype(BF16)
        kda_ref[:, c0:c0 + LANES] = rope(z[:, C_DAK + c0:C_DAK + c0 + LANES]).astype(BF16)

    qn = _rms(z[:, C_QD:C_QD + MLA_Q_RANK], gq_ref[...])
    qf = jnp.dot(qn.astype(BF16), wuq_ref[...], preferred_element_type=F32)
    kvn = _rms(z[:, C_KVD:C_KVD + MLA_KV_RANK], gkv_ref[...])
    kvf = jnp.dot(kvn.astype(BF16), wukv_ref[...], preferred_element_type=F32)
    kr = jnp.where(lo, rope(z[:, C_KR:C_KR + LANES]), 0.0).astype(BF16)
    for hh in range(MLA_HEADS):
        c0 = hh * MLA_QK_PAD
        qm_ref[:, c0:c0 + LANES] = (qf[:, c0:c0 + LANES] * MLA_SCALE).astype(BF16)
        qr = jnp.where(lo, rope(qf[:, c0 + LANES:c0 + 2 * LANES]), 0.0)
        qm_ref[:, c0 + LANES:c0 + 2 * LANES] = (qr * MLA_SCALE).astype(BF16)
        km_ref[:, c0:c0 + LANES] = kvf[:, hh * LANES:(hh + 1) * LANES].astype(BF16)
        km_ref[:, c0 + LANES:c0 + 2 * LANES] = kr
    ones = jnp.ones(cos.shape, BF16)
    for hh in range(MLA_HEADS):
        c0 = 2 * hh * LANES
        vda_ref[:, c0:c0 + LANES] = z[:, C_DAV + hh * LANES:C_DAV + (hh + 1) * LANES].astype(BF16)
        vda_ref[:, c0 + LANES:c0 + 2 * LANES] = ones
        vm_ref[:, c0:c0 + LANES] = kvf[:, 512 + hh * LANES:512 + (hh + 1) * LANES].astype(BF16)
        vm_ref[:, c0 + LANES:c0 + 2 * LANES] = ones
    pool_ref[...] = z[:, C_POOL:C_POOL + 512]


def _inproj(x, modrows, g_mix, w2, gq, wuq, gkv, wukv, tabs, tm):
    b, s, d = x.shape
    tm = min(tm, s)
    cos, sa, sb = tabs
    row = lambda j: pl.BlockSpec((None, 1, d), lambda bi, i: (bi, 0, j))
    tab = pl.BlockSpec((tm, LANES), lambda bi, i: (i, 0))
    out = lambda w, dt: jax.ShapeDtypeStruct((b, s, w), dt)
    ospec = lambda w: pl.BlockSpec((None, tm, w), lambda bi, i: (bi, i, 0))
    return pl.pallas_call(
        _inproj_kernel,
        out_shape=(out(512, BF16), out(512, BF16), out(1024, BF16), out(1024, BF16), out(1024, BF16),
                   out(1024, BF16), out(512, F32)),
        grid=(b, s // tm),
        in_specs=[pl.BlockSpec((None, tm, d), lambda bi, i: (bi, i, 0)),
                  _const_spec((1, d)), row(0), row(1),
                  _const_spec(w2.shape), _const_spec(gq.shape), _const_spec(wuq.shape),
                  _const_spec(gkv.shape), _const_spec(wukv.shape), tab, tab, tab],
        out_specs=(ospec(512), ospec(512), ospec(1024), ospec(1024), ospec(1024), ospec(1024), ospec(512)),
        compiler_params=_params(("arbitrary", "arbitrary")),
        name="in_proj",
    )(x, g_mix, modrows, modrows, w2, gq, wuq, gkv, wukv, cos, sa, sb)


def _attn_kernel(*refs, diff, has_lat, tk, tq, lam_init):
    it = iter(refs)
    q_ref, kc_ref, vc_ref = next(it), next(it), next(it)
    kl_ref = vl_ref = lam_ref = g_ref = None
    if has_lat:
        kl_ref, vl_ref = next(it), next(it)
    if diff:
        lam_ref, g_ref = next(it), next(it)
    o_ref = next(it)
    m_ref, acc_ref = next(it), next(it)
    s_refs = (next(it), next(it))
    p_refs = (next(it), next(it))
    al_refs = (next(it), next(it))
    n_qt = q_ref.shape[0] // tq
    dv = o_ref.shape[-1]

    def stacked_q(qt):
        q = q_ref[qt * tq:(qt + 1) * tq, :]
        if not diff:
            return q
        lane = lax.broadcasted_iota(jnp.int32, q.shape, 1)
        zero = jnp.zeros_like(q)
        return jnp.concatenate([jnp.where(lane < DA_QK, q, zero), jnp.where(lane >= DA_QK, q, zero)], axis=0)

    qs = [stacked_q(qt) for qt in range(n_qt)]
    n_ctx = kc_ref.shape[0]
    n_lat = kl_ref.shape[0] // tk if has_lat else 0
    width = lambda c: n_ctx if c == 0 else tk
    k_of = lambda c: kc_ref[...] if c == 0 else kl_ref[(c - 1) * tk:c * tk, :]
    v_of = lambda c: vc_ref[...] if c == 0 else vl_ref[(c - 1) * tk:c * tk, :]

    def qk(slot, qt, c):
        k = k_of(c)
        s_refs[slot][:, :k.shape[0]] = lax.dot_general(qs[qt], k, (((1,), (1,)), ((), ())),
                                                       preferred_element_type=F32)

    def sm(slot, qt, c):
        s_ref, p_ref = s_refs[slot], p_refs[slot]
        blocks = [slice(j * LANES, (j + 1) * LANES) for j in range(width(c) // LANES)]
        mx = s_ref[:, blocks[0]]
        for blk in blocks[1:]:
            mx = jnp.maximum(mx, s_ref[:, blk])
        m = m_ref[qt]
        m_new = jnp.maximum(m, jnp.max(mx, axis=-1, keepdims=True))
        al_refs[slot][...] = jnp.exp2(m - m_new)
        m_ref[qt] = m_new
        for blk in blocks:
            p_ref[:, blk] = jnp.exp2(s_ref[:, blk] - m_new).astype(BF16)

    def pv(slot, qt, c):
        v = v_of(c)
        new = jnp.dot(p_refs[slot][:, :v.shape[0]], v, preferred_element_type=F32)
        al = al_refs[slot][...]
        for blk in (slice(0, dv), slice(dv, 2 * dv)):
            acc_ref[qt, :, blk] = al * acc_ref[qt, :, blk] + new[:, blk]

    def finish(qt):
        acc = acc_ref[qt]
        o = acc[:, :dv] / acc[:, dv:]
        if diff:
            lv = lam_ref[...]
            a = jnp.sum(lv[0:1, :] * lv[1:2, :], axis=-1, keepdims=True)
            b = jnp.sum(lv[2:3, :] * lv[3:4, :], axis=-1, keepdims=True)
            lam = jnp.exp(a) - jnp.exp(b) + lam_init
            o = o[:tq] - lam * o[tq:]
            o = _rms(o, g_ref[...]) * (1.0 - lam_init)
        o_ref[qt * tq:(qt + 1) * tq, :] = o.astype(o_ref.dtype)

    m_ref[...] = jnp.full(m_ref.shape, NEG_BIG, F32)
    acc_ref[...] = jnp.zeros(acc_ref.shape, F32)
    items = [(qt, 0) for qt in range(n_qt)] + [(qt, c) for qt in range(n_qt) for c in range(1, n_lat + 1)]
    for g in range(len(items) + 2):
        if g < len(items):
            qk(g % 2, *items[g])
        if 0 <= g - 2 < len(items):
            pv(g % 2, *items[g - 2])
        if 0 <= g - 1 < len(items):
            sm((g - 1) % 2, *items[g - 1])
        if 0 <= g - 2 < len(items) and items[g - 2][1] == n_lat:
            finish(items[g - 2][0])


def _attention(q, kc, vc, kl, vl, heads, dq, dv, tq, tk, diff=False, lam=None, g=None, lam_init=0.0,
               name="attn"):
    b, s, _ = q.shape
    c = kc.shape[1]
    tq = min(tq, s)
    n_qt = min(ATTN_Q_TILES, s // tq)
    has_lat = kl is not None
    args = [q, kc, vc]
    specs = [pl.BlockSpec((None, n_qt * tq, dq), lambda bi, hi, qi: (bi, qi, hi)),
             pl.BlockSpec((None, c, dq), lambda bi, hi, qi: (bi, 0, hi)),
             pl.BlockSpec((None, c, 2 * dv), lambda bi, hi, qi: (bi, 0, hi))]
    sl = 0
    if has_lat:
        sl = kl.shape[1]
        tk = min(tk, sl)
        args += [kl, vl]
        specs += [pl.BlockSpec((None, sl, dq), lambda bi, hi, qi: (bi, 0, hi)),
                  pl.BlockSpec((None, sl, 2 * dv), lambda bi, hi, qi: (bi, 0, hi))]
    if diff:
        args += [lam, g]
        specs += [_const_spec(lam.shape), _const_spec(g.shape)]
    kern = functools.partial(_attn_kernel, diff=diff, has_lat=has_lat, tk=tk, tq=tq, lam_init=lam_init)
    rows = 2 * tq if diff else tq
    wmax = max(tk, c) if has_lat else c
    scratch = [pltpu.VMEM((n_qt, rows, LANES), F32), pltpu.VMEM((n_qt, rows, 2 * dv), F32),
               pltpu.VMEM((rows, wmax), F32), pltpu.VMEM((rows, wmax), F32),
               pltpu.VMEM((rows, wmax), BF16), pltpu.VMEM((rows, wmax), BF16),
               pltpu.VMEM((rows, LANES), F32), pltpu.VMEM((rows, LANES), F32)]
    return pl.pallas_call(
        kern,
        out_shape=jax.ShapeDtypeStruct((b, s, heads * dv), BF16),
        grid=(b, heads, s // (n_qt * tq)),
        in_specs=specs,
        out_specs=pl.BlockSpec((None, n_qt * tq, dv), lambda bi, hi, qi: (bi, qi, hi)),
        scratch_shapes=scratch,
        compiler_params=_params(("arbitrary", "arbitrary", "arbitrary")),
        name=name,
    )(*args)


def _pool_kernel(prev_ref, cur_ref, next_ref, w_ref, sc_ref, o_ref, *, seq_len):
    i = pl.program_id(1)
    n = pl.num_programs(1)
    cur = cur_ref[...]
    tm = cur.shape[0]
    prev = jnp.where(i > 0, prev_ref[...], 0.0)
    nxt = jnp.where(i < n - 1, next_ref[...], 0.0)
    ext = jnp.concatenate([prev, cur, nxt], axis=0)
    ne = tm + 16
    t = i * tm + lax.broadcasted_iota(jnp.int32, (tm, 1), 0)
    for g, w in enumerate(POOL_WINDOWS):
        e = ext[:, g * LANES:(g + 1) * LANES]
        acc = e + pltpu.roll(e, 1, 0)
        half = 1
        while 2 * half < w:
            acc = pltpu.roll(acc, half, 0) + pltpu.roll(acc, ne - half, 0)
            half *= 2
        win = acc[8:8 + tm]
        lo = jnp.clip(t - w // 2, 0, seq_len)
        hi = jnp.clip(t - w // 2 + w, 0, seq_len)
        cnt = (hi - lo).astype(F32)
        dlt = win / cnt - cur[:, g * LANES:(g + 1) * LANES]
        y = jnp.dot(dlt.astype(BF16), w_ref[g], preferred_element_type=F32)
        o_ref[:, g * LANES:(g + 1) * LANES] = (y * sc_ref[:, g * LANES:(g + 1) * LANES]).astype(BF16)


def _pool(u, pool_w, pool_scale, tm):
    b, s, w = u.shape
    tm = min(tm, s)
    nb8 = s // 8
    r8 = tm // 8
    return pl.pallas_call(
        functools.partial(_pool_kernel, seq_len=s),
        out_shape=jax.ShapeDtypeStruct((b, s, w), BF16),
        grid=(b, s // tm),
        in_specs=[pl.BlockSpec((None, 8, w), lambda bi, i: (bi, jnp.maximum(i * r8 - 1, 0), 0)),
                  pl.BlockSpec((None, tm, w), lambda bi, i: (bi, i, 0)),
                  pl.BlockSpec((None, 8, w), lambda bi, i: (bi, jnp.minimum((i + 1) * r8, nb8 - 1), 0)),
                  _const_spec(pool_w.shape), _const_spec(pool_scale.shape)],
        out_specs=pl.BlockSpec((None, tm, w), lambda bi, i: (bi, i, 0)),
        compiler_params=_params(("arbitrary", "arbitrary")),
        name="pool_mixer",
    )(u, u, u, pool_w, pool_scale)


def _merge_kernel(*refs, moe, n_exp):
    (x_ref, gmix_ref, sh1_ref, sc1_ref, wgate_ref, oda_ref, omla_ref, opool_ref, wbr_ref, wout_ref,
     gt1_ref, gffn_ref, sh2_ref, sc2_ref) = refs[:14]
    if moe:
        router_ref, xo_ref, h2_ref, gates_ref = refs[14:]
    else:
        xo_ref, h2_ref = refs[14:]
    x = x_ref[...]
    d = x.shape[-1]
    h = _norm_mod(x, gmix_ref[...], sh1_ref[...], sc1_ref[...]).astype(BF16)
    merged = None
    for n, o_ref in enumerate((oda_ref, omla_ref, opool_ref)):
        gate = _sigmoid(jnp.dot(h, wgate_ref[:, n * d:(n + 1) * d], preferred_element_type=F32))
        proj = jnp.dot(o_ref[...], wbr_ref[n], preferred_element_type=F32)
        merged = gate * proj if merged is None else merged + gate * proj
    mix = jnp.dot(merged.astype(BF16), wout_ref[...], preferred_element_type=F32)
    xn = x + gt1_ref[...] * mix
    xo_ref[...] = xn
    h2 = _norm_mod(xn, gffn_ref[...], sh2_ref[...], sc2_ref[...])
    h2_ref[...] = h2.astype(BF16)
    if moe:
        logit = [jnp.sum(h2 * router_ref[e:e + 1, :], axis=-1, keepdims=True) for e in range(n_exp)]

        def top1(vals):
            best, idx = vals[0], jnp.zeros(vals[0].shape, jnp.int32)
            for e in range(1, n_exp):
                better = vals[e] > best
                best = jnp.where(better, vals[e], best)
                idx = jnp.where(better, e, idx)
            return best, idx

        v1, i1 = top1(logit)
        v2, i2 = top1([jnp.where(i1 == e, NEG_BIG, logit[e]) for e in range(n_exp)])
        w1 = 1.0 / (1.0 + jnp.exp(v2 - v1))
        lane = lax.broadcasted_iota(jnp.int32, gates_ref.shape, 1)
        gates_ref[...] = jnp.where(lane == i1, w1, 0.0) + jnp.where(lane == i2, 1.0 - w1, 0.0)


def _merge(x, modrows, g_mix, wgate, o_da, o_mla, o_pool, wbr, wout, g_ffn, router, n_exp, tm):
    b, s, d = x.shape
    tm = min(tm, s)
    moe = router is not None
    row = lambda j: pl.BlockSpec((None, 1, d), lambda bi, i: (bi, 0, j))
    tile = lambda w: pl.BlockSpec((None, tm, w), lambda bi, i: (bi, i, 0))
    args = [x, g_mix, modrows, modrows, wgate, o_da, o_mla, o_pool, wbr, wout, modrows, g_ffn, modrows, modrows]
    specs = [tile(d), _const_spec((1, d)), row(0), row(1), _const_spec(wgate.shape),
             tile(BRANCH_W), tile(BRANCH_W), tile(BRANCH_W), _const_spec(wbr.shape), _const_spec(wout.shape),
             row(2), _const_spec((1, d)), row(3), row(4)]
    out_shape = [jax.ShapeDtypeStruct((b, s, d), F32), jax.ShapeDtypeStruct((b, s, d), BF16)]
    out_specs = [tile(d), tile(d)]
    if moe:
        args.append(router)
        specs.append(_const_spec(router.shape))
        out_shape.append(jax.ShapeDtypeStruct((b, s, LANES), F32))
        out_specs.append(tile(LANES))
    return pl.pallas_call(
        functools.partial(_merge_kernel, moe=moe, n_exp=n_exp),
        out_shape=tuple(out_shape),
        grid=(b, s // tm),
        in_specs=specs,
        out_specs=tuple(out_specs),
        compiler_params=_params(("arbitrary", "arbitrary")),
        name="merge_out",
    )(*args)


def _ffn_kernel(h_ref, x_ref, gt_ref, wg_ref, wu_ref, wd_ref, gfin_ref, o_ref, *, final):
    h = h_ref[...]
    gte = jnp.dot(h, wg_ref[...], preferred_element_type=F32)
    up = jnp.dot(h, wu_ref[...], preferred_element_type=F32)
    act = (gte * _sigmoid(gte) * up).astype(BF16)
    y = jnp.dot(act, wd_ref[...], preferred_element_type=F32)
    xo = x_ref[...] + gt_ref[...] * y
    if final:
        xo = _rms(xo, gfin_ref[...])
    o_ref[...] = xo


def _ffn(h2, x, modrows, wg, wu, wd, g_final, final, tm):
    b, s, d = x.shape
    tm = min(tm, s)
    tile = pl.BlockSpec((None, tm, d), lambda bi, i: (bi, i, 0))
    return pl.pallas_call(
        functools.partial(_ffn_kernel, final=final),
        out_shape=jax.ShapeDtypeStruct((b, s, d), F32),
        grid=(b, s // tm),
        in_specs=[tile, tile, pl.BlockSpec((None, 1, d), lambda bi, i: (bi, 0, 5)),
                  _const_spec(wg.shape, True), _const_spec(wu.shape, True), _const_spec(wd.shape, True),
                  _const_spec((1, d))],
        out_specs=tile,
        compiler_params=_params(("arbitrary", "arbitrary")),
        name="ffn_swiglu",
    )(h2, x, modrows, wg, wu, wd, g_final)


ROUTE_SUB = 256
MOE_TOKENS = 1024
MOE_FF_CHUNK = 1792
MOE_ROWS = 256
MOE_TAIL_ROWS = 64


def _route_kernel(g_ref, rank_ref, rankt_ref, cnt_ref, *, n_exp):
    t = g_ref.shape[0]
    r_io = lax.broadcasted_iota(jnp.int32, (ROUTE_SUB, ROUTE_SUB), 0)
    c_io = lax.broadcasted_iota(jnp.int32, (ROUTE_SUB, ROUTE_SUB), 1)
    tri = jnp.where(c_io <= r_io, 1.0, 0.0).astype(BF16)
    carry = jnp.zeros((1, LANES), F32)
    for j in range(t // ROUTE_SUB):
        rows = slice(j * ROUTE_SUB, (j + 1) * ROUTE_SUB)
        routed = g_ref[rows, :] != 0.0
        incl = jnp.dot(tri, jnp.where(routed, 1.0, 0.0).astype(BF16), preferred_element_type=F32)
        rank_ref[rows, :] = jnp.where(routed, incl - 1.0 + carry, -1.0)
        carry = carry + incl[ROUTE_SUB - 1:ROUTE_SUB, :]
    rankt_ref[...] = rank_ref[...].T[0:n_exp, :]
    cnt_ref[...] = jnp.broadcast_to(carry, cnt_ref.shape).astype(jnp.int32)


def _route(gates, n_exp, tm):
    n = gates.shape[0]
    return pl.pallas_call(
        functools.partial(_route_kernel, n_exp=n_exp),
        out_shape=(jax.ShapeDtypeStruct((n, LANES), F32),
                   jax.ShapeDtypeStruct((n // tm, n_exp, tm), F32),
                   jax.ShapeDtypeStruct((n // tm, 8, LANES), jnp.int32)),
        grid=(n // tm,),
        in_specs=[pl.BlockSpec((tm, LANES), lambda i: (i, 0))],
        out_specs=(pl.BlockSpec((tm, LANES), lambda i: (i, 0)),
                   pl.BlockSpec((None, n_exp, tm), lambda i: (i, 0, 0)),
                   pl.BlockSpec((None, 8, LANES), lambda i: (i, 0, 0))),
        compiler_params=_params(("arbitrary",)),
        name="moe_route",
    )(gates)


def _moe_kernel(cnt_ref, h_ref, rank_ref, rankt_ref, gates_ref, wg_ref, wu_ref, wd_ref, x_ref, gt_ref,
                gfin_ref, o_ref, xg_ref, yg_ref, *, n_exp, final):
    i, e, f = pl.program_id(0), pl.program_id(1), pl.program_id(2)
    nf = pl.num_programs(2)
    t = h_ref.shape[0]
    code = cnt_ref[i * n_exp + e]
    n_full = lax.shift_right_logical(code, 1)

    @pl.when(jnp.logical_and(e == 0, f == 0))
    def _():
        o_ref[...] = jnp.zeros_like(o_ref)

    rank_row = rankt_ref[pl.ds(e, 1), :]

    def block(base, n_rows):
        rows = pl.ds(base, n_rows)

        @pl.when(f == 0)
        def _():
            slot = (base + lax.broadcasted_iota(jnp.int32, (n_rows, t), 0)).astype(F32)
            onehot = jnp.where(rank_row == slot, 1.0, 0.0).astype(BF16)
            xg_ref[rows, :] = jnp.dot(onehot, h_ref[...], preferred_element_type=F32).astype(BF16)

        xb = xg_ref[rows, :]
        gte = jnp.dot(xb, wg_ref[...], preferred_element_type=F32)
        up = jnp.dot(xb, wu_ref[...], preferred_element_type=F32)
        act = (gte * _sigmoid(gte) * up).astype(BF16)
        part = jnp.dot(act, wd_ref[...], preferred_element_type=F32)

        @pl.when(f == 0)
        def _():
            yg_ref[rows, :] = part

        @pl.when(f > 0)
        def _():
            yg_ref[rows, :] += part

        @pl.when(f == nf - 1)
        def _():
            lane = lax.broadcasted_iota(jnp.int32, (t, LANES), 1)
            rank_col = jnp.sum(jnp.where(lane == e, rank_ref[...], 0.0), axis=-1, keepdims=True)
            gate_col = jnp.sum(jnp.where(lane == e, gates_ref[...], 0.0), axis=-1, keepdims=True)
            slot = (base + lax.broadcasted_iota(jnp.int32, (t, n_rows), 1)).astype(F32)
            onehot = jnp.where(rank_col == slot, 1.0, 0.0).astype(BF16)
            o_ref[...] += gate_col * jnp.dot(onehot, yg_ref[rows, :].astype(BF16),
                                             preferred_element_type=F32)

    def full_block(b, carry):
        block(pl.multiple_of(b * MOE_ROWS, MOE_ROWS), MOE_ROWS)
        return carry

    lax.fori_loop(0, n_full, full_block, 0)

    @pl.when(jnp.bitwise_and(code, 1) == 1)
    def _():
        block(pl.multiple_of(n_full * MOE_ROWS, MOE_ROWS), MOE_TAIL_ROWS)

    @pl.when(jnp.logical_and(e == n_exp - 1, f == nf - 1))
    def _():
        xo = x_ref[...] + gt_ref[...] * o_ref[...]
        if final:
            xo = _rms(xo, gfin_ref[...])
        o_ref[...] = xo


def _moe(h2, x, modrows, gates, wg, wu, wd, g_final, final):
    bsz, seq, d = x.shape
    n = bsz * seq
    h2, x, gates = h2.reshape(n, d), x.reshape(n, d), gates.reshape(n, LANES)
    n_exp, _, dff = wg.shape
    tm, tf = min(MOE_TOKENS, seq), MOE_FF_CHUNK
    tiles_per_row = seq // tm
    rank, rank_t, cnt = _route(gates, n_exp, tm)
    counts = cnt[:, 0, :n_exp].reshape(-1)
    rem = counts % MOE_ROWS
    tail = jnp.logical_and(rem > 0, rem <= MOE_TAIL_ROWS)
    n_blocks = 2 * (counts // MOE_ROWS + (rem > MOE_TAIL_ROWS)) + tail
    grid_spec = pltpu.PrefetchScalarGridSpec(
        num_scalar_prefetch=1,
        grid=(n // tm, n_exp, dff // tf),
        in_specs=[pl.BlockSpec((tm, d), lambda i, e, f, c: (i, 0)),
                  pl.BlockSpec((tm, LANES), lambda i, e, f, c: (i, 0)),
                  pl.BlockSpec((None, n_exp, tm), lambda i, e, f, c: (i, 0, 0)),
                  pl.BlockSpec((tm, LANES), lambda i, e, f, c: (i, 0)),
                  pl.BlockSpec((None, d, tf), lambda i, e, f, c: (e, 0, f)),
                  pl.BlockSpec((None, d, tf), lambda i, e, f, c: (e, 0, f)),
                  pl.BlockSpec((None, tf, d), lambda i, e, f, c: (e, f, 0)),
                  pl.BlockSpec((tm, d), lambda i, e, f, c: (i, 0)),
                  pl.BlockSpec((None, 1, d), lambda i, e, f, c: (i // tiles_per_row, 0, N_MOD - 1)),
                  pl.BlockSpec((1, d), lambda i, e, f, c: (0, 0))],
        out_specs=pl.BlockSpec((tm, d), lambda i, e, f, c: (i, 0)),
        scratch_shapes=[pltpu.VMEM((tm, d), BF16), pltpu.VMEM((tm, d), F32)])
    out = pl.pallas_call(
        functools.partial(_moe_kernel, n_exp=n_exp, final=final),
        out_shape=jax.ShapeDtypeStruct((n, d), F32),
        grid_spec=grid_spec,
        compiler_params=_params(("arbitrary",) * 3),
        name="moe_swiglu",
    )(n_blocks, h2, rank, rank_t, gates, wg, wu, wd, x, modrows, g_final)
    return out.reshape(bsz, seq, d)


def _rope_tables(seq):
    axis_dim = DA_QK // 2
    n_freq = axis_dim // 2
    inv = jnp.exp(-math.log(ROPE_BASE) * jnp.arange(n_freq, dtype=F32) * (2.0 / axis_dim))
    t = jnp.arange(seq, dtype=jnp.int32)
    ar = (t // GRID_W).astype(F32)[:, None] * inv
    ac = (t % GRID_W).astype(F32)[:, None] * inv
    cos = jnp.concatenate([jnp.cos(ar), jnp.cos(ar), jnp.cos(ac), jnp.cos(ac)], axis=-1)
    sin = jnp.concatenate([jnp.sin(ar), jnp.sin(ar), jnp.sin(ac), jnp.sin(ac)], axis=-1)
    first = (jnp.arange(DA_QK) % axis_dim) < n_freq
    sa = jnp.where(first, -sin, 0.0)
    sb = jnp.where(first, 0.0, sin)
    rep = lambda a: jnp.tile(a, (1, LANES // DA_QK))
    return rep(cos), rep(sa), rep(sb)


def _identity_tables(seq):
    return jnp.ones((seq, LANES), F32), jnp.zeros((seq, LANES), F32), jnp.zeros((seq, LANES), F32)


def _pack_layer_weights(w_in, w_uq, w_ukv):
    d = w_in.shape[0]
    zpad = jnp.zeros((d, LANES - MLA_ROPE), w_in.dtype)
    o_kr = 3 * 512 + MLA_Q_RANK + MLA_KV_RANK
    o_pool = o_kr + MLA_ROPE
    w2 = jnp.concatenate([w_in[:, :o_kr + MLA_ROPE], zpad, w_in[:, o_pool:o_pool + 512]], axis=1).astype(BF16)
    wgate = w_in[:, o_pool + 512:].astype(BF16)
    hq = MLA_NOPE + MLA_ROPE
    wq = w_uq.reshape(MLA_Q_RANK, MLA_HEADS, hq)
    wq = jnp.concatenate([wq, jnp.zeros((MLA_Q_RANK, MLA_HEADS, MLA_QK_PAD - hq), w_uq.dtype)], axis=-1)
    wuq = wq.reshape(MLA_Q_RANK, MLA_HEADS * MLA_QK_PAD).astype(BF16)
    wkv = w_ukv.reshape(MLA_KV_RANK, MLA_HEADS, MLA_NOPE + MLA_V)
    wukv = jnp.concatenate([wkv[:, :, :MLA_NOPE].reshape(MLA_KV_RANK, -1),
                            wkv[:, :, MLA_NOPE:].reshape(MLA_KV_RANK, -1)], axis=1).astype(BF16)
    return w2, wgate, wuq, wukv


def kernel(x, c, ctx, c_ctx, w_mod, b_mod, g_mix, w_in, da_lambda, da_subln, mla_gq, w_uq, mla_gkv, w_ukv,
           pool_w, pool_scale, w_branch, w_out, g_ffn, ffn_w_gate, ffn_w_up, ffn_w_down, moe_router,
           moe_w_gate, moe_w_up, moe_w_down, g_final):
    bsz, seq, d = x.shape
    n_ctx = ctx.shape[1]
    depth = w_mod.shape[0]

    cond8 = jnp.zeros((8, d), F32).at[:bsz].set(c).at[bsz].set(c_ctx)
    mod = _mod_rows(cond8, w_mod, b_mod)
    lat_tabs = _rope_tables(seq)
    ctx_tabs = _identity_tables(n_ctx)
    g_fin = g_final.reshape(1, d)

    xc = ctx
    for l in range(depth):
        need_ctx = l < depth - 1
        lam_init = 0.8 - 0.6 * math.exp(-0.3 * l)
        mod_lat = mod[l, :bsz][:, None, :]
        mod_ctx = jnp.broadcast_to(mod[l, bsz][None, None, :], (bsz, 1, N_MOD * d))
        w2, wgate, wuq, wukv = _pack_layer_weights(w_in[l], w_uq[l], w_ukv[l])
        gmix = g_mix[l].reshape(1, d)
        gffn = g_ffn[l].reshape(1, d)
        gq = mla_gq[l].reshape(1, -1)
        gkv = mla_gkv[l].reshape(1, -1)
        subln = da_subln[l].reshape(1, -1)
        lam = da_lambda[l]
        pw = pool_w[l].astype(BF16)
        psc = pool_scale[l].reshape(1, -1)
        wbr = w_branch[l].astype(BF16)
        wout = w_out[l].astype(BF16)
        j = l // 2
        dense = l % 2 == 0
        final = l == depth - 1

        qda, kda, vda, qm, km, vm, pin = _inproj(x, mod_lat, gmix, w2, gq, wuq, gkv, wukv, lat_tabs, PROJ_ROWS)
        cqda, ckda, cvda, cqm, ckm, cvm, cpin = _inproj(xc, mod_ctx, gmix, w2, gq, wuq, gkv, wukv, ctx_tabs, PROJ_ROWS)

        o_da = _attention(qda, ckda, cvda, kda, vda, DA_HEADS, LANES, DA_V, ATTN_ROWS // 2, ATTN_KEYS, diff=True,
                          lam=lam, g=subln, lam_init=lam_init, name="diff_attn")
        o_mla = _attention(qm, ckm, cvm, km, vm, MLA_HEADS, MLA_QK_PAD, MLA_V, ATTN_ROWS, ATTN_KEYS, name="mla_attn")
        o_pool = _pool(pin, pw, psc, POOL_ROWS)

        router = None
        n_exp = moe_router.shape[-1]
        if not dense:
            router = moe_router[j].T
        outs = _merge(x, mod_lat, gmix, wgate, o_da, o_mla, o_pool, wbr, wout, gffn, router, n_exp, PROJ_ROWS)

        if dense:
            wg = ffn_w_gate[j].astype(BF16)
            wu = ffn_w_up[j].astype(BF16)
            wd = ffn_w_down[j].astype(BF16)
            x = _ffn(outs[1], outs[0], mod_lat, wg, wu, wd, g_fin, final, PROJ_ROWS)
        else:
            wg = moe_w_gate[j].astype(BF16)
            wu = moe_w_up[j].astype(BF16)
            wd = moe_w_down[j].astype(BF16)
            x = _moe(outs[1], outs[0], mod_lat, outs[2], wg, wu, wd, g_fin, final)

        if need_ctx:
            co_da = _attention(cqda, ckda, cvda, None, None, DA_HEADS, LANES, DA_V, ATTN_ROWS // 2, ATTN_KEYS, diff=True,
                               lam=lam, g=subln, lam_init=lam_init, name="diff_attn_ctx")
            co_mla = _attention(cqm, ckm, cvm, None, None, MLA_HEADS, MLA_QK_PAD, MLA_V, ATTN_ROWS, ATTN_KEYS,
                                name="mla_attn_ctx")
            co_pool = _pool(cpin, pw, psc, POOL_ROWS)
            couts = _merge(xc, mod_ctx, gmix, wgate, co_da, co_mla, co_pool, wbr, wout, gffn, router, n_exp, PROJ_ROWS)
            if dense:
                xc = _ffn(couts[1], couts[0], mod_ctx, wg, wu, wd, g_fin, False, PROJ_ROWS)
            else:
                xc = _moe(couts[1], couts[0], mod_ctx, couts[2], wg, wu, wd, g_fin, False)
    return x
```

```python
import functools
import math

import jax
import jax.numpy as jnp
from jax import lax
from jax.experimental import pallas as pl
from jax.experimental.pallas import tpu as pltpu

F32 = jnp.float32
BF16 = jnp.bfloat16

GRID_W = 64
DA_HEADS = 4
DA_QK = 64
DA_V = 128
MLA_HEADS = 4
MLA_NOPE = 128
MLA_ROPE = 64
MLA_V = 128
MLA_Q_RANK = 384
MLA_KV_RANK = 256
POOL_WINDOWS = (2, 4, 8, 16)
BRANCH_W = 512
ROPE_BASE = 10000.0
EPS = 1e-6
N_MOD = 6
LANES = 128
MLA_QK_PAD = 256
LOG2E = math.log2(math.e)
DA_SCALE = DA_QK ** -0.5 * LOG2E
MLA_SCALE = (MLA_NOPE + MLA_ROPE) ** -0.5 * LOG2E
NEG_BIG = -1e30
PROJ_ROWS = 512
POOL_ROWS = 2048
ATTN_ROWS = 1024
ATTN_KEYS = 512
ATTN_Q_TILES = 2
VMEM_LIMIT = 56 * 1024 * 1024

C_DAQ, C_DAK, C_DAV = 0, 512, 1024
C_QD = 1536
C_KVD = C_QD + MLA_Q_RANK
C_KR = C_KVD + MLA_KV_RANK
C_POOL = C_KR + LANES


def _sigmoid(v):
    return 1.0 / (1.0 + jnp.exp(-v))


def _params(sem, vmem=VMEM_LIMIT):
    return pltpu.CompilerParams(dimension_semantics=sem, vmem_limit_bytes=vmem)


def _const_spec(shape, single=False):
    nd = len(shape)
    if single:
        return pl.BlockSpec(shape, lambda *_: (0,) * nd, pipeline_mode=pl.Buffered(1))
    return pl.BlockSpec(shape, lambda *_: (0,) * nd)


def _mod_kernel(cond_ref, w_ref, b_ref, o_ref):
    c = cond_ref[...]
    s = c * _sigmoid(c)
    o_ref[...] = jnp.dot(s, w_ref[...], preferred_element_type=F32,
                         precision=lax.Precision.HIGHEST) + b_ref[...]


def _mod_rows(cond8, w_mod, b_mod):
    depth, d, n = w_mod.shape
    tn = n // 2
    return pl.pallas_call(
        _mod_kernel,
        out_shape=jax.ShapeDtypeStruct((depth, 8, n), F32),
        grid=(depth, n // tn),
        in_specs=[pl.BlockSpec((8, d), lambda l, j: (0, 0)),
                  pl.BlockSpec((None, d, tn), lambda l, j: (l, 0, j)),
                  pl.BlockSpec((None, 1, tn), lambda l, j: (l, 0, j))],
        out_specs=pl.BlockSpec((None, 8, tn), lambda l, j: (l, 0, j)),
        compiler_params=_params(("arbitrary", "arbitrary")),
        name="adaln_rows",
    )(cond8, w_mod, b_mod.reshape(depth, 1, n))


def _norm_mod(x, g, sh, sc):
    r = lax.rsqrt(jnp.mean(x * x, axis=-1, keepdims=True) + EPS)
    return (x * r * g) * (1.0 + sc) + sh


def _rms(v, g):
    return v * lax.rsqrt(jnp.mean(v * v, axis=-1, keepdims=True) + EPS) * g


def _inproj_kernel(x_ref, g_ref, sh_ref, sc_ref, w_ref, gq_ref, wuq_ref, gkv_ref, wukv_ref,
                   cos_ref, sa_ref, sb_ref,
                   qda_ref, kda_ref, vda_ref, qm_ref, km_ref, vm_ref, pool_ref):
    h = _norm_mod(x_ref[...], g_ref[...], sh_ref[...], sc_ref[...])
    z = jnp.dot(h.astype(BF16), w_ref[...], preferred_element_type=F32)
    cos = cos_ref[...]
    sa = sa_ref[...]
    sb = sb_ref[...]

    def rope(blk):
        return blk * cos + pltpu.roll(blk, LANES - 16, 1) * sa + pltpu.roll(blk, 16, 1) * sb

    lo = lax.broadcasted_iota(jnp.int32, cos.shape, 1) < MLA_ROPE

    for hh in range(DA_HEADS):
        c0 = hh * LANES
        qda_ref[:, c0:c0 + LANES] = (rope(z[:, C_DAQ + c0:C_DAQ + c0 + LANES]) * DA_SCALE).astype(BF16)
        kda_ref[c0:c0 + LANES, :] = rope(z[:, C_DAK + c0:C_DAK + c0 + LANES]).T.astype(BF16)

    qn = _rms(z[:, C_QD:C_QD + MLA_Q_RANK], gq_ref[...])
    qf = jnp.dot(qn.astype(BF16), wuq_ref[...], preferred_element_type=F32)
    kvn = _rms(z[:, C_KVD:C_KVD + MLA_KV_RANK], gkv_ref[...])
    kvf = jnp.dot(kvn.astype(BF16), wukv_ref[...], preferred_element_type=F32)
    kr = jnp.where(lo, rope(z[:, C_KR:C_KR + LANES]), 0.0).T.astype(BF16)
    for hh in range(MLA_HEADS):
        c0 = hh * MLA_QK_PAD
        qm_ref[:, c0:c0 + LANES] = (qf[:, c0:c0 + LANES] * MLA_SCALE).astype(BF16)
        qr = jnp.where(lo, rope(qf[:, c0 + LANES:c0 + 2 * LANES]), 0.0)
        qm_ref[:, c0 + LANES:c0 + 2 * LANES] = (qr * MLA_SCALE).astype(BF16)
        km_ref[c0:c0 + LANES, :] = kvf[:, hh * LANES:(hh + 1) * LANES].T.astype(BF16)
        km_ref[c0 + LANES:c0 + 2 * LANES, :] = kr
    ones = jnp.ones(cos.shape, BF16)
    for hh in range(MLA_HEADS):
        c0 = 2 * hh * LANES
        vda_ref[:, c0:c0 + LANES] = z[:, C_DAV + hh * LANES:C_DAV + (hh + 1) * LANES].astype(BF16)
        vda_ref[:, c0 + LANES:c0 + 2 * LANES] = ones
        vm_ref[:, c0:c0 + LANES] = kvf[:, 512 + hh * LANES:512 + (hh + 1) * LANES].astype(BF16)
        vm_ref[:, c0 + LANES:c0 + 2 * LANES] = ones
    pool_ref[...] = z[:, C_POOL:C_POOL + 512]


def _inproj(x, modrows, g_mix, w2, gq, wuq, gkv, wukv, tabs, tm):
    b, s, d = x.shape
    tm = min(tm, s)
    cos, sa, sb = tabs
    row = lambda j: pl.BlockSpec((None, 1, d), lambda bi, i: (bi, 0, j))
    tab = pl.BlockSpec((tm, LANES), lambda bi, i: (i, 0))
    out = lambda w, dt: jax.ShapeDtypeStruct((b, s, w), dt)
    ospec = lambda w: pl.BlockSpec((None, tm, w), lambda bi, i: (bi, i, 0))
    out_t = lambda w: jax.ShapeDtypeStruct((b, w, s), BF16)
    ospec_t = lambda w: pl.BlockSpec((None, w, tm), lambda bi, i: (bi, 0, i))
    return pl.pallas_call(
        _inproj_kernel,
        out_shape=(out(512, BF16), out_t(512), out(1024, BF16), out(1024, BF16), out_t(1024),
                   out(1024, BF16), out(512, F32)),
        grid=(b, s // tm),
        in_specs=[pl.BlockSpec((None, tm, d), lambda bi, i: (bi, i, 0)),
                  _const_spec((1, d)), row(0), row(1),
                  _const_spec(w2.shape), _const_spec(gq.shape), _const_spec(wuq.shape),
                  _const_spec(gkv.shape), _const_spec(wukv.shape), tab, tab, tab],
        out_specs=(ospec(512), ospec_t(512), ospec(1024), ospec(1024), ospec_t(1024), ospec(1024), ospec(512)),
        compiler_params=_params(("arbitrary", "arbitrary")),
        name="in_proj",
    )(x, g_mix, modrows, modrows, w2, gq, wuq, gkv, wukv, cos, sa, sb)


def _attn_kernel(*refs, diff, has_lat, tk, tq, lam_init):
    it = iter(refs)
    q_ref, kc_ref, vc_ref = next(it), next(it), next(it)
    kl_ref = vl_ref = lam_ref = g_ref = None
    if has_lat:
        kl_ref, vl_ref = next(it), next(it)
    if diff:
        lam_ref, g_ref = next(it), next(it)
    o_ref = next(it)
    m_ref, acc_ref = next(it), next(it)
    s_refs = (next(it), next(it))
    p_refs = (next(it), next(it))
    al_refs = (next(it), next(it))
    n_qt = q_ref.shape[0] // tq
    dv = o_ref.shape[-1]

    def stacked_q(qt):
        q = q_ref[qt * tq:(qt + 1) * tq, :]
        if not diff:
            return q
        lane = lax.broadcasted_iota(jnp.int32, q.shape, 1)
        zero = jnp.zeros_like(q)
        return jnp.concatenate([jnp.where(lane < DA_QK, q, zero), jnp.where(lane >= DA_QK, q, zero)], axis=0)

    qs = [stacked_q(qt) for qt in range(n_qt)]
    n_ctx = kc_ref.shape[1]
    n_lat = kl_ref.shape[1] // tk if has_lat else 0
    width = lambda c: n_ctx if c == 0 else tk
    k_of = lambda c: kc_ref[...] if c == 0 else kl_ref[:, (c - 1) * tk:c * tk]
    v_of = lambda c: vc_ref[...] if c == 0 else vl_ref[(c - 1) * tk:c * tk, :]

    def qk(slot, qt, c):
        k = k_of(c)
        s_refs[slot][:, :k.shape[1]] = jnp.dot(qs[qt], k, preferred_element_type=F32)

    def sm(slot, qt, c):
        s_ref, p_ref = s_refs[slot], p_refs[slot]
        blocks = [slice(j * LANES, (j + 1) * LANES) for j in range(width(c) // LANES)]
        mx = s_ref[:, blocks[0]]
        for blk in blocks[1:]:
            mx = jnp.maximum(mx, s_ref[:, blk])
        m = m_ref[qt]
        m_new = jnp.maximum(m, jnp.max(mx, axis=-1, keepdims=True))
        al_refs[slot][...] = jnp.exp2(m - m_new)
        m_ref[qt] = m_new
        for blk in blocks:
            p_ref[:, blk] = jnp.exp2(s_ref[:, blk] - m_new).astype(BF16)

    def pv(slot, qt, c):
        v = v_of(c)
        new = jnp.dot(p_refs[slot][:, :v.shape[0]], v, preferred_element_type=F32)
        al = al_refs[slot][...]
        for blk in (slice(0, dv), slice(dv, 2 * dv)):
            acc_ref[qt, :, blk] = al * acc_ref[qt, :, blk] + new[:, blk]

    def finish(qt):
        acc = acc_ref[qt]
        o = acc[:, :dv] / acc[:, dv:]
        if diff:
            lv = lam_ref[...]
            a = jnp.sum(lv[0:1, :] * lv[1:2, :], axis=-1, keepdims=True)
            b = jnp.sum(lv[2:3, :] * lv[3:4, :], axis=-1, keepdims=True)
            lam = jnp.exp(a) - jnp.exp(b) + lam_init
            o = o[:tq] - lam * o[tq:]
            o = _rms(o, g_ref[...]) * (1.0 - lam_init)
        o_ref[qt * tq:(qt + 1) * tq, :] = o.astype(o_ref.dtype)

    m_ref[...] = jnp.full(m_ref.shape, NEG_BIG, F32)
    acc_ref[...] = jnp.zeros(acc_ref.shape, F32)
    items = [(qt, 0) for qt in range(n_qt)] + [(qt, c) for qt in range(n_qt) for c in range(1, n_lat + 1)]
    for g in range(len(items) + 2):
        if g < len(items):
            qk(g % 2, *items[g])
        if 0 <= g - 2 < len(items):
            pv(g % 2, *items[g - 2])
        if 0 <= g - 1 < len(items):
            sm((g - 1) % 2, *items[g - 1])
        if 0 <= g - 2 < len(items) and items[g - 2][1] == n_lat:
            finish(items[g - 2][0])


def _attention(q, kc, vc, kl, vl, heads, dq, dv, tq, tk, diff=False, lam=None, g=None, lam_init=0.0,
               name="attn"):
    b, s, _ = q.shape
    c = kc.shape[2]
    tq = min(tq, s)
    n_qt = min(ATTN_Q_TILES, s // tq)
    has_lat = kl is not None
    args = [q, kc, vc]
    specs = [pl.BlockSpec((None, n_qt * tq, dq), lambda bi, hi, qi: (bi, qi, hi)),
             pl.BlockSpec((None, dq, c), lambda bi, hi, qi: (bi, hi, 0)),
             pl.BlockSpec((None, c, 2 * dv), lambda bi, hi, qi: (bi, 0, hi))]
    sl = 0
    if has_lat:
        sl = kl.shape[2]
        tk = min(tk, sl)
        args += [kl, vl]
        specs += [pl.BlockSpec((None, dq, sl), lambda bi, hi, qi: (bi, hi, 0)),
                  pl.BlockSpec((None, sl, 2 * dv), lambda bi, hi, qi: (bi, 0, hi))]
    if diff:
        args += [lam, g]
        specs += [_const_spec(lam.shape), _const_spec(g.shape)]
    kern = functools.partial(_attn_kernel, diff=diff, has_lat=has_lat, tk=tk, tq=tq, lam_init=lam_init)
    rows = 2 * tq if diff else tq
    wmax = max(tk, c) if has_lat else c
    scratch = [pltpu.VMEM((n_qt, rows, LANES), F32), pltpu.VMEM((n_qt, rows, 2 * dv), F32),
               pltpu.VMEM((rows, wmax), F32), pltpu.VMEM((rows, wmax), F32),
               pltpu.VMEM((rows, wmax), BF16), pltpu.VMEM((rows, wmax), BF16),
               pltpu.VMEM((rows, LANES), F32), pltpu.VMEM((rows, LANES), F32)]
    return pl.pallas_call(
        kern,
        out_shape=jax.ShapeDtypeStruct((b, s, heads * dv), BF16),
        grid=(b, heads, s // (n_qt * tq)),
        in_specs=specs,
        out_specs=pl.BlockSpec((None, n_qt * tq, dv), lambda bi, hi, qi: (bi, qi, hi)),
        scratch_shapes=scratch,
        compiler_params=_params(("arbitrary", "arbitrary", "arbitrary")),
        name=name,
    )(*args)


def _pool_kernel(prev_ref, cur_ref, next_ref, w_ref, sc_ref, o_ref, *, seq_len):
    i = pl.program_id(1)
    n = pl.num_programs(1)
    cur = cur_ref[...]
    tm = cur.shape[0]
    prev = jnp.where(i > 0, prev_ref[...], 0.0)
    nxt = jnp.where(i < n - 1, next_ref[...], 0.0)
    ext = jnp.concatenate([prev, cur, nxt], axis=0)
    ne = tm + 16
    t = i * tm + lax.broadcasted_iota(jnp.int32, (tm, 1), 0)
    for g, w in enumerate(POOL_WINDOWS):
        e = ext[:, g * LANES:(g + 1) * LANES]
        acc = e + pltpu.roll(e, 1, 0)
        half = 1
        while 2 * half < w:
            acc = pltpu.roll(acc, half, 0) + pltpu.roll(acc, ne - half, 0)
            half *= 2
        win = acc[8:8 + tm]
        lo = jnp.clip(t - w // 2, 0, seq_len)
        hi = jnp.clip(t - w // 2 + w, 0, seq_len)
        cnt = (hi - lo).astype(F32)
        dlt = win / cnt - cur[:, g * LANES:(g + 1) * LANES]
        y = jnp.dot(dlt.astype(BF16), w_ref[g], preferred_element_type=F32)
        o_ref[:, g * LANES:(g + 1) * LANES] = (y * sc_ref[:, g * LANES:(g + 1) * LANES]).astype(BF16)


def _pool(u, pool_w, pool_scale, tm):
    b, s, w = u.shape
    tm = min(tm, s)
    nb8 = s // 8
    r8 = tm // 8
    return pl.pallas_call(
        functools.partial(_pool_kernel, seq_len=s),
        out_shape=jax.ShapeDtypeStruct((b, s, w), BF16),
        grid=(b, s // tm),
        in_specs=[pl.BlockSpec((None, 8, w), lambda bi, i: (bi, jnp.maximum(i * r8 - 1, 0), 0)),
                  pl.BlockSpec((None, tm, w), lambda bi, i: (bi, i, 0)),
                  pl.BlockSpec((None, 8, w), lambda bi, i: (bi, jnp.minimum((i + 1) * r8, nb8 - 1), 0)),
                  _const_spec(pool_w.shape), _const_spec(pool_scale.shape)],
        out_specs=pl.BlockSpec((None, tm, w), lambda bi, i: (bi, i, 0)),
        compiler_params=_params(("arbitrary", "arbitrary")),
        name="pool_mixer",
    )(u, u, u, pool_w, pool_scale)


def _merge_kernel(*refs, moe, n_exp):
    (x_ref, gmix_ref, sh1_ref, sc1_ref, wgate_ref, oda_ref, omla_ref, opool_ref, wbr_ref, wout_ref,
     gt1_ref, gffn_ref, sh2_ref, sc2_ref) = refs[:14]
    if moe:
        router_ref, xo_ref, h2_ref, gates_ref = refs[14:]
    else:
        xo_ref, h2_ref = refs[14:]
    x = x_ref[...]
    d = x.shape[-1]
    h = _norm_mod(x, gmix_ref[...], sh1_ref[...], sc1_ref[...]).astype(BF16)
    merged = None
    for n, o_ref in enumerate((oda_ref, omla_ref, opool_ref)):
        gate = _sigmoid(jnp.dot(h, wgate_ref[:, n * d:(n + 1) * d], preferred_element_type=F32))
        proj = jnp.dot(o_ref[...], wbr_ref[n], preferred_element_type=F32)
        merged = gate * proj if merged is None else merged + gate * proj
    mix = jnp.dot(merged.astype(BF16), wout_ref[...], preferred_element_type=F32)
    xn = x + gt1_ref[...] * mix
    xo_ref[...] = xn
    h2 = _norm_mod(xn, gffn_ref[...], sh2_ref[...], sc2_ref[...])
    h2_ref[...] = h2.astype(BF16)
    if moe:
        logit = [jnp.sum(h2 * router_ref[e:e + 1, :], axis=-1, keepdims=True) for e in range(n_exp)]

        def top1(vals):
            best, idx = vals[0], jnp.zeros(vals[0].shape, jnp.int32)
            for e in range(1, n_exp):
                better = vals[e] > best
                best = jnp.where(better, vals[e], best)
                idx = jnp.where(better, e, idx)
            return best, idx

        v1, i1 = top1(logit)
        v2, i2 = top1([jnp.where(i1 == e, NEG_BIG, logit[e]) for e in range(n_exp)])
        w1 = 1.0 / (1.0 + jnp.exp(v2 - v1))
        lane = lax.broadcasted_iota(jnp.int32, gates_ref.shape, 1)
        gates_ref[...] = jnp.where(lane == i1, w1, 0.0) + jnp.where(lane == i2, 1.0 - w1, 0.0)


def _merge(x, modrows, g_mix, wgate, o_da, o_mla, o_pool, wbr, wout, g_ffn, router, n_exp, tm):
    b, s, d = x.shape
    tm = min(tm, s)
    moe = router is not None
    row = lambda j: pl.BlockSpec((None, 1, d), lambda bi, i: (bi, 0, j))
    tile = lambda w: pl.BlockSpec((None, tm, w), lambda bi, i: (bi, i, 0))
    args = [x, g_mix, modrows, modrows, wgate, o_da, o_mla, o_pool, wbr, wout, modrows, g_ffn, modrows, modrows]
    specs = [tile(d), _const_spec((1, d)), row(0), row(1), _const_spec(wgate.shape),
             tile(BRANCH_W), tile(BRANCH_W), tile(BRANCH_W), _const_spec(wbr.shape), _const_spec(wout.shape),
             row(2), _const_spec((1, d)), row(3), row(4)]
    out_shape = [jax.ShapeDtypeStruct((b, s, d), F32), jax.ShapeDtypeStruct((b, s, d), BF16)]
    out_specs = [tile(d), tile(d)]
    if moe:
        args.append(router)
        specs.append(_const_spec(router.shape))
        out_shape.append(jax.ShapeDtypeStruct((b, s, LANES), F32))
        out_specs.append(tile(LANES))
    return pl.pallas_call(
        functools.partial(_merge_kernel, moe=moe, n_exp=n_exp),
        out_shape=tuple(out_shape),
        grid=(b, s // tm),
        in_specs=specs,
        out_specs=tuple(out_specs),
        compiler_params=_params(("arbitrary", "arbitrary")),
        name="merge_out",
    )(*args)


def _ffn_kernel(h_ref, x_ref, gt_ref, wg_ref, wu_ref, wd_ref, gfin_ref, o_ref, *, final):
    h = h_ref[...]
    gte = jnp.dot(h, wg_ref[...], preferred_element_type=F32)
    up = jnp.dot(h, wu_ref[...], preferred_element_type=F32)
    act = (gte * _sigmoid(gte) * up).astype(BF16)
    y = jnp.dot(act, wd_ref[...], preferred_element_type=F32)
    xo = x_ref[...] + gt_ref[...] * y
    if final:
        xo = _rms(xo, gfin_ref[...])
    o_ref[...] = xo


def _ffn(h2, x, modrows, wg, wu, wd, g_final, final, tm):
    b, s, d = x.shape
    tm = min(tm, s)
    tile = pl.BlockSpec((None, tm, d), lambda bi, i: (bi, i, 0))
    return pl.pallas_call(
        functools.partial(_ffn_kernel, final=final),
        out_shape=jax.ShapeDtypeStruct((b, s, d), F32),
        grid=(b, s // tm),
        in_specs=[tile, tile, pl.BlockSpec((None, 1, d), lambda bi, i: (bi, 0, 5)),
                  _const_spec(wg.shape, True), _const_spec(wu.shape, True), _const_spec(wd.shape, True),
                  _const_spec((1, d))],
        out_specs=tile,
        compiler_params=_params(("arbitrary", "arbitrary")),
        name="ffn_swiglu",
    )(h2, x, modrows, wg, wu, wd, g_final)


ROUTE_SUB = 256
MOE_TOKENS = 1024
MOE_FF_CHUNK = 1792
MOE_ROWS = 256
MOE_TAIL_ROWS = 64


def _route_kernel(g_ref, rank_ref, rankt_ref, cnt_ref, *, n_exp):
    t = g_ref.shape[0]
    r_io = lax.broadcasted_iota(jnp.int32, (ROUTE_SUB, ROUTE_SUB), 0)
    c_io = lax.broadcasted_iota(jnp.int32, (ROUTE_SUB, ROUTE_SUB), 1)
    tri = jnp.where(c_io <= r_io, 1.0, 0.0).astype(BF16)
    carry = jnp.zeros((1, LANES), F32)
    for j in range(t // ROUTE_SUB):
        rows = slice(j * ROUTE_SUB, (j + 1) * ROUTE_SUB)
        routed = g_ref[rows, :] != 0.0
        incl = jnp.dot(tri, jnp.where(routed, 1.0, 0.0).astype(BF16), preferred_element_type=F32)
        rank_ref[rows, :] = jnp.where(routed, incl - 1.0 + carry, -1.0)
        carry = carry + incl[ROUTE_SUB - 1:ROUTE_SUB, :]
    rankt_ref[...] = rank_ref[...].T[0:n_exp, :]
    cnt_ref[...] = jnp.broadcast_to(carry, cnt_ref.shape).astype(jnp.int32)


def _route(gates, n_exp, tm):
    n = gates.shape[0]
    return pl.pallas_call(
        functools.partial(_route_kernel, n_exp=n_exp),
        out_shape=(jax.ShapeDtypeStruct((n, LANES), F32),
                   jax.ShapeDtypeStruct((n // tm, n_exp, tm), F32),
                   jax.ShapeDtypeStruct((n // tm, 8, LANES), jnp.int32)),
        grid=(n // tm,),
        in_specs=[pl.BlockSpec((tm, LANES), lambda i: (i, 0))],
        out_specs=(pl.BlockSpec((tm, LANES), lambda i: (i, 0)),
                   pl.BlockSpec((None, n_exp, tm), lambda i: (i, 0, 0)),
                   pl.BlockSpec((None, 8, LANES), lambda i: (i, 0, 0))),
        compiler_params=_params(("arbitrary",)),
        name="moe_route",
    )(gates)


def _moe_kernel(cnt_ref, h_ref, rank_ref, rankt_ref, gates_ref, wg_ref, wu_ref, wd_ref, x_ref, gt_ref,
                gfin_ref, o_ref, xg_ref, yg_ref, *, n_exp, final):
    i, e, f = pl.program_id(0), pl.program_id(1), pl.program_id(2)
    nf = pl.num_programs(2)
    t = h_ref.shape[0]
    code = cnt_ref[i * n_exp + e]
    n_full = lax.shift_right_logical(code, 1)

    @pl.when(jnp.logical_and(e == 0, f == 0))
    def _():
        o_ref[...] = jnp.zeros_like(o_ref)

    rank_row = rankt_ref[pl.ds(e, 1), :]

    def block(base, n_rows):
        rows = pl.ds(base, n_rows)

        @pl.when(f == 0)
        def _():
            slot = (base + lax.broadcasted_iota(jnp.int32, (n_rows, t), 0)).astype(F32)
            onehot = jnp.where(rank_row == slot, 1.0, 0.0).astype(BF16)
            xg_ref[rows, :] = jnp.dot(onehot, h_ref[...], preferred_element_type=F32).astype(BF16)

        xb = xg_ref[rows, :]
        gte = jnp.dot(xb, wg_ref[...], preferred_element_type=F32)
        up = jnp.dot(xb, wu_ref[...], preferred_element_type=F32)
        act = (gte * _sigmoid(gte) * up).astype(BF16)
        part = jnp.dot(act, wd_ref[...], preferred_element_type=F32)

        @pl.when(f == 0)
        def _():
            yg_ref[rows, :] = part

        @pl.when(f > 0)
        def _():
            yg_ref[rows, :] += part

        @pl.when(f == nf - 1)
        def _():
            lane = lax.broadcasted_iota(jnp.int32, (t, LANES), 1)
            rank_col = jnp.sum(jnp.where(lane == e, rank_ref[...], 0.0), axis=-1, keepdims=True)
            gate_col = jnp.sum(jnp.where(lane == e, gates_ref[...], 0.0), axis=-1, keepdims=True)
            slot = (base + lax.broadcasted_iota(jnp.int32, (t, n_rows), 1)).astype(F32)
            onehot = jnp.where(rank_col == slot, 1.0, 0.0).astype(BF16)
            o_ref[...] += gate_col * jnp.dot(onehot, yg_ref[rows, :].astype(BF16),
                                             preferred_element_type=F32)

    def full_block(b, carry):
        block(pl.multiple_of(b * MOE_ROWS, MOE_ROWS), MOE_ROWS)
        return carry

    lax.fori_loop(0, n_full, full_block, 0)

    @pl.when(jnp.bitwise_and(code, 1) == 1)
    def _():
        block(pl.multiple_of(n_full * MOE_ROWS, MOE_ROWS), MOE_TAIL_ROWS)

    @pl.when(jnp.logical_and(e == n_exp - 1, f == nf - 1))
    def _():
        xo = x_ref[...] + gt_ref[...] * o_ref[...]
        if final:
            xo = _rms(xo, gfin_ref[...])
        o_ref[...] = xo


def _moe(h2, x, modrows, gates, wg, wu, wd, g_final, final):
    bsz, seq, d = x.shape
    n = bsz * seq
    h2, x, gates = h2.reshape(n, d), x.reshape(n, d), gates.reshape(n, LANES)
    n_exp, _, dff = wg.shape
    tm, tf = min(MOE_TOKENS, seq), MOE_FF_CHUNK
    tiles_per_row = seq // tm
    rank, rank_t, cnt = _route(gates, n_exp, tm)
    counts = cnt[:, 0, :n_exp].reshape(-1)
    rem = counts % MOE_ROWS
    tail = jnp.logical_and(rem > 0, rem <= MOE_TAIL_ROWS)
    n_blocks = 2 * (counts // MOE_ROWS + (rem > MOE_TAIL_ROWS)) + tail
    grid_spec = pltpu.PrefetchScalarGridSpec(
        num_scalar_prefetch=1,
        grid=(n // tm, n_exp, dff // tf),
        in_specs=[pl.BlockSpec((tm, d), lambda i, e, f, c: (i, 0)),
                  pl.BlockSpec((tm, LANES), lambda i, e, f, c: (i, 0)),
                  pl.BlockSpec((None, n_exp, tm), lambda i, e, f, c: (i, 0, 0)),
                  pl.BlockSpec((tm, LANES), lambda i, e, f, c: (i, 0)),
                  pl.BlockSpec((None, d, tf), lambda i, e, f, c: (e, 0, f)),
                  pl.BlockSpec((None, d, tf), lambda i, e, f, c: (e, 0, f)),
                  pl.BlockSpec((None, tf, d), lambda i, e, f, c: (e, f, 0)),
                  pl.BlockSpec((tm, d), lambda i, e, f, c: (i, 0)),
                  pl.BlockSpec((None, 1, d), lambda i, e, f, c: (i // tiles_per_row, 0, N_MOD - 1)),
                  pl.BlockSpec((1, d), lambda i, e, f, c: (0, 0))],
        out_specs=pl.BlockSpec((tm, d), lambda i, e, f, c: (i, 0)),
        scratch_shapes=[pltpu.VMEM((tm, d), BF16), pltpu.VMEM((tm, d), F32)])
    out = pl.pallas_call(
        functools.partial(_moe_kernel, n_exp=n_exp, final=final),
        out_shape=jax.ShapeDtypeStruct((n, d), F32),
        grid_spec=grid_spec,
        compiler_params=_params(("arbitrary",) * 3),
        name="moe_swiglu",
    )(n_blocks, h2, rank, rank_t, gates, wg, wu, wd, x, modrows, g_final)
    return out.reshape(bsz, seq, d)


def _rope_tables(seq):
    axis_dim = DA_QK // 2
    n_freq = axis_dim // 2
    inv = jnp.exp(-math.log(ROPE_BASE) * jnp.arange(n_freq, dtype=F32) * (2.0 / axis_dim))
    t = jnp.arange(seq, dtype=jnp.int32)
    ar = (t // GRID_W).astype(F32)[:, None] * inv
    ac = (t % GRID_W).astype(F32)[:, None] * inv
    cos = jnp.concatenate([jnp.cos(ar), jnp.cos(ar), jnp.cos(ac), jnp.cos(ac)], axis=-1)
    sin = jnp.concatenate([jnp.sin(ar), jnp.sin(ar), jnp.sin(ac), jnp.sin(ac)], axis=-1)
    first = (jnp.arange(DA_QK) % axis_dim) < n_freq
    sa = jnp.where(first, -sin, 0.0)
    sb = jnp.where(first, 0.0, sin)
    rep = lambda a: jnp.tile(a, (1, LANES // DA_QK))
    return rep(cos), rep(sa), rep(sb)


def _identity_tables(seq):
    return jnp.ones((seq, LANES), F32), jnp.zeros((seq, LANES), F32), jnp.zeros((seq, LANES), F32)


def _pack_layer_weights(w_in, w_uq, w_ukv):
    d = w_in.shape[0]
    zpad = jnp.zeros((d, LANES - MLA_ROPE), w_in.dtype)
    o_kr = 3 * 512 + MLA_Q_RANK + MLA_KV_RANK
    o_pool = o_kr + MLA_ROPE
    w2 = jnp.concatenate([w_in[:, :o_kr + MLA_ROPE], zpad, w_in[:, o_pool:o_pool + 512]], axis=1).astype(BF16)
    wgate = w_in[:, o_pool + 512:].astype(BF16)
    hq = MLA_NOPE + MLA_ROPE
    wq = w_uq.reshape(MLA_Q_RANK, MLA_HEADS, hq)
    wq = jnp.concatenate([wq, jnp.zeros((MLA_Q_RANK, MLA_HEADS, MLA_QK_PAD - hq), w_uq.dtype)], axis=-1)
    wuq = wq.reshape(MLA_Q_RANK, MLA_HEADS * MLA_QK_PAD).astype(BF16)
    wkv = w_ukv.reshape(MLA_KV_RANK, MLA_HEADS, MLA_NOPE + MLA_V)
    wukv = jnp.concatenate([wkv[:, :, :MLA_NOPE].reshape(MLA_KV_RANK, -1),
                            wkv[:, :, MLA_NOPE:].reshape(MLA_KV_RANK, -1)], axis=1).astype(BF16)
    return w2, wgate, wuq, wukv


def kernel(x, c, ctx, c_ctx, w_mod, b_mod, g_mix, w_in, da_lambda, da_subln, mla_gq, w_uq, mla_gkv, w_ukv,
           pool_w, pool_scale, w_branch, w_out, g_ffn, ffn_w_gate, ffn_w_up, ffn_w_down, moe_router,
           moe_w_gate, moe_w_up, moe_w_down, g_final):
    bsz, seq, d = x.shape
    n_ctx = ctx.shape[1]
    depth = w_mod.shape[0]

    cond8 = jnp.zeros((8, d), F32).at[:bsz].set(c).at[bsz].set(c_ctx)
    mod = _mod_rows(cond8, w_mod, b_mod)
    lat_tabs = _rope_tables(seq)
    ctx_tabs = _identity_tables(n_ctx)
    g_fin = g_final.reshape(1, d)

    xc = ctx
    for l in range(depth):
        need_ctx = l < depth - 1
        lam_init = 0.8 - 0.6 * math.exp(-0.3 * l)
        mod_lat = mod[l, :bsz][:, None, :]
        mod_ctx = jnp.broadcast_to(mod[l, bsz][None, None, :], (bsz, 1, N_MOD * d))
        w2, wgate, wuq, wukv = _pack_layer_weights(w_in[l], w_uq[l], w_ukv[l])
        gmix = g_mix[l].reshape(1, d)
        gffn = g_ffn[l].reshape(1, d)
        gq = mla_gq[l].reshape(1, -1)
        gkv = mla_gkv[l].reshape(1, -1)
        subln = da_subln[l].reshape(1, -1)
        lam = da_lambda[l]
        pw = pool_w[l].astype(BF16)
        psc = pool_scale[l].reshape(1, -1)
        wbr = w_branch[l].astype(BF16)
        wout = w_out[l].astype(BF16)
        j = l // 2
        dense = l % 2 == 0
        final = l == depth - 1

        qda, kda, vda, qm, km, vm, pin = _inproj(x, mod_lat, gmix, w2, gq, wuq, gkv, wukv, lat_tabs, PROJ_ROWS)
        cqda, ckda, cvda, cqm, ckm, cvm, cpin = _inproj(xc, mod_ctx, gmix, w2, gq, wuq, gkv, wukv, ctx_tabs, PROJ_ROWS)

        o_da = _attention(qda, ckda, cvda, kda, vda, DA_HEADS, LANES, DA_V, ATTN_ROWS // 2, ATTN_KEYS, diff=True,
                          lam=lam, g=subln, lam_init=lam_init, name="diff_attn")
        o_mla = _attention(qm, ckm, cvm, km, vm, MLA_HEADS, MLA_QK_PAD, MLA_V, ATTN_ROWS, ATTN_KEYS, name="mla_attn")
        o_pool = _pool(pin, pw, psc, POOL_ROWS)

        router = None
        n_exp = moe_router.shape[-1]
        if not dense:
            router = moe_router[j].T
        outs = _merge(x, mod_lat, gmix, wgate, o_da, o_mla, o_pool, wbr, wout, gffn, router, n_exp, PROJ_ROWS)

        if dense:
            wg = ffn_w_gate[j].astype(BF16)
            wu = ffn_w_up[j].astype(BF16)
            wd = ffn_w_down[j].astype(BF16)
            x = _ffn(outs[1], outs[0], mod_lat, wg, wu, wd, g_fin, final, PROJ_ROWS)
        else:
            wg = moe_w_gate[j].astype(BF16)
            wu = moe_w_up[j].astype(BF16)
            wd = moe_w_down[j].astype(BF16)
            x = _moe(outs[1], outs[0], mod_lat, outs[2], wg, wu, wd, g_fin, final)

        if need_ctx:
            co_da = _attention(cqda, ckda, cvda, None, None, DA_HEADS, LANES, DA_V, ATTN_ROWS // 2, ATTN_KEYS, diff=True,
                               lam=lam, g=subln, lam_init=lam_init, name="diff_attn_ctx")
            co_mla = _attention(cqm, ckm, cvm, None, None, MLA_HEADS, MLA_QK_PAD, MLA_V, ATTN_ROWS, ATTN_KEYS,
                                name="mla_attn_ctx")
            co_pool = _pool(cpin, pw, psc, POOL_ROWS)
            couts = _merge(xc, mod_ctx, gmix, wgate, co_da, co_mla, co_pool, wbr, wout, gffn, router, n_exp, PROJ_ROWS)
            if dense:
                xc = _ffn(couts[1], couts[0], mod_ctx, wg, wu, wd, g_fin, False, PROJ_ROWS)
            else:
                xc = _moe(couts[1], couts[0], mod_ctx, couts[2], wg, wu, wd, g_fin, False)
    return x
```
